```python
import math
import jax
import jax.numpy as jnp
from jax import lax
import numpy as np

D_MODEL = 2048
BATCH = 1
SEQ = 8192
DEPTH = 1

N_HEADS = 32
N_KV_HEADS = 4
HEAD_DIM = D_MODEL // N_HEADS
WINDOW = 128
N_BUCKETS = 32
MAX_DISTANCE = 128
GLA_HEADS = 4
GLA_DK = D_MODEL // 2 // GLA_HEADS
GLA_DV = D_MODEL // GLA_HEADS
GLA_LOWRANK = 16
GLA_TAU = 16.0
GLA_CHUNK = 64
N_GROUPS = 4
EXPERTS_PER_GROUP = 8
N_EXPERTS = N_GROUPS * EXPERTS_PER_GROUP
TOP_K = 2
D_EXPERT = D_MODEL // 4
MOE_BLOCK = 128

RMS_EPS = 1e-6
NEG_INF = -1e30

ATTN_Q = N_HEADS * HEAD_DIM
ATTN_KV = N_KV_HEADS * HEAD_DIM
GLA_QK = GLA_HEADS * GLA_DK
GLA_V = GLA_HEADS * GLA_DV
MERGE_WIDTH = D_MODEL
SPLIT_SIZES = (ATTN_Q, ATTN_KV, ATTN_KV, GLA_QK, GLA_QK, GLA_V, GLA_V, GLA_LOWRANK, MERGE_WIDTH, MERGE_WIDTH)
IN_COLS = sum(SPLIT_SIZES)

kernel_name = 'hybrid_swa_sink_gla_hmoe_block'


def rms_norm(x, g):
    xf = x.astype(jnp.float32)
    y = xf * lax.rsqrt(jnp.mean(xf * xf, axis=-1, keepdims=True) + RMS_EPS)
    return (y * g.astype(jnp.float32)).astype(x.dtype)


def split_columns(p):
    outs = []
    start = 0
    for size in SPLIT_SIZES:
        outs.append(p[..., start:start + size])
        start += size
    return outs


def t5_bucket(dist):
    max_exact = N_BUCKETS // 2
    n = jnp.maximum(dist, 0)
    ratio = jnp.maximum(n, max_exact).astype(jnp.float32) / max_exact
    large = max_exact + (jnp.log(ratio) / math.log(MAX_DISTANCE / max_exact) * (N_BUCKETS - max_exact)).astype(jnp.int32)
    large = jnp.minimum(large, N_BUCKETS - 1)
    return jnp.where(n < max_exact, n, large)


def sliding_window_attention(q, k, v, sinks, bias_table):
    b, s = q.shape[0], q.shape[1]
    nb = s // WINDOW
    gq = N_HEADS // N_KV_HEADS
    q = q.reshape(b, nb, WINDOW, N_KV_HEADS, gq, HEAD_DIM)

    def band(t):
        pad = jnp.zeros((b, WINDOW, N_KV_HEADS, HEAD_DIM), t.dtype)
        tp = jnp.concatenate([pad, t], axis=1).reshape(b, nb + 1, WINDOW, N_KV_HEADS, HEAD_DIM)
        return jnp.concatenate([tp[:, :-1], tp[:, 1:]], axis=2)

    kb, vb = band(k), band(v)
    q_loc = jnp.arange(WINDOW)[:, None]
    k_loc = jnp.arange(2 * WINDOW)[None, :]
    dist = q_loc + WINDOW - k_loc
    bias = bias_table.astype(jnp.float32)[t5_bucket(dist)]
    bias = bias.transpose(2, 0, 1).reshape(N_KV_HEADS, gq, WINDOW, 2 * WINDOW)
    k_abs = jnp.arange(nb)[:, None, None] * WINDOW + k_loc[None] - WINDOW
    mask = (dist >= 0) & (dist < WINDOW) & (k_abs >= 0)
    logits = jnp.einsum('bnqhgd,bnkhd->bnhgqk', q, kb).astype(jnp.float32) * (HEAD_DIM ** -0.5) + bias
    logits = jnp.where(mask[None, :, None, None], logits, NEG_INF)
    sink = sinks.astype(jnp.float32).reshape(N_KV_HEADS, gq, 1, 1)
    m = jnp.maximum(jnp.max(logits, axis=-1, keepdims=True), sink)
    p = jnp.exp(logits - m)
    denom = jnp.sum(p, axis=-1, keepdims=True) + jnp.exp(sink - m)
    probs = (p / denom).astype(v.dtype)
    out = jnp.einsum('bnhgqk,bnkhd->bnqhgd', probs, vb)
    return out.reshape(b, s, N_HEADS * HEAD_DIM)


def gated_linear_attention(q, k, v, log_a):
    b, s, h, dk = q.shape
    dv = v.shape[-1]
    nc = s // GLA_CHUNK

    def chunks(t):
        return t.astype(jnp.float32).reshape(b, nc, GLA_CHUNK, h, t.shape[-1]).transpose(1, 0, 3, 2, 4)

    q, k, v, la = chunks(q), chunks(k), chunks(v), chunks(log_a)
    q = q * (dk ** -0.5)
    cum = jnp.cumsum(la, axis=3)
    ref = cum[:, :, :, GLA_CHUNK // 2 - 1:GLA_CHUNK // 2]
    last = cum[:, :, :, -1:]
    a_intra = jnp.einsum('nbhid,nbhjd->nbhij', q * jnp.exp(cum - ref), k * jnp.exp(ref - cum))
    causal = jnp.tril(jnp.ones((GLA_CHUNK, GLA_CHUNK), dtype=bool))
    a_intra = jnp.where(causal, a_intra, 0.0)
    o_intra = jnp.einsum('nbhij,nbhjv->nbhiv', a_intra, v)
    q_in = q * jnp.exp(cum)
    k_dec = k * jnp.exp(last - cum)
    decay = jnp.exp(last[:, :, :, 0, :])

    def step(state, xs):
        qc, kc, vc, dc = xs
        o = jnp.einsum('bhid,bhdv->bhiv', qc, state)
        state = state * dc[..., None] + jnp.einsum('bhjd,bhjv->bhdv', kc, vc)
        return state, o

    s0 = jnp.zeros((b, h, dk, dv), jnp.float32)
    _, o_inter = lax.scan(step, s0, (q_in, k_dec, v, decay))
    o = o_intra + o_inter
    return o.transpose(1, 0, 3, 2, 4).reshape(b, s, h, dv)


def hierarchical_moe(h, w_rg, b_rg, w_re, b_re, w_gate, w_up, w_down):
    b, s, d = h.shape
    t = b * s
    ht = h.reshape(t, d)
    g_logits = (ht @ w_rg).astype(jnp.float32)
    g_prob = jax.nn.softmax(g_logits, axis=-1)
    g_idx = jnp.argmax(g_logits + b_rg.astype(jnp.float32), axis=-1)
    p_group = jnp.take_along_axis(g_prob, g_idx[:, None], axis=-1)
    e_logits = (ht @ w_re).astype(jnp.float32).reshape(t, N_GROUPS, EXPERTS_PER_GROUP)
    e_logits = jnp.take_along_axis(e_logits, g_idx[:, None, None], axis=1)[:, 0]
    e_bias = b_re.astype(jnp.float32).reshape(N_GROUPS, EXPERTS_PER_GROUP)[g_idx]
    e_prob = jax.nn.softmax(e_logits, axis=-1)
    _, local = lax.top_k(e_logits + e_bias, TOP_K)
    q_sel = jnp.take_along_axis(e_prob, local, axis=-1)
    gate = p_group * q_sel / jnp.sum(q_sel, axis=-1, keepdims=True)
    expert = g_idx[:, None] * EXPERTS_PER_GROUP + local

    n_pairs = t * TOP_K
    flat_e = expert.reshape(-1)
    order = jnp.argsort(flat_e)
    sorted_e = flat_e[order]
    sorted_tok = order // TOP_K
    sorted_gate = gate.reshape(-1)[order]
    counts = jnp.bincount(flat_e, length=N_EXPERTS)
    padded = (counts + MOE_BLOCK - 1) // MOE_BLOCK * MOE_BLOCK
    start = jnp.cumsum(counts) - counts
    pend = jnp.cumsum(padded)
    pstart = pend - padded
    dest = pstart[sorted_e] + jnp.arange(n_pairs) - start[sorted_e]
    n_blocks = (n_pairs + N_EXPERTS * (MOE_BLOCK - 1) + MOE_BLOCK - 1) // MOE_BLOCK
    buf = jnp.zeros((n_blocks * MOE_BLOCK, d), h.dtype).at[dest].set(ht[sorted_tok])
    block_e = jnp.minimum(jnp.searchsorted(pend, jnp.arange(n_blocks) * MOE_BLOCK, side='right'), N_EXPERTS - 1)

    def expert_block(args):
        xb, e = args
        a = jax.nn.silu(xb @ w_gate[e]) * (xb @ w_up[e])
        return a @ w_down[e]

    yb = lax.map(expert_block, (buf.reshape(n_blocks, MOE_BLOCK, d), block_e)).reshape(-1, d)
    contrib = yb[dest].astype(jnp.float32) * sorted_gate[:, None]
    y = jnp.zeros((t, d), jnp.float32).at[sorted_tok].add(contrib)
    return y.astype(h.dtype).reshape(b, s, d)


def setup_inputs(seed: int = 0) -> dict:
    key = jax.random.key(seed)
    ks = jax.random.split(key, 20)
    f32 = jnp.float32

    def normal(k, shape, scale):
        return jax.random.normal(k, shape, f32) * scale

    return {
        'x': normal(ks[0], (BATCH, SEQ, D_MODEL), 1.0),
        'norm_mix_g': 1.0 + normal(ks[1], (DEPTH, D_MODEL), 0.02),
        'w_in': normal(ks[2], (DEPTH, D_MODEL, IN_COLS), D_MODEL ** -0.5),
        'w_gla_a2': normal(ks[3], (DEPTH, GLA_LOWRANK, GLA_QK), GLA_LOWRANK ** -0.5),
        'b_gla_a': normal(ks[4], (DEPTH, GLA_QK), 0.1),
        'gla_norm_g': 1.0 + normal(ks[5], (DEPTH, GLA_DV), 0.02),
        'attn_sinks': normal(ks[6], (DEPTH, N_HEADS), 0.5),
        'rel_bias_table': normal(ks[7], (N_BUCKETS, N_HEADS), 0.5),
        'w_out': normal(ks[8], (DEPTH, MERGE_WIDTH, D_MODEL), MERGE_WIDTH ** -0.5),
        'norm_ffn_g': 1.0 + normal(ks[9], (DEPTH, D_MODEL), 0.02),
        'w_router_group': normal(ks[10], (DEPTH, D_MODEL, N_GROUPS), D_MODEL ** -0.5),
        'b_router_group': normal(ks[11], (DEPTH, N_GROUPS), 0.01),
        'w_router_expert': normal(ks[12], (DEPTH, D_MODEL, N_EXPERTS), D_MODEL ** -0.5),
        'b_router_expert': normal(ks[13], (DEPTH, N_EXPERTS), 0.01),
        'w_expert_gate': normal(ks[14], (DEPTH, N_EXPERTS, D_MODEL, D_EXPERT), D_MODEL ** -0.5),
        'w_expert_up': normal(ks[15], (DEPTH, N_EXPERTS, D_MODEL, D_EXPERT), D_MODEL ** -0.5),
        'w_expert_down': normal(ks[16], (DEPTH, N_EXPERTS, D_EXPERT, D_MODEL), D_EXPERT ** -0.5),
        'norm_final_g': 1.0 + normal(ks[17], (D_MODEL,), 0.02),
    }


def reference(x, norm_mix_g, w_in, w_gla_a2, b_gla_a, gla_norm_g, attn_sinks, rel_bias_table, w_out,
              norm_ffn_g, w_router_group, b_router_group, w_router_expert, b_router_expert,
              w_expert_gate, w_expert_up, w_expert_down, norm_final_g):
    b, s, _ = x.shape
    for l in range(DEPTH):
        h = rms_norm(x, norm_mix_g[l])
        proj = h @ w_in[l]
        aq, ak, av, gq, gk, gv, gr, ga, gate_attn, gate_gla = split_columns(proj)
        o_attn = sliding_window_attention(
            aq.reshape(b, s, N_HEADS, HEAD_DIM),
            ak.reshape(b, s, N_KV_HEADS, HEAD_DIM),
            av.reshape(b, s, N_KV_HEADS, HEAD_DIM),
            attn_sinks[l], rel_bias_table)
        log_a = jax.nn.log_sigmoid((ga @ w_gla_a2[l] + b_gla_a[l]).astype(jnp.float32)) / GLA_TAU
        o_gla = gated_linear_attention(
            gq.reshape(b, s, GLA_HEADS, GLA_DK),
            gk.reshape(b, s, GLA_HEADS, GLA_DK),
            gv.reshape(b, s, GLA_HEADS, GLA_DV),
            log_a.reshape(b, s, GLA_HEADS, GLA_DK)).astype(x.dtype)
        o_gla = rms_norm(o_gla, gla_norm_g[l]).reshape(b, s, GLA_V) * jax.nn.silu(gr)
        merged = jax.nn.sigmoid(gate_attn) * o_attn + jax.nn.sigmoid(gate_gla) * o_gla
        x = x + merged @ w_out[l]
        h = rms_norm(x, norm_ffn_g[l])
        x = x + hierarchical_moe(h, w_router_group[l], b_router_group[l], w_router_expert[l],
                                 b_router_expert[l], w_expert_gate[l], w_expert_up[l], w_expert_down[l])
    return rms_norm(x, norm_final_g)
```

```python
import functools
import math

import numpy as np
import jax
import jax.numpy as jnp
from jax import lax
from jax.experimental import pallas as pl
from jax.experimental.pallas import tpu as pltpu

F32 = jnp.float32
BF16 = jnp.bfloat16

N_HEADS = 32
N_KV_HEADS = 4
HEAD_DIM = 64
WINDOW = 128
N_BUCKETS = 32
MAX_DISTANCE = 128
GLA_HEADS = 4
GLA_DK = 256
GLA_DV = 512
GLA_LOWRANK = 16
GLA_TAU = 16.0
GLA_CHUNK = 64
N_GROUPS = 4
EXPERTS_PER_GROUP = 8
N_EXPERTS = 32
TOP_K = 2
MOE_BLOCK = 128
RMS_EPS = 1e-6
NEG_INF = -1e30

COL_AQ = 0
COL_AK = 2048
COL_AV = 2304
COL_GQ = 2560
COL_GK = 3584
COL_GV = 4608
COL_GR = 6656
COL_GA = 8704
COL_GATES = 8720
MAIN_COLS = 8704

LANES = 128
VMEM_LIMIT = 56 * 1024 * 1024


def _cparams(sem, vmem=VMEM_LIMIT):
    return pltpu.CompilerParams(dimension_semantics=sem, vmem_limit_bytes=vmem)


def _split3(x):
    hi = x.astype(BF16)
    r1 = x - hi.astype(F32)
    mid = r1.astype(BF16)
    lo = (r1 - mid.astype(F32)).astype(BF16)
    return hi, mid, lo


def _sigmoid(x):
    return 1.0 / (1.0 + jnp.exp(-x))


def _dot(a, b):
    return jnp.dot(a, b, preferred_element_type=F32)


def _dot_nt(a, b):
    return lax.dot_general(a, b, (((1,), (1,)), ((), ())), preferred_element_type=F32)


def _dot_tn(a, b):
    return lax.dot_general(a, b, (((0,), (0,)), ((), ())), preferred_element_type=F32)


def _norm_loga_kernel(x_ref, g_ref, wga_ref, w2_ref, b_ref, h_ref, la_ref):
    x = x_ref[...]
    ms = jnp.mean(x * x, axis=-1, keepdims=True)
    hb = (x * lax.rsqrt(ms + RMS_EPS) * g_ref[...]).astype(BF16)
    h_ref[...] = hb
    ga = _dot(hb, wga_ref[...])
    z = _dot(ga.astype(BF16), w2_ref[...]) + b_ref[...]
    la_ref[...] = (jnp.minimum(z, 0.0) - jnp.log1p(jnp.exp(-jnp.abs(z)))) * (1.0 / GLA_TAU)


def _norm_loga(x, g, wga, w2, b, tm):
    s, d = x.shape
    nq = w2.shape[1]
    return pl.pallas_call(
        _norm_loga_kernel,
        grid=(s // tm,),
        in_specs=[
            pl.BlockSpec((tm, d), lambda i: (i, 0)),
            pl.BlockSpec((1, d), lambda i: (0, 0)),
            pl.BlockSpec((d, LANES), lambda i: (0, 0)),
            pl.BlockSpec((LANES, nq), lambda i: (0, 0)),
            pl.BlockSpec((1, nq), lambda i: (0, 0)),
        ],
        out_specs=[
            pl.BlockSpec((tm, d), lambda i: (i, 0)),
            pl.BlockSpec((tm, nq), lambda i: (i, 0)),
        ],
        out_shape=[
            jax.ShapeDtypeStruct((s, d), BF16),
            jax.ShapeDtypeStruct((s, nq), F32),
        ],
        compiler_params=_cparams(("parallel",)),
        name="norm_loga",
    )(x, g, wga, w2, b)


def _proj_f32w_kernel(h_ref, w_ref, o_ref, wbf_ref):
    @pl.when(pl.program_id(1) == 0)
    def _():
        wbf_ref[...] = w_ref[...].astype(BF16)

    o_ref[...] = _dot(h_ref[...], wbf_ref[...]).astype(o_ref.dtype)


def _proj_bf16w_kernel(h_ref, w_ref, o_ref):
    o_ref[...] = _dot(h_ref[...], w_ref[...]).astype(o_ref.dtype)


def _in_proj(h, w, n_cols, tm, tn):
    s, d = h.shape
    grid = (n_cols // tn, s // tm)
    in_specs = [
        pl.BlockSpec((tm, d), lambda j, i: (i, 0)),
        pl.BlockSpec((d, tn), lambda j, i: (0, j)),
    ]
    out_spec = pl.BlockSpec((tm, tn), lambda j, i: (i, j))
    out_shape = jax.ShapeDtypeStruct((s, n_cols), BF16)
    if w.dtype == BF16:
        return pl.pallas_call(
            _proj_bf16w_kernel, grid=grid, in_specs=in_specs, out_specs=out_spec,
            out_shape=out_shape, compiler_params=_cparams(("parallel", "arbitrary")),
            name="in_proj_gates",
        )(h, w)
    return pl.pallas_call(
        _proj_f32w_kernel, grid=grid, in_specs=in_specs, out_specs=out_spec,
        out_shape=out_shape, scratch_shapes=[pltpu.VMEM((d, tn), BF16)],
        compiler_params=_cparams(("parallel", "arbitrary")),
        name="in_proj_main",
    )(h, w)


def _bucket_map():
    q_loc = np.arange(WINDOW)[:, None]
    k_loc = np.arange(2 * WINDOW)[None, :]
    n = np.maximum(q_loc + WINDOW - k_loc, 0)
    max_exact = N_BUCKETS // 2
    ratio = np.maximum(n, max_exact).astype(np.float32) / np.float32(max_exact)
    large = max_exact + (np.log(ratio) / np.float32(math.log(MAX_DISTANCE / max_exact))
                         * (N_BUCKETS - max_exact)).astype(np.int32)
    large = np.minimum(large, N_BUCKETS - 1)
    return np.where(n < max_exact, n, large).astype(np.int32).reshape(1, -1)


def _rel_bias_kernel(tab_ref, bucket_ref, o_ref):
    nb = tab_ref.shape[1]
    width = bucket_ref.shape[1]
    onehot = (lax.broadcasted_iota(jnp.int32, (nb, width), 0) == bucket_ref[...]).astype(BF16)
    hi, mid, lo = _split3(tab_ref[...])
    o_ref[...] = _dot(hi, onehot) + _dot(mid, onehot) + _dot(lo, onehot)


def _rel_bias(table_t, bucket):
    nh = table_t.shape[0]
    width = bucket.shape[1]
    return pl.pallas_call(
        _rel_bias_kernel,
        out_shape=jax.ShapeDtypeStruct((nh, width), F32),
        compiler_params=pltpu.CompilerParams(vmem_limit_bytes=VMEM_LIMIT),
        name="rel_bias",
    )(table_t, bucket)


def _swa_kernel(sink_ref, q_ref, kc_ref, kp_ref, vc_ref, vp_ref, bias_ref, o_ref):
    n = pl.program_id(0)
    w = WINDOW
    hd = HEAD_DIM
    gq = N_HEADS // N_KV_HEADS
    qi = lax.broadcasted_iota(jnp.int32, (w, 2 * w), 0)
    ki = lax.broadcasted_iota(jnp.int32, (w, 2 * w), 1)
    dist = qi + w - ki
    first_key = jnp.where(n > 0, 0, w)
    valid = (dist >= 0) & (dist < w) & (ki >= first_key)
    zeros = jnp.zeros((2 * w, hd), BF16)
    scale = HEAD_DIM ** -0.5
    for g in range(N_KV_HEADS):
        cs = slice(g * hd, (g + 1) * hd)
        k_g = jnp.concatenate([kp_ref[:, cs], kc_ref[:, cs]], axis=0)
        v_g = jnp.concatenate([vp_ref[:, cs], vc_ref[:, cs]], axis=0)
        k_lo = jnp.concatenate([k_g, zeros], axis=1)
        k_hi = jnp.concatenate([zeros, k_g], axis=1)
        v_lo = jnp.concatenate([v_g, zeros], axis=1)
        v_hi = jnp.concatenate([zeros, v_g], axis=1)
        for j in range(gq // 2):
            head = g * gq + 2 * j
            q_pair = q_ref[:, head * hd:(head + 2) * hd]
            acc = None
            for half, (k_pl, v_pl) in enumerate(((k_lo, v_lo), (k_hi, v_hi))):
                hh = head + half
                s = _dot_nt(q_pair, k_pl) * scale + bias_ref[hh]
                s = jnp.where(valid, s, NEG_INF)
                sink = sink_ref[hh]
                m = jnp.maximum(jnp.max(s, axis=-1, keepdims=True), sink)
                p = jnp.exp(s - m)
                denom = jnp.sum(p, axis=-1, keepdims=True) + jnp.exp(sink - m)
                probs = (p * (1.0 / denom)).astype(BF16)
                part = _dot(probs, v_pl)
                acc = part if acc is None else acc + part
            o_ref[:, head * hd:(head + 2) * hd] = acc.astype(o_ref.dtype)


def _swa(proj, sinks, bias):
    s = proj.shape[0]
    w = WINDOW
    nb = s // w
    kvw = N_KV_HEADS * HEAD_DIM
    dq = N_HEADS * HEAD_DIM
    prev = lambda n: jnp.maximum(n - 1, 0)
    return pl.pallas_call(
        _swa_kernel,
        grid=(nb,),
        in_specs=[
            pl.BlockSpec(memory_space=pltpu.SMEM),
            pl.BlockSpec((w, dq), lambda n: (n, COL_AQ // dq)),
            pl.BlockSpec((w, kvw), lambda n: (n, COL_AK // kvw)),
            pl.BlockSpec((w, kvw), lambda n: (prev(n), COL_AK // kvw)),
            pl.BlockSpec((w, kvw), lambda n: (n, COL_AV // kvw)),
            pl.BlockSpec((w, kvw), lambda n: (prev(n), COL_AV // kvw)),
            pl.BlockSpec((N_HEADS, w, 2 * w), lambda n: (0, 0, 0)),
        ],
        out_specs=pl.BlockSpec((w, dq), lambda n: (n, 0)),
        out_shape=jax.ShapeDtypeStruct((s, dq), BF16),
        compiler_params=_cparams(("parallel",)),
        name="swa",
    )(sinks, proj, proj, proj, proj, proj, bias)


def _gla_kernel(q_ref, k_ref, v_ref, gr_ref, la_ref, gn_ref, o_ref, state_ref):
    c = GLA_CHUNK
    rows = q_ref.shape[0]
    dk = q_ref.shape[1]
    dv = v_ref.shape[1]

    @pl.when(pl.program_id(1) == 0)
    def _():
        state_ref[...] = jnp.zeros_like(state_ref)

    la = la_ref[...]
    la_parts = _split3(la)
    ri = lax.broadcasted_iota(jnp.int32, (rows, rows), 0)
    ci = lax.broadcasted_iota(jnp.int32, (rows, rows), 1)
    tri = ((ri // c == ci // c) & (ri >= ci)).astype(BF16)
    cum = _dot(tri, la_parts[0]) + _dot(tri, la_parts[1]) + _dot(tri, la_parts[2])
    ones = jnp.ones((c, LANES), BF16)
    causal = (lax.broadcasted_iota(jnp.int32, (c, c), 0) >= lax.broadcasted_iota(jnp.int32, (c, c), 1))
    gn = gn_ref[...]
    for ch in range(rows // c):
        sl = slice(ch * c, (ch + 1) * c)
        cum_c = cum[sl]
        ref = cum_c[c // 2 - 1:c // 2]
        last = cum_c[c - 1:c]
        q = q_ref[sl, :].astype(F32) * (dk ** -0.5)
        k = k_ref[sl, :].astype(F32)
        v = v_ref[sl, :]
        q_intra = (q * jnp.exp(cum_c - ref)).astype(BF16)
        k_intra = (k * jnp.exp(ref - cum_c)).astype(BF16)
        a = jnp.where(causal, _dot_nt(q_intra, k_intra), 0.0)
        o = _dot(a.astype(BF16), v)
        state = state_ref[...]
        o = o + _dot((q * jnp.exp(cum_c)).astype(BF16), state.astype(BF16))
        k_dec = (k * jnp.exp(last - cum_c)).astype(BF16)
        upd = _dot_tn(k_dec, v)
        last_col = (_dot_tn(la_parts[0][sl], ones) + _dot_tn(la_parts[1][sl], ones)
                    + _dot_tn(la_parts[2][sl], ones))
        decay = jnp.exp(last_col)
        for jv in range(dv // LANES):
            ls = slice(jv * LANES, (jv + 1) * LANES)
            state_ref[:, ls] = state[:, ls] * decay + upd[:, ls]
        ms = jnp.mean(o * o, axis=-1, keepdims=True)
        on = o * lax.rsqrt(ms + RMS_EPS) * gn
        gr = gr_ref[sl, :].astype(F32)
        o_ref[sl, :] = (on * (gr * _sigmoid(gr))).astype(o_ref.dtype)


def _gla(proj, la, gn, rows):
    s = proj.shape[0]
    dk, dv = GLA_DK, GLA_DV
    return pl.pallas_call(
        _gla_kernel,
        grid=(GLA_HEADS, s // rows),
        in_specs=[
            pl.BlockSpec((rows, dk), lambda h, i: (i, COL_GQ // dk + h)),
            pl.BlockSpec((rows, dk), lambda h, i: (i, COL_GK // dk + h)),
            pl.BlockSpec((rows, dv), lambda h, i: (i, COL_GV // dv + h)),
            pl.BlockSpec((rows, dv), lambda h, i: (i, COL_GR // dv + h)),
            pl.BlockSpec((rows, dk), lambda h, i: (i, h)),
            pl.BlockSpec((1, dv), lambda h, i: (0, 0)),
        ],
        out_specs=pl.BlockSpec((rows, dv), lambda h, i: (i, h)),
        out_shape=jax.ShapeDtypeStruct((s, GLA_HEADS * dv), BF16),
        scratch_shapes=[pltpu.VMEM((dk, dv), F32)],
        compiler_params=_cparams(("parallel", "arbitrary")),
        name="gla",
    )(proj, proj, proj, proj, la, gn)


def _merge_out_kernel(oa_ref, og_ref, ga_ref, gg_ref, x_ref, wo_ref, g_ref, wr_ref,
                      x1_ref, h2_ref, lt_ref):
    merged = (_sigmoid(ga_ref[...].astype(F32)) * oa_ref[...].astype(F32)
              + _sigmoid(gg_ref[...].astype(F32)) * og_ref[...].astype(F32))
    x1 = x_ref[...] + _dot(merged.astype(BF16), wo_ref[...])
    x1_ref[...] = x1
    ms = jnp.mean(x1 * x1, axis=-1, keepdims=True)
    h2 = x1 * lax.rsqrt(ms + RMS_EPS) * g_ref[...]
    h2_ref[...] = h2
    lt_ref[...] = _dot_nt(wr_ref[...], h2.astype(BF16))


def _merge_out(o_attn, o_gla, gates, x, wo, g, wr, tm):
    s, d = x.shape
    nr = wr.shape[0]
    row = lambda i: (i, 0)
    return pl.pallas_call(
        _merge_out_kernel,
        grid=(s // tm,),
        in_specs=[
            pl.BlockSpec((tm, d), row),
            pl.BlockSpec((tm, d), row),
            pl.BlockSpec((tm, d), lambda i: (i, 0)),
            pl.BlockSpec((tm, d), lambda i: (i, 1)),
            pl.BlockSpec((tm, d), row),
            pl.BlockSpec((d, d), lambda i: (0, 0)),
            pl.BlockSpec((1, d), lambda i: (0, 0)),
            pl.BlockSpec((nr, d), lambda i: (0, 0)),
        ],
        out_specs=[
            pl.BlockSpec((tm, d), row),
            pl.BlockSpec((tm, d), row),
            pl.BlockSpec((nr, tm), lambda i: (0, i)),
        ],
        out_shape=[
            jax.ShapeDtypeStruct((s, d), F32),
            jax.ShapeDtypeStruct((s, d), F32),
            jax.ShapeDtypeStruct((nr, s), F32),
        ],
        compiler_params=_cparams(("parallel",)),
        name="merge_out",
    )(o_attn, o_gla, gates, gates, x, wo, g, wr)


ROUTER_ROWS = 8 + N_EXPERTS


def _route_kernel(lt_ref, bg_ref, be_ref, idx_ref, gate_ref, cnt_ref, carry_ref):
    tb = lt_ref.shape[1]
    epg = EXPERTS_PER_GROUP

    @pl.when(pl.program_id(0) == 0)
    def _():
        carry_ref[...] = jnp.zeros_like(carry_ref)

    logits = lt_ref[...]
    gl = logits[0:N_GROUPS]
    g_exp = jnp.exp(gl - jnp.max(gl, axis=0, keepdims=True))
    g_prob = g_exp / jnp.sum(g_exp, axis=0, keepdims=True)
    gb = gl + bg_ref[0:N_GROUPS]
    rowg = lax.broadcasted_iota(jnp.int32, (N_GROUPS, tb), 0)
    g_idx = jnp.min(jnp.where(gb == jnp.max(gb, axis=0, keepdims=True), rowg, N_GROUPS),
                    axis=0, keepdims=True)
    p_group = jnp.sum(jnp.where(rowg == g_idx, g_prob, 0.0), axis=0, keepdims=True)

    el = logits[8:8 + epg]
    eb = jnp.broadcast_to(be_ref[0:epg], (epg, tb))
    for g in range(1, N_GROUPS):
        pick = g_idx == g
        el = jnp.where(pick, logits[8 + g * epg:8 + (g + 1) * epg], el)
        eb = jnp.where(pick, be_ref[g * epg:(g + 1) * epg], eb)
    e_exp = jnp.exp(el - jnp.max(el, axis=0, keepdims=True))
    e_prob = e_exp / jnp.sum(e_exp, axis=0, keepdims=True)
    score = el + eb
    rowe = lax.broadcasted_iota(jnp.int32, (epg, tb), 0)
    i1 = jnp.min(jnp.where(score == jnp.max(score, axis=0, keepdims=True), rowe, epg),
                 axis=0, keepdims=True)
    score2 = jnp.where(rowe == i1, -jnp.inf, score)
    i2 = jnp.min(jnp.where(score2 == jnp.max(score2, axis=0, keepdims=True), rowe, epg),
                 axis=0, keepdims=True)
    q1 = jnp.sum(jnp.where(rowe == i1, e_prob, 0.0), axis=0, keepdims=True)
    q2 = jnp.sum(jnp.where(rowe == i2, e_prob, 0.0), axis=0, keepdims=True)
    qs = q1 + q2
    gate1 = p_group * q1 / qs
    gate2 = p_group * q2 / qs
    e1 = g_idx * epg + i1
    e2 = g_idx * epg + i2

    rowx = lax.broadcasted_iota(jnp.int32, (N_EXPERTS, tb), 0)
    hit1 = rowx == e1
    hit2 = rowx == e2
    member = (hit1 | hit2).astype(BF16)
    before = (lax.broadcasted_iota(jnp.int32, (tb, tb), 0)
              < lax.broadcasted_iota(jnp.int32, (tb, tb), 1)).astype(BF16)
    carry = carry_ref[...]
    count = _dot(member, before) + carry[:, 0:1]
    r1 = jnp.sum(jnp.where(hit1, count, 0.0), axis=0, keepdims=True).astype(jnp.int32)
    r2 = jnp.sum(jnp.where(hit2, count, 0.0), axis=0, keepdims=True).astype(jnp.int32)
    carry = carry + jnp.sum(member.astype(F32), axis=1, keepdims=True)
    carry_ref[...] = carry
    cnt_ref[...] = carry

    row8 = lax.broadcasted_iota(jnp.int32, (8, tb), 0)
    idx_ref[...] = jnp.where(row8 == 0, e1, jnp.where(row8 == 1, e2,
                             jnp.where(row8 == 2, r1, jnp.where(row8 == 3, r2, 0))))
    gate_ref[...] = jnp.where(row8 == 0, gate1, jnp.where(row8 == 1, gate2, 0.0))


def _route(lt, bg, be, tb):
    nr, s = lt.shape
    return pl.pallas_call(
        _route_kernel,
        grid=(s // tb,),
        in_specs=[
            pl.BlockSpec((nr, tb), lambda i: (0, i)),
            pl.BlockSpec((8, 1), lambda i: (0, 0)),
            pl.BlockSpec((N_EXPERTS, 1), lambda i: (0, 0)),
        ],
        out_specs=[
            pl.BlockSpec((8, tb), lambda i: (0, i)),
            pl.BlockSpec((8, tb), lambda i: (0, i)),
            pl.BlockSpec((N_EXPERTS, LANES), lambda i: (0, 0)),
        ],
        out_shape=[
            jax.ShapeDtypeStruct((8, s), jnp.int32),
            jax.ShapeDtypeStruct((8, s), F32),
            jax.ShapeDtypeStruct((N_EXPERTS, LANES), F32),
        ],
        scratch_shapes=[pltpu.VMEM((N_EXPERTS, LANES), F32)],
        compiler_params=_cparams(("arbitrary",)),
        name="route",
    )(lt, bg, be)


def _expert_kernel(be_ref, nu_ref, tok_ref, h2_hbm, wg_ref, wu_ref, wd_ref, y_ref,
                   xbuf, sem, wgb, wub, wdb):
    b = pl.program_id(0)
    n_used = nu_ref[0]
    rows = MOE_BLOCK
    slot = b % 2

    def row_copy(blk, r, s_):
        t = tok_ref[blk * rows + r]
        return pltpu.make_async_copy(h2_hbm.at[pl.ds(t, 1)], xbuf.at[s_, pl.ds(r, 1)], sem.at[s_])

    def issue(blk, s_):
        def body(r, carry):
            row_copy(blk, r, s_).start()
            return carry
        lax.fori_loop(0, rows, body, 0)

    def wait_all(blk, s_):
        def body(r, carry):
            row_copy(blk, r, s_).wait()
            return carry
        lax.fori_loop(0, rows, body, 0)

    @pl.when((b == 0) & (n_used > 0))
    def _():
        issue(0, 0)

    @pl.when(b + 1 < n_used)
    def _():
        issue(b + 1, 1 - slot)

    @pl.when(b < n_used)
    def _():
        wait_all(b, slot)
        changed = (b == 0) | (be_ref[b] != be_ref[jnp.maximum(b - 1, 0)])

        @pl.when(changed)
        def _():
            wgb[...] = wg_ref[...].astype(BF16)
            wub[...] = wu_ref[...].astype(BF16)
            wdb[...] = wd_ref[...].astype(BF16)

        xb = xbuf[slot].astype(BF16)
        hg = _dot(xb, wgb[...])
        hu = _dot(xb, wub[...])
        act = (hg * _sigmoid(hg)) * hu
        y_ref[...] = _dot(act.astype(BF16), wdb[...])

    @pl.when(b >= n_used)
    def _():
        y_ref[...] = jnp.zeros_like(y_ref)


def _experts(block_e, n_used, row_token, h2, wg, wu, wd):
    n_rows = row_token.shape[0]
    n_blocks = n_rows // MOE_BLOCK
    d = h2.shape[1]
    de = wg.shape[2]
    grid_spec = pltpu.PrefetchScalarGridSpec(
        num_scalar_prefetch=3,
        grid=(n_blocks,),
        in_specs=[
            pl.BlockSpec(memory_space=pl.ANY),
            pl.BlockSpec((None, d, de), lambda b, be, nu, tok: (be[b], 0, 0)),
            pl.BlockSpec((None, d, de), lambda b, be, nu, tok: (be[b], 0, 0)),
            pl.BlockSpec((None, de, d), lambda b, be, nu, tok: (be[b], 0, 0)),
        ],
        out_specs=pl.BlockSpec((MOE_BLOCK, d), lambda b, be, nu, tok: (b, 0)),
        scratch_shapes=[
            pltpu.VMEM((2, MOE_BLOCK, d), F32),
            pltpu.SemaphoreType.DMA((2,)),
            pltpu.VMEM((d, de), BF16),
            pltpu.VMEM((d, de), BF16),
            pltpu.VMEM((de, d), BF16),
        ],
    )
    return pl.pallas_call(
        _expert_kernel,
        grid_spec=grid_spec,
        out_shape=jax.ShapeDtypeStruct((n_rows, d), F32),
        compiler_params=_cparams(("arbitrary",)),
        name="experts",
    )(block_e, n_used, row_token, h2, wg, wu, wd)


def _combine_kernel(dest_ref, yb_hbm, gate_ref, x1_ref, g_ref, o_ref, ybuf, sem):
    i = pl.program_id(0)
    n_steps = pl.num_programs(0)
    tm = x1_ref.shape[0]
    n_tok = n_steps * tm
    slot = i % 2

    def row_copy(step, k, r, s_):
        row = dest_ref[k * n_tok + step * tm + r]
        return pltpu.make_async_copy(yb_hbm.at[pl.ds(row, 1)], ybuf.at[s_, k, pl.ds(r, 1)], sem.at[s_])

    def issue(step, s_):
        def body(r, carry):
            row_copy(step, 0, r, s_).start()
            row_copy(step, 1, r, s_).start()
            return carry
        lax.fori_loop(0, tm, body, 0)

    def wait_all(step, s_):
        def body(r, carry):
            row_copy(step, 0, r, s_).wait()
            row_copy(step, 1, r, s_).wait()
            return carry
        lax.fori_loop(0, tm, body, 0)

    @pl.when(i == 0)
    def _():
        issue(0, 0)

    @pl.when(i + 1 < n_steps)
    def _():
        issue(i + 1, 1 - slot)

    wait_all(i, slot)
    gates = gate_ref[...]
    y = ybuf[slot, 0] * gates[:, 0:1] + ybuf[slot, 1] * gates[:, 1:2]
    x2 = x1_ref[...] + y
    ms = jnp.mean(x2 * x2, axis=-1, keepdims=True)
    o_ref[...] = x2 * lax.rsqrt(ms + RMS_EPS) * g_ref[...]


def _combine(dest, yb, gates, x1, g, tm):
    s, d = x1.shape
    grid_spec = pltpu.PrefetchScalarGridSpec(
        num_scalar_prefetch=1,
        grid=(s // tm,),
        in_specs=[
            pl.BlockSpec(memory_space=pl.ANY),
            pl.BlockSpec((tm, TOP_K), lambda i, dest: (i, 0)),
            pl.BlockSpec((tm, d), lambda i, dest: (i, 0)),
            pl.BlockSpec((1, d), lambda i, dest: (0, 0)),
        ],
        out_specs=pl.BlockSpec((tm, d), lambda i, dest: (i, 0)),
        scratch_shapes=[
            pltpu.VMEM((2, TOP_K, tm, d), F32),
            pltpu.SemaphoreType.DMA((2,)),
        ],
    )
    return pl.pallas_call(
        _combine_kernel,
        grid_spec=grid_spec,
        out_shape=jax.ShapeDtypeStruct((s, d), F32),
        compiler_params=_cparams(("arbitrary",)),
        name="combine",
    )(dest, yb, gates, x1, g)


def _tile(n, pref):
    return pref if n % pref == 0 else n


def kernel(x, norm_mix_g, w_in, w_gla_a2, b_gla_a, gla_norm_g, attn_sinks, rel_bias_table, w_out,
           norm_ffn_g, w_router_group, b_router_group, w_router_expert, b_router_expert,
           w_expert_gate, w_expert_up, w_expert_down, norm_final_g):
    b, s, d = x.shape
    assert b == 1 and w_in.shape[0] == 1, "single batch, single layer"
    x2d = x.reshape(s, d)
    w_in0 = w_in[0]

    wga = jnp.pad(w_in0[:, COL_GA:COL_GA + GLA_LOWRANK], ((0, 0), (0, LANES - GLA_LOWRANK))).astype(BF16)
    w2 = jnp.pad(w_gla_a2[0], ((0, LANES - GLA_LOWRANK), (0, 0))).astype(BF16)
    h, log_a = _norm_loga(x2d, norm_mix_g[0].reshape(1, d), wga, w2, b_gla_a[0].reshape(1, -1),
                          _tile(s, 512))
    proj = _in_proj(h, w_in0, MAIN_COLS, _tile(s, 1024), 512)
    w_gates = w_in0[:, COL_GATES:].astype(BF16)
    gates = _in_proj(h, w_gates, w_gates.shape[1], _tile(s, 1024), 512)

    bias = _rel_bias(rel_bias_table.T, jnp.asarray(_bucket_map()))
    bias = bias.reshape(N_HEADS, WINDOW, 2 * WINDOW)
    o_attn = _swa(proj, attn_sinks[0], bias)
    o_gla = _gla(proj, log_a, gla_norm_g[0].reshape(1, -1), _tile(s, 256))

    wr = jnp.zeros((ROUTER_ROWS, d), F32)
    wr = wr.at[0:N_GROUPS].set(w_router_group[0].T).at[8:].set(w_router_expert[0].T).astype(BF16)
    x1, h2, logits_t = _merge_out(o_attn, o_gla, gates, x2d, w_out[0].astype(BF16),
                                  norm_ffn_g[0].reshape(1, d), wr, _tile(s, 256))

    bg = jnp.pad(b_router_group[0], (0, 8 - N_GROUPS)).reshape(8, 1)
    idx, gate, counts = _route(logits_t, bg, b_router_expert[0].reshape(N_EXPERTS, 1), _tile(s, 512))
    counts = counts[:, 0].astype(jnp.int32)
    padded = (counts + MOE_BLOCK - 1) // MOE_BLOCK * MOE_BLOCK
    pend = jnp.cumsum(padded)
    pstart = pend - padded
    dest = pstart[idx[0:TOP_K]] + idx[TOP_K:2 * TOP_K]
    n_pairs = s * TOP_K
    n_blocks = (n_pairs + N_EXPERTS * (MOE_BLOCK - 1) + MOE_BLOCK - 1) // MOE_BLOCK
    n_used = (pend[-1] // MOE_BLOCK).reshape(1)
    block_e = jnp.minimum(jnp.searchsorted(pend, jnp.arange(n_blocks) * MOE_BLOCK, side='right'),
                          N_EXPERTS - 1).astype(jnp.int32)
    dest_flat = dest.reshape(-1)
    tok_of_pair = jnp.tile(jnp.arange(s, dtype=jnp.int32), TOP_K)
    row_token = jnp.zeros((n_blocks * MOE_BLOCK,), jnp.int32).at[dest_flat].set(tok_of_pair)

    yb = _experts(block_e, n_used, row_token, h2, w_expert_gate[0], w_expert_up[0], w_expert_down[0])
    out = _combine(dest_flat, yb, gate[0:TOP_K].T, x1, norm_final_g.reshape(1, d), _tile(s, 128))
    return out.reshape(b, s, d)
```

```python
import functools
import math

import numpy as np
import jax
import jax.numpy as jnp
from jax import lax
from jax.experimental import pallas as pl
from jax.experimental.pallas import tpu as pltpu

F32 = jnp.float32
BF16 = jnp.bfloat16

N_HEADS = 32
N_KV_HEADS = 4
HEAD_DIM = 64
WINDOW = 128
N_BUCKETS = 32
MAX_DISTANCE = 128
GLA_HEADS = 4
GLA_DK = 256
GLA_DV = 512
GLA_LOWRANK = 16
GLA_TAU = 16.0
GLA_CHUNK = 64
N_GROUPS = 4
EXPERTS_PER_GROUP = 8
N_EXPERTS = 32
TOP_K = 2
MOE_BLOCK = 128
RMS_EPS = 1e-6
NEG_INF = -1e30

COL_GA = 8704
COL_GATES = 8720
MAIN_COLS = 8704
GATE_COLS = 4096
COL_AQ = GATE_COLS + 0
COL_AK = GATE_COLS + 2048
COL_AV = GATE_COLS + 2304
COL_GQ = GATE_COLS + 2560
COL_GK = GATE_COLS + 3584
COL_GV = GATE_COLS + 4608
COL_GR = GATE_COLS + 6656

LANES = 128
VMEM_LIMIT = 56 * 1024 * 1024


def _cparams(sem, vmem=VMEM_LIMIT):
    return pltpu.CompilerParams(dimension_semantics=sem, vmem_limit_bytes=vmem)


def _split3(x):
    hi = x.astype(BF16)
    r1 = x - hi.astype(F32)
    mid = r1.astype(BF16)
    lo = (r1 - mid.astype(F32)).astype(BF16)
    return hi, mid, lo


def _sigmoid(x):
    return 1.0 / (1.0 + jnp.exp(-x))


def _dot(a, b):
    return jnp.dot(a, b, preferred_element_type=F32)


def _dot_nt(a, b):
    return lax.dot_general(a, b, (((1,), (1,)), ((), ())), preferred_element_type=F32)


def _dot_tn(a, b):
    return lax.dot_general(a, b, (((0,), (0,)), ((), ())), preferred_element_type=F32)


def _norm_loga_kernel(x_ref, g_ref, wga_ref, w2_ref, b_ref, h_ref, la_ref):
    x = x_ref[...]
    ms = jnp.mean(x * x, axis=-1, keepdims=True)
    hb = (x * lax.rsqrt(ms + RMS_EPS) * g_ref[...]).astype(BF16)
    h_ref[...] = hb
    lane = lax.broadcasted_iota(jnp.int32, wga_ref.shape, 1)
    wga = jnp.where(lane < GLA_LOWRANK, wga_ref[...], 0.0).astype(BF16)
    ga = _dot(hb, wga)
    z = _dot(ga.astype(BF16), w2_ref[...]) + b_ref[...]
    la_ref[...] = (jnp.minimum(z, 0.0) - jnp.log1p(jnp.exp(-jnp.abs(z)))) * (1.0 / GLA_TAU)


def _norm_loga(x, g, wga, w2, b, tm):
    s, d = x.shape
    nq = w2.shape[1]
    return pl.pallas_call(
        _norm_loga_kernel,
        grid=(s // tm,),
        in_specs=[
            pl.BlockSpec((tm, d), lambda i: (i, 0)),
            pl.BlockSpec((1, d), lambda i: (0, 0)),
            pl.BlockSpec((d, LANES), lambda i: (0, COL_GA // LANES)),
            pl.BlockSpec((LANES, nq), lambda i: (0, 0)),
            pl.BlockSpec((1, nq), lambda i: (0, 0)),
        ],
        out_specs=[
            pl.BlockSpec((tm, d), lambda i: (i, 0)),
            pl.BlockSpec((tm, nq), lambda i: (i, 0)),
        ],
        out_shape=[
            jax.ShapeDtypeStruct((s, d), BF16),
            jax.ShapeDtypeStruct((s, nq), F32),
        ],
        compiler_params=_cparams(("parallel",)),
        name="norm_loga",
    )(x, g, wga, w2, b)


def _in_proj_kernel(h_ref, wa_ref, wb_ref, o_ref, wbf_ref, *, n_main_tiles, shift):
    j = pl.program_id(0)

    @pl.when((pl.program_id(1) == 0) & (j < n_main_tiles))
    def _():
        wbf_ref[...] = wa_ref[...].astype(BF16)

    @pl.when((pl.program_id(1) == 0) & (j >= n_main_tiles))
    def _():
        w = jnp.concatenate([wa_ref[:, shift:], wb_ref[:, :shift]], axis=1)
        wbf_ref[...] = w.astype(BF16)

    o_ref[...] = _dot(h_ref[...], wbf_ref[...]).astype(o_ref.dtype)


def _in_proj(h, w, tm, tn):
    s, d = h.shape
    n_main = MAIN_COLS // tn
    n_gate = GATE_COLS // tn
    n_tiles = n_main + n_gate
    sub = tn // LANES
    shift = COL_GATES - MAIN_COLS
    kern = functools.partial(_in_proj_kernel, n_main_tiles=n_main, shift=shift)
    return pl.pallas_call(
        kern,
        grid=(n_tiles, s // tm),
        in_specs=[
            pl.BlockSpec((tm, d), lambda j, i: (i, 0)),
            pl.BlockSpec((d, tn), lambda j, i: (0, j)),
            pl.BlockSpec((d, LANES), lambda j, i: (0, jnp.where(j >= n_main, (j + 1) * sub, 0))),
        ],
        out_specs=pl.BlockSpec((tm, tn), lambda j, i: (i, (j + n_gate) % n_tiles)),
        out_shape=jax.ShapeDtypeStruct((s, n_tiles * tn), BF16),
        scratch_shapes=[pltpu.VMEM((d, tn), BF16)],
        compiler_params=_cparams(("parallel", "arbitrary")),
        name="in_proj",
    )(h, w, w)


def _bucket_map():
    q_loc = np.arange(WINDOW)[:, None]
    k_loc = np.arange(2 * WINDOW)[None, :]
    n = np.maximum(q_loc + WINDOW - k_loc, 0)
    max_exact = N_BUCKETS // 2
    ratio = np.maximum(n, max_exact).astype(np.float32) / np.float32(max_exact)
    large = max_exact + (np.log(ratio) / np.float32(math.log(MAX_DISTANCE / max_exact))
                         * (N_BUCKETS - max_exact)).astype(np.int32)
    large = np.minimum(large, N_BUCKETS - 1)
    return np.where(n < max_exact, n, large).astype(np.int32).reshape(1, -1)


def _rel_bias_kernel(tab_ref, bucket_ref, o_ref):
    nb = tab_ref.shape[1]
    width = bucket_ref.shape[1]
    onehot = (lax.broadcasted_iota(jnp.int32, (nb, width), 0) == bucket_ref[...]).astype(BF16)
    hi, mid, lo = _split3(tab_ref[...])
    o_ref[...] = _dot(hi, onehot) + _dot(mid, onehot) + _dot(lo, onehot)


def _rel_bias(table_t, bucket):
    nh = table_t.shape[0]
    width = bucket.shape[1]
    return pl.pallas_call(
        _rel_bias_kernel,
        out_shape=jax.ShapeDtypeStruct((nh, width), F32),
        compiler_params=pltpu.CompilerParams(vmem_limit_bytes=VMEM_LIMIT),
        name="rel_bias",
    )(table_t, bucket)


def _swa_kernel(sink_ref, q_ref, kc_ref, kp_ref, vc_ref, vp_ref, bias_ref, o_ref):
    n = pl.program_id(0)
    w = WINDOW
    hd = HEAD_DIM
    gq = N_HEADS // N_KV_HEADS
    qi = lax.broadcasted_iota(jnp.int32, (w, 2 * w), 0)
    ki = lax.broadcasted_iota(jnp.int32, (w, 2 * w), 1)
    dist = qi + w - ki
    first_key = jnp.where(n > 0, 0, w)
    valid = (dist >= 0) & (dist < w) & (ki >= first_key)
    zeros = jnp.zeros((2 * w, hd), BF16)
    scale = HEAD_DIM ** -0.5
    for g in range(N_KV_HEADS):
        cs = slice(g * hd, (g + 1) * hd)
        k_g = jnp.concatenate([kp_ref[:, cs], kc_ref[:, cs]], axis=0)
        v_g = jnp.concatenate([vp_ref[:, cs], vc_ref[:, cs]], axis=0)
        k_lo = jnp.concatenate([k_g, zeros], axis=1)
        k_hi = jnp.concatenate([zeros, k_g], axis=1)
        v_lo = jnp.concatenate([v_g, zeros], axis=1)
        v_hi = jnp.concatenate([zeros, v_g], axis=1)
        for j in range(gq // 2):
            head = g * gq + 2 * j
            q_pair = q_ref[:, head * hd:(head + 2) * hd]
            acc = None
            for half, (k_pl, v_pl) in enumerate(((k_lo, v_lo), (k_hi, v_hi))):
                hh = head + half
                s = _dot_nt(q_pair, k_pl) * scale + bias_ref[hh]
                s = jnp.where(valid, s, NEG_INF)
                sink = sink_ref[hh]
                m = jnp.maximum(jnp.max(s, axis=-1, keepdims=True), sink)
                p = jnp.exp(s - m)
                denom = jnp.sum(p, axis=-1, keepdims=True) + jnp.exp(sink - m)
                probs = (p * (1.0 / denom)).astype(BF16)
                part = _dot(probs, v_pl)
                acc = part if acc is None else acc + part
            o_ref[:, head * hd:(head + 2) * hd] = acc.astype(o_ref.dtype)


def _swa(proj, sinks, bias):
    s = proj.shape[0]
    w = WINDOW
    nb = s // w
    kvw = N_KV_HEADS * HEAD_DIM
    dq = N_HEADS * HEAD_DIM
    prev = lambda n: jnp.maximum(n - 1, 0)
    return pl.pallas_call(
        _swa_kernel,
        grid=(nb,),
        in_specs=[
            pl.BlockSpec(memory_space=pltpu.SMEM),
            pl.BlockSpec((w, dq), lambda n: (n, COL_AQ // dq)),
            pl.BlockSpec((w, kvw), lambda n: (n, COL_AK // kvw)),
            pl.BlockSpec((w, kvw), lambda n: (prev(n), COL_AK // kvw)),
            pl.BlockSpec((w, kvw), lambda n: (n, COL_AV // kvw)),
            pl.BlockSpec((w, kvw), lambda n: (prev(n), COL_AV // kvw)),
            pl.BlockSpec((N_HEADS, w, 2 * w), lambda n: (0, 0, 0)),
        ],
        out_specs=pl.BlockSpec((w, dq), lambda n: (n, 0)),
        out_shape=jax.ShapeDtypeStruct((s, dq), BF16),
        compiler_params=_cparams(("parallel",)),
        name="swa",
    )(sinks, proj, proj, proj, proj, proj, bias)


def _gla_kernel(q_ref, k_ref, v_ref, gr_ref, la_ref, gn_ref, o_ref, state_ref):
    c = GLA_CHUNK
    rows = q_ref.shape[0]
    dk = q_ref.shape[1]
    dv = v_ref.shape[1]

    @pl.when(pl.program_id(1) == 0)
    def _():
        state_ref[...] = jnp.zeros_like(state_ref)

    la = la_ref[...]
    la_parts = _split3(la)
    ri = lax.broadcasted_iota(jnp.int32, (rows, rows), 0)
    ci = lax.broadcasted_iota(jnp.int32, (rows, rows), 1)
    tri = ((ri // c == ci // c) & (ri >= ci)).astype(BF16)
    cum = _dot(tri, la_parts[0]) + _dot(tri, la_parts[1]) + _dot(tri, la_parts[2])
    ones = jnp.ones((c, LANES), BF16)
    causal = (lax.broadcasted_iota(jnp.int32, (c, c), 0) >= lax.broadcasted_iota(jnp.int32, (c, c), 1))
    gn = gn_ref[...]
    for ch in range(rows // c):
        sl = slice(ch * c, (ch + 1) * c)
        cum_c = cum[sl]
        ref = cum_c[c // 2 - 1:c // 2]
        last = cum_c[c - 1:c]
        q = q_ref[sl, :].astype(F32) * (dk ** -0.5)
        k = k_ref[sl, :].astype(F32)
        v = v_ref[sl, :]
        q_intra = (q * jnp.exp(cum_c - ref)).astype(BF16)
        k_intra = (k * jnp.exp(ref - cum_c)).astype(BF16)
        a = jnp.where(causal, _dot_nt(q_intra, k_intra), 0.0)
        o = _dot(a.astype(BF16), v)
        state = state_ref[...]
        o = o + _dot((q * jnp.exp(cum_c)).astype(BF16), state.astype(BF16))
        k_dec = (k * jnp.exp(last - cum_c)).astype(BF16)
        upd = _dot_tn(k_dec, v)
        last_col = (_dot_tn(la_parts[0][sl], ones) + _dot_tn(la_parts[1][sl], ones)
                    + _dot_tn(la_parts[2][sl], ones))
        decay = jnp.exp(last_col)
        for jv in range(dv // LANES):
            ls = slice(jv * LANES, (jv + 1) * LANES)
            state_ref[:, ls] = state[:, ls] * decay + upd[:, ls]
        ms = jnp.mean(o * o, axis=-1, keepdims=True)
        on = o * lax.rsqrt(ms + RMS_EPS) * gn
        gr = gr_ref[sl, :].astype(F32)
        o_ref[sl, :] = (on * (gr * _sigmoid(gr))).astype(o_ref.dtype)


def _gla(proj, la, gn, rows):
    s = proj.shape[0]
    dk, dv = GLA_DK, GLA_DV
    return pl.pallas_call(
        _gla_kernel,
        grid=(GLA_HEADS, s // rows),
        in_specs=[
            pl.BlockSpec((rows, dk), lambda h, i: (i, COL_GQ // dk + h)),
            pl.BlockSpec((rows, dk), lambda h, i: (i, COL_GK // dk + h)),
            pl.BlockSpec((rows, dv), lambda h, i: (i, COL_GV // dv + h)),
            pl.BlockSpec((rows, dv), lambda h, i: (i, COL_GR // dv + h)),
            pl.BlockSpec((rows, dk), lambda h, i: (i, h)),
            pl.BlockSpec((1, dv), lambda h, i: (0, 0)),
        ],
        out_specs=pl.BlockSpec((rows, dv), lambda h, i: (i, h)),
        out_shape=jax.ShapeDtypeStruct((s, GLA_HEADS * dv), BF16),
        scratch_shapes=[pltpu.VMEM((dk, dv), F32)],
        compiler_params=_cparams(("parallel", "arbitrary")),
        name="gla",
    )(proj, proj, proj, proj, la, gn)


def _merge_out_kernel(oa_ref, og_ref, ga_ref, gg_ref, x_ref, wo_ref, g_ref, wr_ref,
                      x1_ref, h2_ref, lt_ref):
    merged = (_sigmoid(ga_ref[...].astype(F32)) * oa_ref[...].astype(F32)
              + _sigmoid(gg_ref[...].astype(F32)) * og_ref[...].astype(F32))
    x1 = x_ref[...] + _dot(merged.astype(BF16), wo_ref[...])
    x1_ref[...] = x1
    ms = jnp.mean(x1 * x1, axis=-1, keepdims=True)
    h2 = x1 * lax.rsqrt(ms + RMS_EPS) * g_ref[...]
    h2_ref[...] = h2
    lt_ref[...] = _dot_nt(wr_ref[...], h2.astype(BF16))


def _merge_out(o_attn, o_gla, gates, x, wo, g, wr, tm):
    s, d = x.shape
    nr = wr.shape[0]
    row = lambda i: (i, 0)
    return pl.pallas_call(
        _merge_out_kernel,
        grid=(s // tm,),
        in_specs=[
            pl.BlockSpec((tm, d), row),
            pl.BlockSpec((tm, d), row),
            pl.BlockSpec((tm, d), lambda i: (i, 0)),
            pl.BlockSpec((tm, d), lambda i: (i, 1)),
            pl.BlockSpec((tm, d), row),
            pl.BlockSpec((d, d), lambda i: (0, 0)),
            pl.BlockSpec((1, d), lambda i: (0, 0)),
            pl.BlockSpec((nr, d), lambda i: (0, 0)),
        ],
        out_specs=[
            pl.BlockSpec((tm, d), row),
            pl.BlockSpec((tm, d), row),
            pl.BlockSpec((nr, tm), lambda i: (0, i)),
        ],
        out_shape=[
            jax.ShapeDtypeStruct((s, d), F32),
            jax.ShapeDtypeStruct((s, d), F32),
            jax.ShapeDtypeStruct((nr, s), F32),
        ],
        compiler_params=_cparams(("parallel",)),
        name="merge_out",
    )(o_attn, o_gla, gates, gates, x, wo, g, wr)


ROUTER_ROWS = 8 + N_EXPERTS


def _route_kernel(lt_ref, bg_ref, be_ref, idx_ref, gate_ref, cnt_ref, carry_ref):
    tb = lt_ref.shape[1]
    epg = EXPERTS_PER_GROUP

    @pl.when(pl.program_id(0) == 0)
    def _():
        carry_ref[...] = jnp.zeros_like(carry_ref)

    logits = lt_ref[...]
    gl = logits[0:N_GROUPS]
    g_exp = jnp.exp(gl - jnp.max(gl, axis=0, keepdims=True))
    g_prob = g_exp / jnp.sum(g_exp, axis=0, keepdims=True)
    gb = gl + bg_ref[0:N_GROUPS]
    rowg = lax.broadcasted_iota(jnp.int32, (N_GROUPS, tb), 0)
    g_idx = jnp.min(jnp.where(gb == jnp.max(gb, axis=0, keepdims=True), rowg, N_GROUPS),
                    axis=0, keepdims=True)
    p_group = jnp.sum(jnp.where(rowg == g_idx, g_prob, 0.0), axis=0, keepdims=True)

    el = logits[8:8 + epg]
    eb = jnp.broadcast_to(be_ref[0:epg], (epg, tb))
    for g in range(1, N_GROUPS):
        pick = g_idx == g
        el = jnp.where(pick, logits[8 + g * epg:8 + (g + 1) * epg], el)
        eb = jnp.where(pick, be_ref[g * epg:(g + 1) * epg], eb)
    e_exp = jnp.exp(el - jnp.max(el, axis=0, keepdims=True))
    e_prob = e_exp / jnp.sum(e_exp, axis=0, keepdims=True)
    score = el + eb
    rowe = lax.broadcasted_iota(jnp.int32, (epg, tb), 0)
    i1 = jnp.min(jnp.where(score == jnp.max(score, axis=0, keepdims=True), rowe, epg),
                 axis=0, keepdims=True)
    score2 = jnp.where(rowe == i1, -jnp.inf, score)
    i2 = jnp.min(jnp.where(score2 == jnp.max(score2, axis=0, keepdims=True), rowe, epg),
                 axis=0, keepdims=True)
    q1 = jnp.sum(jnp.where(rowe == i1, e_prob, 0.0), axis=0, keepdims=True)
    q2 = jnp.sum(jnp.where(rowe == i2, e_prob, 0.0), axis=0, keepdims=True)
    qs = q1 + q2
    gate1 = p_group * q1 / qs
    gate2 = p_group * q2 / qs
    e1 = g_idx * epg + i1
    e2 = g_idx * epg + i2

    rowx = lax.broadcasted_iota(jnp.int32, (N_EXPERTS, tb), 0)
    hit1 = rowx == e1
    hit2 = rowx == e2
    member = (hit1 | hit2).astype(BF16)
    before = (lax.broadcasted_iota(jnp.int32, (tb, tb), 0)
              < lax.broadcasted_iota(jnp.int32, (tb, tb), 1)).astype(BF16)
    carry = carry_ref[...]
    count = _dot(member, before) + carry[:, 0:1]
    r1 = jnp.sum(jnp.where(hit1, count, 0.0), axis=0, keepdims=True).astype(jnp.int32)
    r2 = jnp.sum(jnp.where(hit2, count, 0.0), axis=0, keepdims=True).astype(jnp.int32)
    carry = carry + jnp.sum(member.astype(F32), axis=1, keepdims=True)
    carry_ref[...] = carry
    cnt_ref[...] = carry

    row8 = lax.broadcasted_iota(jnp.int32, (8, tb), 0)
    idx_ref[...] = jnp.where(row8 == 0, e1, jnp.where(row8 == 1, e2,
                             jnp.where(row8 == 2, r1, jnp.where(row8 == 3, r2, 0))))
    gate_ref[...] = jnp.where(row8 == 0, gate1, jnp.where(row8 == 1, gate2, 0.0))


def _route(lt, bg, be, tb):
    nr, s = lt.shape
    return pl.pallas_call(
        _route_kernel,
        grid=(s // tb,),
        in_specs=[
            pl.BlockSpec((nr, tb), lambda i: (0, i)),
            pl.BlockSpec((8, 1), lambda i: (0, 0)),
            pl.BlockSpec((N_EXPERTS, 1), lambda i: (0, 0)),
        ],
        out_specs=[
            pl.BlockSpec((8, tb), lambda i: (0, i)),
            pl.BlockSpec((8, tb), lambda i: (0, i)),
            pl.BlockSpec((N_EXPERTS, LANES), lambda i: (0, 0)),
        ],
        out_shape=[
            jax.ShapeDtypeStruct((8, s), jnp.int32),
            jax.ShapeDtypeStruct((8, s), F32),
            jax.ShapeDtypeStruct((N_EXPERTS, LANES), F32),
        ],
        scratch_shapes=[pltpu.VMEM((N_EXPERTS, LANES), F32)],
        compiler_params=_cparams(("arbitrary",)),
        name="route",
    )(lt, bg, be)


def _expert_kernel(be_ref, nx_ref, nu_ref, tok_ref, h2_hbm, wg_hbm, wu_hbm, wd_hbm, y_ref,
                   xbuf, xsem, wgf, wuf, wdf, wsem, wgb, wub, wdb):
    b = pl.program_id(0)
    n_used = nu_ref[0]
    rows = MOE_BLOCK
    slot = b % 2

    def row_copy(blk, r, s_):
        t = tok_ref[blk * rows + r]
        return pltpu.make_async_copy(h2_hbm.at[pl.ds(t, 1)], xbuf.at[s_, pl.ds(r, 1)], xsem.at[s_])

    def wait_rows(s_):
        pltpu.make_async_copy(h2_hbm.at[pl.ds(0, rows)], xbuf.at[s_], xsem.at[s_]).wait()

    def weight_copies(e):
        return (pltpu.make_async_copy(wg_hbm.at[e], wgf, wsem.at[0]),
                pltpu.make_async_copy(wu_hbm.at[e], wuf, wsem.at[1]),
                pltpu.make_async_copy(wd_hbm.at[e], wdf, wsem.at[2]))

    @pl.when(b == 0)
    def _():
        for cp in weight_copies(be_ref[0]):
            cp.start()

        def body(r, carry):
            row_copy(0, r, 0).start()
            return carry
        lax.fori_loop(0, rows, body, 0)

    @pl.when(b < n_used)
    def _():
        wait_rows(slot)
        first_of_expert = (b == 0) | (be_ref[b] != be_ref[jnp.maximum(b - 1, 0)])

        @pl.when(first_of_expert)
        def _():
            for cp in weight_copies(be_ref[b]):
                cp.wait()
            wgb[...] = wgf[...].astype(BF16)
            wub[...] = wuf[...].astype(BF16)
            wdb[...] = wdf[...].astype(BF16)
            nxt = nx_ref[b]

            @pl.when(nxt >= 0)
            def _():
                for cp in weight_copies(nxt):
                    cp.start()

        xb = xbuf[slot].astype(BF16)
        for r in range(rows):
            row_copy(b + 1, r, 1 - slot).start()
        hg = _dot(xb, wgb[...])
        hu = _dot(xb, wub[...])
        act = (hg * _sigmoid(hg)) * hu
        y_ref[...] = _dot(act.astype(BF16), wdb[...])

    @pl.when(b == n_used)
    def _():
        wait_rows(slot)

    @pl.when(b >= n_used)
    def _():
        y_ref[...] = jnp.zeros_like(y_ref)


def _experts(block_e, next_e, n_used, row_token, h2, wg, wu, wd):
    n_rows = row_token.shape[0]
    n_blocks = n_rows // MOE_BLOCK
    _, d, de = wg.shape
    grid_spec = pltpu.PrefetchScalarGridSpec(
        num_scalar_prefetch=4,
        grid=(n_blocks,),
        in_specs=[pl.BlockSpec(memory_space=pl.ANY)] * 4,
        out_specs=pl.BlockSpec((MOE_BLOCK, d), lambda b, *_: (b, 0)),
        scratch_shapes=[
            pltpu.VMEM((2, MOE_BLOCK, d), F32),
            pltpu.SemaphoreType.DMA((2,)),
            pltpu.VMEM((d, de), F32),
            pltpu.VMEM((d, de), F32),
            pltpu.VMEM((de, d), F32),
            pltpu.SemaphoreType.DMA((3,)),
            pltpu.VMEM((d, de), BF16),
            pltpu.VMEM((d, de), BF16),
            pltpu.VMEM((de, d), BF16),
        ],
    )
    return pl.pallas_call(
        _expert_kernel,
        grid_spec=grid_spec,
        out_shape=jax.ShapeDtypeStruct((n_rows, d), F32),
        compiler_params=_cparams(("arbitrary",)),
        name="experts",
    )(block_e, next_e, n_used, row_token, h2, wg, wu, wd)


def _combine_kernel(dest_ref, yb_hbm, gate_ref, x1_ref, g_ref, o_ref, ybuf_even, ybuf_odd, sem):
    i = pl.program_id(0)
    n_steps = pl.num_programs(0)
    tm = x1_ref.shape[0]
    n_tok = n_steps * tm
    bufs = (ybuf_even, ybuf_odd)

    def row_copy(step, k, r, parity):
        row = dest_ref[k * n_tok + step * tm + r]
        return pltpu.make_async_copy(yb_hbm.at[pl.ds(row, 1)], bufs[parity].at[k, pl.ds(r, 1)],
                                     sem.at[parity])

    def wait_rows(parity):
        for k in range(TOP_K):
            pltpu.make_async_copy(yb_hbm.at[pl.ds(0, tm)], bufs[parity].at[k], sem.at[parity]).wait()

    def compute(parity):
        gates = gate_ref[...]
        y = bufs[parity][0] * gates[:, 0:1] + bufs[parity][1] * gates[:, 1:2]
        x2 = x1_ref[...] + y
        ms = jnp.mean(x2 * x2, axis=-1, keepdims=True)
        o_ref[...] = x2 * lax.rsqrt(ms + RMS_EPS) * g_ref[...]

    @pl.when(i == 0)
    def _():
        def body(r, carry):
            row_copy(0, 0, r, 0).start()
            row_copy(0, 1, r, 0).start()
            return carry
        lax.fori_loop(0, tm, body, 0)

    for parity in range(2):
        @pl.when((i % 2 == parity) & (i + 1 < n_steps))
        def _():
            wait_rows(parity)
            for r in range(tm):
                for k in range(TOP_K):
                    row_copy(i + 1, k, r, 1 - parity).start()
            compute(parity)

        @pl.when((i % 2 == parity) & (i + 1 == n_steps))
        def _():
            wait_rows(parity)
            compute(parity)


def _combine(dest, yb, gates, x1, g, tm):
    s, d = x1.shape
    grid_spec = pltpu.PrefetchScalarGridSpec(
        num_scalar_prefetch=1,
        grid=(s // tm,),
        in_specs=[
            pl.BlockSpec(memory_space=pl.ANY),
            pl.BlockSpec((tm, TOP_K), lambda i, dest: (i, 0)),
            pl.BlockSpec((tm, d), lambda i, dest: (i, 0)),
            pl.BlockSpec((1, d), lambda i, dest: (0, 0)),
        ],
        out_specs=pl.BlockSpec((tm, d), lambda i, dest: (i, 0)),
        scratch_shapes=[
            pltpu.VMEM((TOP_K, tm, d), F32),
            pltpu.VMEM((TOP_K, tm, d), F32),
            pltpu.SemaphoreType.DMA((2,)),
        ],
    )
    return pl.pallas_call(
        _combine_kernel,
        grid_spec=grid_spec,
        out_shape=jax.ShapeDtypeStruct((s, d), F32),
        compiler_params=_cparams(("arbitrary",)),
        name="combine",
    )(dest, yb, gates, x1, g)


def _tile(n, pref):
    return pref if n % pref == 0 else n


def kernel(x, norm_mix_g, w_in, w_gla_a2, b_gla_a, gla_norm_g, attn_sinks, rel_bias_table, w_out,
           norm_ffn_g, w_router_group, b_router_group, w_router_expert, b_router_expert,
           w_expert_gate, w_expert_up, w_expert_down, norm_final_g):
    b, s, d = x.shape
    assert b == 1 and w_in.shape[0] == 1, "single batch, single layer"
    assert (s * TOP_K) % MOE_BLOCK == 0
    x2d = x.reshape(s, d)
    w_in0 = w_in[0]

    w2 = jnp.pad(w_gla_a2[0], ((0, LANES - GLA_LOWRANK), (0, 0))).astype(BF16)
    h, log_a = _norm_loga(x2d, norm_mix_g[0].reshape(1, d), w_in0, w2, b_gla_a[0].reshape(1, -1),
                          _tile(s, 512))
    proj = _in_proj(h, w_in0, _tile(s, 1024), 512)

    bias = _rel_bias(rel_bias_table.T, jnp.asarray(_bucket_map()))
    bias = bias.reshape(N_HEADS, WINDOW, 2 * WINDOW)
    o_attn = _swa(proj, attn_sinks[0], bias)
    o_gla = _gla(proj, log_a, gla_norm_g[0].reshape(1, -1), _tile(s, 256))

    wr = jnp.zeros((ROUTER_ROWS, d), F32)
    wr = wr.at[0:N_GROUPS].set(w_router_group[0].T).at[8:].set(w_router_expert[0].T).astype(BF16)
    x1, h2, logits_t = _merge_out(o_attn, o_gla, proj, x2d, w_out[0].astype(BF16),
                                  norm_ffn_g[0].reshape(1, d), wr, _tile(s, 256))

    bg = jnp.pad(b_router_group[0], (0, 8 - N_GROUPS)).reshape(8, 1)
    idx, gate, counts = _route(logits_t, bg, b_router_expert[0].reshape(N_EXPERTS, 1), _tile(s, 512))
    counts = counts[:, 0].astype(jnp.int32)
    padded = (counts + MOE_BLOCK - 1) // MOE_BLOCK * MOE_BLOCK
    pend = jnp.cumsum(padded)
    pstart = pend - padded
    expert = idx[0:TOP_K]
    eids = jnp.arange(N_EXPERTS, dtype=jnp.int32)[:, None, None]
    dest = jnp.sum(jnp.where(expert[None] == eids, pstart[:, None, None], 0), axis=0) + idx[TOP_K:2 * TOP_K]
    n_pairs = s * TOP_K
    n_blocks = (n_pairs + N_EXPERTS * (MOE_BLOCK - 1) + MOE_BLOCK - 1) // MOE_BLOCK
    n_used = (pend[-1] // MOE_BLOCK).reshape(1)
    block_start = jnp.arange(n_blocks, dtype=jnp.int32) * MOE_BLOCK
    block_e = jnp.minimum(jnp.sum((pend[None, :] <= block_start[:, None]).astype(jnp.int32), axis=1),
                          N_EXPERTS - 1)
    dest_flat = dest.reshape(-1)
    tok_of_pair = jnp.tile(jnp.arange(s, dtype=jnp.int32), TOP_K)
    row_token = jnp.zeros((n_blocks * MOE_BLOCK,), jnp.int32).at[dest_flat].set(tok_of_pair)

    e_col = jnp.arange(N_EXPERTS, dtype=jnp.int32)[:, None]
    e_row = jnp.arange(N_EXPERTS, dtype=jnp.int32)[None, :]
    later = (e_row > e_col) & (counts > 0)[None, :]
    next_nonempty = jnp.min(jnp.where(later, e_row, N_EXPERTS), axis=1)
    next_nonempty = jnp.where(next_nonempty < N_EXPERTS, next_nonempty, -1)
    next_e = jnp.sum(jnp.where(block_e[:, None] == e_row, next_nonempty[None, :], 0), axis=1)
    yb = _experts(block_e, next_e, n_used, row_token, h2, w_expert_gate[0], w_expert_up[0], w_expert_down[0])
    out = _combine(dest_flat, yb, gate[0:TOP_K].T, x1, norm_final_g.reshape(1, d), _tile(s, 128))
    return out.reshape(b, s, d)
```

```python
import functools
import math

import numpy as np
import jax
import jax.numpy as jnp
from jax import lax
from jax.experimental import pallas as pl
from jax.experimental.pallas import tpu as pltpu

F32 = jnp.float32
BF16 = jnp.bfloat16

N_HEADS = 32
N_KV_HEADS = 4
HEAD_DIM = 64
WINDOW = 128
N_BUCKETS = 32
MAX_DISTANCE = 128
GLA_HEADS = 4
GLA_DK = 256
GLA_DV = 512
GLA_LOWRANK = 16
GLA_TAU = 16.0
GLA_CHUNK = 64
N_GROUPS = 4
EXPERTS_PER_GROUP = 8
N_EXPERTS = 32
TOP_K = 2
MOE_BLOCK = 128
RMS_EPS = 1e-6
NEG_INF = -1e30

COL_GA = 8704
COL_GATES = 8720
MAIN_COLS = 8704
GATE_COLS = 4096
COL_AQ = GATE_COLS + 0
COL_AK = GATE_COLS + 2048
COL_AV = GATE_COLS + 2304
COL_GQ = GATE_COLS + 2560
COL_GK = GATE_COLS + 3584
COL_GV = GATE_COLS + 4608
COL_GR = GATE_COLS + 6656

LANES = 128
VMEM_LIMIT = 56 * 1024 * 1024


def _cparams(sem, vmem=VMEM_LIMIT):
    return pltpu.CompilerParams(dimension_semantics=sem, vmem_limit_bytes=vmem)


def _split3(x):
    hi = x.astype(BF16)
    r1 = x - hi.astype(F32)
    mid = r1.astype(BF16)
    lo = (r1 - mid.astype(F32)).astype(BF16)
    return hi, mid, lo


def _sigmoid(x):
    return 1.0 / (1.0 + jnp.exp(-x))


def _dot(a, b):
    return jnp.dot(a, b, preferred_element_type=F32)


def _dot_nt(a, b):
    return lax.dot_general(a, b, (((1,), (1,)), ((), ())), preferred_element_type=F32)


def _dot_tn(a, b):
    return lax.dot_general(a, b, (((0,), (0,)), ((), ())), preferred_element_type=F32)


def _norm_loga_kernel(x_ref, g_ref, wga_ref, w2_ref, b_ref, h_ref, la_ref):
    x = x_ref[...]
    ms = jnp.mean(x * x, axis=-1, keepdims=True)
    hb = (x * lax.rsqrt(ms + RMS_EPS) * g_ref[...]).astype(BF16)
    h_ref[...] = hb
    row = lax.broadcasted_iota(jnp.int32, wga_ref.shape, 0)
    wga = jnp.where(row < GLA_LOWRANK, wga_ref[...], 0.0).astype(BF16)
    ga = _dot_nt(hb, wga)
    z = _dot(ga.astype(BF16), w2_ref[...]) + b_ref[...]
    la_ref[...] = (jnp.minimum(z, 0.0) - jnp.log1p(jnp.exp(-jnp.abs(z)))) * (1.0 / GLA_TAU)


def _norm_loga(x, g, wga, w2, b, tm):
    s, d = x.shape
    nq = w2.shape[1]
    return pl.pallas_call(
        _norm_loga_kernel,
        grid=(s // tm,),
        in_specs=[
            pl.BlockSpec((tm, d), lambda i: (i, 0)),
            pl.BlockSpec((1, d), lambda i: (0, 0)),
            pl.BlockSpec((LANES, d), lambda i: (COL_GA // LANES, 0)),
            pl.BlockSpec((LANES, nq), lambda i: (0, 0)),
            pl.BlockSpec((1, nq), lambda i: (0, 0)),
        ],
        out_specs=[
            pl.BlockSpec((tm, d), lambda i: (i, 0)),
            pl.BlockSpec((tm, nq), lambda i: (i, 0)),
        ],
        out_shape=[
            jax.ShapeDtypeStruct((s, d), BF16),
            jax.ShapeDtypeStruct((s, nq), F32),
        ],
        compiler_params=_cparams(("parallel",)),
        name="norm_loga",
    )(x, g, wga, w2, b)


def _in_proj_kernel(h_hbm, wt_hbm, o_ref, h_vmem, w_stage, wbf, sem, *, n_main, n_tiles, tn):
    j = pl.program_id(0)
    i = pl.program_id(1)
    tm = o_ref.shape[0]

    def w_copy(jj):
        row0 = jnp.where(jj < n_main, jj * tn, COL_GATES + (jj - n_main) * tn)
        return pltpu.make_async_copy(wt_hbm.at[pl.ds(pl.multiple_of(row0, 8), tn)], w_stage, sem.at[1])

    @pl.when((j == 0) & (i == 0))
    def _():
        h_copy = pltpu.make_async_copy(h_hbm, h_vmem, sem.at[0])
        h_copy.start()
        w_copy(0).start()
        h_copy.wait()

    @pl.when(i == 0)
    def _():
        w_copy(j).wait()
        wbf[...] = w_stage[...].astype(BF16)

        @pl.when(j + 1 < n_tiles)
        def _():
            w_copy(j + 1).start()

    h = h_vmem[pl.ds(pl.multiple_of(i * tm, tm), tm), :]
    o_ref[...] = _dot_nt(h, wbf[...]).astype(o_ref.dtype)


def _in_proj(h, w_t, tm, tn):
    s, d = h.shape
    n_main = MAIN_COLS // tn
    n_gate = GATE_COLS // tn
    n_tiles = n_main + n_gate
    kern = functools.partial(_in_proj_kernel, n_main=n_main, n_tiles=n_tiles, tn=tn)
    return pl.pallas_call(
        kern,
        grid=(n_tiles, s // tm),
        in_specs=[pl.BlockSpec(memory_space=pl.ANY), pl.BlockSpec(memory_space=pl.ANY)],
        out_specs=pl.BlockSpec((tm, tn), lambda j, i: (i, (j + n_gate) % n_tiles)),
        out_shape=jax.ShapeDtypeStruct((s, n_tiles * tn), BF16),
        scratch_shapes=[
            pltpu.VMEM((s, d), BF16),
            pltpu.VMEM((tn, d), F32),
            pltpu.VMEM((tn, d), BF16),
            pltpu.SemaphoreType.DMA((2,)),
        ],
        compiler_params=_cparams(("arbitrary", "arbitrary")),
        name="in_proj",
    )(h, w_t)


def _bucket_map():
    q_loc = np.arange(WINDOW)[:, None]
    k_loc = np.arange(2 * WINDOW)[None, :]
    n = np.maximum(q_loc + WINDOW - k_loc, 0)
    max_exact = N_BUCKETS // 2
    ratio = np.maximum(n, max_exact).astype(np.float32) / np.float32(max_exact)
    large = max_exact + (np.log(ratio) / np.float32(math.log(MAX_DISTANCE / max_exact))
                         * (N_BUCKETS - max_exact)).astype(np.int32)
    large = np.minimum(large, N_BUCKETS - 1)
    return np.where(n < max_exact, n, large).astype(np.int32).reshape(1, -1)


def _rel_bias_kernel(tab_ref, bucket_ref, o_ref):
    nb = tab_ref.shape[1]
    width = bucket_ref.shape[1]
    onehot = (lax.broadcasted_iota(jnp.int32, (nb, width), 0) == bucket_ref[...]).astype(BF16)
    hi, mid, lo = _split3(tab_ref[...])
    o_ref[...] = _dot(hi, onehot) + _dot(mid, onehot) + _dot(lo, onehot)


def _rel_bias(table_t, bucket):
    nh = table_t.shape[0]
    width = bucket.shape[1]
    return pl.pallas_call(
        _rel_bias_kernel,
        out_shape=jax.ShapeDtypeStruct((nh, width), F32),
        compiler_params=pltpu.CompilerParams(vmem_limit_bytes=VMEM_LIMIT),
        name="rel_bias",
    )(table_t, bucket)


def _swa_kernel(sink_ref, q_ref, kc_ref, kp_ref, vc_ref, vp_ref, bias_ref, o_ref):
    n = pl.program_id(0)
    w = WINDOW
    hd = HEAD_DIM
    gq = N_HEADS // N_KV_HEADS
    qi = lax.broadcasted_iota(jnp.int32, (w, 2 * w), 0)
    ki = lax.broadcasted_iota(jnp.int32, (w, 2 * w), 1)
    dist = qi + w - ki
    first_key = jnp.where(n > 0, 0, w)
    valid = (dist >= 0) & (dist < w) & (ki >= first_key)
    zeros = jnp.zeros((2 * w, hd), BF16)
    scale = HEAD_DIM ** -0.5
    for g in range(N_KV_HEADS):
        cs = slice(g * hd, (g + 1) * hd)
        k_g = jnp.concatenate([kp_ref[:, cs], kc_ref[:, cs]], axis=0)
        v_g = jnp.concatenate([vp_ref[:, cs], vc_ref[:, cs]], axis=0)
        k_lo = jnp.concatenate([k_g, zeros], axis=1)
        k_hi = jnp.concatenate([zeros, k_g], axis=1)
        v_lo = jnp.concatenate([v_g, zeros], axis=1)
        v_hi = jnp.concatenate([zeros, v_g], axis=1)
        for j in range(gq // 2):
            head = g * gq + 2 * j
            q_pair = q_ref[:, head * hd:(head + 2) * hd]
            acc = None
            for half, (k_pl, v_pl) in enumerate(((k_lo, v_lo), (k_hi, v_hi))):
                hh = head + half
                s = _dot_nt(q_pair, k_pl) * scale + bias_ref[hh]
                s = jnp.where(valid, s, NEG_INF)
                sink = sink_ref[hh]
                m = jnp.maximum(jnp.max(s, axis=-1, keepdims=True), sink)
                p = jnp.exp(s - m)
                denom = jnp.sum(p, axis=-1, keepdims=True) + jnp.exp(sink - m)
                probs = (p * (1.0 / denom)).astype(BF16)
                part = _dot(probs, v_pl)
                acc = part if acc is None else acc + part
            o_ref[:, head * hd:(head + 2) * hd] = acc.astype(o_ref.dtype)


def _swa(proj, sinks, bias):
    s = proj.shape[0]
    w = WINDOW
    nb = s // w
    kvw = N_KV_HEADS * HEAD_DIM
    dq = N_HEADS * HEAD_DIM
    prev = lambda n: jnp.maximum(n - 1, 0)
    return pl.pallas_call(
        _swa_kernel,
        grid=(nb,),
        in_specs=[
            pl.BlockSpec(memory_space=pltpu.SMEM),
            pl.BlockSpec((w, dq), lambda n: (n, COL_AQ // dq)),
            pl.BlockSpec((w, kvw), lambda n: (n, COL_AK // kvw)),
            pl.BlockSpec((w, kvw), lambda n: (prev(n), COL_AK // kvw)),
            pl.BlockSpec((w, kvw), lambda n: (n, COL_AV // kvw)),
            pl.BlockSpec((w, kvw), lambda n: (prev(n), COL_AV // kvw)),
            pl.BlockSpec((N_HEADS, w, 2 * w), lambda n: (0, 0, 0)),
        ],
        out_specs=pl.BlockSpec((w, dq), lambda n: (n, 0)),
        out_shape=jax.ShapeDtypeStruct((s, dq), BF16),
        compiler_params=_cparams(("parallel",)),
        name="swa",
    )(sinks, proj, proj, proj, proj, proj, bias)


def _gla_kernel(q_ref, k_ref, v_ref, gr_ref, la_ref, gn_ref, o_ref, state_ref):
    c = GLA_CHUNK
    rows = q_ref.shape[0]
    dk = q_ref.shape[1]
    dv = v_ref.shape[1]

    @pl.when(pl.program_id(1) == 0)
    def _():
        state_ref[...] = jnp.zeros_like(state_ref)

    la = la_ref[...]
    la_parts = _split3(la)
    ri = lax.broadcasted_iota(jnp.int32, (rows, rows), 0)
    ci = lax.broadcasted_iota(jnp.int32, (rows, rows), 1)
    tri = ((ri // c == ci // c) & (ri >= ci)).astype(BF16)
    cum = _dot(tri, la_parts[0]) + _dot(tri, la_parts[1]) + _dot(tri, la_parts[2])
    ones = jnp.ones((c, LANES), BF16)
    causal = (lax.broadcasted_iota(jnp.int32, (c, c), 0) >= lax.broadcasted_iota(jnp.int32, (c, c), 1))
    gn = gn_ref[...]
    for ch in range(rows // c):
        sl = slice(ch * c, (ch + 1) * c)
        cum_c = cum[sl]
        ref = cum_c[c // 2 - 1:c // 2]
        last = cum_c[c - 1:c]
        q = q_ref[sl, :].astype(F32) * (dk ** -0.5)
        k = k_ref[sl, :].astype(F32)
        v = v_ref[sl, :]
        q_intra = (q * jnp.exp(cum_c - ref)).astype(BF16)
        k_intra = (k * jnp.exp(ref - cum_c)).astype(BF16)
        a = jnp.where(causal, _dot_nt(q_intra, k_intra), 0.0)
        o = _dot(a.astype(BF16), v)
        state = state_ref[...]
        o = o + _dot((q * jnp.exp(cum_c)).astype(BF16), state.astype(BF16))
        k_dec = (k * jnp.exp(last - cum_c)).astype(BF16)
        upd = _dot_tn(k_dec, v)
        last_col = (_dot_tn(la_parts[0][sl], ones) + _dot_tn(la_parts[1][sl], ones)
                    + _dot_tn(la_parts[2][sl], ones))
        decay = jnp.exp(last_col)
        for jv in range(dv // LANES):
            ls = slice(jv * LANES, (jv + 1) * LANES)
            state_ref[:, ls] = state[:, ls] * decay + upd[:, ls]
        ms = jnp.mean(o * o, axis=-1, keepdims=True)
        on = o * lax.rsqrt(ms + RMS_EPS) * gn
        gr = gr_ref[sl, :].astype(F32)
        o_ref[sl, :] = (on * (gr * _sigmoid(gr))).astype(o_ref.dtype)


def _gla(proj, la, gn, rows):
    s = proj.shape[0]
    dk, dv = GLA_DK, GLA_DV
    return pl.pallas_call(
        _gla_kernel,
        grid=(GLA_HEADS, s // rows),
        in_specs=[
            pl.BlockSpec((rows, dk), lambda h, i: (i, COL_GQ // dk + h)),
            pl.BlockSpec((rows, dk), lambda h, i: (i, COL_GK // dk + h)),
            pl.BlockSpec((rows, dv), lambda h, i: (i, COL_GV // dv + h)),
            pl.BlockSpec((rows, dv), lambda h, i: (i, COL_GR // dv + h)),
            pl.BlockSpec((rows, dk), lambda h, i: (i, h)),
            pl.BlockSpec((1, dv), lambda h, i: (0, 0)),
        ],
        out_specs=pl.BlockSpec((rows, dv), lambda h, i: (i, h)),
        out_shape=jax.ShapeDtypeStruct((s, GLA_HEADS * dv), BF16),
        scratch_shapes=[pltpu.VMEM((dk, dv), F32)],
        compiler_params=_cparams(("parallel", "arbitrary")),
        name="gla",
    )(proj, proj, proj, proj, la, gn)


def _merge_out_kernel(oa_ref, og_ref, ga_ref, gg_ref, x_ref, wo_ref, g_ref, wr_ref,
                      x1_ref, h2_ref, lt_ref):
    merged = (_sigmoid(ga_ref[...].astype(F32)) * oa_ref[...].astype(F32)
              + _sigmoid(gg_ref[...].astype(F32)) * og_ref[...].astype(F32))
    x1 = x_ref[...] + _dot(merged.astype(BF16), wo_ref[...])
    x1_ref[...] = x1
    ms = jnp.mean(x1 * x1, axis=-1, keepdims=True)
    h2 = x1 * lax.rsqrt(ms + RMS_EPS) * g_ref[...]
    h2_ref[...] = h2
    lt_ref[...] = _dot_nt(wr_ref[...], h2.astype(BF16))


def _merge_out(o_attn, o_gla, gates, x, wo, g, wr, tm):
    s, d = x.shape
    nr = wr.shape[0]
    row = lambda i: (i, 0)
    return pl.pallas_call(
        _merge_out_kernel,
        grid=(s // tm,),
        in_specs=[
            pl.BlockSpec((tm, d), row),
            pl.BlockSpec((tm, d), row),
            pl.BlockSpec((tm, d), lambda i: (i, 0)),
            pl.BlockSpec((tm, d), lambda i: (i, 1)),
            pl.BlockSpec((tm, d), row),
            pl.BlockSpec((d, d), lambda i: (0, 0)),
            pl.BlockSpec((1, d), lambda i: (0, 0)),
            pl.BlockSpec((nr, d), lambda i: (0, 0)),
        ],
        out_specs=[
            pl.BlockSpec((tm, d), row),
            pl.BlockSpec((tm, d), row),
            pl.BlockSpec((nr, tm), lambda i: (0, i)),
        ],
        out_shape=[
            jax.ShapeDtypeStruct((s, d), F32),
            jax.ShapeDtypeStruct((s, d), F32),
            jax.ShapeDtypeStruct((nr, s), F32),
        ],
        compiler_params=_cparams(("parallel",)),
        name="merge_out",
    )(o_attn, o_gla, gates, gates, x, wo, g, wr)


ROUTER_ROWS = 8 + N_EXPERTS


def _route_kernel(lt_ref, bg_ref, be_ref, idx_ref, gate_ref, cnt_ref, carry_ref):
    tb = lt_ref.shape[1]
    epg = EXPERTS_PER_GROUP

    @pl.when(pl.program_id(0) == 0)
    def _():
        carry_ref[...] = jnp.zeros_like(carry_ref)

    logits = lt_ref[...]
    gl = logits[0:N_GROUPS]
    g_exp = jnp.exp(gl - jnp.max(gl, axis=0, keepdims=True))
    g_prob = g_exp / jnp.sum(g_exp, axis=0, keepdims=True)
    gb = gl + bg_ref[0:N_GROUPS]
    rowg = lax.broadcasted_iota(jnp.int32, (N_GROUPS, tb), 0)
    g_idx = jnp.min(jnp.where(gb == jnp.max(gb, axis=0, keepdims=True), rowg, N_GROUPS),
                    axis=0, keepdims=True)
    p_group = jnp.sum(jnp.where(rowg == g_idx, g_prob, 0.0), axis=0, keepdims=True)

    el = logits[8:8 + epg]
    eb = jnp.broadcast_to(be_ref[0:epg], (epg, tb))
    for g in range(1, N_GROUPS):
        pick = g_idx == g
        el = jnp.where(pick, logits[8 + g * epg:8 + (g + 1) * epg], el)
        eb = jnp.where(pick, be_ref[g * epg:(g + 1) * epg], eb)
    e_exp = jnp.exp(el - jnp.max(el, axis=0, keepdims=True))
    e_prob = e_exp / jnp.sum(e_exp, axis=0, keepdims=True)
    score = el + eb
    rowe = lax.broadcasted_iota(jnp.int32, (epg, tb), 0)
    i1 = jnp.min(jnp.where(score == jnp.max(score, axis=0, keepdims=True), rowe, epg),
                 axis=0, keepdims=True)
    score2 = jnp.where(rowe == i1, -jnp.inf, score)
    i2 = jnp.min(jnp.where(score2 == jnp.max(score2, axis=0, keepdims=True), rowe, epg),
                 axis=0, keepdims=True)
    q1 = jnp.sum(jnp.where(rowe == i1, e_prob, 0.0), axis=0, keepdims=True)
    q2 = jnp.sum(jnp.where(rowe == i2, e_prob, 0.0), axis=0, keepdims=True)
    qs = q1 + q2
    gate1 = p_group * q1 / qs
    gate2 = p_group * q2 / qs
    e1 = g_idx * epg + i1
    e2 = g_idx * epg + i2

    rowx = lax.broadcasted_iota(jnp.int32, (N_EXPERTS, tb), 0)
    hit1 = rowx == e1
    hit2 = rowx == e2
    member = (hit1 | hit2).astype(BF16)
    before = (lax.broadcasted_iota(jnp.int32, (tb, tb), 0)
              < lax.broadcasted_iota(jnp.int32, (tb, tb), 1)).astype(BF16)
    carry = carry_ref[...]
    count = _dot(member, before) + carry[:, 0:1]
    r1 = jnp.sum(jnp.where(hit1, count, 0.0), axis=0, keepdims=True).astype(jnp.int32)
    r2 = jnp.sum(jnp.where(hit2, count, 0.0), axis=0, keepdims=True).astype(jnp.int32)
    carry = carry + jnp.sum(member.astype(F32), axis=1, keepdims=True)
    carry_ref[...] = carry
    cnt_ref[...] = carry

    row8 = lax.broadcasted_iota(jnp.int32, (8, tb), 0)
    idx_ref[...] = jnp.where(row8 == 0, e1, jnp.where(row8 == 1, e2,
                             jnp.where(row8 == 2, r1, jnp.where(row8 == 3, r2, 0))))
    gate_ref[...] = jnp.where(row8 == 0, gate1, jnp.where(row8 == 1, gate2, 0.0))


def _route(lt, bg, be, tb):
    nr, s = lt.shape
    return pl.pallas_call(
        _route_kernel,
        grid=(s // tb,),
        in_specs=[
            pl.BlockSpec((nr, tb), lambda i: (0, i)),
            pl.BlockSpec((8, 1), lambda i: (0, 0)),
            pl.BlockSpec((N_EXPERTS, 1), lambda i: (0, 0)),
        ],
        out_specs=[
            pl.BlockSpec((8, tb), lambda i: (0, i)),
            pl.BlockSpec((8, tb), lambda i: (0, i)),
            pl.BlockSpec((N_EXPERTS, LANES), lambda i: (0, 0)),
        ],
        out_shape=[
            jax.ShapeDtypeStruct((8, s), jnp.int32),
            jax.ShapeDtypeStruct((8, s), F32),
            jax.ShapeDtypeStruct((N_EXPERTS, LANES), F32),
        ],
        scratch_shapes=[pltpu.VMEM((N_EXPERTS, LANES), F32)],
        compiler_params=_cparams(("arbitrary",)),
        name="route",
    )(lt, bg, be)


def _expert_kernel(be_ref, nx_ref, nu_ref, tok_ref, h2_hbm, wg_hbm, wu_hbm, wd_hbm, y_ref,
                   xbuf, xsem, wgf, wuf, wdf, wsem, wgb, wub, wdb):
    b = pl.program_id(0)
    n_used = nu_ref[0]
    rows = MOE_BLOCK
    slot = b % 2

    def row_copy(blk, r, s_):
        t = tok_ref[blk * rows + r]
        return pltpu.make_async_copy(h2_hbm.at[pl.ds(t, 1)], xbuf.at[s_, pl.ds(r, 1)], xsem.at[s_])

    def wait_rows(s_):
        pltpu.make_async_copy(h2_hbm.at[pl.ds(0, rows)], xbuf.at[s_], xsem.at[s_]).wait()

    def weight_copies(e):
        return (pltpu.make_async_copy(wg_hbm.at[e], wgf, wsem.at[0]),
                pltpu.make_async_copy(wu_hbm.at[e], wuf, wsem.at[1]),
                pltpu.make_async_copy(wd_hbm.at[e], wdf, wsem.at[2]))

    @pl.when(b == 0)
    def _():
        for cp in weight_copies(be_ref[0]):
            cp.start()

        def body(r, carry):
            row_copy(0, r, 0).start()
            return carry
        lax.fori_loop(0, rows, body, 0)

    @pl.when(b < n_used)
    def _():
        wait_rows(slot)
        first_of_expert = (b == 0) | (be_ref[b] != be_ref[jnp.maximum(b - 1, 0)])

        @pl.when(first_of_expert)
        def _():
            for cp in weight_copies(be_ref[b]):
                cp.wait()
            wgb[...] = wgf[...].astype(BF16)
            wub[...] = wuf[...].astype(BF16)
            wdb[...] = wdf[...].astype(BF16)
            nxt = nx_ref[b]

            @pl.when(nxt >= 0)
            def _():
                for cp in weight_copies(nxt):
                    cp.start()

        xb = xbuf[slot].astype(BF16)
        for r in range(rows):
            row_copy(b + 1, r, 1 - slot).start()
        hg = _dot(xb, wgb[...])
        hu = _dot(xb, wub[...])
        act = (hg * _sigmoid(hg)) * hu
        y_ref[...] = _dot(act.astype(BF16), wdb[...])

    @pl.when(b == n_used)
    def _():
        wait_rows(slot)

    @pl.when(b >= n_used)
    def _():
        y_ref[...] = jnp.zeros_like(y_ref)


def _experts(block_e, next_e, n_used, row_token, h2, wg, wu, wd):
    n_rows = row_token.shape[0]
    n_blocks = n_rows // MOE_BLOCK
    _, d, de = wg.shape
    grid_spec = pltpu.PrefetchScalarGridSpec(
        num_scalar_prefetch=4,
        grid=(n_blocks,),
        in_specs=[pl.BlockSpec(memory_space=pl.ANY)] * 4,
        out_specs=pl.BlockSpec((MOE_BLOCK, d), lambda b, *_: (b, 0)),
        scratch_shapes=[
            pltpu.VMEM((2, MOE_BLOCK, d), F32),
            pltpu.SemaphoreType.DMA((2,)),
            pltpu.VMEM((d, de), F32),
            pltpu.VMEM((d, de), F32),
            pltpu.VMEM((de, d), F32),
            pltpu.SemaphoreType.DMA((3,)),
            pltpu.VMEM((d, de), BF16),
            pltpu.VMEM((d, de), BF16),
            pltpu.VMEM((de, d), BF16),
        ],
    )
    return pl.pallas_call(
        _expert_kernel,
        grid_spec=grid_spec,
        out_shape=jax.ShapeDtypeStruct((n_rows, d), F32),
        compiler_params=_cparams(("arbitrary",)),
        name="experts",
    )(block_e, next_e, n_used, row_token, h2, wg, wu, wd)


def _combine_kernel(dest_ref, yb_hbm, gate_ref, x1_ref, g_ref, o_ref, ybuf_even, ybuf_odd, sem):
    i = pl.program_id(0)
    n_steps = pl.num_programs(0)
    tm = x1_ref.shape[0]
    n_tok = n_steps * tm
    bufs = (ybuf_even, ybuf_odd)

    def row_copy(step, k, r, parity):
        row = dest_ref[k * n_tok + step * tm + r]
        return pltpu.make_async_copy(yb_hbm.at[pl.ds(row, 1)], bufs[parity].at[k, pl.ds(r, 1)],
                                     sem.at[parity])

    def wait_rows(parity):
        for k in range(TOP_K):
            pltpu.make_async_copy(yb_hbm.at[pl.ds(0, tm)], bufs[parity].at[k], sem.at[parity]).wait()

    def compute(parity):
        gates = gate_ref[...]
        y = bufs[parity][0] * gates[:, 0:1] + bufs[parity][1] * gates[:, 1:2]
        x2 = x1_ref[...] + y
        ms = jnp.mean(x2 * x2, axis=-1, keepdims=True)
        o_ref[...] = x2 * lax.rsqrt(ms + RMS_EPS) * g_ref[...]

    @pl.when(i == 0)
    def _():
        def body(r, carry):
            row_copy(0, 0, r, 0).start()
            row_copy(0, 1, r, 0).start()
            return carry
        lax.fori_loop(0, tm, body, 0)

    for parity in range(2):
        @pl.when((i % 2 == parity) & (i + 1 < n_steps))
        def _():
            wait_rows(parity)
            for r in range(tm):
                for k in range(TOP_K):
                    row_copy(i + 1, k, r, 1 - parity).start()
            compute(parity)

        @pl.when((i % 2 == parity) & (i + 1 == n_steps))
        def _():
            wait_rows(parity)
            compute(parity)


def _combine(dest, yb, gates, x1, g, tm):
    s, d = x1.shape
    grid_spec = pltpu.PrefetchScalarGridSpec(
        num_scalar_prefetch=1,
        grid=(s // tm,),
        in_specs=[
            pl.BlockSpec(memory_space=pl.ANY),
            pl.BlockSpec((tm, TOP_K), lambda i, dest: (i, 0)),
            pl.BlockSpec((tm, d), lambda i, dest: (i, 0)),
            pl.BlockSpec((1, d), lambda i, dest: (0, 0)),
        ],
        out_specs=pl.BlockSpec((tm, d), lambda i, dest: (i, 0)),
        scratch_shapes=[
            pltpu.VMEM((TOP_K, tm, d), F32),
            pltpu.VMEM((TOP_K, tm, d), F32),
            pltpu.SemaphoreType.DMA((2,)),
        ],
    )
    return pl.pallas_call(
        _combine_kernel,
        grid_spec=grid_spec,
        out_shape=jax.ShapeDtypeStruct((s, d), F32),
        compiler_params=_cparams(("arbitrary",)),
        name="combine",
    )(dest, yb, gates, x1, g)


def _tile(n, pref):
    return pref if n % pref == 0 else n


def kernel(x, norm_mix_g, w_in, w_gla_a2, b_gla_a, gla_norm_g, attn_sinks, rel_bias_table, w_out,
           norm_ffn_g, w_router_group, b_router_group, w_router_expert, b_router_expert,
           w_expert_gate, w_expert_up, w_expert_down, norm_final_g):
    b, s, d = x.shape
    assert b == 1 and w_in.shape[0] == 1, "single batch, single layer"
    assert (s * TOP_K) % MOE_BLOCK == 0
    x2d = x.reshape(s, d)
    w_in_t = w_in[0].T

    w2 = jnp.pad(w_gla_a2[0], ((0, LANES - GLA_LOWRANK), (0, 0))).astype(BF16)
    h, log_a = _norm_loga(x2d, norm_mix_g[0].reshape(1, d), w_in_t, w2, b_gla_a[0].reshape(1, -1),
                          _tile(s, 512))
    proj = _in_proj(h, w_in_t, _tile(s, 1024), 512)

    bias = _rel_bias(rel_bias_table.T, jnp.asarray(_bucket_map()))
    bias = bias.reshape(N_HEADS, WINDOW, 2 * WINDOW)
    o_attn = _swa(proj, attn_sinks[0], bias)
    o_gla = _gla(proj, log_a, gla_norm_g[0].reshape(1, -1), _tile(s, 256))

    wr = jnp.zeros((ROUTER_ROWS, d), F32)
    wr = wr.at[0:N_GROUPS].set(w_router_group[0].T).at[8:].set(w_router_expert[0].T).astype(BF16)
    x1, h2, logits_t = _merge_out(o_attn, o_gla, proj, x2d, w_out[0].astype(BF16),
                                  norm_ffn_g[0].reshape(1, d), wr, _tile(s, 256))

    bg = jnp.pad(b_router_group[0], (0, 8 - N_GROUPS)).reshape(8, 1)
    idx, gate, counts = _route(logits_t, bg, b_router_expert[0].reshape(N_EXPERTS, 1), _tile(s, 512))
    counts = counts[:, 0].astype(jnp.int32)
    padded = (counts + MOE_BLOCK - 1) // MOE_BLOCK * MOE_BLOCK
    pend = jnp.cumsum(padded)
    pstart = pend - padded
    expert = idx[0:TOP_K]
    eids = jnp.arange(N_EXPERTS, dtype=jnp.int32)[:, None, None]
    dest = jnp.sum(jnp.where(expert[None] == eids, pstart[:, None, None], 0), axis=0) + idx[TOP_K:2 * TOP_K]
    n_pairs = s * TOP_K
    n_blocks = (n_pairs + N_EXPERTS * (MOE_BLOCK - 1) + MOE_BLOCK - 1) // MOE_BLOCK
    n_used = (pend[-1] // MOE_BLOCK).reshape(1)
    block_start = jnp.arange(n_blocks, dtype=jnp.int32) * MOE_BLOCK
    block_e = jnp.minimum(jnp.sum((pend[None, :] <= block_start[:, None]).astype(jnp.int32), axis=1),
                          N_EXPERTS - 1)
    dest_flat = dest.reshape(-1)
    tok_of_pair = jnp.tile(jnp.arange(s, dtype=jnp.int32), TOP_K)
    row_token = jnp.zeros((n_blocks * MOE_BLOCK,), jnp.int32).at[dest_flat].set(tok_of_pair)

    e_col = jnp.arange(N_EXPERTS, dtype=jnp.int32)[:, None]
    e_row = jnp.arange(N_EXPERTS, dtype=jnp.int32)[None, :]
    later = (e_row > e_col) & (counts > 0)[None, :]
    next_nonempty = jnp.min(jnp.where(later, e_row, N_EXPERTS), axis=1)
    next_nonempty = jnp.where(next_nonempty < N_EXPERTS, next_nonempty, -1)
    next_e = jnp.sum(jnp.where(block_e[:, None] == e_row, next_nonempty[None, :], 0), axis=1)
    yb = _experts(block_e, next_e, n_used, row_token, h2, w_expert_gate[0], w_expert_up[0], w_expert_down[0])
    out = _combine(dest_flat, yb, gate[0:TOP_K].T, x1, norm_final_g.reshape(1, d), _tile(s, 128))
    return out.reshape(b, s, d)
```

```python
import functools
import math

import numpy as np
import jax
import jax.numpy as jnp
from jax import lax
from jax.experimental import pallas as pl
from jax.experimental.pallas import tpu as pltpu

F32 = jnp.float32
BF16 = jnp.bfloat16

N_HEADS = 32
N_KV_HEADS = 4
HEAD_DIM = 64
WINDOW = 128
N_BUCKETS = 32
MAX_DISTANCE = 128
GLA_HEADS = 4
GLA_DK = 256
GLA_DV = 512
GLA_LOWRANK = 16
GLA_TAU = 16.0
GLA_CHUNK = 64
N_GROUPS = 4
EXPERTS_PER_GROUP = 8
N_EXPERTS = 32
TOP_K = 2
MOE_BLOCK = 128
RMS_EPS = 1e-6
NEG_INF = -1e30

COL_GA = 8704
COL_GATES = 8720
MAIN_COLS = 8704
GATE_COLS = 4096
COL_AQ = GATE_COLS + 0
COL_AK = GATE_COLS + 2048
COL_AV = GATE_COLS + 2304
COL_GQ = GATE_COLS + 2560
COL_GK = GATE_COLS + 3584
COL_GV = GATE_COLS + 4608
COL_GR = GATE_COLS + 6656

LANES = 128
VMEM_LIMIT = 56 * 1024 * 1024


def _cparams(sem, vmem=VMEM_LIMIT):
    return pltpu.CompilerParams(dimension_semantics=sem, vmem_limit_bytes=vmem)


def _split3(x):
    hi = x.astype(BF16)
    r1 = x - hi.astype(F32)
    mid = r1.astype(BF16)
    lo = (r1 - mid.astype(F32)).astype(BF16)
    return hi, mid, lo


def _sigmoid(x):
    return 1.0 / (1.0 + jnp.exp(-x))


def _dot(a, b):
    return jnp.dot(a, b, preferred_element_type=F32)


def _dot_nt(a, b):
    return lax.dot_general(a, b, (((1,), (1,)), ((), ())), preferred_element_type=F32)


def _dot_tn(a, b):
    return lax.dot_general(a, b, (((0,), (0,)), ((), ())), preferred_element_type=F32)


def _norm_loga_kernel(x_ref, g_ref, wga_ref, w2_ref, b_ref, h_ref, la_ref):
    x = x_ref[...]
    ms = jnp.mean(x * x, axis=-1, keepdims=True)
    hb = (x * lax.rsqrt(ms + RMS_EPS) * g_ref[...]).astype(BF16)
    h_ref[...] = hb
    row = lax.broadcasted_iota(jnp.int32, wga_ref.shape, 0)
    wga = jnp.where(row < GLA_LOWRANK, wga_ref[...], 0.0).astype(BF16)
    ga = _dot_nt(hb, wga)
    z = _dot(ga.astype(BF16), w2_ref[...]) + b_ref[...]
    la_ref[...] = (jnp.minimum(z, 0.0) - jnp.log1p(jnp.exp(-jnp.abs(z)))) * (1.0 / GLA_TAU)


def _norm_loga(x, g, wga, w2, b, tm):
    s, d = x.shape
    nq = w2.shape[1]
    return pl.pallas_call(
        _norm_loga_kernel,
        grid=(s // tm,),
        in_specs=[
            pl.BlockSpec((tm, d), lambda i: (i, 0)),
            pl.BlockSpec((1, d), lambda i: (0, 0)),
            pl.BlockSpec((LANES, d), lambda i: (COL_GA // LANES, 0)),
            pl.BlockSpec((LANES, nq), lambda i: (0, 0)),
            pl.BlockSpec((1, nq), lambda i: (0, 0)),
        ],
        out_specs=[
            pl.BlockSpec((tm, d), lambda i: (i, 0)),
            pl.BlockSpec((tm, nq), lambda i: (i, 0)),
        ],
        out_shape=[
            jax.ShapeDtypeStruct((s, d), BF16),
            jax.ShapeDtypeStruct((s, nq), F32),
        ],
        compiler_params=_cparams(("parallel",)),
        name="norm_loga",
    )(x, g, wga, w2, b)


def _in_proj_kernel(h_hbm, wt_hbm, o_ref, h_vmem, w_stage, wbf, sem, *, n_main, n_tiles, tn):
    j = pl.program_id(0)
    i = pl.program_id(1)
    tm = o_ref.shape[0]

    def w_copy(jj):
        row0 = jnp.where(jj < n_main, jj * tn, COL_GATES + (jj - n_main) * tn)
        return pltpu.make_async_copy(wt_hbm.at[pl.ds(pl.multiple_of(row0, 8), tn)], w_stage, sem.at[1])

    @pl.when((j == 0) & (i == 0))
    def _():
        h_copy = pltpu.make_async_copy(h_hbm, h_vmem, sem.at[0])
        h_copy.start()
        w_copy(0).start()
        h_copy.wait()

    @pl.when(i == 0)
    def _():
        w_copy(j).wait()
        wbf[...] = w_stage[...].astype(BF16)

        @pl.when(j + 1 < n_tiles)
        def _():
            w_copy(j + 1).start()

    h = h_vmem[pl.ds(pl.multiple_of(i * tm, tm), tm), :]
    o_ref[...] = _dot_nt(h, wbf[...]).astype(o_ref.dtype)


def _in_proj(h, w_t, tm, tn):
    s, d = h.shape
    n_main = MAIN_COLS // tn
    n_gate = GATE_COLS // tn
    n_tiles = n_main + n_gate
    kern = functools.partial(_in_proj_kernel, n_main=n_main, n_tiles=n_tiles, tn=tn)
    return pl.pallas_call(
        kern,
        grid=(n_tiles, s // tm),
        in_specs=[pl.BlockSpec(memory_space=pl.ANY), pl.BlockSpec(memory_space=pl.ANY)],
        out_specs=pl.BlockSpec((tm, tn), lambda j, i: (i, (j + n_gate) % n_tiles)),
        out_shape=jax.ShapeDtypeStruct((s, n_tiles * tn), BF16),
        scratch_shapes=[
            pltpu.VMEM((s, d), BF16),
            pltpu.VMEM((tn, d), F32),
            pltpu.VMEM((tn, d), BF16),
            pltpu.SemaphoreType.DMA((2,)),
        ],
        compiler_params=_cparams(("arbitrary", "arbitrary")),
        name="in_proj",
    )(h, w_t)


def _bucket_map():
    q_loc = np.arange(WINDOW)[:, None]
    k_loc = np.arange(2 * WINDOW)[None, :]
    n = np.maximum(q_loc + WINDOW - k_loc, 0)
    max_exact = N_BUCKETS // 2
    ratio = np.maximum(n, max_exact).astype(np.float32) / np.float32(max_exact)
    large = max_exact + (np.log(ratio) / np.float32(math.log(MAX_DISTANCE / max_exact))
                         * (N_BUCKETS - max_exact)).astype(np.int32)
    large = np.minimum(large, N_BUCKETS - 1)
    return np.where(n < max_exact, n, large).astype(np.int32).reshape(1, -1)


def _rel_bias_kernel(tab_ref, bucket_ref, o_ref):
    nb = tab_ref.shape[1]
    width = bucket_ref.shape[1]
    onehot = (lax.broadcasted_iota(jnp.int32, (nb, width), 0) == bucket_ref[...]).astype(BF16)
    hi, mid, lo = _split3(tab_ref[...])
    o_ref[...] = _dot(hi, onehot) + _dot(mid, onehot) + _dot(lo, onehot)


def _rel_bias(table_t, bucket):
    nh = table_t.shape[0]
    width = bucket.shape[1]
    return pl.pallas_call(
        _rel_bias_kernel,
        out_shape=jax.ShapeDtypeStruct((nh, width), F32),
        compiler_params=pltpu.CompilerParams(vmem_limit_bytes=VMEM_LIMIT),
        name="rel_bias",
    )(table_t, bucket)


def _swa_kernel(sink_ref, q_ref, kc_ref, kp_ref, vc_ref, vp_ref, bias_ref, o_ref):
    n = pl.program_id(0)
    w = WINDOW
    hd = HEAD_DIM
    gq = N_HEADS // N_KV_HEADS
    qi = lax.broadcasted_iota(jnp.int32, (w, 2 * w), 0)
    ki = lax.broadcasted_iota(jnp.int32, (w, 2 * w), 1)
    dist = qi + w - ki
    first_key = jnp.where(n > 0, 0, w)
    valid = (dist >= 0) & (dist < w) & (ki >= first_key)
    zeros = jnp.zeros((2 * w, hd), BF16)
    scale = HEAD_DIM ** -0.5
    for g in range(N_KV_HEADS):
        cs = slice(g * hd, (g + 1) * hd)
        k_g = jnp.concatenate([kp_ref[:, cs], kc_ref[:, cs]], axis=0)
        v_g = jnp.concatenate([vp_ref[:, cs], vc_ref[:, cs]], axis=0)
        k_lo = jnp.concatenate([k_g, zeros], axis=1)
        k_hi = jnp.concatenate([zeros, k_g], axis=1)
        v_lo = jnp.concatenate([v_g, zeros], axis=1)
        v_hi = jnp.concatenate([zeros, v_g], axis=1)
        for j in range(gq // 2):
            head = g * gq + 2 * j
            q_pair = q_ref[:, head * hd:(head + 2) * hd]
            acc = None
            for half, (k_pl, v_pl) in enumerate(((k_lo, v_lo), (k_hi, v_hi))):
                hh = head + half
                s = _dot_nt(q_pair, k_pl) * scale + bias_ref[hh]
                s = jnp.where(valid, s, NEG_INF)
                sink = sink_ref[hh]
                m = jnp.maximum(jnp.max(s, axis=-1, keepdims=True), sink)
                p = jnp.exp(s - m)
                denom = jnp.sum(p, axis=-1, keepdims=True) + jnp.exp(sink - m)
                probs = (p * (1.0 / denom)).astype(BF16)
                part = _dot(probs, v_pl)
                acc = part if acc is None else acc + part
            o_ref[:, head * hd:(head + 2) * hd] = acc.astype(o_ref.dtype)


def _swa(proj, sinks, bias):
    s = proj.shape[0]
    w = WINDOW
    nb = s // w
    kvw = N_KV_HEADS * HEAD_DIM
    dq = N_HEADS * HEAD_DIM
    prev = lambda n: jnp.maximum(n - 1, 0)
    return pl.pallas_call(
        _swa_kernel,
        grid=(nb,),
        in_specs=[
            pl.BlockSpec(memory_space=pltpu.SMEM),
            pl.BlockSpec((w, dq), lambda n: (n, COL_AQ // dq)),
            pl.BlockSpec((w, kvw), lambda n: (n, COL_AK // kvw)),
            pl.BlockSpec((w, kvw), lambda n: (prev(n), COL_AK // kvw)),
            pl.BlockSpec((w, kvw), lambda n: (n, COL_AV // kvw)),
            pl.BlockSpec((w, kvw), lambda n: (prev(n), COL_AV // kvw)),
            pl.BlockSpec((N_HEADS, w, 2 * w), lambda n: (0, 0, 0)),
        ],
        out_specs=pl.BlockSpec((w, dq), lambda n: (n, 0)),
        out_shape=jax.ShapeDtypeStruct((s, dq), BF16),
        compiler_params=_cparams(("parallel",)),
        name="swa",
    )(sinks, proj, proj, proj, proj, proj, bias)


def _gla_kernel(q_ref, k_ref, v_ref, gr_ref, la_ref, gn_ref, o_ref, state_ref):
    c = GLA_CHUNK
    rows = q_ref.shape[0]
    dk = q_ref.shape[1]
    dv = v_ref.shape[1]

    @pl.when(pl.program_id(1) == 0)
    def _():
        state_ref[...] = jnp.zeros_like(state_ref)

    la = la_ref[...]
    la_parts = _split3(la)
    ri = lax.broadcasted_iota(jnp.int32, (rows, rows), 0)
    ci = lax.broadcasted_iota(jnp.int32, (rows, rows), 1)
    tri = ((ri // c == ci // c) & (ri >= ci)).astype(BF16)
    cum = _dot(tri, la_parts[0]) + _dot(tri, la_parts[1]) + _dot(tri, la_parts[2])
    ones = jnp.ones((c, LANES), BF16)
    causal = (lax.broadcasted_iota(jnp.int32, (c, c), 0) >= lax.broadcasted_iota(jnp.int32, (c, c), 1))
    gn = gn_ref[...]
    for ch in range(rows // c):
        sl = slice(ch * c, (ch + 1) * c)
        cum_c = cum[sl]
        ref = cum_c[c // 2 - 1:c // 2]
        last = cum_c[c - 1:c]
        q = q_ref[sl, :].astype(F32) * (dk ** -0.5)
        k = k_ref[sl, :].astype(F32)
        v = v_ref[sl, :]
        q_intra = (q * jnp.exp(cum_c - ref)).astype(BF16)
        k_intra = (k * jnp.exp(ref - cum_c)).astype(BF16)
        a = jnp.where(causal, _dot_nt(q_intra, k_intra), 0.0)
        o = _dot(a.astype(BF16), v)
        state = state_ref[...]
        o = o + _dot((q * jnp.exp(cum_c)).astype(BF16), state.astype(BF16))
        k_dec = (k * jnp.exp(last - cum_c)).astype(BF16)
        upd = _dot_tn(k_dec, v)
        last_col = (_dot_tn(la_parts[0][sl], ones) + _dot_tn(la_parts[1][sl], ones)
                    + _dot_tn(la_parts[2][sl], ones))
        decay = jnp.exp(last_col)
        for jv in range(dv // LANES):
            ls = slice(jv * LANES, (jv + 1) * LANES)
            state_ref[:, ls] = state[:, ls] * decay + upd[:, ls]
        ms = jnp.mean(o * o, axis=-1, keepdims=True)
        on = o * lax.rsqrt(ms + RMS_EPS) * gn
        gr = gr_ref[sl, :].astype(F32)
        o_ref[sl, :] = (on * (gr * _sigmoid(gr))).astype(o_ref.dtype)


def _gla(proj, la, gn, rows):
    s = proj.shape[0]
    dk, dv = GLA_DK, GLA_DV
    return pl.pallas_call(
        _gla_kernel,
        grid=(GLA_HEADS, s // rows),
        in_specs=[
            pl.BlockSpec((rows, dk), lambda h, i: (i, COL_GQ // dk + h)),
            pl.BlockSpec((rows, dk), lambda h, i: (i, COL_GK // dk + h)),
            pl.BlockSpec((rows, dv), lambda h, i: (i, COL_GV // dv + h)),
            pl.BlockSpec((rows, dv), lambda h, i: (i, COL_GR // dv + h)),
            pl.BlockSpec((rows, dk), lambda h, i: (i, h)),
            pl.BlockSpec((1, dv), lambda h, i: (0, 0)),
        ],
        out_specs=pl.BlockSpec((rows, dv), lambda h, i: (i, h)),
        out_shape=jax.ShapeDtypeStruct((s, GLA_HEADS * dv), BF16),
        scratch_shapes=[pltpu.VMEM((dk, dv), F32)],
        compiler_params=_cparams(("parallel", "arbitrary")),
        name="gla",
    )(proj, proj, proj, proj, la, gn)


def _merge_out_kernel(oa_ref, og_ref, ga_ref, gg_ref, x_ref, wo_ref, g_ref, wr_ref,
                      x1_ref, h2_ref, lt_ref):
    merged = (_sigmoid(ga_ref[...].astype(F32)) * oa_ref[...].astype(F32)
              + _sigmoid(gg_ref[...].astype(F32)) * og_ref[...].astype(F32))
    x1 = x_ref[...] + _dot(merged.astype(BF16), wo_ref[...])
    x1_ref[...] = x1
    ms = jnp.mean(x1 * x1, axis=-1, keepdims=True)
    h2 = x1 * lax.rsqrt(ms + RMS_EPS) * g_ref[...]
    h2b = h2.astype(BF16)
    h2_ref[...] = h2b.reshape(h2_ref.shape)
    lt_ref[...] = _dot_nt(wr_ref[...], h2b)


def _merge_out(o_attn, o_gla, gates, x, wo, g, wr, tm):
    s, d = x.shape
    nr = wr.shape[0]
    row = lambda i: (i, 0)
    return pl.pallas_call(
        _merge_out_kernel,
        grid=(s // tm,),
        in_specs=[
            pl.BlockSpec((tm, d), row),
            pl.BlockSpec((tm, d), row),
            pl.BlockSpec((tm, d), lambda i: (i, 0)),
            pl.BlockSpec((tm, d), lambda i: (i, 1)),
            pl.BlockSpec((tm, d), row),
            pl.BlockSpec((d, d), lambda i: (0, 0)),
            pl.BlockSpec((1, d), lambda i: (0, 0)),
            pl.BlockSpec((nr, d), lambda i: (0, 0)),
        ],
        out_specs=[
            pl.BlockSpec((tm, d), row),
            pl.BlockSpec((tm, d // LANES, LANES), lambda i: (i, 0, 0)),
            pl.BlockSpec((nr, tm), lambda i: (0, i)),
        ],
        out_shape=[
            jax.ShapeDtypeStruct((s, d), F32),
            jax.ShapeDtypeStruct((s, d // LANES, LANES), BF16),
            jax.ShapeDtypeStruct((nr, s), F32),
        ],
        compiler_params=_cparams(("parallel",)),
        name="merge_out",
    )(o_attn, o_gla, gates, gates, x, wo, g, wr)


ROUTER_ROWS = 8 + N_EXPERTS


def _route_kernel(lt_ref, bg_ref, be_ref, idx_ref, gate_ref, cnt_ref, carry_ref):
    tb = lt_ref.shape[1]
    epg = EXPERTS_PER_GROUP

    @pl.when(pl.program_id(0) == 0)
    def _():
        carry_ref[...] = jnp.zeros_like(carry_ref)

    logits = lt_ref[...]
    gl = logits[0:N_GROUPS]
    g_exp = jnp.exp(gl - jnp.max(gl, axis=0, keepdims=True))
    g_prob = g_exp / jnp.sum(g_exp, axis=0, keepdims=True)
    gb = gl + bg_ref[0:N_GROUPS]
    rowg = lax.broadcasted_iota(jnp.int32, (N_GROUPS, tb), 0)
    g_idx = jnp.min(jnp.where(gb == jnp.max(gb, axis=0, keepdims=True), rowg, N_GROUPS),
                    axis=0, keepdims=True)
    p_group = jnp.sum(jnp.where(rowg == g_idx, g_prob, 0.0), axis=0, keepdims=True)

    el = logits[8:8 + epg]
    eb = jnp.broadcast_to(be_ref[0:epg], (epg, tb))
    for g in range(1, N_GROUPS):
        pick = g_idx == g
        el = jnp.where(pick, logits[8 + g * epg:8 + (g + 1) * epg], el)
        eb = jnp.where(pick, be_ref[g * epg:(g + 1) * epg], eb)
    e_exp = jnp.exp(el - jnp.max(el, axis=0, keepdims=True))
    e_prob = e_exp / jnp.sum(e_exp, axis=0, keepdims=True)
    score = el + eb
    rowe = lax.broadcasted_iota(jnp.int32, (epg, tb), 0)
    i1 = jnp.min(jnp.where(score == jnp.max(score, axis=0, keepdims=True), rowe, epg),
                 axis=0, keepdims=True)
    score2 = jnp.where(rowe == i1, -jnp.inf, score)
    i2 = jnp.min(jnp.where(score2 == jnp.max(score2, axis=0, keepdims=True), rowe, epg),
                 axis=0, keepdims=True)
    q1 = jnp.sum(jnp.where(rowe == i1, e_prob, 0.0), axis=0, keepdims=True)
    q2 = jnp.sum(jnp.where(rowe == i2, e_prob, 0.0), axis=0, keepdims=True)
    qs = q1 + q2
    gate1 = p_group * q1 / qs
    gate2 = p_group * q2 / qs
    e1 = g_idx * epg + i1
    e2 = g_idx * epg + i2

    rowx = lax.broadcasted_iota(jnp.int32, (N_EXPERTS, tb), 0)
    hit1 = rowx == e1
    hit2 = rowx == e2
    member = (hit1 | hit2).astype(BF16)
    before = (lax.broadcasted_iota(jnp.int32, (tb, tb), 0)
              < lax.broadcasted_iota(jnp.int32, (tb, tb), 1)).astype(BF16)
    carry = carry_ref[...]
    count = _dot(member, before) + carry[:, 0:1]
    r1 = jnp.sum(jnp.where(hit1, count, 0.0), axis=0, keepdims=True).astype(jnp.int32)
    r2 = jnp.sum(jnp.where(hit2, count, 0.0), axis=0, keepdims=True).astype(jnp.int32)
    carry = carry + jnp.sum(member.astype(F32), axis=1, keepdims=True)
    carry_ref[...] = carry
    cnt_ref[...] = carry

    row8 = lax.broadcasted_iota(jnp.int32, (8, tb), 0)
    idx_ref[...] = jnp.where(row8 == 0, e1, jnp.where(row8 == 1, e2,
                             jnp.where(row8 == 2, r1, jnp.where(row8 == 3, r2, 0))))
    gate_ref[...] = jnp.where(row8 == 0, gate1, jnp.where(row8 == 1, gate2, 0.0))


def _route(lt, bg, be, tb):
    nr, s = lt.shape
    return pl.pallas_call(
        _route_kernel,
        grid=(s // tb,),
        in_specs=[
            pl.BlockSpec((nr, tb), lambda i: (0, i)),
            pl.BlockSpec((8, 1), lambda i: (0, 0)),
            pl.BlockSpec((N_EXPERTS, 1), lambda i: (0, 0)),
        ],
        out_specs=[
            pl.BlockSpec((8, tb), lambda i: (0, i)),
            pl.BlockSpec((8, tb), lambda i: (0, i)),
            pl.BlockSpec((N_EXPERTS, LANES), lambda i: (0, 0)),
        ],
        out_shape=[
            jax.ShapeDtypeStruct((8, s), jnp.int32),
            jax.ShapeDtypeStruct((8, s), F32),
            jax.ShapeDtypeStruct((N_EXPERTS, LANES), F32),
        ],
        scratch_shapes=[pltpu.VMEM((N_EXPERTS, LANES), F32)],
        compiler_params=_cparams(("arbitrary",)),
        name="route",
    )(lt, bg, be)


def _expert_kernel(be_ref, nx_ref, nu_ref, tok_ref, h2_hbm, wg_hbm, wu_hbm, wd_hbm, y_ref,
                   xbuf, xsem, wgf, wuf, wdf, wsem, wgb, wub, wdb):
    b = pl.program_id(0)
    n_used = nu_ref[0]
    rows = MOE_BLOCK
    slot = b % 2

    def row_copy(blk, r, s_):
        t = tok_ref[blk * rows + r]
        return pltpu.make_async_copy(h2_hbm.at[t], xbuf.at[s_, r], xsem.at[s_])

    def wait_rows(s_):
        pltpu.make_async_copy(h2_hbm.at[pl.ds(0, rows)], xbuf.at[s_], xsem.at[s_]).wait()

    def weight_copies(e):
        return (pltpu.make_async_copy(wg_hbm.at[e], wgf, wsem.at[0]),
                pltpu.make_async_copy(wu_hbm.at[e], wuf, wsem.at[1]),
                pltpu.make_async_copy(wd_hbm.at[e], wdf, wsem.at[2]))

    @pl.when(b == 0)
    def _():
        for cp in weight_copies(be_ref[0]):
            cp.start()

        def body(r, carry):
            row_copy(0, r, 0).start()
            return carry
        lax.fori_loop(0, rows, body, 0)

    @pl.when(b < n_used)
    def _():
        wait_rows(slot)
        first_of_expert = (b == 0) | (be_ref[b] != be_ref[jnp.maximum(b - 1, 0)])

        @pl.when(first_of_expert)
        def _():
            for cp in weight_copies(be_ref[b]):
                cp.wait()
            wgb[...] = wgf[...].astype(BF16)
            wub[...] = wuf[...].astype(BF16)
            wdb[...] = wdf[...].astype(BF16)
            nxt = nx_ref[b]

            @pl.when(nxt >= 0)
            def _():
                for cp in weight_copies(nxt):
                    cp.start()

        xb = xbuf[slot].reshape(rows, wgb.shape[0])
        for r in range(rows):
            row_copy(b + 1, r, 1 - slot).start()
        hg = _dot(xb, wgb[...])
        hu = _dot(xb, wub[...])
        act = (hg * _sigmoid(hg)) * hu
        y_ref[...] = _dot(act.astype(BF16), wdb[...]).astype(BF16).reshape(y_ref.shape)

    @pl.when(b == n_used)
    def _():
        wait_rows(slot)

    @pl.when(b >= n_used)
    def _():
        y_ref[...] = jnp.zeros_like(y_ref)


def _experts(block_e, next_e, n_used, row_token, h2, wg, wu, wd):
    n_rows = row_token.shape[0]
    n_blocks = n_rows // MOE_BLOCK
    _, d, de = wg.shape
    grid_spec = pltpu.PrefetchScalarGridSpec(
        num_scalar_prefetch=4,
        grid=(n_blocks,),
        in_specs=[pl.BlockSpec(memory_space=pl.ANY)] * 4,
        out_specs=pl.BlockSpec((MOE_BLOCK, d // LANES, LANES), lambda b, *_: (b, 0, 0)),
        scratch_shapes=[
            pltpu.VMEM((2, MOE_BLOCK, d // LANES, LANES), BF16),
            pltpu.SemaphoreType.DMA((2,)),
            pltpu.VMEM((d, de), F32),
            pltpu.VMEM((d, de), F32),
            pltpu.VMEM((de, d), F32),
            pltpu.SemaphoreType.DMA((3,)),
            pltpu.VMEM((d, de), BF16),
            pltpu.VMEM((d, de), BF16),
            pltpu.VMEM((de, d), BF16),
        ],
    )
    return pl.pallas_call(
        _expert_kernel,
        grid_spec=grid_spec,
        out_shape=jax.ShapeDtypeStruct((n_rows, d // LANES, LANES), BF16),
        compiler_params=_cparams(("arbitrary",)),
        name="experts",
    )(block_e, next_e, n_used, row_token, h2, wg, wu, wd)


def _combine_kernel(dest_ref, yb_hbm, gate_ref, x1_ref, g_ref, o_ref, ybuf_even, ybuf_odd, sem):
    i = pl.program_id(0)
    n_steps = pl.num_programs(0)
    tm = x1_ref.shape[0]
    n_tok = n_steps * tm
    bufs = (ybuf_even, ybuf_odd)

    def row_copy(step, k, r, parity):
        row = dest_ref[k * n_tok + step * tm + r]
        return pltpu.make_async_copy(yb_hbm.at[row], bufs[parity].at[k, r], sem.at[parity])

    def wait_rows(parity):
        for k in range(TOP_K):
            pltpu.make_async_copy(yb_hbm.at[pl.ds(0, tm)], bufs[parity].at[k], sem.at[parity]).wait()

    def compute(parity):
        gates = gate_ref[...]
        y0 = bufs[parity][0].reshape(x1_ref.shape).astype(F32)
        y1 = bufs[parity][1].reshape(x1_ref.shape).astype(F32)
        y = y0 * gates[:, 0:1] + y1 * gates[:, 1:2]
        x2 = x1_ref[...] + y
        ms = jnp.mean(x2 * x2, axis=-1, keepdims=True)
        o_ref[...] = x2 * lax.rsqrt(ms + RMS_EPS) * g_ref[...]

    @pl.when(i == 0)
    def _():
        def body(r, carry):
            row_copy(0, 0, r, 0).start()
            row_copy(0, 1, r, 0).start()
            return carry
        lax.fori_loop(0, tm, body, 0)

    for parity in range(2):
        @pl.when((i % 2 == parity) & (i + 1 < n_steps))
        def _():
            wait_rows(parity)
            for r in range(tm):
                for k in range(TOP_K):
                    row_copy(i + 1, k, r, 1 - parity).start()
            compute(parity)

        @pl.when((i % 2 == parity) & (i + 1 == n_steps))
        def _():
            wait_rows(parity)
            compute(parity)


def _combine(dest, yb, gates, x1, g, tm):
    s, d = x1.shape
    grid_spec = pltpu.PrefetchScalarGridSpec(
        num_scalar_prefetch=1,
        grid=(s // tm,),
        in_specs=[
            pl.BlockSpec(memory_space=pl.ANY),
            pl.BlockSpec((tm, TOP_K), lambda i, dest: (i, 0)),
            pl.BlockSpec((tm, d), lambda i, dest: (i, 0)),
            pl.BlockSpec((1, d), lambda i, dest: (0, 0)),
        ],
        out_specs=pl.BlockSpec((tm, d), lambda i, dest: (i, 0)),
        scratch_shapes=[
            pltpu.VMEM((TOP_K, tm, d // LANES, LANES), BF16),
            pltpu.VMEM((TOP_K, tm, d // LANES, LANES), BF16),
            pltpu.SemaphoreType.DMA((2,)),
        ],
    )
    return pl.pallas_call(
        _combine_kernel,
        grid_spec=grid_spec,
        out_shape=jax.ShapeDtypeStruct((s, d), F32),
        compiler_params=_cparams(("arbitrary",)),
        name="combine",
    )(dest, yb, gates, x1, g)


def _tile(n, pref):
    return pref if n % pref == 0 else n


def kernel(x, norm_mix_g, w_in, w_gla_a2, b_gla_a, gla_norm_g, attn_sinks, rel_bias_table, w_out,
           norm_ffn_g, w_router_group, b_router_group, w_router_expert, b_router_expert,
           w_expert_gate, w_expert_up, w_expert_down, norm_final_g):
    b, s, d = x.shape
    assert b == 1 and w_in.shape[0] == 1, "single batch, single layer"
    assert (s * TOP_K) % MOE_BLOCK == 0
    x2d = x.reshape(s, d)
    w_in_t = w_in[0].T

    w2 = jnp.pad(w_gla_a2[0], ((0, LANES - GLA_LOWRANK), (0, 0))).astype(BF16)
    h, log_a = _norm_loga(x2d, norm_mix_g[0].reshape(1, d), w_in_t, w2, b_gla_a[0].reshape(1, -1),
                          _tile(s, 512))
    proj = _in_proj(h, w_in_t, _tile(s, 1024), 512)

    bias = _rel_bias(rel_bias_table.T, jnp.asarray(_bucket_map()))
    bias = bias.reshape(N_HEADS, WINDOW, 2 * WINDOW)
    o_attn = _swa(proj, attn_sinks[0], bias)
    o_gla = _gla(proj, log_a, gla_norm_g[0].reshape(1, -1), _tile(s, 256))

    wr = jnp.zeros((ROUTER_ROWS, d), F32)
    wr = wr.at[0:N_GROUPS].set(w_router_group[0].T).at[8:].set(w_router_expert[0].T).astype(BF16)
    x1, h2, logits_t = _merge_out(o_attn, o_gla, proj, x2d, w_out[0].astype(BF16),
                                  norm_ffn_g[0].reshape(1, d), wr, _tile(s, 256))

    bg = jnp.pad(b_router_group[0], (0, 8 - N_GROUPS)).reshape(8, 1)
    idx, gate, counts = _route(logits_t, bg, b_router_expert[0].reshape(N_EXPERTS, 1), _tile(s, 512))
    counts = counts[:, 0].astype(jnp.int32)
    padded = (counts + MOE_BLOCK - 1) // MOE_BLOCK * MOE_BLOCK
    pend = jnp.cumsum(padded)
    pstart = pend - padded
    expert = idx[0:TOP_K]
    eids = jnp.arange(N_EXPERTS, dtype=jnp.int32)[:, None, None]
    dest = jnp.sum(jnp.where(expert[None] == eids, pstart[:, None, None], 0), axis=0) + idx[TOP_K:2 * TOP_K]
    n_pairs = s * TOP_K
    n_blocks = (n_pairs + N_EXPERTS * (MOE_BLOCK - 1) + MOE_BLOCK - 1) // MOE_BLOCK
    n_used = (pend[-1] // MOE_BLOCK).reshape(1)
    block_start = jnp.arange(n_blocks, dtype=jnp.int32) * MOE_BLOCK
    block_e = jnp.minimum(jnp.sum((pend[None, :] <= block_start[:, None]).astype(jnp.int32), axis=1),
                          N_EXPERTS - 1)
    dest_flat = dest.reshape(-1)
    tok_of_pair = jnp.tile(jnp.arange(s, dtype=jnp.int32), TOP_K)
    row_token = jnp.zeros((n_blocks * MOE_BLOCK,), jnp.int32).at[dest_flat].set(tok_of_pair)

    e_col = jnp.arange(N_EXPERTS, dtype=jnp.int32)[:, None]
    e_row = jnp.arange(N_EXPERTS, dtype=jnp.int32)[None, :]
    later = (e_row > e_col) & (counts > 0)[None, :]
    next_nonempty = jnp.min(jnp.where(later, e_row, N_EXPERTS), axis=1)
    next_nonempty = jnp.where(next_nonempty < N_EXPERTS, next_nonempty, -1)
    next_e = jnp.sum(jnp.where(block_e[:, None] == e_row, next_nonempty[None, :], 0), axis=1)
    yb = _experts(block_e, next_e, n_used, row_token, h2, w_expert_gate[0], w_expert_up[0], w_expert_down[0])
    out = _combine(dest_flat, yb, gate[0:TOP_K].T, x1, norm_final_g.reshape(1, d), _tile(s, 128))
    return out.reshape(b, s, d)
```

```python
import functools
import math

import numpy as np
import jax
import jax.numpy as jnp
from jax import lax
from jax.experimental import pallas as pl
from jax.experimental.pallas import tpu as pltpu

F32 = jnp.float32
BF16 = jnp.bfloat16

N_HEADS = 32
N_KV_HEADS = 4
HEAD_DIM = 64
WINDOW = 128
N_BUCKETS = 32
MAX_DISTANCE = 128
GLA_HEADS = 4
GLA_DK = 256
GLA_DV = 512
GLA_LOWRANK = 16
GLA_TAU = 16.0
GLA_CHUNK = 64
N_GROUPS = 4
EXPERTS_PER_GROUP = 8
N_EXPERTS = 32
TOP_K = 2
MOE_BLOCK = 128
RMS_EPS = 1e-6
NEG_INF = -1e30

COL_GA = 8704
COL_GATES = 8720
MAIN_COLS = 8704
GATE_COLS = 4096
COL_AQ = GATE_COLS + 0
COL_AK = GATE_COLS + 2048
COL_AV = GATE_COLS + 2304
COL_GQ = GATE_COLS + 2560
COL_GK = GATE_COLS + 3584
COL_GV = GATE_COLS + 4608
COL_GR = GATE_COLS + 6656

LANES = 128
VMEM_LIMIT = 56 * 1024 * 1024


def _cparams(sem, vmem=VMEM_LIMIT):
    return pltpu.CompilerParams(dimension_semantics=sem, vmem_limit_bytes=vmem)


def _split3(x):
    hi = x.astype(BF16)
    r1 = x - hi.astype(F32)
    mid = r1.astype(BF16)
    lo = (r1 - mid.astype(F32)).astype(BF16)
    return hi, mid, lo


def _sigmoid(x):
    return 1.0 / (1.0 + jnp.exp(-x))


def _dot(a, b):
    return jnp.dot(a, b, preferred_element_type=F32)


def _dot_nt(a, b):
    return lax.dot_general(a, b, (((1,), (1,)), ((), ())), preferred_element_type=F32)


def _dot_tn(a, b):
    return lax.dot_general(a, b, (((0,), (0,)), ((), ())), preferred_element_type=F32)


def _norm_loga_kernel(x_ref, g_ref, wga_ref, w2_ref, b_ref, h_ref, la_ref):
    x = x_ref[...]
    ms = jnp.mean(x * x, axis=-1, keepdims=True)
    hb = (x * lax.rsqrt(ms + RMS_EPS) * g_ref[...]).astype(BF16)
    h_ref[...] = hb
    row = lax.broadcasted_iota(jnp.int32, wga_ref.shape, 0)
    wga = jnp.where(row < GLA_LOWRANK, wga_ref[...], 0.0).astype(BF16)
    ga = _dot_nt(hb, wga)
    z = _dot(ga.astype(BF16), w2_ref[...]) + b_ref[...]
    la_ref[...] = (jnp.minimum(z, 0.0) - jnp.log1p(jnp.exp(-jnp.abs(z)))) * (1.0 / GLA_TAU)


def _norm_loga(x, g, wga, w2, b, tm):
    s, d = x.shape
    nq = w2.shape[1]
    return pl.pallas_call(
        _norm_loga_kernel,
        grid=(s // tm,),
        in_specs=[
            pl.BlockSpec((tm, d), lambda i: (i, 0)),
            pl.BlockSpec((1, d), lambda i: (0, 0)),
            pl.BlockSpec((LANES, d), lambda i: (COL_GA // LANES, 0)),
            pl.BlockSpec((LANES, nq), lambda i: (0, 0)),
            pl.BlockSpec((1, nq), lambda i: (0, 0)),
        ],
        out_specs=[
            pl.BlockSpec((tm, d), lambda i: (i, 0)),
            pl.BlockSpec((tm, nq), lambda i: (i, 0)),
        ],
        out_shape=[
            jax.ShapeDtypeStruct((s, d), BF16),
            jax.ShapeDtypeStruct((s, nq), F32),
        ],
        compiler_params=_cparams(("parallel",)),
        name="norm_loga",
    )(x, g, wga, w2, b)


def _in_proj_kernel(h_hbm, wt_hbm, o_ref, h_vmem, w_stage, wbf, sem, *, n_main, n_tiles, tn):
    j = pl.program_id(0)
    i = pl.program_id(1)
    tm = o_ref.shape[0]

    def w_copy(jj):
        row0 = jnp.where(jj < n_main, jj * tn, COL_GATES + (jj - n_main) * tn)
        return pltpu.make_async_copy(wt_hbm.at[pl.ds(pl.multiple_of(row0, 8), tn)], w_stage, sem.at[1])

    @pl.when((j == 0) & (i == 0))
    def _():
        h_copy = pltpu.make_async_copy(h_hbm, h_vmem, sem.at[0])
        h_copy.start()
        w_copy(0).start()
        h_copy.wait()

    @pl.when(i == 0)
    def _():
        w_copy(j).wait()
        wbf[...] = w_stage[...].astype(BF16)

        @pl.when(j + 1 < n_tiles)
        def _():
            w_copy(j + 1).start()

    h = h_vmem[pl.ds(pl.multiple_of(i * tm, tm), tm), :]
    o_ref[...] = _dot_nt(h, wbf[...]).astype(o_ref.dtype)


def _in_proj(h, w_t, tm, tn):
    s, d = h.shape
    n_main = MAIN_COLS // tn
    n_gate = GATE_COLS // tn
    n_tiles = n_main + n_gate
    kern = functools.partial(_in_proj_kernel, n_main=n_main, n_tiles=n_tiles, tn=tn)
    return pl.pallas_call(
        kern,
        grid=(n_tiles, s // tm),
        in_specs=[pl.BlockSpec(memory_space=pl.ANY), pl.BlockSpec(memory_space=pl.ANY)],
        out_specs=pl.BlockSpec((tm, tn), lambda j, i: (i, (j + n_gate) % n_tiles)),
        out_shape=jax.ShapeDtypeStruct((s, n_tiles * tn), BF16),
        scratch_shapes=[
            pltpu.VMEM((s, d), BF16),
            pltpu.VMEM((tn, d), F32),
            pltpu.VMEM((tn, d), BF16),
            pltpu.SemaphoreType.DMA((2,)),
        ],
        compiler_params=_cparams(("arbitrary", "arbitrary")),
        name="in_proj",
    )(h, w_t)


def _bucket_map():
    q_loc = np.arange(WINDOW)[:, None]
    k_loc = np.arange(2 * WINDOW)[None, :]
    n = np.maximum(q_loc + WINDOW - k_loc, 0)
    max_exact = N_BUCKETS // 2
    ratio = np.maximum(n, max_exact).astype(np.float32) / np.float32(max_exact)
    large = max_exact + (np.log(ratio) / np.float32(math.log(MAX_DISTANCE / max_exact))
                         * (N_BUCKETS - max_exact)).astype(np.int32)
    large = np.minimum(large, N_BUCKETS - 1)
    return np.where(n < max_exact, n, large).astype(np.int32).reshape(1, -1)


def _rel_bias_kernel(tab_ref, bucket_ref, o_ref):
    nb = tab_ref.shape[1]
    width = bucket_ref.shape[1]
    onehot = (lax.broadcasted_iota(jnp.int32, (nb, width), 0) == bucket_ref[...]).astype(BF16)
    hi, mid, lo = _split3(tab_ref[...])
    o_ref[...] = _dot(hi, onehot) + _dot(mid, onehot) + _dot(lo, onehot)


def _rel_bias(table_t, bucket):
    nh = table_t.shape[0]
    width = bucket.shape[1]
    return pl.pallas_call(
        _rel_bias_kernel,
        out_shape=jax.ShapeDtypeStruct((nh, width), F32),
        compiler_params=pltpu.CompilerParams(vmem_limit_bytes=VMEM_LIMIT),
        name="rel_bias",
    )(table_t, bucket)


def _swa_kernel(sink_ref, q_ref, kc_ref, kp_ref, vc_ref, vp_ref, bias_ref, o_ref):
    n = pl.program_id(0)
    w = WINDOW
    hd = HEAD_DIM
    gq = N_HEADS // N_KV_HEADS
    qi = lax.broadcasted_iota(jnp.int32, (w, 2 * w), 0)
    ki = lax.broadcasted_iota(jnp.int32, (w, 2 * w), 1)
    dist = qi + w - ki
    first_key = jnp.where(n > 0, 0, w)
    valid = (dist >= 0) & (dist < w) & (ki >= first_key)
    zeros = jnp.zeros((2 * w, hd), BF16)
    scale = HEAD_DIM ** -0.5
    for g in range(N_KV_HEADS):
        cs = slice(g * hd, (g + 1) * hd)
        k_g = jnp.concatenate([kp_ref[:, cs], kc_ref[:, cs]], axis=0)
        v_g = jnp.concatenate([vp_ref[:, cs], vc_ref[:, cs]], axis=0)
        k_lo = jnp.concatenate([k_g, zeros], axis=1)
        k_hi = jnp.concatenate([zeros, k_g], axis=1)
        v_lo = jnp.concatenate([v_g, zeros], axis=1)
        v_hi = jnp.concatenate([zeros, v_g], axis=1)
        for j in range(gq // 2):
            head = g * gq + 2 * j
            q_pair = q_ref[:, head * hd:(head + 2) * hd]
            acc = None
            for half, (k_pl, v_pl) in enumerate(((k_lo, v_lo), (k_hi, v_hi))):
                hh = head + half
                s = _dot_nt(q_pair, k_pl) * scale + bias_ref[hh]
                s = jnp.where(valid, s, NEG_INF)
                sink = sink_ref[hh]
                m = jnp.maximum(jnp.max(s, axis=-1, keepdims=True), sink)
                p = jnp.exp(s - m)
                denom = jnp.sum(p, axis=-1, keepdims=True) + jnp.exp(sink - m)
                probs = (p * (1.0 / denom)).astype(BF16)
                part = _dot(probs, v_pl)
                acc = part if acc is None else acc + part
            o_ref[:, head * hd:(head + 2) * hd] = acc.astype(o_ref.dtype)


def _swa(proj, sinks, bias):
    s = proj.shape[0]
    w = WINDOW
    nb = s // w
    kvw = N_KV_HEADS * HEAD_DIM
    dq = N_HEADS * HEAD_DIM
    prev = lambda n: jnp.maximum(n - 1, 0)
    return pl.pallas_call(
        _swa_kernel,
        grid=(nb,),
        in_specs=[
            pl.BlockSpec(memory_space=pltpu.SMEM),
            pl.BlockSpec((w, dq), lambda n: (n, COL_AQ // dq)),
            pl.BlockSpec((w, kvw), lambda n: (n, COL_AK // kvw)),
            pl.BlockSpec((w, kvw), lambda n: (prev(n), COL_AK // kvw)),
            pl.BlockSpec((w, kvw), lambda n: (n, COL_AV // kvw)),
            pl.BlockSpec((w, kvw), lambda n: (prev(n), COL_AV // kvw)),
            pl.BlockSpec((N_HEADS, w, 2 * w), lambda n: (0, 0, 0)),
        ],
        out_specs=pl.BlockSpec((w, dq), lambda n: (n, 0)),
        out_shape=jax.ShapeDtypeStruct((s, dq), BF16),
        compiler_params=_cparams(("parallel",)),
        name="swa",
    )(sinks, proj, proj, proj, proj, proj, bias)


def _gla_kernel(q_ref, k_ref, v_ref, gr_ref, la_ref, gn_ref, o_ref, state_ref):
    c = GLA_CHUNK
    rows = q_ref.shape[0]
    dk = q_ref.shape[1]
    dv = v_ref.shape[1]

    @pl.when(pl.program_id(1) == 0)
    def _():
        state_ref[...] = jnp.zeros_like(state_ref)

    la = la_ref[...]
    la_parts = _split3(la)
    ri = lax.broadcasted_iota(jnp.int32, (rows, rows), 0)
    ci = lax.broadcasted_iota(jnp.int32, (rows, rows), 1)
    tri = ((ri // c == ci // c) & (ri >= ci)).astype(BF16)
    cum = _dot(tri, la_parts[0]) + _dot(tri, la_parts[1]) + _dot(tri, la_parts[2])
    ones = jnp.ones((c, LANES), BF16)
    causal = (lax.broadcasted_iota(jnp.int32, (c, c), 0) >= lax.broadcasted_iota(jnp.int32, (c, c), 1))
    gn = gn_ref[...]
    for ch in range(rows // c):
        sl = slice(ch * c, (ch + 1) * c)
        cum_c = cum[sl]
        ref = cum_c[c // 2 - 1:c // 2]
        last = cum_c[c - 1:c]
        q = q_ref[sl, :].astype(F32) * (dk ** -0.5)
        k = k_ref[sl, :].astype(F32)
        v = v_ref[sl, :]
        q_intra = (q * jnp.exp(cum_c - ref)).astype(BF16)
        k_intra = (k * jnp.exp(ref - cum_c)).astype(BF16)
        a = jnp.where(causal, _dot_nt(q_intra, k_intra), 0.0)
        o = _dot(a.astype(BF16), v)
        state = state_ref[...]
        o = o + _dot((q * jnp.exp(cum_c)).astype(BF16), state.astype(BF16))
        k_dec = (k * jnp.exp(last - cum_c)).astype(BF16)
        upd = _dot_tn(k_dec, v)
        last_col = (_dot_tn(la_parts[0][sl], ones) + _dot_tn(la_parts[1][sl], ones)
                    + _dot_tn(la_parts[2][sl], ones))
        decay = jnp.exp(last_col)
        for jv in range(dv // LANES):
            ls = slice(jv * LANES, (jv + 1) * LANES)
            state_ref[:, ls] = state[:, ls] * decay + upd[:, ls]
        ms = jnp.mean(o * o, axis=-1, keepdims=True)
        on = o * lax.rsqrt(ms + RMS_EPS) * gn
        gr = gr_ref[sl, :].astype(F32)
        o_ref[sl, :] = (on * (gr * _sigmoid(gr))).astype(o_ref.dtype)


def _gla(proj, la, gn, rows):
    s = proj.shape[0]
    dk, dv = GLA_DK, GLA_DV
    return pl.pallas_call(
        _gla_kernel,
        grid=(GLA_HEADS, s // rows),
        in_specs=[
            pl.BlockSpec((rows, dk), lambda h, i: (i, COL_GQ // dk + h)),
            pl.BlockSpec((rows, dk), lambda h, i: (i, COL_GK // dk + h)),
            pl.BlockSpec((rows, dv), lambda h, i: (i, COL_GV // dv + h)),
            pl.BlockSpec((rows, dv), lambda h, i: (i, COL_GR // dv + h)),
            pl.BlockSpec((rows, dk), lambda h, i: (i, h)),
            pl.BlockSpec((1, dv), lambda h, i: (0, 0)),
        ],
        out_specs=pl.BlockSpec((rows, dv), lambda h, i: (i, h)),
        out_shape=jax.ShapeDtypeStruct((s, GLA_HEADS * dv), BF16),
        scratch_shapes=[pltpu.VMEM((dk, dv), F32)],
        compiler_params=_cparams(("parallel", "arbitrary")),
        name="gla",
    )(proj, proj, proj, proj, la, gn)


def _merge_out_kernel(oa_ref, og_ref, ga_ref, gg_ref, x_ref, wo_ref, g_ref, wr_ref,
                      x1_ref, h2_ref, lt_ref):
    merged = (_sigmoid(ga_ref[...].astype(F32)) * oa_ref[...].astype(F32)
              + _sigmoid(gg_ref[...].astype(F32)) * og_ref[...].astype(F32))
    x1 = x_ref[...] + _dot(merged.astype(BF16), wo_ref[...])
    x1_ref[...] = x1
    ms = jnp.mean(x1 * x1, axis=-1, keepdims=True)
    h2 = x1 * lax.rsqrt(ms + RMS_EPS) * g_ref[...]
    h2b = h2.astype(BF16)
    h2_ref[...] = h2b.reshape(h2_ref.shape)
    lt_ref[...] = _dot_nt(wr_ref[...], h2b)


def _merge_out(o_attn, o_gla, gates, x, wo, g, wr, tm):
    s, d = x.shape
    nr = wr.shape[0]
    row = lambda i: (i, 0)
    return pl.pallas_call(
        _merge_out_kernel,
        grid=(s // tm,),
        in_specs=[
            pl.BlockSpec((tm, d), row),
            pl.BlockSpec((tm, d), row),
            pl.BlockSpec((tm, d), lambda i: (i, 0)),
            pl.BlockSpec((tm, d), lambda i: (i, 1)),
            pl.BlockSpec((tm, d), row),
            pl.BlockSpec((d, d), lambda i: (0, 0)),
            pl.BlockSpec((1, d), lambda i: (0, 0)),
            pl.BlockSpec((nr, d), lambda i: (0, 0)),
        ],
        out_specs=[
            pl.BlockSpec((tm, d), row),
            pl.BlockSpec((tm, d // LANES, LANES), lambda i: (i, 0, 0)),
            pl.BlockSpec((nr, tm), lambda i: (0, i)),
        ],
        out_shape=[
            jax.ShapeDtypeStruct((s, d), F32),
            jax.ShapeDtypeStruct((s, d // LANES, LANES), BF16),
            jax.ShapeDtypeStruct((nr, s), F32),
        ],
        compiler_params=_cparams(("parallel",)),
        name="merge_out",
    )(o_attn, o_gla, gates, gates, x, wo, g, wr)


ROUTER_ROWS = 8 + N_EXPERTS


def _route_kernel(lt_ref, bg_ref, be_ref, idx_ref, gate_ref, cnt_ref, carry_ref):
    tb = lt_ref.shape[1]
    epg = EXPERTS_PER_GROUP

    @pl.when(pl.program_id(0) == 0)
    def _():
        carry_ref[...] = jnp.zeros_like(carry_ref)

    logits = lt_ref[...]
    gl = logits[0:N_GROUPS]
    g_exp = jnp.exp(gl - jnp.max(gl, axis=0, keepdims=True))
    g_prob = g_exp / jnp.sum(g_exp, axis=0, keepdims=True)
    gb = gl + bg_ref[0:N_GROUPS]
    rowg = lax.broadcasted_iota(jnp.int32, (N_GROUPS, tb), 0)
    g_idx = jnp.min(jnp.where(gb == jnp.max(gb, axis=0, keepdims=True), rowg, N_GROUPS),
                    axis=0, keepdims=True)
    p_group = jnp.sum(jnp.where(rowg == g_idx, g_prob, 0.0), axis=0, keepdims=True)

    el = logits[8:8 + epg]
    eb = jnp.broadcast_to(be_ref[0:epg], (epg, tb))
    for g in range(1, N_GROUPS):
        pick = g_idx == g
        el = jnp.where(pick, logits[8 + g * epg:8 + (g + 1) * epg], el)
        eb = jnp.where(pick, be_ref[g * epg:(g + 1) * epg], eb)
    e_exp = jnp.exp(el - jnp.max(el, axis=0, keepdims=True))
    e_prob = e_exp / jnp.sum(e_exp, axis=0, keepdims=True)
    score = el + eb
    rowe = lax.broadcasted_iota(jnp.int32, (epg, tb), 0)
    i1 = jnp.min(jnp.where(score == jnp.max(score, axis=0, keepdims=True), rowe, epg),
                 axis=0, keepdims=True)
    score2 = jnp.where(rowe == i1, -jnp.inf, score)
    i2 = jnp.min(jnp.where(score2 == jnp.max(score2, axis=0, keepdims=True), rowe, epg),
                 axis=0, keepdims=True)
    q1 = jnp.sum(jnp.where(rowe == i1, e_prob, 0.0), axis=0, keepdims=True)
    q2 = jnp.sum(jnp.where(rowe == i2, e_prob, 0.0), axis=0, keepdims=True)
    qs = q1 + q2
    gate1 = p_group * q1 / qs
    gate2 = p_group * q2 / qs
    e1 = g_idx * epg + i1
    e2 = g_idx * epg + i2

    rowx = lax.broadcasted_iota(jnp.int32, (N_EXPERTS, tb), 0)
    hit1 = rowx == e1
    hit2 = rowx == e2
    member = (hit1 | hit2).astype(BF16)
    before = (lax.broadcasted_iota(jnp.int32, (tb, tb), 0)
              < lax.broadcasted_iota(jnp.int32, (tb, tb), 1)).astype(BF16)
    carry = carry_ref[...]
    count = _dot(member, before) + carry[:, 0:1]
    r1 = jnp.sum(jnp.where(hit1, count, 0.0), axis=0, keepdims=True).astype(jnp.int32)
    r2 = jnp.sum(jnp.where(hit2, count, 0.0), axis=0, keepdims=True).astype(jnp.int32)
    carry = carry + jnp.sum(member.astype(F32), axis=1, keepdims=True)
    carry_ref[...] = carry
    cnt_ref[...] = carry

    row8 = lax.broadcasted_iota(jnp.int32, (8, tb), 0)
    idx_ref[...] = jnp.where(row8 == 0, e1, jnp.where(row8 == 1, e2,
                             jnp.where(row8 == 2, r1, jnp.where(row8 == 3, r2, 0))))
    gate_ref[...] = jnp.where(row8 == 0, gate1, jnp.where(row8 == 1, gate2, 0.0))


def _route(lt, bg, be, tb):
    nr, s = lt.shape
    return pl.pallas_call(
        _route_kernel,
        grid=(s // tb,),
        in_specs=[
            pl.BlockSpec((nr, tb), lambda i: (0, i)),
            pl.BlockSpec((8, 1), lambda i: (0, 0)),
            pl.BlockSpec((N_EXPERTS, 1), lambda i: (0, 0)),
        ],
        out_specs=[
            pl.BlockSpec((8, tb), lambda i: (0, i)),
            pl.BlockSpec((8, tb), lambda i: (0, i)),
            pl.BlockSpec((N_EXPERTS, LANES), lambda i: (0, 0)),
        ],
        out_shape=[
            jax.ShapeDtypeStruct((8, s), jnp.int32),
            jax.ShapeDtypeStruct((8, s), F32),
            jax.ShapeDtypeStruct((N_EXPERTS, LANES), F32),
        ],
        scratch_shapes=[pltpu.VMEM((N_EXPERTS, LANES), F32)],
        compiler_params=_cparams(("arbitrary",)),
        name="route",
    )(lt, bg, be)


def _expert_kernel(be_ref, nx_ref, nu_ref, tok_ref, h2_hbm, wg_hbm, wu_hbm, wd_hbm, y_ref,
                   xbuf, xsem, wgf, wuf, wdf, wsem, wgb, wub, wdb):
    b = pl.program_id(0)
    n_used = nu_ref[0]
    rows = MOE_BLOCK
    slot = b % 2

    def row_copy(blk, r, s_):
        t = tok_ref[blk * rows + r]
        return pltpu.make_async_copy(h2_hbm.at[t], xbuf.at[s_, r], xsem.at[s_])

    def wait_rows(s_):
        pltpu.make_async_copy(h2_hbm.at[pl.ds(0, rows)], xbuf.at[s_], xsem.at[s_]).wait()

    def weight_copies(e):
        return (pltpu.make_async_copy(wg_hbm.at[e], wgf, wsem.at[0]),
                pltpu.make_async_copy(wu_hbm.at[e], wuf, wsem.at[1]),
                pltpu.make_async_copy(wd_hbm.at[e], wdf, wsem.at[2]))

    weight_priority = 1

    @pl.when(b == 0)
    def _():
        for cp in weight_copies(be_ref[0]):
            cp.start(priority=weight_priority)

        def body(r, carry):
            row_copy(0, r, 0).start()
            return carry
        lax.fori_loop(0, rows, body, 0)

    @pl.when(b < n_used)
    def _():
        wait_rows(slot)
        first_of_expert = (b == 0) | (be_ref[b] != be_ref[jnp.maximum(b - 1, 0)])

        @pl.when(first_of_expert)
        def _():
            for cp in weight_copies(be_ref[b]):
                cp.wait()
            wgb[...] = wgf[...].astype(BF16)
            wub[...] = wuf[...].astype(BF16)
            wdb[...] = wdf[...].astype(BF16)
            nxt = nx_ref[b]

            @pl.when(nxt >= 0)
            def _():
                for cp in weight_copies(nxt):
                    cp.start(priority=weight_priority)

        xb = xbuf[slot].reshape(rows, wgb.shape[0])
        for r in range(rows):
            row_copy(b + 1, r, 1 - slot).start()
        hg = _dot(xb, wgb[...])
        hu = _dot(xb, wub[...])
        act = (hg * _sigmoid(hg)) * hu
        y_ref[...] = _dot(act.astype(BF16), wdb[...]).astype(BF16).reshape(y_ref.shape)

    @pl.when(b == n_used)
    def _():
        wait_rows(slot)

    @pl.when(b >= n_used)
    def _():
        y_ref[...] = jnp.zeros_like(y_ref)


def _experts(block_e, next_e, n_used, row_token, h2, wg, wu, wd):
    n_rows = row_token.shape[0]
    n_blocks = n_rows // MOE_BLOCK
    _, d, de = wg.shape
    grid_spec = pltpu.PrefetchScalarGridSpec(
        num_scalar_prefetch=4,
        grid=(n_blocks,),
        in_specs=[pl.BlockSpec(memory_space=pl.ANY)] * 4,
        out_specs=pl.BlockSpec((MOE_BLOCK, d // LANES, LANES), lambda b, *_: (b, 0, 0)),
        scratch_shapes=[
            pltpu.VMEM((2, MOE_BLOCK, d // LANES, LANES), BF16),
            pltpu.SemaphoreType.DMA((2,)),
            pltpu.VMEM((d, de), F32),
            pltpu.VMEM((d, de), F32),
            pltpu.VMEM((de, d), F32),
            pltpu.SemaphoreType.DMA((3,)),
            pltpu.VMEM((d, de), BF16),
            pltpu.VMEM((d, de), BF16),
            pltpu.VMEM((de, d), BF16),
        ],
    )
    return pl.pallas_call(
        _expert_kernel,
        grid_spec=grid_spec,
        out_shape=jax.ShapeDtypeStruct((n_rows, d // LANES, LANES), BF16),
        compiler_params=_cparams(("arbitrary",)),
        name="experts",
    )(block_e, next_e, n_used, row_token, h2, wg, wu, wd)


def _combine_kernel(dest_ref, yb_hbm, gate_ref, x1_ref, g_ref, o_ref, ybuf_even, ybuf_odd, sem):
    i = pl.program_id(0)
    n_steps = pl.num_programs(0)
    tm = x1_ref.shape[0]
    n_tok = n_steps * tm
    bufs = (ybuf_even, ybuf_odd)

    def row_copy(step, k, r, parity):
        row = dest_ref[k * n_tok + step * tm + r]
        return pltpu.make_async_copy(yb_hbm.at[row], bufs[parity].at[k, r], sem.at[parity])

    def wait_rows(parity):
        for k in range(TOP_K):
            pltpu.make_async_copy(yb_hbm.at[pl.ds(0, tm)], bufs[parity].at[k], sem.at[parity]).wait()

    def compute(parity):
        gates = gate_ref[...]
        y0 = bufs[parity][0].reshape(x1_ref.shape).astype(F32)
        y1 = bufs[parity][1].reshape(x1_ref.shape).astype(F32)
        y = y0 * gates[:, 0:1] + y1 * gates[:, 1:2]
        x2 = x1_ref[...] + y
        ms = jnp.mean(x2 * x2, axis=-1, keepdims=True)
        o_ref[...] = x2 * lax.rsqrt(ms + RMS_EPS) * g_ref[...]

    @pl.when(i == 0)
    def _():
        def body(r, carry):
            row_copy(0, 0, r, 0).start()
            row_copy(0, 1, r, 0).start()
            return carry
        lax.fori_loop(0, tm, body, 0)

    for parity in range(2):
        @pl.when((i % 2 == parity) & (i + 1 < n_steps))
        def _():
            wait_rows(parity)
            for r in range(tm):
                for k in range(TOP_K):
                    row_copy(i + 1, k, r, 1 - parity).start(priority=(r + k) % 2)
            compute(parity)

        @pl.when((i % 2 == parity) & (i + 1 == n_steps))
        def _():
            wait_rows(parity)
            compute(parity)


def _combine(dest, yb, gates, x1, g, tm):
    s, d = x1.shape
    grid_spec = pltpu.PrefetchScalarGridSpec(
        num_scalar_prefetch=1,
        grid=(s // tm,),
        in_specs=[
            pl.BlockSpec(memory_space=pl.ANY),
            pl.BlockSpec((tm, TOP_K), lambda i, dest: (i, 0)),
            pl.BlockSpec((tm, d), lambda i, dest: (i, 0)),
            pl.BlockSpec((1, d), lambda i, dest: (0, 0)),
        ],
        out_specs=pl.BlockSpec((tm, d), lambda i, dest: (i, 0)),
        scratch_shapes=[
            pltpu.VMEM((TOP_K, tm, d // LANES, LANES), BF16),
            pltpu.VMEM((TOP_K, tm, d // LANES, LANES), BF16),
            pltpu.SemaphoreType.DMA((2,)),
        ],
    )
    return pl.pallas_call(
        _combine_kernel,
        grid_spec=grid_spec,
        out_shape=jax.ShapeDtypeStruct((s, d), F32),
        compiler_params=_cparams(("arbitrary",)),
        name="combine",
    )(dest, yb, gates, x1, g)


def _tile(n, pref):
    return pref if n % pref == 0 else n


def kernel(x, norm_mix_g, w_in, w_gla_a2, b_gla_a, gla_norm_g, attn_sinks, rel_bias_table, w_out,
           norm_ffn_g, w_router_group, b_router_group, w_router_expert, b_router_expert,
           w_expert_gate, w_expert_up, w_expert_down, norm_final_g):
    b, s, d = x.shape
    assert b == 1 and w_in.shape[0] == 1, "single batch, single layer"
    assert (s * TOP_K) % MOE_BLOCK == 0
    x2d = x.reshape(s, d)
    w_in_t = w_in[0].T

    w2 = jnp.pad(w_gla_a2[0], ((0, LANES - GLA_LOWRANK), (0, 0))).astype(BF16)
    h, log_a = _norm_loga(x2d, norm_mix_g[0].reshape(1, d), w_in_t, w2, b_gla_a[0].reshape(1, -1),
                          _tile(s, 512))
    proj = _in_proj(h, w_in_t, _tile(s, 2048), 512)

    bias = _rel_bias(rel_bias_table.T, jnp.asarray(_bucket_map()))
    bias = bias.reshape(N_HEADS, WINDOW, 2 * WINDOW)
    o_attn = _swa(proj, attn_sinks[0], bias)
    o_gla = _gla(proj, log_a, gla_norm_g[0].reshape(1, -1), _tile(s, 256))

    wr = jnp.zeros((ROUTER_ROWS, d), F32)
    wr = wr.at[0:N_GROUPS].set(w_router_group[0].T).at[8:].set(w_router_expert[0].T).astype(BF16)
    x1, h2, logits_t = _merge_out(o_attn, o_gla, proj, x2d, w_out[0].astype(BF16),
                                  norm_ffn_g[0].reshape(1, d), wr, _tile(s, 256))

    bg = jnp.pad(b_router_group[0], (0, 8 - N_GROUPS)).reshape(8, 1)
    idx, gate, counts = _route(logits_t, bg, b_router_expert[0].reshape(N_EXPERTS, 1), _tile(s, 512))
    counts = counts[:, 0].astype(jnp.int32)
    padded = (counts + MOE_BLOCK - 1) // MOE_BLOCK * MOE_BLOCK
    pend = jnp.cumsum(padded)
    pstart = pend - padded
    expert = idx[0:TOP_K]
    eids = jnp.arange(N_EXPERTS, dtype=jnp.int32)[:, None, None]
    dest = jnp.sum(jnp.where(expert[None] == eids, pstart[:, None, None], 0), axis=0) + idx[TOP_K:2 * TOP_K]
    n_pairs = s * TOP_K
    n_blocks = (n_pairs + N_EXPERTS * (MOE_BLOCK - 1) + MOE_BLOCK - 1) // MOE_BLOCK
    n_used = (pend[-1] // MOE_BLOCK).reshape(1)
    block_start = jnp.arange(n_blocks, dtype=jnp.int32) * MOE_BLOCK
    block_e = jnp.minimum(jnp.sum((pend[None, :] <= block_start[:, None]).astype(jnp.int32), axis=1),
                          N_EXPERTS - 1)
    dest_flat = dest.reshape(-1)
    tok_of_pair = jnp.tile(jnp.arange(s, dtype=jnp.int32), TOP_K)
    row_token = jnp.zeros((n_blocks * MOE_BLOCK,), jnp.int32).at[dest_flat].set(tok_of_pair)

    e_col = jnp.arange(N_EXPERTS, dtype=jnp.int32)[:, None]
    e_row = jnp.arange(N_EXPERTS, dtype=jnp.int32)[None, :]
    later = (e_row > e_col) & (counts > 0)[None, :]
    next_nonempty = jnp.min(jnp.where(later, e_row, N_EXPERTS), axis=1)
    next_nonempty = jnp.where(next_nonempty < N_EXPERTS, next_nonempty, -1)
    next_e = jnp.sum(jnp.where(block_e[:, None] == e_row, next_nonempty[None, :], 0), axis=1)
    yb = _experts(block_e, next_e, n_used, row_token, h2, w_expert_gate[0], w_expert_up[0], w_expert_down[0])
    out = _combine(dest_flat, yb, gate[0:TOP_K].T, x1, norm_final_g.reshape(1, d), _tile(s, 128))
    return out.reshape(b, s, d)
```

```python
import functools
import math

import numpy as np
import jax
import jax.numpy as jnp
from jax import lax
from jax.experimental import pallas as pl
from jax.experimental.pallas import tpu as pltpu

F32 = jnp.float32
BF16 = jnp.bfloat16

N_HEADS = 32
N_KV_HEADS = 4
HEAD_DIM = 64
WINDOW = 128
N_BUCKETS = 32
MAX_DISTANCE = 128
GLA_HEADS = 4
GLA_DK = 256
GLA_DV = 512
GLA_LOWRANK = 16
GLA_TAU = 16.0
GLA_CHUNK = 64
N_GROUPS = 4
EXPERTS_PER_GROUP = 8
N_EXPERTS = 32
TOP_K = 2
MOE_BLOCK = 128
RMS_EPS = 1e-6
NEG_INF = -1e30

COL_GA = 8704
COL_GATES = 8720
MAIN_COLS = 8704
GATE_COLS = 4096
COL_AQ = GATE_COLS + 0
COL_AK = GATE_COLS + 2048
COL_AV = GATE_COLS + 2304
COL_GQ = GATE_COLS + 2560
COL_GK = GATE_COLS + 3584
COL_GV = GATE_COLS + 4608
COL_GR = GATE_COLS + 6656

LANES = 128
VMEM_LIMIT = 56 * 1024 * 1024


def _cparams(sem, vmem=VMEM_LIMIT):
    return pltpu.CompilerParams(dimension_semantics=sem, vmem_limit_bytes=vmem)


def _split3(x):
    hi = x.astype(BF16)
    r1 = x - hi.astype(F32)
    mid = r1.astype(BF16)
    lo = (r1 - mid.astype(F32)).astype(BF16)
    return hi, mid, lo


def _sigmoid(x):
    return 1.0 / (1.0 + jnp.exp(-x))


def _dot(a, b):
    return jnp.dot(a, b, preferred_element_type=F32)


def _dot_nt(a, b):
    return lax.dot_general(a, b, (((1,), (1,)), ((), ())), preferred_element_type=F32)


def _dot_tn(a, b):
    return lax.dot_general(a, b, (((0,), (0,)), ((), ())), preferred_element_type=F32)


def _norm_loga_kernel(x_ref, g_ref, wga_ref, w2_ref, b_ref, h_ref, la_ref):
    x = x_ref[...]
    ms = jnp.mean(x * x, axis=-1, keepdims=True)
    hb = (x * lax.rsqrt(ms + RMS_EPS) * g_ref[...]).astype(BF16)
    h_ref[...] = hb
    row = lax.broadcasted_iota(jnp.int32, wga_ref.shape, 0)
    wga = jnp.where(row < GLA_LOWRANK, wga_ref[...], 0.0).astype(BF16)
    ga = _dot_nt(hb, wga)
    z = _dot(ga.astype(BF16), w2_ref[...]) + b_ref[...]
    la_ref[...] = (jnp.minimum(z, 0.0) - jnp.log1p(jnp.exp(-jnp.abs(z)))) * (1.0 / GLA_TAU)


def _norm_loga(x, g, wga, w2, b, tm):
    s, d = x.shape
    nq = w2.shape[1]
    return pl.pallas_call(
        _norm_loga_kernel,
        grid=(s // tm,),
        in_specs=[
            pl.BlockSpec((tm, d), lambda i: (i, 0)),
            pl.BlockSpec((1, d), lambda i: (0, 0)),
            pl.BlockSpec((LANES, d), lambda i: (COL_GA // LANES, 0)),
            pl.BlockSpec((LANES, nq), lambda i: (0, 0)),
            pl.BlockSpec((1, nq), lambda i: (0, 0)),
        ],
        out_specs=[
            pl.BlockSpec((tm, d), lambda i: (i, 0)),
            pl.BlockSpec((tm, nq), lambda i: (i, 0)),
        ],
        out_shape=[
            jax.ShapeDtypeStruct((s, d), BF16),
            jax.ShapeDtypeStruct((s, nq), F32),
        ],
        compiler_params=_cparams(("parallel",)),
        name="norm_loga",
    )(x, g, wga, w2, b)


def _in_proj_kernel(h_hbm, wt_hbm, o_ref, h_vmem, w_stage, wbf, sem, *, n_main, n_tiles, tn):
    j = pl.program_id(0)
    i = pl.program_id(1)
    tm = o_ref.shape[0]

    def w_copy(jj):
        row0 = jnp.where(jj < n_main, jj * tn, COL_GATES + (jj - n_main) * tn)
        return pltpu.make_async_copy(wt_hbm.at[pl.ds(pl.multiple_of(row0, 8), tn)], w_stage, sem.at[1])

    @pl.when((j == 0) & (i == 0))
    def _():
        h_copy = pltpu.make_async_copy(h_hbm, h_vmem, sem.at[0])
        h_copy.start()
        w_copy(0).start()
        h_copy.wait()

    @pl.when(i == 0)
    def _():
        w_copy(j).wait()
        wbf[...] = w_stage[...].astype(BF16)

        @pl.when(j + 1 < n_tiles)
        def _():
            w_copy(j + 1).start()

    h = h_vmem[pl.ds(pl.multiple_of(i * tm, tm), tm), :]
    o_ref[...] = _dot_nt(h, wbf[...]).astype(o_ref.dtype)


def _in_proj(h, w_t, tm, tn):
    s, d = h.shape
    n_main = MAIN_COLS // tn
    n_gate = GATE_COLS // tn
    n_tiles = n_main + n_gate
    kern = functools.partial(_in_proj_kernel, n_main=n_main, n_tiles=n_tiles, tn=tn)
    return pl.pallas_call(
        kern,
        grid=(n_tiles, s // tm),
        in_specs=[pl.BlockSpec(memory_space=pl.ANY), pl.BlockSpec(memory_space=pl.ANY)],
        out_specs=pl.BlockSpec((tm, tn), lambda j, i: (i, (j + n_gate) % n_tiles)),
        out_shape=jax.ShapeDtypeStruct((s, n_tiles * tn), BF16),
        scratch_shapes=[
            pltpu.VMEM((s, d), BF16),
            pltpu.VMEM((tn, d), F32),
            pltpu.VMEM((tn, d), BF16),
            pltpu.SemaphoreType.DMA((2,)),
        ],
        compiler_params=_cparams(("arbitrary", "arbitrary")),
        name="in_proj",
    )(h, w_t)


def _folded_maps():
    j = np.arange(WINDOW)[:, None]
    i = np.arange(WINDOW)[None, :]
    n = (i - j) % WINDOW
    max_exact = N_BUCKETS // 2
    ratio = np.maximum(n, max_exact).astype(np.float32) / np.float32(max_exact)
    large = max_exact + (np.log(ratio) / np.float32(math.log(MAX_DISTANCE / max_exact))
                         * (N_BUCKETS - max_exact)).astype(np.int32)
    large = np.minimum(large, N_BUCKETS - 1)
    bucket = np.where(n < max_exact, n, large).astype(np.int32).reshape(1, -1)
    is_prev = (j > i).astype(np.int32).reshape(1, -1)
    return bucket, is_prev


def _rel_bias_kernel(tab_ref, bucket_ref, prev_ref, o_ref):
    nb = tab_ref.shape[1]
    width = bucket_ref.shape[1]
    onehot = (lax.broadcasted_iota(jnp.int32, (nb, width), 0) == bucket_ref[...]).astype(BF16)
    hi, mid, lo = _split3(tab_ref[...])
    bias = _dot(hi, onehot) + _dot(mid, onehot) + _dot(lo, onehot)
    o_ref[0] = bias
    o_ref[1] = jnp.where(prev_ref[...] > 0, NEG_INF, bias)


def _rel_bias(table_t, bucket, is_prev):
    nh = table_t.shape[0]
    width = bucket.shape[1]
    return pl.pallas_call(
        _rel_bias_kernel,
        out_shape=jax.ShapeDtypeStruct((2, nh, width), F32),
        compiler_params=pltpu.CompilerParams(vmem_limit_bytes=VMEM_LIMIT),
        name="rel_bias",
    )(table_t, bucket, is_prev)


SWA_SCORES_AHEAD = 10


def _swa_kernel(sink_ref, q_ref, kc_ref, kp_ref, vc_ref, vp_ref, bias_ref, o_ref):
    w = WINDOW
    hd = HEAD_DIM
    gq = N_HEADS // N_KV_HEADS
    is_prev = (lax.broadcasted_iota(jnp.int32, (w, w), 0) > lax.broadcasted_iota(jnp.int32, (w, w), 1))
    zeros = jnp.zeros((2 * w, hd), BF16)
    scale = HEAD_DIM ** -0.5
    k_pl, v_pl = [], []
    for g in range(N_KV_HEADS):
        cs = slice(g * hd, (g + 1) * hd)
        k_g = jnp.concatenate([kp_ref[:, cs], kc_ref[:, cs]], axis=0) * scale
        v_g = jnp.concatenate([vp_ref[:, cs], vc_ref[:, cs]], axis=0)
        k_pl.append((jnp.concatenate([k_g, zeros], axis=1), jnp.concatenate([zeros, k_g], axis=1)))
        v_pl.append((jnp.concatenate([v_g, zeros], axis=1), jnp.concatenate([zeros, v_g], axis=1)))

    def scores(hh):
        pair = hh - hh % 2
        return _dot_nt(k_pl[hh // gq][hh % 2], q_ref[:, pair * hd:(pair + 2) * hd])

    def attend(hh, st2):
        st = jnp.where(is_prev, st2[:w], st2[w:]) + bias_ref[hh]
        sink = sink_ref[hh]
        m = jnp.maximum(jnp.max(st, axis=0, keepdims=True), sink)
        p = jnp.exp(st - m)
        recip = 1.0 / (jnp.sum(p, axis=0, keepdims=True) + jnp.exp(sink - m))
        probs = p * recip
        p_cat = jnp.concatenate([jnp.where(is_prev, probs, 0.0), jnp.where(is_prev, 0.0, probs)], axis=0)
        return _dot_tn(p_cat.astype(BF16), v_pl[hh // gq][hh % 2])

    ahead = SWA_SCORES_AHEAD
    st2 = {hh: scores(hh) for hh in range(ahead)}
    acc = None
    for hh in range(N_HEADS):
        if hh + ahead < N_HEADS:
            st2[hh + ahead] = scores(hh + ahead)
        part = attend(hh, st2.pop(hh))
        if hh % 2 == 0:
            acc = part
        else:
            o_ref[:, (hh - 1) * hd:(hh + 1) * hd] = (acc + part).astype(o_ref.dtype)


def _swa(proj, sinks, bias):
    s = proj.shape[0]
    w = WINDOW
    nb = s // w
    kvw = N_KV_HEADS * HEAD_DIM
    dq = N_HEADS * HEAD_DIM
    prev = lambda n: jnp.maximum(n - 1, 0)
    return pl.pallas_call(
        _swa_kernel,
        grid=(nb,),
        in_specs=[
            pl.BlockSpec(memory_space=pltpu.SMEM),
            pl.BlockSpec((w, dq), lambda n: (n, COL_AQ // dq)),
            pl.BlockSpec((w, kvw), lambda n: (n, COL_AK // kvw)),
            pl.BlockSpec((w, kvw), lambda n: (prev(n), COL_AK // kvw)),
            pl.BlockSpec((w, kvw), lambda n: (n, COL_AV // kvw)),
            pl.BlockSpec((w, kvw), lambda n: (prev(n), COL_AV // kvw)),
            pl.BlockSpec((None, N_HEADS, w, w), lambda n: (jnp.where(n == 0, 1, 0), 0, 0, 0)),
        ],
        out_specs=pl.BlockSpec((w, dq), lambda n: (n, 0)),
        out_shape=jax.ShapeDtypeStruct((s, dq), BF16),
        compiler_params=_cparams(("parallel",)),
        name="swa",
    )(sinks, proj, proj, proj, proj, proj, bias)


def _gla_kernel(q_ref, k_ref, v_ref, gr_ref, la_ref, gn_ref, o_ref, state_ref):
    c = GLA_CHUNK
    rows = q_ref.shape[0]
    dk = q_ref.shape[1]
    dv = v_ref.shape[1]

    @pl.when(pl.program_id(1) == 0)
    def _():
        state_ref[...] = jnp.zeros_like(state_ref)

    la = la_ref[...]
    la_parts = _split3(la)
    ri = lax.broadcasted_iota(jnp.int32, (rows, rows), 0)
    ci = lax.broadcasted_iota(jnp.int32, (rows, rows), 1)
    tri = ((ri // c == ci // c) & (ri >= ci)).astype(BF16)
    cum = _dot(tri, la_parts[0]) + _dot(tri, la_parts[1]) + _dot(tri, la_parts[2])
    causal = (lax.broadcasted_iota(jnp.int32, (c, c), 0) >= lax.broadcasted_iota(jnp.int32, (c, c), 1))
    gn = gn_ref[...]
    n_chunks = rows // c

    o_intra, q_in, upd, decay = [], [], [], []
    for ch in range(n_chunks):
        sl = slice(ch * c, (ch + 1) * c)
        cum_c = cum[sl]
        ref = cum_c[c // 2 - 1:c // 2]
        last = cum_c[c - 1:c]
        q = q_ref[sl, :].astype(F32) * (dk ** -0.5)
        k = k_ref[sl, :].astype(F32)
        v = v_ref[sl, :]
        q_intra = (q * jnp.exp(cum_c - ref)).astype(BF16)
        k_intra = (k * jnp.exp(ref - cum_c)).astype(BF16)
        a = jnp.where(causal, _dot_nt(q_intra, k_intra), 0.0)
        o_intra.append(_dot(a.astype(BF16), v))
        q_in.append((q * jnp.exp(cum_c)).astype(BF16))
        k_dec = (k * jnp.exp(last - cum_c)).astype(BF16)
        upd.append(_dot_tn(k_dec, v))
        decay.append(jnp.broadcast_to(jnp.exp(last), (LANES, dk)).T)

    state = state_ref[...]
    for ch in range(n_chunks):
        sl = slice(ch * c, (ch + 1) * c)
        o = o_intra[ch] + _dot(q_in[ch], state.astype(BF16))
        state = jnp.concatenate(
            [state[:, jv * LANES:(jv + 1) * LANES] * decay[ch] + upd[ch][:, jv * LANES:(jv + 1) * LANES]
             for jv in range(dv // LANES)], axis=1)
        ms = jnp.mean(o * o, axis=-1, keepdims=True)
        on = o * lax.rsqrt(ms + RMS_EPS) * gn
        gr = gr_ref[sl, :].astype(F32)
        o_ref[sl, :] = (on * (gr * _sigmoid(gr))).astype(o_ref.dtype)
    state_ref[...] = state


def _gla(proj, la, gn, rows):
    s = proj.shape[0]
    dk, dv = GLA_DK, GLA_DV
    return pl.pallas_call(
        _gla_kernel,
        grid=(GLA_HEADS, s // rows),
        in_specs=[
            pl.BlockSpec((rows, dk), lambda h, i: (i, COL_GQ // dk + h)),
            pl.BlockSpec((rows, dk), lambda h, i: (i, COL_GK // dk + h)),
            pl.BlockSpec((rows, dv), lambda h, i: (i, COL_GV // dv + h)),
            pl.BlockSpec((rows, dv), lambda h, i: (i, COL_GR // dv + h)),
            pl.BlockSpec((rows, dk), lambda h, i: (i, h)),
            pl.BlockSpec((1, dv), lambda h, i: (0, 0)),
        ],
        out_specs=pl.BlockSpec((rows, dv), lambda h, i: (i, h)),
        out_shape=jax.ShapeDtypeStruct((s, GLA_HEADS * dv), BF16),
        scratch_shapes=[pltpu.VMEM((dk, dv), F32)],
        compiler_params=_cparams(("parallel", "arbitrary")),
        name="gla",
    )(proj, proj, proj, proj, la, gn)


def _merge_out_kernel(oa_ref, og_ref, ga_ref, gg_ref, x_ref, wo_ref, g_ref, wr_ref,
                      x1_ref, h2_ref, lt_ref):
    merged = (_sigmoid(ga_ref[...].astype(F32)) * oa_ref[...].astype(F32)
              + _sigmoid(gg_ref[...].astype(F32)) * og_ref[...].astype(F32))
    x1 = x_ref[...] + _dot(merged.astype(BF16), wo_ref[...])
    x1_ref[...] = x1
    ms = jnp.mean(x1 * x1, axis=-1, keepdims=True)
    h2 = x1 * lax.rsqrt(ms + RMS_EPS) * g_ref[...]
    h2b = h2.astype(BF16)
    h2_ref[...] = h2b.reshape(h2_ref.shape)
    lt_ref[...] = _dot_nt(wr_ref[...], h2b)


def _merge_out(o_attn, o_gla, gates, x, wo, g, wr, tm):
    s, d = x.shape
    nr = wr.shape[0]
    row = lambda i: (i, 0)
    return pl.pallas_call(
        _merge_out_kernel,
        grid=(s // tm,),
        in_specs=[
            pl.BlockSpec((tm, d), row),
            pl.BlockSpec((tm, d), row),
            pl.BlockSpec((tm, d), lambda i: (i, 0)),
            pl.BlockSpec((tm, d), lambda i: (i, 1)),
            pl.BlockSpec((tm, d), row),
            pl.BlockSpec((d, d), lambda i: (0, 0)),
            pl.BlockSpec((1, d), lambda i: (0, 0)),
            pl.BlockSpec((nr, d), lambda i: (0, 0)),
        ],
        out_specs=[
            pl.BlockSpec((tm, d), row),
            pl.BlockSpec((tm, d // LANES, LANES), lambda i: (i, 0, 0)),
            pl.BlockSpec((nr, tm), lambda i: (0, i)),
        ],
        out_shape=[
            jax.ShapeDtypeStruct((s, d), F32),
            jax.ShapeDtypeStruct((s, d // LANES, LANES), BF16),
            jax.ShapeDtypeStruct((nr, s), F32),
        ],
        compiler_params=_cparams(("parallel",)),
        name="merge_out",
    )(o_attn, o_gla, gates, gates, x, wo, g, wr)


ROUTER_ROWS = 8 + N_EXPERTS


def _route_kernel(lt_ref, bg_ref, be_ref, idx_ref, gate_ref, cnt_ref, carry_ref):
    tb = lt_ref.shape[1]
    epg = EXPERTS_PER_GROUP

    @pl.when(pl.program_id(0) == 0)
    def _():
        carry_ref[...] = jnp.zeros_like(carry_ref)

    logits = lt_ref[...]
    gl = logits[0:N_GROUPS]
    g_exp = jnp.exp(gl - jnp.max(gl, axis=0, keepdims=True))
    g_prob = g_exp / jnp.sum(g_exp, axis=0, keepdims=True)
    gb = gl + bg_ref[0:N_GROUPS]
    rowg = lax.broadcasted_iota(jnp.int32, (N_GROUPS, tb), 0)
    g_idx = jnp.min(jnp.where(gb == jnp.max(gb, axis=0, keepdims=True), rowg, N_GROUPS),
                    axis=0, keepdims=True)
    p_group = jnp.sum(jnp.where(rowg == g_idx, g_prob, 0.0), axis=0, keepdims=True)

    el = logits[8:8 + epg]
    eb = jnp.broadcast_to(be_ref[0:epg], (epg, tb))
    for g in range(1, N_GROUPS):
        pick = g_idx == g
        el = jnp.where(pick, logits[8 + g * epg:8 + (g + 1) * epg], el)
        eb = jnp.where(pick, be_ref[g * epg:(g + 1) * epg], eb)
    e_exp = jnp.exp(el - jnp.max(el, axis=0, keepdims=True))
    e_prob = e_exp / jnp.sum(e_exp, axis=0, keepdims=True)
    score = el + eb
    rowe = lax.broadcasted_iota(jnp.int32, (epg, tb), 0)
    i1 = jnp.min(jnp.where(score == jnp.max(score, axis=0, keepdims=True), rowe, epg),
                 axis=0, keepdims=True)
    score2 = jnp.where(rowe == i1, -jnp.inf, score)
    i2 = jnp.min(jnp.where(score2 == jnp.max(score2, axis=0, keepdims=True), rowe, epg),
                 axis=0, keepdims=True)
    q1 = jnp.sum(jnp.where(rowe == i1, e_prob, 0.0), axis=0, keepdims=True)
    q2 = jnp.sum(jnp.where(rowe == i2, e_prob, 0.0), axis=0, keepdims=True)
    qs = q1 + q2
    gate1 = p_group * q1 / qs
    gate2 = p_group * q2 / qs
    e1 = g_idx * epg + i1
    e2 = g_idx * epg + i2

    rowx = lax.broadcasted_iota(jnp.int32, (N_EXPERTS, tb), 0)
    hit1 = rowx == e1
    hit2 = rowx == e2
    member = (hit1 | hit2).astype(BF16)
    before = (lax.broadcasted_iota(jnp.int32, (tb, tb), 0)
              < lax.broadcasted_iota(jnp.int32, (tb, tb), 1)).astype(BF16)
    carry = carry_ref[...]
    count = _dot(member, before) + carry[:, 0:1]
    r1 = jnp.sum(jnp.where(hit1, count, 0.0), axis=0, keepdims=True).astype(jnp.int32)
    r2 = jnp.sum(jnp.where(hit2, count, 0.0), axis=0, keepdims=True).astype(jnp.int32)
    carry = carry + jnp.sum(member.astype(F32), axis=1, keepdims=True)
    carry_ref[...] = carry
    cnt_ref[...] = carry

    row8 = lax.broadcasted_iota(jnp.int32, (8, tb), 0)
    idx_ref[...] = jnp.where(row8 == 0, e1, jnp.where(row8 == 1, e2,
                             jnp.where(row8 == 2, r1, jnp.where(row8 == 3, r2, 0))))
    gate_ref[...] = jnp.where(row8 == 0, gate1, jnp.where(row8 == 1, gate2, 0.0))


def _route(lt, bg, be, tb):
    nr, s = lt.shape
    return pl.pallas_call(
        _route_kernel,
        grid=(s // tb,),
        in_specs=[
            pl.BlockSpec((nr, tb), lambda i: (0, i)),
            pl.BlockSpec((8, 1), lambda i: (0, 0)),
            pl.BlockSpec((N_EXPERTS, 1), lambda i: (0, 0)),
        ],
        out_specs=[
            pl.BlockSpec((8, tb), lambda i: (0, i)),
            pl.BlockSpec((8, tb), lambda i: (0, i)),
            pl.BlockSpec((N_EXPERTS, LANES), lambda i: (0, 0)),
        ],
        out_shape=[
            jax.ShapeDtypeStruct((8, s), jnp.int32),
            jax.ShapeDtypeStruct((8, s), F32),
            jax.ShapeDtypeStruct((N_EXPERTS, LANES), F32),
        ],
        scratch_shapes=[pltpu.VMEM((N_EXPERTS, LANES), F32)],
        compiler_params=_cparams(("arbitrary",)),
        name="route",
    )(lt, bg, be)


def _expert_kernel(be_ref, nx_ref, nu_ref, tok_ref, h2_hbm, wg_hbm, wu_hbm, wd_hbm, y_ref,
                   xbuf, xsem, wgf, wuf, wdf, wsem, wgb, wub, wdb):
    b = pl.program_id(0)
    n_used = nu_ref[0]
    rows = MOE_BLOCK
    slot = b % 2

    def row_copy(blk, r, s_):
        t = tok_ref[blk * rows + r]
        return pltpu.make_async_copy(h2_hbm.at[t], xbuf.at[s_, r], xsem.at[s_])

    def wait_rows(s_):
        pltpu.make_async_copy(h2_hbm.at[pl.ds(0, rows)], xbuf.at[s_], xsem.at[s_]).wait()

    def weight_copies(e):
        return (pltpu.make_async_copy(wg_hbm.at[e], wgf, wsem.at[0]),
                pltpu.make_async_copy(wu_hbm.at[e], wuf, wsem.at[1]),
                pltpu.make_async_copy(wd_hbm.at[e], wdf, wsem.at[2]))

    weight_priority = 1

    @pl.when(b == 0)
    def _():
        for cp in weight_copies(be_ref[0]):
            cp.start(priority=weight_priority)

        def body(r, carry):
            row_copy(0, r, 0).start()
            return carry
        lax.fori_loop(0, rows, body, 0)

    @pl.when(b < n_used)
    def _():
        wait_rows(slot)
        first_of_expert = (b == 0) | (be_ref[b] != be_ref[jnp.maximum(b - 1, 0)])

        @pl.when(first_of_expert)
        def _():
            for cp in weight_copies(be_ref[b]):
                cp.wait()
            wgb[...] = wgf[...].astype(BF16)
            wub[...] = wuf[...].astype(BF16)
            wdb[...] = wdf[...].astype(BF16)
            nxt = nx_ref[b]

            @pl.when(nxt >= 0)
            def _():
                for cp in weight_copies(nxt):
                    cp.start(priority=weight_priority)

        xb = xbuf[slot].reshape(rows, wgb.shape[0])
        for r in range(rows):
            row_copy(b + 1, r, 1 - slot).start()
        hg = _dot(xb, wgb[...])
        hu = _dot(xb, wub[...])
        act = (hg * _sigmoid(hg)) * hu
        y_ref[...] = _dot(act.astype(BF16), wdb[...]).astype(BF16).reshape(y_ref.shape)

    @pl.when(b == n_used)
    def _():
        wait_rows(slot)

    @pl.when(b >= n_used)
    def _():
        y_ref[...] = jnp.zeros_like(y_ref)


def _experts(block_e, next_e, n_used, row_token, h2, wg, wu, wd):
    n_rows = row_token.shape[0]
    n_blocks = n_rows // MOE_BLOCK
    _, d, de = wg.shape
    grid_spec = pltpu.PrefetchScalarGridSpec(
        num_scalar_prefetch=4,
        grid=(n_blocks,),
        in_specs=[pl.BlockSpec(memory_space=pl.ANY)] * 4,
        out_specs=pl.BlockSpec((MOE_BLOCK, d // LANES, LANES), lambda b, *_: (b, 0, 0)),
        scratch_shapes=[
            pltpu.VMEM((2, MOE_BLOCK, d // LANES, LANES), BF16),
            pltpu.SemaphoreType.DMA((2,)),
            pltpu.VMEM((d, de), F32),
            pltpu.VMEM((d, de), F32),
            pltpu.VMEM((de, d), F32),
            pltpu.SemaphoreType.DMA((3,)),
            pltpu.VMEM((d, de), BF16),
            pltpu.VMEM((d, de), BF16),
            pltpu.VMEM((de, d), BF16),
        ],
    )
    return pl.pallas_call(
        _expert_kernel,
        grid_spec=grid_spec,
        out_shape=jax.ShapeDtypeStruct((n_rows, d // LANES, LANES), BF16),
        compiler_params=_cparams(("arbitrary",)),
        name="experts",
    )(block_e, next_e, n_used, row_token, h2, wg, wu, wd)


def _combine_kernel(dest_ref, yb_hbm, gate_ref, x1_ref, g_ref, o_ref, ybuf_even, ybuf_odd, sem):
    i = pl.program_id(0)
    n_steps = pl.num_programs(0)
    tm = x1_ref.shape[0]
    n_tok = n_steps * tm
    bufs = (ybuf_even, ybuf_odd)

    def row_copy(step, k, r, parity):
        row = dest_ref[k * n_tok + step * tm + r]
        return pltpu.make_async_copy(yb_hbm.at[row], bufs[parity].at[k, r], sem.at[parity])

    def wait_rows(parity):
        for k in range(TOP_K):
            pltpu.make_async_copy(yb_hbm.at[pl.ds(0, tm)], bufs[parity].at[k], sem.at[parity]).wait()

    def compute(parity):
        gates = gate_ref[...]
        y0 = bufs[parity][0].reshape(x1_ref.shape).astype(F32)
        y1 = bufs[parity][1].reshape(x1_ref.shape).astype(F32)
        y = y0 * gates[:, 0:1] + y1 * gates[:, 1:2]
        x2 = x1_ref[...] + y
        ms = jnp.mean(x2 * x2, axis=-1, keepdims=True)
        o_ref[...] = x2 * lax.rsqrt(ms + RMS_EPS) * g_ref[...]

    @pl.when(i == 0)
    def _():
        def body(r, carry):
            row_copy(0, 0, r, 0).start()
            row_copy(0, 1, r, 0).start()
            return carry
        lax.fori_loop(0, tm, body, 0)

    for parity in range(2):
        @pl.when((i % 2 == parity) & (i + 1 < n_steps))
        def _():
            wait_rows(parity)
            for r in range(tm):
                for k in range(TOP_K):
                    row_copy(i + 1, k, r, 1 - parity).start(priority=(r + k) % 2)
            compute(parity)

        @pl.when((i % 2 == parity) & (i + 1 == n_steps))
        def _():
            wait_rows(parity)
            compute(parity)


def _combine(dest, yb, gates, x1, g, tm):
    s, d = x1.shape
    grid_spec = pltpu.PrefetchScalarGridSpec(
        num_scalar_prefetch=1,
        grid=(s // tm,),
        in_specs=[
            pl.BlockSpec(memory_space=pl.ANY),
            pl.BlockSpec((tm, TOP_K), lambda i, dest: (i, 0)),
            pl.BlockSpec((tm, d), lambda i, dest: (i, 0)),
            pl.BlockSpec((1, d), lambda i, dest: (0, 0)),
        ],
        out_specs=pl.BlockSpec((tm, d), lambda i, dest: (i, 0)),
        scratch_shapes=[
            pltpu.VMEM((TOP_K, tm, d // LANES, LANES), BF16),
            pltpu.VMEM((TOP_K, tm, d // LANES, LANES), BF16),
            pltpu.SemaphoreType.DMA((2,)),
        ],
    )
    return pl.pallas_call(
        _combine_kernel,
        grid_spec=grid_spec,
        out_shape=jax.ShapeDtypeStruct((s, d), F32),
        compiler_params=_cparams(("arbitrary",)),
        name="combine",
    )(dest, yb, gates, x1, g)


def _tile(n, pref):
    return pref if n % pref == 0 else n


def kernel(x, norm_mix_g, w_in, w_gla_a2, b_gla_a, gla_norm_g, attn_sinks, rel_bias_table, w_out,
           norm_ffn_g, w_router_group, b_router_group, w_router_expert, b_router_expert,
           w_expert_gate, w_expert_up, w_expert_down, norm_final_g):
    b, s, d = x.shape
    assert b == 1 and w_in.shape[0] == 1, "single batch, single layer"
    assert (s * TOP_K) % MOE_BLOCK == 0
    x2d = x.reshape(s, d)
    w_in_t = w_in[0].T

    w2 = jnp.pad(w_gla_a2[0], ((0, LANES - GLA_LOWRANK), (0, 0))).astype(BF16)
    h, log_a = _norm_loga(x2d, norm_mix_g[0].reshape(1, d), w_in_t, w2, b_gla_a[0].reshape(1, -1),
                          _tile(s, 512))
    proj = _in_proj(h, w_in_t, _tile(s, 2048), 512)

    bucket, is_prev = _folded_maps()
    bias = _rel_bias(rel_bias_table.T, jnp.asarray(bucket), jnp.asarray(is_prev))
    bias = bias.reshape(2, N_HEADS, WINDOW, WINDOW)
    o_attn = _swa(proj, attn_sinks[0], bias)
    o_gla = _gla(proj, log_a, gla_norm_g[0].reshape(1, -1), _tile(s, 256))

    wr = jnp.zeros((ROUTER_ROWS, d), F32)
    wr = wr.at[0:N_GROUPS].set(w_router_group[0].T).at[8:].set(w_router_expert[0].T).astype(BF16)
    x1, h2, logits_t = _merge_out(o_attn, o_gla, proj, x2d, w_out[0].astype(BF16),
                                  norm_ffn_g[0].reshape(1, d), wr, _tile(s, 256))

    bg = jnp.pad(b_router_group[0], (0, 8 - N_GROUPS)).reshape(8, 1)
    idx, gate, counts = _route(logits_t, bg, b_router_expert[0].reshape(N_EXPERTS, 1), _tile(s, 512))
    counts = counts[:, 0].astype(jnp.int32)
    padded = (counts + MOE_BLOCK - 1) // MOE_BLOCK * MOE_BLOCK
    pend = jnp.cumsum(padded)
    pstart = pend - padded
    expert = idx[0:TOP_K]
    eids = jnp.arange(N_EXPERTS, dtype=jnp.int32)[:, None, None]
    dest = jnp.sum(jnp.where(expert[None] == eids, pstart[:, None, None], 0), axis=0) + idx[TOP_K:2 * TOP_K]
    n_pairs = s * TOP_K
    n_blocks = (n_pairs + N_EXPERTS * (MOE_BLOCK - 1) + MOE_BLOCK - 1) // MOE_BLOCK
    n_used = (pend[-1] // MOE_BLOCK).reshape(1)
    block_start = jnp.arange(n_blocks, dtype=jnp.int32) * MOE_BLOCK
    block_e = jnp.minimum(jnp.sum((pend[None, :] <= block_start[:, None]).astype(jnp.int32), axis=1),
                          N_EXPERTS - 1)
    dest_flat = dest.reshape(-1)
    tok_of_pair = jnp.tile(jnp.arange(s, dtype=jnp.int32), TOP_K)
    row_token = jnp.zeros((n_blocks * MOE_BLOCK,), jnp.int32).at[dest_flat].set(tok_of_pair)

    e_col = jnp.arange(N_EXPERTS, dtype=jnp.int32)[:, None]
    e_row = jnp.arange(N_EXPERTS, dtype=jnp.int32)[None, :]
    later = (e_row > e_col) & (counts > 0)[None, :]
    next_nonempty = jnp.min(jnp.where(later, e_row, N_EXPERTS), axis=1)
    next_nonempty = jnp.where(next_nonempty < N_EXPERTS, next_nonempty, -1)
    next_e = jnp.sum(jnp.where(block_e[:, None] == e_row, next_nonempty[None, :], 0), axis=1)
    yb = _experts(block_e, next_e, n_used, row_token, h2, w_expert_gate[0], w_expert_up[0], w_expert_down[0])
    out = _combine(dest_flat, yb, gate[0:TOP_K].T, x1, norm_final_g.reshape(1, d), _tile(s, 128))
    return out.reshape(b, s, d)
```

```python
import functools
import math

import numpy as np
import jax
import jax.numpy as jnp
from jax import lax
from jax.experimental import pallas as pl
from jax.experimental.pallas import tpu as pltpu

F32 = jnp.float32
BF16 = jnp.bfloat16

N_HEADS = 32
N_KV_HEADS = 4
HEAD_DIM = 64
WINDOW = 128
N_BUCKETS = 32
MAX_DISTANCE = 128
GLA_HEADS = 4
GLA_DK = 256
GLA_DV = 512
GLA_LOWRANK = 16
GLA_TAU = 16.0
GLA_CHUNK = 64
N_GROUPS = 4
EXPERTS_PER_GROUP = 8
N_EXPERTS = 32
TOP_K = 2
MOE_BLOCK = 128
RMS_EPS = 1e-6
NEG_INF = -1e30

COL_GA = 8704
COL_GATES = 8720
MAIN_COLS = 8704
GATE_COLS = 4096
COL_AQ = GATE_COLS + 0
COL_AK = GATE_COLS + 2048
COL_AV = GATE_COLS + 2304
COL_GQ = GATE_COLS + 2560
COL_GK = GATE_COLS + 3584
COL_GV = GATE_COLS + 4608
COL_GR = GATE_COLS + 6656

LANES = 128
VMEM_LIMIT = 56 * 1024 * 1024
EXPERT_VMEM_LIMIT = 60 * 1024 * 1024


def _cparams(sem, vmem=VMEM_LIMIT):
    return pltpu.CompilerParams(dimension_semantics=sem, vmem_limit_bytes=vmem)


def _split3(x):
    hi = x.astype(BF16)
    r1 = x - hi.astype(F32)
    mid = r1.astype(BF16)
    lo = (r1 - mid.astype(F32)).astype(BF16)
    return hi, mid, lo


def _sigmoid(x):
    return 1.0 / (1.0 + jnp.exp(-x))


def _dot(a, b):
    return jnp.dot(a, b, preferred_element_type=F32)


def _dot_nt(a, b):
    return lax.dot_general(a, b, (((1,), (1,)), ((), ())), preferred_element_type=F32)


def _dot_tn(a, b):
    return lax.dot_general(a, b, (((0,), (0,)), ((), ())), preferred_element_type=F32)


def _norm_loga_kernel(x_ref, g_ref, wga_ref, w2_ref, b_ref, h_ref, la_ref):
    x = x_ref[...]
    ms = jnp.mean(x * x, axis=-1, keepdims=True)
    hb = (x * lax.rsqrt(ms + RMS_EPS) * g_ref[...]).astype(BF16)
    h_ref[...] = hb
    row = lax.broadcasted_iota(jnp.int32, wga_ref.shape, 0)
    wga = jnp.where(row < GLA_LOWRANK, wga_ref[...], 0.0).astype(BF16)
    ga = _dot_nt(hb, wga)
    z = _dot(ga.astype(BF16), w2_ref[...]) + b_ref[...]
    la_ref[...] = (jnp.minimum(z, 0.0) - jnp.log1p(jnp.exp(-jnp.abs(z)))) * (1.0 / GLA_TAU)


def _norm_loga(x, g, wga, w2, b, tm):
    s, d = x.shape
    nq = w2.shape[1]
    return pl.pallas_call(
        _norm_loga_kernel,
        grid=(s // tm,),
        in_specs=[
            pl.BlockSpec((tm, d), lambda i: (i, 0)),
            pl.BlockSpec((1, d), lambda i: (0, 0)),
            pl.BlockSpec((LANES, d), lambda i: (COL_GA // LANES, 0)),
            pl.BlockSpec((LANES, nq), lambda i: (0, 0)),
            pl.BlockSpec((1, nq), lambda i: (0, 0)),
        ],
        out_specs=[
            pl.BlockSpec((tm, d), lambda i: (i, 0)),
            pl.BlockSpec((tm, nq), lambda i: (i, 0)),
        ],
        out_shape=[
            jax.ShapeDtypeStruct((s, d), BF16),
            jax.ShapeDtypeStruct((s, nq), F32),
        ],
        compiler_params=_cparams(("parallel",)),
        name="norm_loga",
    )(x, g, wga, w2, b)


def _in_proj_kernel(h_hbm, wt_hbm, o_ref, h_vmem, w_stage, wbf, sem, *, n_main, n_tiles, tn):
    j = pl.program_id(0)
    i = pl.program_id(1)
    tm = o_ref.shape[0]

    def w_copy(jj):
        row0 = jnp.where(jj < n_main, jj * tn, COL_GATES + (jj - n_main) * tn)
        return pltpu.make_async_copy(wt_hbm.at[pl.ds(pl.multiple_of(row0, 8), tn)], w_stage, sem.at[1])

    @pl.when((j == 0) & (i == 0))
    def _():
        h_copy = pltpu.make_async_copy(h_hbm, h_vmem, sem.at[0])
        h_copy.start()
        w_copy(0).start()
        h_copy.wait()

    @pl.when(i == 0)
    def _():
        w_copy(j).wait()
        wbf[...] = w_stage[...].astype(BF16)

        @pl.when(j + 1 < n_tiles)
        def _():
            w_copy(j + 1).start()

    h = h_vmem[pl.ds(pl.multiple_of(i * tm, tm), tm), :]
    o_ref[...] = _dot_nt(h, wbf[...]).astype(o_ref.dtype)


def _in_proj(h, w_t, tm, tn):
    s, d = h.shape
    n_main = MAIN_COLS // tn
    n_gate = GATE_COLS // tn
    n_tiles = n_main + n_gate
    kern = functools.partial(_in_proj_kernel, n_main=n_main, n_tiles=n_tiles, tn=tn)
    return pl.pallas_call(
        kern,
        grid=(n_tiles, s // tm),
        in_specs=[pl.BlockSpec(memory_space=pl.ANY), pl.BlockSpec(memory_space=pl.ANY)],
        out_specs=pl.BlockSpec((tm, tn), lambda j, i: (i, (j + n_gate) % n_tiles)),
        out_shape=jax.ShapeDtypeStruct((s, n_tiles * tn), BF16),
        scratch_shapes=[
            pltpu.VMEM((s, d), BF16),
            pltpu.VMEM((tn, d), F32),
            pltpu.VMEM((tn, d), BF16),
            pltpu.SemaphoreType.DMA((2,)),
        ],
        compiler_params=_cparams(("arbitrary", "arbitrary")),
        name="in_proj",
    )(h, w_t)


def _folded_maps():
    j = np.arange(WINDOW)[:, None]
    i = np.arange(WINDOW)[None, :]
    n = (i - j) % WINDOW
    max_exact = N_BUCKETS // 2
    ratio = np.maximum(n, max_exact).astype(np.float32) / np.float32(max_exact)
    large = max_exact + (np.log(ratio) / np.float32(math.log(MAX_DISTANCE / max_exact))
                         * (N_BUCKETS - max_exact)).astype(np.int32)
    large = np.minimum(large, N_BUCKETS - 1)
    bucket = np.where(n < max_exact, n, large).astype(np.int32).reshape(1, -1)
    is_prev = (j > i).astype(np.int32).reshape(1, -1)
    return bucket, is_prev


def _rel_bias_kernel(tab_ref, bucket_ref, prev_ref, o_ref):
    nb = tab_ref.shape[1]
    width = bucket_ref.shape[1]
    onehot = (lax.broadcasted_iota(jnp.int32, (nb, width), 0) == bucket_ref[...]).astype(BF16)
    hi, mid, lo = _split3(tab_ref[...])
    bias = _dot(hi, onehot) + _dot(mid, onehot) + _dot(lo, onehot)
    o_ref[0] = bias
    o_ref[1] = jnp.where(prev_ref[...] > 0, NEG_INF, bias)


def _rel_bias(table_t, bucket, is_prev):
    nh = table_t.shape[0]
    width = bucket.shape[1]
    return pl.pallas_call(
        _rel_bias_kernel,
        out_shape=jax.ShapeDtypeStruct((2, nh, width), F32),
        compiler_params=pltpu.CompilerParams(vmem_limit_bytes=VMEM_LIMIT),
        name="rel_bias",
    )(table_t, bucket, is_prev)


SWA_SCORES_AHEAD = 10


def _swa_kernel(sink_ref, q_ref, kc_ref, kp_ref, vc_ref, vp_ref, bias_ref, o_ref):
    w = WINDOW
    hd = HEAD_DIM
    gq = N_HEADS // N_KV_HEADS
    is_prev = (lax.broadcasted_iota(jnp.int32, (w, w), 0) > lax.broadcasted_iota(jnp.int32, (w, w), 1))
    zeros = jnp.zeros((2 * w, hd), BF16)
    scale = HEAD_DIM ** -0.5
    k_pl, v_pl = [], []
    for g in range(N_KV_HEADS):
        cs = slice(g * hd, (g + 1) * hd)
        k_g = jnp.concatenate([kp_ref[:, cs], kc_ref[:, cs]], axis=0) * scale
        v_g = jnp.concatenate([vp_ref[:, cs], vc_ref[:, cs]], axis=0)
        k_pl.append((jnp.concatenate([k_g, zeros], axis=1), jnp.concatenate([zeros, k_g], axis=1)))
        v_pl.append((jnp.concatenate([v_g, zeros], axis=1), jnp.concatenate([zeros, v_g], axis=1)))

    def scores(hh):
        pair = hh - hh % 2
        return _dot_nt(k_pl[hh // gq][hh % 2], q_ref[:, pair * hd:(pair + 2) * hd])

    def attend(hh, st2):
        st = jnp.where(is_prev, st2[:w], st2[w:]) + bias_ref[hh]
        sink = sink_ref[hh]
        m = jnp.maximum(jnp.max(st, axis=0, keepdims=True), sink)
        p = jnp.exp(st - m)
        recip = 1.0 / (jnp.sum(p, axis=0, keepdims=True) + jnp.exp(sink - m))
        probs = p * recip
        p_cat = jnp.concatenate([jnp.where(is_prev, probs, 0.0), jnp.where(is_prev, 0.0, probs)], axis=0)
        return _dot_tn(p_cat.astype(BF16), v_pl[hh // gq][hh % 2])

    ahead = SWA_SCORES_AHEAD
    st2 = {hh: scores(hh) for hh in range(ahead)}
    acc = None
    for hh in range(N_HEADS):
        if hh + ahead < N_HEADS:
            st2[hh + ahead] = scores(hh + ahead)
        part = attend(hh, st2.pop(hh))
        if hh % 2 == 0:
            acc = part
        else:
            o_ref[:, (hh - 1) * hd:(hh + 1) * hd] = (acc + part).astype(o_ref.dtype)


def _swa(proj, sinks, bias):
    s = proj.shape[0]
    w = WINDOW
    nb = s // w
    kvw = N_KV_HEADS * HEAD_DIM
    dq = N_HEADS * HEAD_DIM
    prev = lambda n: jnp.maximum(n - 1, 0)
    return pl.pallas_call(
        _swa_kernel,
        grid=(nb,),
        in_specs=[
            pl.BlockSpec(memory_space=pltpu.SMEM),
            pl.BlockSpec((w, dq), lambda n: (n, COL_AQ // dq)),
            pl.BlockSpec((w, kvw), lambda n: (n, COL_AK // kvw)),
            pl.BlockSpec((w, kvw), lambda n: (prev(n), COL_AK // kvw)),
            pl.BlockSpec((w, kvw), lambda n: (n, COL_AV // kvw)),
            pl.BlockSpec((w, kvw), lambda n: (prev(n), COL_AV // kvw)),
            pl.BlockSpec((None, N_HEADS, w, w), lambda n: (jnp.where(n == 0, 1, 0), 0, 0, 0)),
        ],
        out_specs=pl.BlockSpec((w, dq), lambda n: (n, 0)),
        out_shape=jax.ShapeDtypeStruct((s, dq), BF16),
        compiler_params=_cparams(("parallel",)),
        name="swa",
    )(sinks, proj, proj, proj, proj, proj, bias)


def _gla_kernel(q_ref, k_ref, v_ref, gr_ref, la_ref, gn_ref, o_ref, state_ref):
    c = GLA_CHUNK
    rows = q_ref.shape[0]
    dk = q_ref.shape[1]
    dv = v_ref.shape[1]

    @pl.when(pl.program_id(1) == 0)
    def _():
        state_ref[...] = jnp.zeros_like(state_ref)

    la = la_ref[...]
    la_parts = _split3(la)
    ri = lax.broadcasted_iota(jnp.int32, (rows, rows), 0)
    ci = lax.broadcasted_iota(jnp.int32, (rows, rows), 1)
    tri = ((ri // c == ci // c) & (ri >= ci)).astype(BF16)
    cum = _dot(tri, la_parts[0]) + _dot(tri, la_parts[1]) + _dot(tri, la_parts[2])
    causal = (lax.broadcasted_iota(jnp.int32, (c, c), 0) >= lax.broadcasted_iota(jnp.int32, (c, c), 1))
    gn = gn_ref[...]
    n_chunks = rows // c

    o_intra, q_in, upd, decay = [], [], [], []
    for ch in range(n_chunks):
        sl = slice(ch * c, (ch + 1) * c)
        cum_c = cum[sl]
        ref = cum_c[c // 2 - 1:c // 2]
        last = cum_c[c - 1:c]
        q = q_ref[sl, :].astype(F32) * (dk ** -0.5)
        k = k_ref[sl, :].astype(F32)
        v = v_ref[sl, :]
        q_intra = (q * jnp.exp(cum_c - ref)).astype(BF16)
        k_intra = (k * jnp.exp(ref - cum_c)).astype(BF16)
        a = jnp.where(causal, _dot_nt(q_intra, k_intra), 0.0)
        o_intra.append(_dot(a.astype(BF16), v))
        q_in.append((q * jnp.exp(cum_c)).astype(BF16))
        k_dec = (k * jnp.exp(last - cum_c)).astype(BF16)
        upd.append(_dot_tn(k_dec, v))
        decay.append(jnp.broadcast_to(jnp.exp(last), (LANES, dk)).T)

    state = state_ref[...]
    for ch in range(n_chunks):
        sl = slice(ch * c, (ch + 1) * c)
        o = o_intra[ch] + _dot(q_in[ch], state.astype(BF16))
        state = jnp.concatenate(
            [state[:, jv * LANES:(jv + 1) * LANES] * decay[ch] + upd[ch][:, jv * LANES:(jv + 1) * LANES]
             for jv in range(dv // LANES)], axis=1)
        ms = jnp.mean(o * o, axis=-1, keepdims=True)
        on = o * lax.rsqrt(ms + RMS_EPS) * gn
        gr = gr_ref[sl, :].astype(F32)
        o_ref[sl, :] = (on * (gr * _sigmoid(gr))).astype(o_ref.dtype)
    state_ref[...] = state


def _gla(proj, la, gn, rows):
    s = proj.shape[0]
    dk, dv = GLA_DK, GLA_DV
    return pl.pallas_call(
        _gla_kernel,
        grid=(GLA_HEADS, s // rows),
        in_specs=[
            pl.BlockSpec((rows, dk), lambda h, i: (i, COL_GQ // dk + h)),
            pl.BlockSpec((rows, dk), lambda h, i: (i, COL_GK // dk + h)),
            pl.BlockSpec((rows, dv), lambda h, i: (i, COL_GV // dv + h)),
            pl.BlockSpec((rows, dv), lambda h, i: (i, COL_GR // dv + h)),
            pl.BlockSpec((rows, dk), lambda h, i: (i, h)),
            pl.BlockSpec((1, dv), lambda h, i: (0, 0)),
        ],
        out_specs=pl.BlockSpec((rows, dv), lambda h, i: (i, h)),
        out_shape=jax.ShapeDtypeStruct((s, GLA_HEADS * dv), BF16),
        scratch_shapes=[pltpu.VMEM((dk, dv), F32)],
        compiler_params=_cparams(("parallel", "arbitrary")),
        name="gla",
    )(proj, proj, proj, proj, la, gn)


def _merge_out_kernel(oa_ref, og_ref, ga_ref, gg_ref, x_ref, wo_ref, g_ref, wr_ref,
                      x1_ref, h2_ref, lt_ref):
    merged = (_sigmoid(ga_ref[...].astype(F32)) * oa_ref[...].astype(F32)
              + _sigmoid(gg_ref[...].astype(F32)) * og_ref[...].astype(F32))
    x1 = x_ref[...] + _dot(merged.astype(BF16), wo_ref[...])
    x1_ref[...] = x1
    ms = jnp.mean(x1 * x1, axis=-1, keepdims=True)
    h2 = x1 * lax.rsqrt(ms + RMS_EPS) * g_ref[...]
    h2b = h2.astype(BF16)
    h2_ref[...] = h2b.reshape(h2_ref.shape)
    lt_ref[...] = _dot_nt(wr_ref[...], h2b)


def _merge_out(o_attn, o_gla, gates, x, wo, g, wr, tm):
    s, d = x.shape
    nr = wr.shape[0]
    row = lambda i: (i, 0)
    return pl.pallas_call(
        _merge_out_kernel,
        grid=(s // tm,),
        in_specs=[
            pl.BlockSpec((tm, d), row),
            pl.BlockSpec((tm, d), row),
            pl.BlockSpec((tm, d), lambda i: (i, 0)),
            pl.BlockSpec((tm, d), lambda i: (i, 1)),
            pl.BlockSpec((tm, d), row),
            pl.BlockSpec((d, d), lambda i: (0, 0)),
            pl.BlockSpec((1, d), lambda i: (0, 0)),
            pl.BlockSpec((nr, d), lambda i: (0, 0)),
        ],
        out_specs=[
            pl.BlockSpec((tm, d), row),
            pl.BlockSpec((tm, d // LANES, LANES), lambda i: (i, 0, 0)),
            pl.BlockSpec((nr, tm), lambda i: (0, i)),
        ],
        out_shape=[
            jax.ShapeDtypeStruct((s, d), F32),
            jax.ShapeDtypeStruct((s, d // LANES, LANES), BF16),
            jax.ShapeDtypeStruct((nr, s), F32),
        ],
        compiler_params=_cparams(("parallel",)),
        name="merge_out",
    )(o_attn, o_gla, gates, gates, x, wo, g, wr)


ROUTER_ROWS = 8 + N_EXPERTS


def _route_kernel(lt_ref, bg_ref, be_ref, idx_ref, gate_ref, cnt_ref, carry_ref):
    tb = lt_ref.shape[1]
    epg = EXPERTS_PER_GROUP

    @pl.when(pl.program_id(0) == 0)
    def _():
        carry_ref[...] = jnp.zeros_like(carry_ref)

    logits = lt_ref[...]
    gl = logits[0:N_GROUPS]
    g_exp = jnp.exp(gl - jnp.max(gl, axis=0, keepdims=True))
    g_prob = g_exp / jnp.sum(g_exp, axis=0, keepdims=True)
    gb = gl + bg_ref[0:N_GROUPS]
    rowg = lax.broadcasted_iota(jnp.int32, (N_GROUPS, tb), 0)
    g_idx = jnp.min(jnp.where(gb == jnp.max(gb, axis=0, keepdims=True), rowg, N_GROUPS),
                    axis=0, keepdims=True)
    p_group = jnp.sum(jnp.where(rowg == g_idx, g_prob, 0.0), axis=0, keepdims=True)

    el = logits[8:8 + epg]
    eb = jnp.broadcast_to(be_ref[0:epg], (epg, tb))
    for g in range(1, N_GROUPS):
        pick = g_idx == g
        el = jnp.where(pick, logits[8 + g * epg:8 + (g + 1) * epg], el)
        eb = jnp.where(pick, be_ref[g * epg:(g + 1) * epg], eb)
    e_exp = jnp.exp(el - jnp.max(el, axis=0, keepdims=True))
    e_prob = e_exp / jnp.sum(e_exp, axis=0, keepdims=True)
    score = el + eb
    rowe = lax.broadcasted_iota(jnp.int32, (epg, tb), 0)
    i1 = jnp.min(jnp.where(score == jnp.max(score, axis=0, keepdims=True), rowe, epg),
                 axis=0, keepdims=True)
    score2 = jnp.where(rowe == i1, -jnp.inf, score)
    i2 = jnp.min(jnp.where(score2 == jnp.max(score2, axis=0, keepdims=True), rowe, epg),
                 axis=0, keepdims=True)
    q1 = jnp.sum(jnp.where(rowe == i1, e_prob, 0.0), axis=0, keepdims=True)
    q2 = jnp.sum(jnp.where(rowe == i2, e_prob, 0.0), axis=0, keepdims=True)
    qs = q1 + q2
    gate1 = p_group * q1 / qs
    gate2 = p_group * q2 / qs
    e1 = g_idx * epg + i1
    e2 = g_idx * epg + i2

    rowx = lax.broadcasted_iota(jnp.int32, (N_EXPERTS, tb), 0)
    hit1 = rowx == e1
    hit2 = rowx == e2
    member = (hit1 | hit2).astype(BF16)
    before = (lax.broadcasted_iota(jnp.int32, (tb, tb), 0)
              < lax.broadcasted_iota(jnp.int32, (tb, tb), 1)).astype(BF16)
    carry = carry_ref[...]
    count = _dot(member, before) + carry[:, 0:1]
    r1 = jnp.sum(jnp.where(hit1, count, 0.0), axis=0, keepdims=True).astype(jnp.int32)
    r2 = jnp.sum(jnp.where(hit2, count, 0.0), axis=0, keepdims=True).astype(jnp.int32)
    carry = carry + jnp.sum(member.astype(F32), axis=1, keepdims=True)
    carry_ref[...] = carry
    cnt_ref[...] = carry

    row8 = lax.broadcasted_iota(jnp.int32, (8, tb), 0)
    idx_ref[...] = jnp.where(row8 == 0, e1, jnp.where(row8 == 1, e2,
                             jnp.where(row8 == 2, r1, jnp.where(row8 == 3, r2, 0))))
    gate_ref[...] = jnp.where(row8 == 0, gate1, jnp.where(row8 == 1, gate2, 0.0))


def _route(lt, bg, be, tb):
    nr, s = lt.shape
    return pl.pallas_call(
        _route_kernel,
        grid=(s // tb,),
        in_specs=[
            pl.BlockSpec((nr, tb), lambda i: (0, i)),
            pl.BlockSpec((8, 1), lambda i: (0, 0)),
            pl.BlockSpec((N_EXPERTS, 1), lambda i: (0, 0)),
        ],
        out_specs=[
            pl.BlockSpec((8, tb), lambda i: (0, i)),
            pl.BlockSpec((8, tb), lambda i: (0, i)),
            pl.BlockSpec((N_EXPERTS, LANES), lambda i: (0, 0)),
        ],
        out_shape=[
            jax.ShapeDtypeStruct((8, s), jnp.int32),
            jax.ShapeDtypeStruct((8, s), F32),
            jax.ShapeDtypeStruct((N_EXPERTS, LANES), F32),
        ],
        scratch_shapes=[pltpu.VMEM((N_EXPERTS, LANES), F32)],
        compiler_params=_cparams(("arbitrary",)),
        name="route",
    )(lt, bg, be)


def _expert_kernel(be_ref, nx_ref, nu_ref, pair_ref, h2_hbm, wg_hbm, wu_hbm, wd_hbm, y_hbm,
                   h2v, xg, ystage, wgf, wuf, wdf, wgb, wub, wdb, hsem, wsem, ysem):
    b = pl.program_id(0)
    n_used = nu_ref[0]
    rows = MOE_BLOCK
    n_tok = h2v.shape[0]
    slot = b % 2

    def weight_copies(e):
        return (pltpu.make_async_copy(wg_hbm.at[e], wgf, wsem.at[0]),
                pltpu.make_async_copy(wu_hbm.at[e], wuf, wsem.at[1]),
                pltpu.make_async_copy(wd_hbm.at[e], wdf, wsem.at[2]))

    def gather_row(blk, r, half):
        pair = jnp.maximum(pair_ref[blk * rows + r], 0)
        xg[half, r] = h2v[jnp.where(pair >= n_tok, pair - n_tok, pair)]

    def row_write(blk, r, half, dummy):
        pair = pair_ref[blk * rows + r]
        dst = jnp.where((pair < 0) | dummy, TOP_K * n_tok + half * rows + r, pair)
        return pltpu.make_async_copy(ystage.at[half, r], y_hbm.at[dst], ysem.at[half])

    def wait_row_writes(half):
        pltpu.make_async_copy(ystage.at[half], y_hbm.at[pl.ds(0, rows)], ysem.at[half]).wait()

    weight_priority = 1

    @pl.when(b == 0)
    def _():
        h2_copy = pltpu.make_async_copy(h2_hbm, h2v, hsem.at[0])
        h2_copy.start()
        for cp in weight_copies(be_ref[0]):
            cp.start(priority=weight_priority)
        ystage[...] = jnp.zeros_like(ystage)
        h2_copy.wait()
        spare_fill = pltpu.make_async_copy(ystage.at[0], y_hbm.at[pl.ds(TOP_K * n_tok, rows)], hsem.at[0])
        spare_fill.start()
        spare_fill.wait()

        def body(r, carry):
            gather_row(0, r, 0)
            return carry
        lax.fori_loop(0, rows, body, 0)

    @pl.when(b < n_used)
    def _():
        first_of_expert = (b == 0) | (be_ref[b] != be_ref[jnp.maximum(b - 1, 0)])

        @pl.when(first_of_expert)
        def _():
            for cp in weight_copies(be_ref[b]):
                cp.wait()
            wgb[...] = wgf[...].astype(BF16)
            wub[...] = wuf[...].astype(BF16)
            wdb[...] = wdf[...].astype(BF16)
            nxt = nx_ref[b]

            @pl.when(nxt >= 0)
            def _():
                for cp in weight_copies(nxt):
                    cp.start(priority=weight_priority)

        xb = xg[slot].reshape(rows, wgb.shape[0])
        prev_blk = jnp.maximum(b - 1, 0)
        for r in range(rows):
            row_write(prev_blk, r, 1 - slot, b == 0).start()
        for r in range(rows):
            gather_row(b + 1, r, 1 - slot)
        hg = _dot(xb, wgb[...])
        hu = _dot(xb, wub[...])
        act = (hg * _sigmoid(hg)) * hu
        y = _dot(act.astype(BF16), wdb[...]).astype(BF16)

        @pl.when(b >= 1)
        def _():
            wait_row_writes(slot)

        ystage[slot] = y.reshape(ystage.shape[1:])

    @pl.when(b == n_used)
    def _():
        def body(r, carry):
            row_write(b - 1, r, 1 - slot, False).start()
            return carry
        lax.fori_loop(0, rows, body, 0)
        wait_row_writes(slot)
        wait_row_writes(1 - slot)


def _experts(block_e, next_e, n_used, row_pair, h2c, wg, wu, wd):
    n_rows = row_pair.shape[0]
    n_blocks = n_rows // MOE_BLOCK
    n_tok = h2c.shape[0]
    _, d, de = wg.shape
    slab = (d // LANES, LANES)
    grid_spec = pltpu.PrefetchScalarGridSpec(
        num_scalar_prefetch=4,
        grid=(n_blocks,),
        in_specs=[pl.BlockSpec(memory_space=pl.ANY)] * 4,
        out_specs=pl.BlockSpec(memory_space=pl.ANY),
        scratch_shapes=[
            pltpu.VMEM((n_tok,) + slab, BF16),
            pltpu.VMEM((2, MOE_BLOCK) + slab, BF16),
            pltpu.VMEM((2, MOE_BLOCK) + slab, BF16),
            pltpu.VMEM((d, de), F32),
            pltpu.VMEM((d, de), F32),
            pltpu.VMEM((de, d), F32),
            pltpu.VMEM((d, de), BF16),
            pltpu.VMEM((d, de), BF16),
            pltpu.VMEM((de, d), BF16),
            pltpu.SemaphoreType.DMA((1,)),
            pltpu.SemaphoreType.DMA((3,)),
            pltpu.SemaphoreType.DMA((2,)),
        ],
    )
    return pl.pallas_call(
        _expert_kernel,
        grid_spec=grid_spec,
        out_shape=jax.ShapeDtypeStruct((TOP_K * n_tok + 2 * MOE_BLOCK,) + slab, BF16),
        compiler_params=_cparams(("arbitrary",), vmem=EXPERT_VMEM_LIMIT),
        name="experts",
    )(block_e, next_e, n_used, row_pair, h2c, wg, wu, wd)


def _combine_kernel(y0_ref, y1_ref, gate_ref, x1_ref, g_ref, o_ref):
    gates = gate_ref[...]
    y0 = y0_ref[...].reshape(x1_ref.shape).astype(F32)
    y1 = y1_ref[...].reshape(x1_ref.shape).astype(F32)
    x2 = x1_ref[...] + (y0 * gates[:, 0:1] + y1 * gates[:, 1:2])
    ms = jnp.mean(x2 * x2, axis=-1, keepdims=True)
    o_ref[...] = x2 * lax.rsqrt(ms + RMS_EPS) * g_ref[...]


def _combine(y2, gates, x1, g, tm):
    s, d = x1.shape
    slab = (d // LANES, LANES)
    n_steps = s // tm
    return pl.pallas_call(
        _combine_kernel,
        grid=(n_steps,),
        in_specs=[
            pl.BlockSpec((tm,) + slab, lambda i: (i, 0, 0)),
            pl.BlockSpec((tm,) + slab, lambda i: (n_steps + i, 0, 0)),
            pl.BlockSpec((tm, TOP_K), lambda i: (i, 0)),
            pl.BlockSpec((tm, d), lambda i: (i, 0)),
            pl.BlockSpec((1, d), lambda i: (0, 0)),
        ],
        out_specs=pl.BlockSpec((tm, d), lambda i: (i, 0)),
        out_shape=jax.ShapeDtypeStruct((s, d), F32),
        compiler_params=_cparams(("parallel",)),
        name="combine",
    )(y2, y2, gates, x1, g)


def _tile(n, pref):
    return pref if n % pref == 0 else n


def kernel(x, norm_mix_g, w_in, w_gla_a2, b_gla_a, gla_norm_g, attn_sinks, rel_bias_table, w_out,
           norm_ffn_g, w_router_group, b_router_group, w_router_expert, b_router_expert,
           w_expert_gate, w_expert_up, w_expert_down, norm_final_g):
    b, s, d = x.shape
    assert b == 1 and w_in.shape[0] == 1, "single batch, single layer"
    assert (s * TOP_K) % MOE_BLOCK == 0
    x2d = x.reshape(s, d)
    w_in_t = w_in[0].T

    w2 = jnp.pad(w_gla_a2[0], ((0, LANES - GLA_LOWRANK), (0, 0))).astype(BF16)
    h, log_a = _norm_loga(x2d, norm_mix_g[0].reshape(1, d), w_in_t, w2, b_gla_a[0].reshape(1, -1),
                          _tile(s, 512))
    proj = _in_proj(h, w_in_t, _tile(s, 2048), 512)

    bucket, is_prev = _folded_maps()
    bias = _rel_bias(rel_bias_table.T, jnp.asarray(bucket), jnp.asarray(is_prev))
    bias = bias.reshape(2, N_HEADS, WINDOW, WINDOW)
    o_attn = _swa(proj, attn_sinks[0], bias)
    o_gla = _gla(proj, log_a, gla_norm_g[0].reshape(1, -1), _tile(s, 256))

    wr = jnp.zeros((ROUTER_ROWS, d), F32)
    wr = wr.at[0:N_GROUPS].set(w_router_group[0].T).at[8:].set(w_router_expert[0].T).astype(BF16)
    x1, h2, logits_t = _merge_out(o_attn, o_gla, proj, x2d, w_out[0].astype(BF16),
                                  norm_ffn_g[0].reshape(1, d), wr, _tile(s, 256))

    bg = jnp.pad(b_router_group[0], (0, 8 - N_GROUPS)).reshape(8, 1)
    idx, gate, counts = _route(logits_t, bg, b_router_expert[0].reshape(N_EXPERTS, 1), _tile(s, 512))
    counts = counts[:, 0].astype(jnp.int32)
    padded = (counts + MOE_BLOCK - 1) // MOE_BLOCK * MOE_BLOCK
    pend = jnp.cumsum(padded)
    pstart = pend - padded
    expert = idx[0:TOP_K]
    eids = jnp.arange(N_EXPERTS, dtype=jnp.int32)[:, None, None]
    dest = jnp.sum(jnp.where(expert[None] == eids, pstart[:, None, None], 0), axis=0) + idx[TOP_K:2 * TOP_K]
    n_pairs = s * TOP_K
    n_blocks = (n_pairs + N_EXPERTS * (MOE_BLOCK - 1) + MOE_BLOCK - 1) // MOE_BLOCK
    n_used = (pend[-1] // MOE_BLOCK).reshape(1)
    block_start = jnp.arange(n_blocks, dtype=jnp.int32) * MOE_BLOCK
    block_e = jnp.minimum(jnp.sum((pend[None, :] <= block_start[:, None]).astype(jnp.int32), axis=1),
                          N_EXPERTS - 1)
    row_pair = jnp.full((n_blocks * MOE_BLOCK,), -1, jnp.int32).at[dest.reshape(-1)].set(
        jnp.arange(n_pairs, dtype=jnp.int32), unique_indices=True)

    e_col = jnp.arange(N_EXPERTS, dtype=jnp.int32)[:, None]
    e_row = jnp.arange(N_EXPERTS, dtype=jnp.int32)[None, :]
    later = (e_row > e_col) & (counts > 0)[None, :]
    next_nonempty = jnp.min(jnp.where(later, e_row, N_EXPERTS), axis=1)
    next_nonempty = jnp.where(next_nonempty < N_EXPERTS, next_nonempty, -1)
    next_e = jnp.sum(jnp.where(block_e[:, None] == e_row, next_nonempty[None, :], 0), axis=1)
    y2 = _experts(block_e, next_e, n_used, row_pair, h2, w_expert_gate[0], w_expert_up[0], w_expert_down[0])
    out = _combine(y2, gate[0:TOP_K].T, x1, norm_final_g.reshape(1, d), _tile(s, 256))
    return out.reshape(b, s, d)
```

```python
import functools
import math

import numpy as np
import jax
import jax.numpy as jnp
from jax import lax
from jax.experimental import pallas as pl
from jax.experimental.pallas import tpu as pltpu

F32 = jnp.float32
BF16 = jnp.bfloat16

N_HEADS = 32
N_KV_HEADS = 4
HEAD_DIM = 64
WINDOW = 128
N_BUCKETS = 32
MAX_DISTANCE = 128
GLA_HEADS = 4
GLA_DK = 256
GLA_DV = 512
GLA_LOWRANK = 16
GLA_TAU = 16.0
GLA_CHUNK = 64
N_GROUPS = 4
EXPERTS_PER_GROUP = 8
N_EXPERTS = 32
TOP_K = 2
MOE_BLOCK = 128
RMS_EPS = 1e-6
NEG_INF = -1e30

COL_GA = 8704
COL_GATES = 8720
MAIN_COLS = 8704
GATE_COLS = 4096
COL_AQ = GATE_COLS + 0
COL_AK = GATE_COLS + 2048
COL_AV = GATE_COLS + 2304
COL_GQ = GATE_COLS + 2560
COL_GK = GATE_COLS + 3584
COL_GV = GATE_COLS + 4608
COL_GR = GATE_COLS + 6656

LANES = 128
VMEM_LIMIT = 56 * 1024 * 1024
EXPERT_VMEM_LIMIT = 60 * 1024 * 1024


def _cparams(sem, vmem=VMEM_LIMIT):
    return pltpu.CompilerParams(dimension_semantics=sem, vmem_limit_bytes=vmem)


def _split3(x):
    hi = x.astype(BF16)
    r1 = x - hi.astype(F32)
    mid = r1.astype(BF16)
    lo = (r1 - mid.astype(F32)).astype(BF16)
    return hi, mid, lo


def _sigmoid(x):
    return 0.5 * jnp.tanh(0.5 * x) + 0.5


def _dot(a, b):
    return jnp.dot(a, b, preferred_element_type=F32)


def _dot_nt(a, b):
    return lax.dot_general(a, b, (((1,), (1,)), ((), ())), preferred_element_type=F32)


def _dot_tn(a, b):
    return lax.dot_general(a, b, (((0,), (0,)), ((), ())), preferred_element_type=F32)


def _norm_loga_kernel(x_ref, g_ref, wga_ref, w2_ref, b_ref, h_ref, la_ref):
    x = x_ref[...]
    ms = jnp.mean(x * x, axis=-1, keepdims=True)
    hb = (x * lax.rsqrt(ms + RMS_EPS) * g_ref[...]).astype(BF16)
    h_ref[...] = hb
    row = lax.broadcasted_iota(jnp.int32, wga_ref.shape, 0)
    wga = jnp.where(row < GLA_LOWRANK, wga_ref[...], 0.0).astype(BF16)
    ga = _dot_nt(hb, wga)
    z = _dot(ga.astype(BF16), w2_ref[...]) + b_ref[...]
    la_ref[...] = (jnp.minimum(z, 0.0) - jnp.log1p(jnp.exp(-jnp.abs(z)))) * (1.0 / GLA_TAU)


def _norm_loga(x, g, wga, w2, b, tm):
    s, d = x.shape
    nq = w2.shape[1]
    return pl.pallas_call(
        _norm_loga_kernel,
        grid=(s // tm,),
        in_specs=[
            pl.BlockSpec((tm, d), lambda i: (i, 0)),
            pl.BlockSpec((1, d), lambda i: (0, 0)),
            pl.BlockSpec((LANES, d), lambda i: (COL_GA // LANES, 0)),
            pl.BlockSpec((LANES, nq), lambda i: (0, 0)),
            pl.BlockSpec((1, nq), lambda i: (0, 0)),
        ],
        out_specs=[
            pl.BlockSpec((tm, d), lambda i: (i, 0)),
            pl.BlockSpec((tm, nq), lambda i: (i, 0)),
        ],
        out_shape=[
            jax.ShapeDtypeStruct((s, d), BF16),
            jax.ShapeDtypeStruct((s, nq), F32),
        ],
        compiler_params=_cparams(("parallel",)),
        name="norm_loga",
    )(x, g, wga, w2, b)


def _in_proj_kernel(h_hbm, wt_hbm, o_ref, h_vmem, w_stage, wbf, sem, *, n_main, n_tiles, tn):
    j = pl.program_id(0)
    i = pl.program_id(1)
    tm = o_ref.shape[0]

    def w_copy(jj):
        row0 = jnp.where(jj < n_main, jj * tn, COL_GATES + (jj - n_main) * tn)
        return pltpu.make_async_copy(wt_hbm.at[pl.ds(pl.multiple_of(row0, 8), tn)], w_stage, sem.at[1])

    @pl.when((j == 0) & (i == 0))
    def _():
        h_copy = pltpu.make_async_copy(h_hbm, h_vmem, sem.at[0])
        h_copy.start()
        w_copy(0).start()
        h_copy.wait()

    @pl.when(i == 0)
    def _():
        w_copy(j).wait()
        wbf[...] = w_stage[...].astype(BF16)

        @pl.when(j + 1 < n_tiles)
        def _():
            w_copy(j + 1).start()

    h = h_vmem[pl.ds(pl.multiple_of(i * tm, tm), tm), :]
    o_ref[...] = _dot_nt(h, wbf[...]).astype(o_ref.dtype)


def _in_proj(h, w_t, tm, tn):
    s, d = h.shape
    n_main = MAIN_COLS // tn
    n_gate = GATE_COLS // tn
    n_tiles = n_main + n_gate
    kern = functools.partial(_in_proj_kernel, n_main=n_main, n_tiles=n_tiles, tn=tn)
    return pl.pallas_call(
        kern,
        grid=(n_tiles, s // tm),
        in_specs=[pl.BlockSpec(memory_space=pl.ANY), pl.BlockSpec(memory_space=pl.ANY)],
        out_specs=pl.BlockSpec((tm, tn), lambda j, i: (i, (j + n_gate) % n_tiles)),
        out_shape=jax.ShapeDtypeStruct((s, n_tiles * tn), BF16),
        scratch_shapes=[
            pltpu.VMEM((s, d), BF16),
            pltpu.VMEM((tn, d), F32),
            pltpu.VMEM((tn, d), BF16),
            pltpu.SemaphoreType.DMA((2,)),
        ],
        compiler_params=_cparams(("arbitrary", "arbitrary")),
        name="in_proj",
    )(h, w_t)


def _folded_maps():
    j = np.arange(WINDOW)[:, None]
    i = np.arange(WINDOW)[None, :]
    n = (i - j) % WINDOW
    max_exact = N_BUCKETS // 2
    ratio = np.maximum(n, max_exact).astype(np.float32) / np.float32(max_exact)
    large = max_exact + (np.log(ratio) / np.float32(math.log(MAX_DISTANCE / max_exact))
                         * (N_BUCKETS - max_exact)).astype(np.int32)
    large = np.minimum(large, N_BUCKETS - 1)
    bucket = np.where(n < max_exact, n, large).astype(np.int32).reshape(1, -1)
    is_prev = (j > i).astype(np.int32).reshape(1, -1)
    return bucket, is_prev


def _rel_bias_kernel(tab_ref, bucket_ref, prev_ref, o_ref):
    nb = tab_ref.shape[1]
    width = bucket_ref.shape[1]
    onehot = (lax.broadcasted_iota(jnp.int32, (nb, width), 0) == bucket_ref[...]).astype(BF16)
    hi, mid, lo = _split3(tab_ref[...])
    bias = _dot(hi, onehot) + _dot(mid, onehot) + _dot(lo, onehot)
    o_ref[0] = bias
    o_ref[1] = jnp.where(prev_ref[...] > 0, NEG_INF, bias)


def _rel_bias(table_t, bucket, is_prev):
    nh = table_t.shape[0]
    width = bucket.shape[1]
    return pl.pallas_call(
        _rel_bias_kernel,
        out_shape=jax.ShapeDtypeStruct((2, nh, width), F32),
        compiler_params=pltpu.CompilerParams(vmem_limit_bytes=VMEM_LIMIT),
        name="rel_bias",
    )(table_t, bucket, is_prev)


SWA_SCORES_AHEAD = 10


def _swa_kernel(sink_ref, q_ref, kc_ref, kp_ref, vc_ref, vp_ref, bias_ref, o_ref):
    w = WINDOW
    hd = HEAD_DIM
    gq = N_HEADS // N_KV_HEADS
    is_prev = (lax.broadcasted_iota(jnp.int32, (w, w), 0) > lax.broadcasted_iota(jnp.int32, (w, w), 1))
    zeros = jnp.zeros((2 * w, hd), BF16)
    scale = HEAD_DIM ** -0.5
    k_pl, v_pl = [], []
    for g in range(N_KV_HEADS):
        cs = slice(g * hd, (g + 1) * hd)
        k_g = jnp.concatenate([kp_ref[:, cs], kc_ref[:, cs]], axis=0) * scale
        v_g = jnp.concatenate([vp_ref[:, cs], vc_ref[:, cs]], axis=0)
        k_pl.append((jnp.concatenate([k_g, zeros], axis=1), jnp.concatenate([zeros, k_g], axis=1)))
        v_pl.append((jnp.concatenate([v_g, zeros], axis=1), jnp.concatenate([zeros, v_g], axis=1)))

    def scores(hh):
        pair = hh - hh % 2
        return _dot_nt(k_pl[hh // gq][hh % 2], q_ref[:, pair * hd:(pair + 2) * hd])

    def attend(hh, st2):
        st = jnp.where(is_prev, st2[:w], st2[w:]) + bias_ref[hh]
        sink = sink_ref[hh]
        m = jnp.maximum(jnp.max(st, axis=0, keepdims=True), sink)
        p = jnp.exp(st - m)
        recip = 1.0 / (jnp.sum(p, axis=0, keepdims=True) + jnp.exp(sink - m))
        probs = p * recip
        p_cat = jnp.concatenate([jnp.where(is_prev, probs, 0.0), jnp.where(is_prev, 0.0, probs)], axis=0)
        return _dot_tn(p_cat.astype(BF16), v_pl[hh // gq][hh % 2])

    ahead = SWA_SCORES_AHEAD
    st2 = {hh: scores(hh) for hh in range(ahead)}
    acc = None
    for hh in range(N_HEADS):
        if hh + ahead < N_HEADS:
            st2[hh + ahead] = scores(hh + ahead)
        part = attend(hh, st2.pop(hh))
        if hh % 2 == 0:
            acc = part
        else:
            o_ref[:, (hh - 1) * hd:(hh + 1) * hd] = (acc + part).astype(o_ref.dtype)


def _swa(proj, sinks, bias):
    s = proj.shape[0]
    w = WINDOW
    nb = s // w
    kvw = N_KV_HEADS * HEAD_DIM
    dq = N_HEADS * HEAD_DIM
    prev = lambda n: jnp.maximum(n - 1, 0)
    return pl.pallas_call(
        _swa_kernel,
        grid=(nb,),
        in_specs=[
            pl.BlockSpec(memory_space=pltpu.SMEM),
            pl.BlockSpec((w, dq), lambda n: (n, COL_AQ // dq)),
            pl.BlockSpec((w, kvw), lambda n: (n, COL_AK // kvw)),
            pl.BlockSpec((w, kvw), lambda n: (prev(n), COL_AK // kvw)),
            pl.BlockSpec((w, kvw), lambda n: (n, COL_AV // kvw)),
            pl.BlockSpec((w, kvw), lambda n: (prev(n), COL_AV // kvw)),
            pl.BlockSpec((None, N_HEADS, w, w), lambda n: (jnp.where(n == 0, 1, 0), 0, 0, 0)),
        ],
        out_specs=pl.BlockSpec((w, dq), lambda n: (n, 0)),
        out_shape=jax.ShapeDtypeStruct((s, dq), BF16),
        compiler_params=_cparams(("parallel",)),
        name="swa",
    )(sinks, proj, proj, proj, proj, proj, bias)


def _gla_kernel(*refs):
    nh = GLA_HEADS
    q_refs, k_refs, v_refs, gr_refs = (refs[i * nh:(i + 1) * nh] for i in range(4))
    la_ref, gn_ref, o_ref, state_ref = refs[4 * nh:]
    c = GLA_CHUNK
    rows, dk = q_refs[0].shape
    dv = v_refs[0].shape[1]

    @pl.when(pl.program_id(0) == 0)
    def _():
        state_ref[...] = jnp.zeros_like(state_ref)

    la_parts = _split3(la_ref[...])
    ri = lax.broadcasted_iota(jnp.int32, (rows, rows), 0)
    ci = lax.broadcasted_iota(jnp.int32, (rows, rows), 1)
    tri = ((ri // c == ci // c) & (ri >= ci)).astype(BF16)
    cum_all = _dot(tri, la_parts[0]) + _dot(tri, la_parts[1]) + _dot(tri, la_parts[2])
    causal = (lax.broadcasted_iota(jnp.int32, (c, c), 0) >= lax.broadcasted_iota(jnp.int32, (c, c), 1))
    gn = gn_ref[...]
    n_chunks = rows // c

    o_intra, q_in, upd, decay = {}, {}, {}, {}
    for h in range(nh):
        for ch in range(n_chunks):
            sl = slice(ch * c, (ch + 1) * c)
            cum_c = cum_all[sl, h * dk:(h + 1) * dk]
            ref = cum_c[c // 2 - 1:c // 2]
            last = cum_c[c - 1:c]
            q = q_refs[h][sl, :].astype(F32) * (dk ** -0.5)
            k = k_refs[h][sl, :].astype(F32)
            v = v_refs[h][sl, :]
            q_intra = (q * jnp.exp(cum_c - ref)).astype(BF16)
            k_intra = (k * jnp.exp(ref - cum_c)).astype(BF16)
            a = jnp.where(causal, _dot_nt(q_intra, k_intra), 0.0)
            o_intra[h, ch] = _dot(a.astype(BF16), v)
            q_in[h, ch] = (q * jnp.exp(cum_c)).astype(BF16)
            k_dec = (k * jnp.exp(last - cum_c)).astype(BF16)
            upd[h, ch] = _dot_tn(k_dec, v)
            decay[h, ch] = jnp.broadcast_to(jnp.exp(last), (LANES, dk)).T

    state = [state_ref[h] for h in range(nh)]
    for ch in range(n_chunks):
        sl = slice(ch * c, (ch + 1) * c)
        for h in range(nh):
            o = o_intra[h, ch] + _dot(q_in[h, ch], state[h].astype(BF16))
            state[h] = jnp.concatenate(
                [state[h][:, jv * LANES:(jv + 1) * LANES] * decay[h, ch]
                 + upd[h, ch][:, jv * LANES:(jv + 1) * LANES] for jv in range(dv // LANES)], axis=1)
            ms = jnp.mean(o * o, axis=-1, keepdims=True)
            on = o * lax.rsqrt(ms + RMS_EPS) * gn
            gr = gr_refs[h][sl, :].astype(F32)
            o_ref[sl, h * dv:(h + 1) * dv] = (on * (gr * _sigmoid(gr))).astype(o_ref.dtype)
    for h in range(nh):
        state_ref[h] = state[h]


def _gla(proj, la, gn, rows):
    s = proj.shape[0]
    dk, dv = GLA_DK, GLA_DV
    nh = GLA_HEADS

    def head_specs(width, col0):
        return [pl.BlockSpec((rows, width), functools.partial(lambda i, h: (i, col0 // width + h), h=h))
                for h in range(nh)]

    in_specs = (head_specs(dk, COL_GQ) + head_specs(dk, COL_GK) + head_specs(dv, COL_GV)
                + head_specs(dv, COL_GR)
                + [pl.BlockSpec((rows, nh * dk), lambda i: (i, 0)), pl.BlockSpec((1, dv), lambda i: (0, 0))])
    return pl.pallas_call(
        _gla_kernel,
        grid=(s // rows,),
        in_specs=in_specs,
        out_specs=pl.BlockSpec((rows, nh * dv), lambda i: (i, 0)),
        out_shape=jax.ShapeDtypeStruct((s, nh * dv), BF16),
        scratch_shapes=[pltpu.VMEM((nh, dk, dv), F32)],
        compiler_params=_cparams(("arbitrary",)),
        name="gla",
    )(*([proj] * (4 * nh)), la, gn)


MERGE_SUB_ROWS = 128


def _merge_out_kernel(oa_ref, og_ref, ga_ref, gg_ref, x_ref, wo_ref, g_ref, wr_ref,
                      x1_ref, h2_ref, lt_ref):
    sub = MERGE_SUB_ROWS
    tiles = [slice(t * sub, (t + 1) * sub) for t in range(x_ref.shape[0] // sub)]
    merged = [(_sigmoid(ga_ref[rs, :].astype(F32)) * oa_ref[rs, :].astype(F32)
               + _sigmoid(gg_ref[rs, :].astype(F32)) * og_ref[rs, :].astype(F32)).astype(BF16) for rs in tiles]
    projected = [_dot(m, wo_ref[...]) for m in merged]
    for rs, y in zip(tiles, projected):
        x1 = x_ref[rs, :] + y
        x1_ref[rs, :] = x1
        ms = jnp.mean(x1 * x1, axis=-1, keepdims=True)
        h2b = (x1 * lax.rsqrt(ms + RMS_EPS) * g_ref[...]).astype(BF16)
        h2_ref[rs] = h2b.reshape((sub,) + h2_ref.shape[1:])
        lt_ref[:, rs] = _dot_nt(wr_ref[...], h2b)


def _merge_out(o_attn, o_gla, gates, x, wo, g, wr, tm):
    s, d = x.shape
    nr = wr.shape[0]
    row = lambda i: (i, 0)
    return pl.pallas_call(
        _merge_out_kernel,
        grid=(s // tm,),
        in_specs=[
            pl.BlockSpec((tm, d), row),
            pl.BlockSpec((tm, d), row),
            pl.BlockSpec((tm, d), lambda i: (i, 0)),
            pl.BlockSpec((tm, d), lambda i: (i, 1)),
            pl.BlockSpec((tm, d), row),
            pl.BlockSpec((d, d), lambda i: (0, 0)),
            pl.BlockSpec((1, d), lambda i: (0, 0)),
            pl.BlockSpec((nr, d), lambda i: (0, 0)),
        ],
        out_specs=[
            pl.BlockSpec((tm, d), row),
            pl.BlockSpec((tm, d // LANES, LANES), lambda i: (i, 0, 0)),
            pl.BlockSpec((nr, tm), lambda i: (0, i)),
        ],
        out_shape=[
            jax.ShapeDtypeStruct((s, d), F32),
            jax.ShapeDtypeStruct((s, d // LANES, LANES), BF16),
            jax.ShapeDtypeStruct((nr, s), F32),
        ],
        compiler_params=_cparams(("parallel",)),
        name="merge_out",
    )(o_attn, o_gla, gates, gates, x, wo, g, wr)


ROUTER_ROWS = 8 + N_EXPERTS


def _route_kernel(lt_ref, bg_ref, be_ref, idx_ref, gate_ref, cnt_ref, carry_ref):
    tb = lt_ref.shape[1]
    epg = EXPERTS_PER_GROUP

    @pl.when(pl.program_id(0) == 0)
    def _():
        carry_ref[...] = jnp.zeros_like(carry_ref)

    logits = lt_ref[...]
    gl = logits[0:N_GROUPS]
    g_exp = jnp.exp(gl - jnp.max(gl, axis=0, keepdims=True))
    g_prob = g_exp / jnp.sum(g_exp, axis=0, keepdims=True)
    gb = gl + bg_ref[0:N_GROUPS]
    rowg = lax.broadcasted_iota(jnp.int32, (N_GROUPS, tb), 0)
    g_idx = jnp.min(jnp.where(gb == jnp.max(gb, axis=0, keepdims=True), rowg, N_GROUPS),
                    axis=0, keepdims=True)
    p_group = jnp.sum(jnp.where(rowg == g_idx, g_prob, 0.0), axis=0, keepdims=True)

    el = logits[8:8 + epg]
    eb = jnp.broadcast_to(be_ref[0:epg], (epg, tb))
    for g in range(1, N_GROUPS):
        pick = g_idx == g
        el = jnp.where(pick, logits[8 + g * epg:8 + (g + 1) * epg], el)
        eb = jnp.where(pick, be_ref[g * epg:(g + 1) * epg], eb)
    e_exp = jnp.exp(el - jnp.max(el, axis=0, keepdims=True))
    e_prob = e_exp / jnp.sum(e_exp, axis=0, keepdims=True)
    score = el + eb
    rowe = lax.broadcasted_iota(jnp.int32, (epg, tb), 0)
    i1 = jnp.min(jnp.where(score == jnp.max(score, axis=0, keepdims=True), rowe, epg),
                 axis=0, keepdims=True)
    score2 = jnp.where(rowe == i1, -jnp.inf, score)
    i2 = jnp.min(jnp.where(score2 == jnp.max(score2, axis=0, keepdims=True), rowe, epg),
                 axis=0, keepdims=True)
    q1 = jnp.sum(jnp.where(rowe == i1, e_prob, 0.0), axis=0, keepdims=True)
    q2 = jnp.sum(jnp.where(rowe == i2, e_prob, 0.0), axis=0, keepdims=True)
    qs = q1 + q2
    gate1 = p_group * q1 / qs
    gate2 = p_group * q2 / qs
    e1 = g_idx * epg + i1
    e2 = g_idx * epg + i2

    rowx = lax.broadcasted_iota(jnp.int32, (N_EXPERTS, tb), 0)
    hit1 = rowx == e1
    hit2 = rowx == e2
    member = (hit1 | hit2).astype(BF16)
    before = (lax.broadcasted_iota(jnp.int32, (tb, tb), 0)
              < lax.broadcasted_iota(jnp.int32, (tb, tb), 1)).astype(BF16)
    carry = carry_ref[...]
    count = _dot(member, before) + carry[:, 0:1]
    r1 = jnp.sum(jnp.where(hit1, count, 0.0), axis=0, keepdims=True).astype(jnp.int32)
    r2 = jnp.sum(jnp.where(hit2, count, 0.0), axis=0, keepdims=True).astype(jnp.int32)
    carry = carry + jnp.sum(member.astype(F32), axis=1, keepdims=True)
    carry_ref[...] = carry
    cnt_ref[...] = carry

    row8 = lax.broadcasted_iota(jnp.int32, (8, tb), 0)
    idx_ref[...] = jnp.where(row8 == 0, e1, jnp.where(row8 == 1, e2,
                             jnp.where(row8 == 2, r1, jnp.where(row8 == 3, r2, 0))))
    gate_ref[...] = jnp.where(row8 == 0, gate1, jnp.where(row8 == 1, gate2, 0.0))


def _route(lt, bg, be, tb):
    nr, s = lt.shape
    return pl.pallas_call(
        _route_kernel,
        grid=(s // tb,),
        in_specs=[
            pl.BlockSpec((nr, tb), lambda i: (0, i)),
            pl.BlockSpec((8, 1), lambda i: (0, 0)),
            pl.BlockSpec((N_EXPERTS, 1), lambda i: (0, 0)),
        ],
        out_specs=[
            pl.BlockSpec((8, tb), lambda i: (0, i)),
            pl.BlockSpec((8, tb), lambda i: (0, i)),
            pl.BlockSpec((N_EXPERTS, LANES), lambda i: (0, 0)),
        ],
        out_shape=[
            jax.ShapeDtypeStruct((8, s), jnp.int32),
            jax.ShapeDtypeStruct((8, s), F32),
            jax.ShapeDtypeStruct((N_EXPERTS, LANES), F32),
        ],
        scratch_shapes=[pltpu.VMEM((N_EXPERTS, LANES), F32)],
        compiler_params=_cparams(("arbitrary",)),
        name="route",
    )(lt, bg, be)


def _expert_kernel(be_ref, nx_ref, nu_ref, pair_ref, h2_hbm, wg_hbm, wu_hbm, wd_hbm, y_hbm,
                   h2v, xg, ystage, wgf, wuf, wdf, wgb, wub, wdb, hsem, wsem, ysem):
    b = pl.program_id(0)
    n_used = nu_ref[0]
    rows = MOE_BLOCK
    n_tok = h2v.shape[0]
    slot = b % 2

    def weight_copies(e):
        return (pltpu.make_async_copy(wg_hbm.at[e], wgf, wsem.at[0]),
                pltpu.make_async_copy(wu_hbm.at[e], wuf, wsem.at[1]),
                pltpu.make_async_copy(wd_hbm.at[e], wdf, wsem.at[2]))

    def gather_row(blk, r, half):
        pair = jnp.maximum(pair_ref[blk * rows + r], 0)
        xg[half, r] = h2v[jnp.where(pair >= n_tok, pair - n_tok, pair)]

    def row_write(blk, r, half, dummy):
        pair = pair_ref[blk * rows + r]
        dst = jnp.where((pair < 0) | dummy, TOP_K * n_tok + half * rows + r, pair)
        return pltpu.make_async_copy(ystage.at[half, r], y_hbm.at[dst], ysem.at[half])

    def wait_row_writes(half):
        pltpu.make_async_copy(ystage.at[half], y_hbm.at[pl.ds(0, rows)], ysem.at[half]).wait()

    weight_priority = 1

    @pl.when(b == 0)
    def _():
        h2_copy = pltpu.make_async_copy(h2_hbm, h2v, hsem.at[0])
        h2_copy.start()
        for cp in weight_copies(be_ref[0]):
            cp.start(priority=weight_priority)
        ystage[...] = jnp.zeros_like(ystage)
        h2_copy.wait()
        spare_fill = pltpu.make_async_copy(ystage.at[0], y_hbm.at[pl.ds(TOP_K * n_tok, rows)], hsem.at[0])
        spare_fill.start()
        spare_fill.wait()

        def body(r, carry):
            gather_row(0, r, 0)
            return carry
        lax.fori_loop(0, rows, body, 0)

    @pl.when(b < n_used)
    def _():
        first_of_expert = (b == 0) | (be_ref[b] != be_ref[jnp.maximum(b - 1, 0)])

        @pl.when(first_of_expert)
        def _():
            for cp in weight_copies(be_ref[b]):
                cp.wait()
            wgb[...] = wgf[...].astype(BF16)
            wub[...] = wuf[...].astype(BF16)
            wdb[...] = wdf[...].astype(BF16)
            nxt = nx_ref[b]

            @pl.when(nxt >= 0)
            def _():
                for cp in weight_copies(nxt):
                    cp.start(priority=weight_priority)

        xb = xg[slot].reshape(rows, wgb.shape[0])
        prev_blk = jnp.maximum(b - 1, 0)
        for r in range(rows):
            row_write(prev_blk, r, 1 - slot, b == 0).start()
        for r in range(rows):
            gather_row(b + 1, r, 1 - slot)
        hg = _dot(xb, wgb[...])
        hu = _dot(xb, wub[...])
        act = (hg * _sigmoid(hg)) * hu
        y = _dot(act.astype(BF16), wdb[...]).astype(BF16)

        @pl.when(b >= 1)
        def _():
            wait_row_writes(slot)

        ystage[slot] = y.reshape(ystage.shape[1:])

    @pl.when(b == n_used)
    def _():
        def body(r, carry):
            row_write(b - 1, r, 1 - slot, False).start()
            return carry
        lax.fori_loop(0, rows, body, 0)
        wait_row_writes(slot)
        wait_row_writes(1 - slot)


def _experts(block_e, next_e, n_used, row_pair, h2c, wg, wu, wd):
    n_rows = row_pair.shape[0]
    n_blocks = n_rows // MOE_BLOCK
    n_tok = h2c.shape[0]
    _, d, de = wg.shape
    slab = (d // LANES, LANES)
    grid_spec = pltpu.PrefetchScalarGridSpec(
        num_scalar_prefetch=4,
        grid=(n_blocks,),
        in_specs=[pl.BlockSpec(memory_space=pl.ANY)] * 4,
        out_specs=pl.BlockSpec(memory_space=pl.ANY),
        scratch_shapes=[
            pltpu.VMEM((n_tok,) + slab, BF16),
            pltpu.VMEM((2, MOE_BLOCK) + slab, BF16),
            pltpu.VMEM((2, MOE_BLOCK) + slab, BF16),
            pltpu.VMEM((d, de), F32),
            pltpu.VMEM((d, de), F32),
            pltpu.VMEM((de, d), F32),
            pltpu.VMEM((d, de), BF16),
            pltpu.VMEM((d, de), BF16),
            pltpu.VMEM((de, d), BF16),
            pltpu.SemaphoreType.DMA((1,)),
            pltpu.SemaphoreType.DMA((3,)),
            pltpu.SemaphoreType.DMA((2,)),
        ],
    )
    return pl.pallas_call(
        _expert_kernel,
        grid_spec=grid_spec,
        out_shape=jax.ShapeDtypeStruct((TOP_K * n_tok + 2 * MOE_BLOCK,) + slab, BF16),
        compiler_params=_cparams(("arbitrary",), vmem=EXPERT_VMEM_LIMIT),
        name="experts",
    )(block_e, next_e, n_used, row_pair, h2c, wg, wu, wd)


def _combine_kernel(y0_ref, y1_ref, gate_ref, x1_ref, g_ref, o_ref):
    gates = gate_ref[...]
    y0 = y0_ref[...].reshape(x1_ref.shape).astype(F32)
    y1 = y1_ref[...].reshape(x1_ref.shape).astype(F32)
    x2 = x1_ref[...] + (y0 * gates[:, 0:1] + y1 * gates[:, 1:2])
    ms = jnp.mean(x2 * x2, axis=-1, keepdims=True)
    o_ref[...] = x2 * lax.rsqrt(ms + RMS_EPS) * g_ref[...]


def _combine(y2, gates, x1, g, tm):
    s, d = x1.shape
    slab = (d // LANES, LANES)
    n_steps = s // tm
    return pl.pallas_call(
        _combine_kernel,
        grid=(n_steps,),
        in_specs=[
            pl.BlockSpec((tm,) + slab, lambda i: (i, 0, 0)),
            pl.BlockSpec((tm,) + slab, lambda i: (n_steps + i, 0, 0)),
            pl.BlockSpec((tm, TOP_K), lambda i: (i, 0)),
            pl.BlockSpec((tm, d), lambda i: (i, 0)),
            pl.BlockSpec((1, d), lambda i: (0, 0)),
        ],
        out_specs=pl.BlockSpec((tm, d), lambda i: (i, 0)),
        out_shape=jax.ShapeDtypeStruct((s, d), F32),
        compiler_params=_cparams(("parallel",)),
        name="combine",
    )(y2, y2, gates, x1, g)


def _tile(n, pref):
    return pref if n % pref == 0 else n


def kernel(x, norm_mix_g, w_in, w_gla_a2, b_gla_a, gla_norm_g, attn_sinks, rel_bias_table, w_out,
           norm_ffn_g, w_router_group, b_router_group, w_router_expert, b_router_expert,
           w_expert_gate, w_expert_up, w_expert_down, norm_final_g):
    b, s, d = x.shape
    assert b == 1 and w_in.shape[0] == 1, "single batch, single layer"
    assert (s * TOP_K) % MOE_BLOCK == 0
    x2d = x.reshape(s, d)
    w_in_t = w_in[0].T

    w2 = jnp.pad(w_gla_a2[0], ((0, LANES - GLA_LOWRANK), (0, 0))).astype(BF16)
    h, log_a = _norm_loga(x2d, norm_mix_g[0].reshape(1, d), w_in_t, w2, b_gla_a[0].reshape(1, -1),
                          _tile(s, 512))
    proj = _in_proj(h, w_in_t, _tile(s, 4096), 512)

    bucket, is_prev = _folded_maps()
    bias = _rel_bias(rel_bias_table.T, jnp.asarray(bucket), jnp.asarray(is_prev))
    bias = bias.reshape(2, N_HEADS, WINDOW, WINDOW)
    o_attn = _swa(proj, attn_sinks[0], bias)
    o_gla = _gla(proj, log_a, gla_norm_g[0].reshape(1, -1), _tile(s, 256))

    wr = jnp.zeros((ROUTER_ROWS, d), F32)
    wr = wr.at[0:N_GROUPS].set(w_router_group[0].T).at[8:].set(w_router_expert[0].T).astype(BF16)
    x1, h2, logits_t = _merge_out(o_attn, o_gla, proj, x2d, w_out[0].astype(BF16),
                                  norm_ffn_g[0].reshape(1, d), wr, _tile(s, 256))

    bg = jnp.pad(b_router_group[0], (0, 8 - N_GROUPS)).reshape(8, 1)
    idx, gate, counts = _route(logits_t, bg, b_router_expert[0].reshape(N_EXPERTS, 1), _tile(s, 512))
    counts = counts[:, 0].astype(jnp.int32)
    padded = (counts + MOE_BLOCK - 1) // MOE_BLOCK * MOE_BLOCK
    pend = jnp.cumsum(padded)
    pstart = pend - padded
    expert = idx[0:TOP_K]
    eids = jnp.arange(N_EXPERTS, dtype=jnp.int32)[:, None, None]
    dest = jnp.sum(jnp.where(expert[None] == eids, pstart[:, None, None], 0), axis=0) + idx[TOP_K:2 * TOP_K]
    n_pairs = s * TOP_K
    n_blocks = (n_pairs + N_EXPERTS * (MOE_BLOCK - 1) + MOE_BLOCK - 1) // MOE_BLOCK
    n_used = (pend[-1] // MOE_BLOCK).reshape(1)
    block_start = jnp.arange(n_blocks, dtype=jnp.int32) * MOE_BLOCK
    block_e = jnp.minimum(jnp.sum((pend[None, :] <= block_start[:, None]).astype(jnp.int32), axis=1),
                          N_EXPERTS - 1)
    row_pair = jnp.full((n_blocks * MOE_BLOCK,), -1, jnp.int32).at[dest.reshape(-1)].set(
        jnp.arange(n_pairs, dtype=jnp.int32), unique_indices=True)

    e_col = jnp.arange(N_EXPERTS, dtype=jnp.int32)[:, None]
    e_row = jnp.arange(N_EXPERTS, dtype=jnp.int32)[None, :]
    later = (e_row > e_col) & (counts > 0)[None, :]
    next_nonempty = jnp.min(jnp.where(later, e_row, N_EXPERTS), axis=1)
    next_nonempty = jnp.where(next_nonempty < N_EXPERTS, next_nonempty, -1)
    next_e = jnp.sum(jnp.where(block_e[:, None] == e_row, next_nonempty[None, :], 0), axis=1)
    y2 = _experts(block_e, next_e, n_used, row_pair, h2, w_expert_gate[0], w_expert_up[0], w_expert_down[0])
    out = _combine(y2, gate[0:TOP_K].T, x1, norm_final_g.reshape(1, d), _tile(s, 256))
    return out.reshape(b, s, d)
```

```python
import functools
import math

import numpy as np
import jax
import jax.numpy as jnp
from jax import lax
from jax.experimental import pallas as pl
from jax.experimental.pallas import tpu as pltpu

F32 = jnp.float32
BF16 = jnp.bfloat16

N_HEADS = 32
N_KV_HEADS = 4
HEAD_DIM = 64
WINDOW = 128
N_BUCKETS = 32
MAX_DISTANCE = 128
GLA_HEADS = 4
GLA_DK = 256
GLA_DV = 512
GLA_LOWRANK = 16
GLA_TAU = 16.0
GLA_CHUNK = 64
N_GROUPS = 4
EXPERTS_PER_GROUP = 8
N_EXPERTS = 32
TOP_K = 2
MOE_BLOCK = 128
RMS_EPS = 1e-6
NEG_INF = -1e30

COL_GA = 8704
COL_GATES = 8720
MAIN_COLS = 8704
GATE_COLS = 4096
COL_AQ = GATE_COLS + 0
COL_AK = GATE_COLS + 2048
COL_AV = GATE_COLS + 2304
COL_GQ = GATE_COLS + 2560
COL_GK = GATE_COLS + 3584
COL_GV = GATE_COLS + 4608
COL_GR = GATE_COLS + 6656

LANES = 128
VMEM_LIMIT = 56 * 1024 * 1024
EXPERT_VMEM_LIMIT = 60 * 1024 * 1024


def _cparams(sem, vmem=VMEM_LIMIT):
    return pltpu.CompilerParams(dimension_semantics=sem, vmem_limit_bytes=vmem)


def _split3(x):
    hi = x.astype(BF16)
    r1 = x - hi.astype(F32)
    mid = r1.astype(BF16)
    lo = (r1 - mid.astype(F32)).astype(BF16)
    return hi, mid, lo


def _sigmoid(x):
    return 0.5 * jnp.tanh(0.5 * x) + 0.5


def _dot(a, b):
    return jnp.dot(a, b, preferred_element_type=F32)


def _dot_nt(a, b):
    return lax.dot_general(a, b, (((1,), (1,)), ((), ())), preferred_element_type=F32)


def _dot_tn(a, b):
    return lax.dot_general(a, b, (((0,), (0,)), ((), ())), preferred_element_type=F32)


def _norm_loga_kernel(x_ref, g_ref, wga_ref, w2_ref, b_ref, h_ref, la_ref):
    x = x_ref[...]
    ms = jnp.mean(x * x, axis=-1, keepdims=True)
    hb = (x * lax.rsqrt(ms + RMS_EPS) * g_ref[...]).astype(BF16)
    h_ref[...] = hb
    row = lax.broadcasted_iota(jnp.int32, wga_ref.shape, 0)
    wga = jnp.where(row < GLA_LOWRANK, wga_ref[...], 0.0).astype(BF16)
    ga = _dot_nt(hb, wga)
    z = _dot(ga.astype(BF16), w2_ref[...]) + b_ref[...]
    la_ref[...] = (jnp.minimum(z, 0.0) - jnp.log1p(jnp.exp(-jnp.abs(z)))) * (1.0 / GLA_TAU)


def _norm_loga(x, g, wga, w2, b, tm):
    s, d = x.shape
    nq = w2.shape[1]
    return pl.pallas_call(
        _norm_loga_kernel,
        grid=(s // tm,),
        in_specs=[
            pl.BlockSpec((tm, d), lambda i: (i, 0)),
            pl.BlockSpec((1, d), lambda i: (0, 0)),
            pl.BlockSpec((LANES, d), lambda i: (COL_GA // LANES, 0)),
            pl.BlockSpec((LANES, nq), lambda i: (0, 0)),
            pl.BlockSpec((1, nq), lambda i: (0, 0)),
        ],
        out_specs=[
            pl.BlockSpec((tm, d), lambda i: (i, 0)),
            pl.BlockSpec((tm, nq), lambda i: (i, 0)),
        ],
        out_shape=[
            jax.ShapeDtypeStruct((s, d), BF16),
            jax.ShapeDtypeStruct((s, nq), F32),
        ],
        compiler_params=_cparams(("parallel",)),
        name="norm_loga",
    )(x, g, wga, w2, b)


def _in_proj_kernel(h_hbm, wt_hbm, o_ref, h_vmem, w_stage, wbf, sem, *, n_main, n_tiles, tn):
    j = pl.program_id(0)
    i = pl.program_id(1)
    tm = o_ref.shape[0]

    def w_copy(jj):
        row0 = jnp.where(jj < n_main, jj * tn, COL_GATES + (jj - n_main) * tn)
        return pltpu.make_async_copy(wt_hbm.at[pl.ds(pl.multiple_of(row0, 8), tn)], w_stage, sem.at[1])

    @pl.when((j == 0) & (i == 0))
    def _():
        h_copy = pltpu.make_async_copy(h_hbm, h_vmem, sem.at[0])
        h_copy.start()
        w_copy(0).start()
        h_copy.wait()

    @pl.when(i == 0)
    def _():
        w_copy(j).wait()
        wbf[...] = w_stage[...].astype(BF16)

        @pl.when(j + 1 < n_tiles)
        def _():
            w_copy(j + 1).start()

    h = h_vmem[pl.ds(pl.multiple_of(i * tm, tm), tm), :]
    o_ref[...] = _dot_nt(h, wbf[...]).astype(o_ref.dtype)


def _in_proj(h, w_t, tm, tn):
    s, d = h.shape
    n_main = MAIN_COLS // tn
    n_gate = GATE_COLS // tn
    n_tiles = n_main + n_gate
    kern = functools.partial(_in_proj_kernel, n_main=n_main, n_tiles=n_tiles, tn=tn)
    return pl.pallas_call(
        kern,
        grid=(n_tiles, s // tm),
        in_specs=[pl.BlockSpec(memory_space=pl.ANY), pl.BlockSpec(memory_space=pl.ANY)],
        out_specs=pl.BlockSpec((tm, tn), lambda j, i: (i, (j + n_gate) % n_tiles)),
        out_shape=jax.ShapeDtypeStruct((s, n_tiles * tn), BF16),
        scratch_shapes=[
            pltpu.VMEM((s, d), BF16),
            pltpu.VMEM((tn, d), F32),
            pltpu.VMEM((tn, d), BF16),
            pltpu.SemaphoreType.DMA((2,)),
        ],
        compiler_params=_cparams(("arbitrary", "arbitrary")),
        name="in_proj",
    )(h, w_t)


def _folded_maps():
    j = np.arange(WINDOW)[:, None]
    i = np.arange(WINDOW)[None, :]
    n = (i - j) % WINDOW
    max_exact = N_BUCKETS // 2
    ratio = np.maximum(n, max_exact).astype(np.float32) / np.float32(max_exact)
    large = max_exact + (np.log(ratio) / np.float32(math.log(MAX_DISTANCE / max_exact))
                         * (N_BUCKETS - max_exact)).astype(np.int32)
    large = np.minimum(large, N_BUCKETS - 1)
    bucket = np.where(n < max_exact, n, large).astype(np.int32).reshape(1, -1)
    is_prev = (j > i).astype(np.int32).reshape(1, -1)
    return bucket, is_prev


def _rel_bias_kernel(tab_ref, bucket_ref, prev_ref, o_ref):
    nb = tab_ref.shape[1]
    width = bucket_ref.shape[1]
    onehot = (lax.broadcasted_iota(jnp.int32, (nb, width), 0) == bucket_ref[...]).astype(BF16)
    hi, mid, lo = _split3(tab_ref[...])
    bias = _dot(hi, onehot) + _dot(mid, onehot) + _dot(lo, onehot)
    o_ref[0] = bias
    o_ref[1] = jnp.where(prev_ref[...] > 0, NEG_INF, bias)


def _rel_bias(table_t, bucket, is_prev):
    nh = table_t.shape[0]
    width = bucket.shape[1]
    return pl.pallas_call(
        _rel_bias_kernel,
        out_shape=jax.ShapeDtypeStruct((2, nh, width), F32),
        compiler_params=pltpu.CompilerParams(vmem_limit_bytes=VMEM_LIMIT),
        name="rel_bias",
    )(table_t, bucket, is_prev)


SWA_SCORES_AHEAD = 10


def _swa_kernel(sink_ref, q_ref, kc_ref, kp_ref, vc_ref, vp_ref, bias_ref, o_ref):
    w = WINDOW
    hd = HEAD_DIM
    gq = N_HEADS // N_KV_HEADS
    is_prev = (lax.broadcasted_iota(jnp.int32, (w, w), 0) > lax.broadcasted_iota(jnp.int32, (w, w), 1))
    zeros = jnp.zeros((2 * w, hd), BF16)
    scale = HEAD_DIM ** -0.5
    k_pl, v_pl = [], []
    for g in range(N_KV_HEADS):
        cs = slice(g * hd, (g + 1) * hd)
        k_g = jnp.concatenate([kp_ref[:, cs], kc_ref[:, cs]], axis=0) * scale
        v_g = jnp.concatenate([vp_ref[:, cs], vc_ref[:, cs]], axis=0)
        k_pl.append((jnp.concatenate([k_g, zeros], axis=1), jnp.concatenate([zeros, k_g], axis=1)))
        v_pl.append((jnp.concatenate([v_g, zeros], axis=1), jnp.concatenate([zeros, v_g], axis=1)))

    def scores(hh):
        pair = hh - hh % 2
        return _dot_nt(k_pl[hh // gq][hh % 2], q_ref[:, pair * hd:(pair + 2) * hd])

    def attend(hh, st2):
        st = jnp.where(is_prev, st2[:w], st2[w:]) + bias_ref[hh]
        sink = sink_ref[hh]
        m = jnp.maximum(jnp.max(st, axis=0, keepdims=True), sink)
        p = jnp.exp(st - m)
        recip = 1.0 / (jnp.sum(p, axis=0, keepdims=True) + jnp.exp(sink - m))
        probs = p * recip
        p_cat = jnp.concatenate([jnp.where(is_prev, probs, 0.0), jnp.where(is_prev, 0.0, probs)], axis=0)
        return _dot_tn(p_cat.astype(BF16), v_pl[hh // gq][hh % 2])

    ahead = SWA_SCORES_AHEAD
    st2 = {hh: scores(hh) for hh in range(ahead)}
    acc = None
    for hh in range(N_HEADS):
        if hh + ahead < N_HEADS:
            st2[hh + ahead] = scores(hh + ahead)
        part = attend(hh, st2.pop(hh))
        if hh % 2 == 0:
            acc = part
        else:
            o_ref[:, (hh - 1) * hd:(hh + 1) * hd] = (acc + part).astype(o_ref.dtype)


def _swa(proj, sinks, bias):
    s = proj.shape[0]
    w = WINDOW
    nb = s // w
    kvw = N_KV_HEADS * HEAD_DIM
    dq = N_HEADS * HEAD_DIM
    prev = lambda n: jnp.maximum(n - 1, 0)
    return pl.pallas_call(
        _swa_kernel,
        grid=(nb,),
        in_specs=[
            pl.BlockSpec(memory_space=pltpu.SMEM),
            pl.BlockSpec((w, dq), lambda n: (n, COL_AQ // dq)),
            pl.BlockSpec((w, kvw), lambda n: (n, COL_AK // kvw)),
            pl.BlockSpec((w, kvw), lambda n: (prev(n), COL_AK // kvw)),
            pl.BlockSpec((w, kvw), lambda n: (n, COL_AV // kvw)),
            pl.BlockSpec((w, kvw), lambda n: (prev(n), COL_AV // kvw)),
            pl.BlockSpec((None, N_HEADS, w, w), lambda n: (jnp.where(n == 0, 1, 0), 0, 0, 0)),
        ],
        out_specs=pl.BlockSpec((w, dq), lambda n: (n, 0)),
        out_shape=jax.ShapeDtypeStruct((s, dq), BF16),
        compiler_params=_cparams(("parallel",)),
        name="swa",
    )(sinks, proj, proj, proj, proj, proj, bias)


def _gla_kernel(*refs):
    nh = GLA_HEADS
    q_refs, k_refs, v_refs, gr_refs = (refs[i * nh:(i + 1) * nh] for i in range(4))
    la_ref, gn_ref, o_ref, state_ref = refs[4 * nh:]
    c = GLA_CHUNK
    rows, dk = q_refs[0].shape
    dv = v_refs[0].shape[1]

    @pl.when(pl.program_id(0) == 0)
    def _():
        state_ref[...] = jnp.zeros_like(state_ref)

    la_parts = _split3(la_ref[...])
    ri = lax.broadcasted_iota(jnp.int32, (rows, rows), 0)
    ci = lax.broadcasted_iota(jnp.int32, (rows, rows), 1)
    tri = ((ri // c == ci // c) & (ri >= ci)).astype(BF16)
    cum_all = _dot(tri, la_parts[0]) + _dot(tri, la_parts[1]) + _dot(tri, la_parts[2])
    causal = (lax.broadcasted_iota(jnp.int32, (c, c), 0) >= lax.broadcasted_iota(jnp.int32, (c, c), 1))
    gn = gn_ref[...]
    n_chunks = rows // c

    o_intra, q_in, upd, decay = {}, {}, {}, {}
    for h in range(nh):
        for ch in range(n_chunks):
            sl = slice(ch * c, (ch + 1) * c)
            cum_c = cum_all[sl, h * dk:(h + 1) * dk]
            ref = cum_c[c // 2 - 1:c // 2]
            last = cum_c[c - 1:c]
            q = q_refs[h][sl, :].astype(F32) * (dk ** -0.5)
            k = k_refs[h][sl, :].astype(F32)
            v = v_refs[h][sl, :]
            q_intra = (q * jnp.exp(cum_c - ref)).astype(BF16)
            k_intra = (k * jnp.exp(ref - cum_c)).astype(BF16)
            a = jnp.where(causal, _dot_nt(q_intra, k_intra), 0.0)
            o_intra[h, ch] = _dot(a.astype(BF16), v)
            q_in[h, ch] = (q * jnp.exp(cum_c)).astype(BF16)
            k_dec = (k * jnp.exp(last - cum_c)).astype(BF16)
            upd[h, ch] = _dot_tn(k_dec, v)
            decay[h, ch] = jnp.broadcast_to(jnp.exp(last), (LANES, dk)).T

    state = [state_ref[h] for h in range(nh)]
    for ch in range(n_chunks):
        sl = slice(ch * c, (ch + 1) * c)
        for h in range(nh):
            o = o_intra[h, ch] + _dot(q_in[h, ch], state[h].astype(BF16))
            state[h] = jnp.concatenate(
                [state[h][:, jv * LANES:(jv + 1) * LANES] * decay[h, ch]
                 + upd[h, ch][:, jv * LANES:(jv + 1) * LANES] for jv in range(dv // LANES)], axis=1)
            ms = jnp.mean(o * o, axis=-1, keepdims=True)
            on = o * lax.rsqrt(ms + RMS_EPS) * gn
            gr = gr_refs[h][sl, :].astype(F32)
            o_ref[sl, h * dv:(h + 1) * dv] = (on * (gr * _sigmoid(gr))).astype(o_ref.dtype)
    for h in range(nh):
        state_ref[h] = state[h]


def _gla(proj, la, gn, rows):
    s = proj.shape[0]
    dk, dv = GLA_DK, GLA_DV
    nh = GLA_HEADS

    def head_specs(width, col0):
        return [pl.BlockSpec((rows, width), functools.partial(lambda i, h: (i, col0 // width + h), h=h))
                for h in range(nh)]

    in_specs = (head_specs(dk, COL_GQ) + head_specs(dk, COL_GK) + head_specs(dv, COL_GV)
                + head_specs(dv, COL_GR)
                + [pl.BlockSpec((rows, nh * dk), lambda i: (i, 0)), pl.BlockSpec((1, dv), lambda i: (0, 0))])
    return pl.pallas_call(
        _gla_kernel,
        grid=(s // rows,),
        in_specs=in_specs,
        out_specs=pl.BlockSpec((rows, nh * dv), lambda i: (i, 0)),
        out_shape=jax.ShapeDtypeStruct((s, nh * dv), BF16),
        scratch_shapes=[pltpu.VMEM((nh, dk, dv), F32)],
        compiler_params=_cparams(("arbitrary",)),
        name="gla",
    )(*([proj] * (4 * nh)), la, gn)


MERGE_SUB_ROWS = 128


def _merge_out_kernel(oa_ref, og_ref, ga_ref, gg_ref, x_ref, wo_ref, g_ref, wr_ref,
                      x1_ref, h2_ref, lt_ref):
    sub = MERGE_SUB_ROWS
    tiles = [slice(t * sub, (t + 1) * sub) for t in range(x_ref.shape[0] // sub)]
    merged = [(_sigmoid(ga_ref[rs, :].astype(F32)) * oa_ref[rs, :].astype(F32)
               + _sigmoid(gg_ref[rs, :].astype(F32)) * og_ref[rs, :].astype(F32)).astype(BF16) for rs in tiles]
    projected = [_dot(m, wo_ref[...]) for m in merged]
    for rs, y in zip(tiles, projected):
        x1 = x_ref[rs, :] + y
        x1_ref[rs, :] = x1
        ms = jnp.mean(x1 * x1, axis=-1, keepdims=True)
        h2b = (x1 * lax.rsqrt(ms + RMS_EPS) * g_ref[...]).astype(BF16)
        h2_ref[rs] = h2b.reshape((sub,) + h2_ref.shape[1:])
        lt_ref[:, rs] = _dot_nt(wr_ref[...], h2b)


def _merge_out(o_attn, o_gla, gates, x, wo, g, wr, tm):
    s, d = x.shape
    nr = wr.shape[0]
    row = lambda i: (i, 0)
    return pl.pallas_call(
        _merge_out_kernel,
        grid=(s // tm,),
        in_specs=[
            pl.BlockSpec((tm, d), row),
            pl.BlockSpec((tm, d), row),
            pl.BlockSpec((tm, d), lambda i: (i, 0)),
            pl.BlockSpec((tm, d), lambda i: (i, 1)),
            pl.BlockSpec((tm, d), row),
            pl.BlockSpec((d, d), lambda i: (0, 0)),
            pl.BlockSpec((1, d), lambda i: (0, 0)),
            pl.BlockSpec((nr, d), lambda i: (0, 0)),
        ],
        out_specs=[
            pl.BlockSpec((tm, d), row),
            pl.BlockSpec((tm, d // LANES, LANES), lambda i: (i, 0, 0)),
            pl.BlockSpec((nr, tm), lambda i: (0, i)),
        ],
        out_shape=[
            jax.ShapeDtypeStruct((s, d), F32),
            jax.ShapeDtypeStruct((s, d // LANES, LANES), BF16),
            jax.ShapeDtypeStruct((nr, s), F32),
        ],
        compiler_params=_cparams(("parallel",)),
        name="merge_out",
    )(o_attn, o_gla, gates, gates, x, wo, g, wr)


ROUTER_ROWS = 8 + N_EXPERTS


def _route_kernel(lt_ref, bg_ref, be_ref, idx_ref, gate_ref, cnt_ref, carry_ref):
    tb = lt_ref.shape[1]
    epg = EXPERTS_PER_GROUP

    @pl.when(pl.program_id(0) == 0)
    def _():
        carry_ref[...] = jnp.zeros_like(carry_ref)

    logits = lt_ref[...]
    gl = logits[0:N_GROUPS]
    g_exp = jnp.exp(gl - jnp.max(gl, axis=0, keepdims=True))
    g_prob = g_exp / jnp.sum(g_exp, axis=0, keepdims=True)
    gb = gl + bg_ref[0:N_GROUPS]
    rowg = lax.broadcasted_iota(jnp.int32, (N_GROUPS, tb), 0)
    g_idx = jnp.min(jnp.where(gb == jnp.max(gb, axis=0, keepdims=True), rowg, N_GROUPS),
                    axis=0, keepdims=True)
    p_group = jnp.sum(jnp.where(rowg == g_idx, g_prob, 0.0), axis=0, keepdims=True)

    el = logits[8:8 + epg]
    eb = jnp.broadcast_to(be_ref[0:epg], (epg, tb))
    for g in range(1, N_GROUPS):
        pick = g_idx == g
        el = jnp.where(pick, logits[8 + g * epg:8 + (g + 1) * epg], el)
        eb = jnp.where(pick, be_ref[g * epg:(g + 1) * epg], eb)
    e_exp = jnp.exp(el - jnp.max(el, axis=0, keepdims=True))
    e_prob = e_exp / jnp.sum(e_exp, axis=0, keepdims=True)
    score = el + eb
    rowe = lax.broadcasted_iota(jnp.int32, (epg, tb), 0)
    i1 = jnp.min(jnp.where(score == jnp.max(score, axis=0, keepdims=True), rowe, epg),
                 axis=0, keepdims=True)
    score2 = jnp.where(rowe == i1, -jnp.inf, score)
    i2 = jnp.min(jnp.where(score2 == jnp.max(score2, axis=0, keepdims=True), rowe, epg),
                 axis=0, keepdims=True)
    q1 = jnp.sum(jnp.where(rowe == i1, e_prob, 0.0), axis=0, keepdims=True)
    q2 = jnp.sum(jnp.where(rowe == i2, e_prob, 0.0), axis=0, keepdims=True)
    qs = q1 + q2
    gate1 = p_group * q1 / qs
    gate2 = p_group * q2 / qs
    e1 = g_idx * epg + i1
    e2 = g_idx * epg + i2

    rowx = lax.broadcasted_iota(jnp.int32, (N_EXPERTS, tb), 0)
    hit1 = rowx == e1
    hit2 = rowx == e2
    member = (hit1 | hit2).astype(BF16)
    before = (lax.broadcasted_iota(jnp.int32, (tb, tb), 0)
              < lax.broadcasted_iota(jnp.int32, (tb, tb), 1)).astype(BF16)
    carry = carry_ref[...]
    count = _dot(member, before) + carry[:, 0:1]
    r1 = jnp.sum(jnp.where(hit1, count, 0.0), axis=0, keepdims=True).astype(jnp.int32)
    r2 = jnp.sum(jnp.where(hit2, count, 0.0), axis=0, keepdims=True).astype(jnp.int32)
    carry = carry + jnp.sum(member.astype(F32), axis=1, keepdims=True)
    carry_ref[...] = carry
    cnt_ref[...] = carry

    row8 = lax.broadcasted_iota(jnp.int32, (8, tb), 0)
    idx_ref[...] = jnp.where(row8 == 0, e1, jnp.where(row8 == 1, e2,
                             jnp.where(row8 == 2, r1, jnp.where(row8 == 3, r2, 0))))
    gate_ref[...] = jnp.where(row8 == 0, gate1, jnp.where(row8 == 1, gate2, 0.0))


def _route(lt, bg, be, tb):
    nr, s = lt.shape
    return pl.pallas_call(
        _route_kernel,
        grid=(s // tb,),
        in_specs=[
            pl.BlockSpec((nr, tb), lambda i: (0, i)),
            pl.BlockSpec((8, 1), lambda i: (0, 0)),
            pl.BlockSpec((N_EXPERTS, 1), lambda i: (0, 0)),
        ],
        out_specs=[
            pl.BlockSpec((8, tb), lambda i: (0, i)),
            pl.BlockSpec((8, tb), lambda i: (0, i)),
            pl.BlockSpec((N_EXPERTS, LANES), lambda i: (0, 0)),
        ],
        out_shape=[
            jax.ShapeDtypeStruct((8, s), jnp.int32),
            jax.ShapeDtypeStruct((8, s), F32),
            jax.ShapeDtypeStruct((N_EXPERTS, LANES), F32),
        ],
        scratch_shapes=[pltpu.VMEM((N_EXPERTS, LANES), F32)],
        compiler_params=_cparams(("arbitrary",)),
        name="route",
    )(lt, bg, be)


def _expert_kernel(be_ref, nx_ref, nu_ref, pair_ref, h2_hbm, wg_hbm, wu_hbm, wd_hbm, y_hbm,
                   h2v, xg, ystage, wgf, wuf, wdf, wgb, wub, wdb, hsem, wsem, ysem):
    b = pl.program_id(0)
    n_used = nu_ref[0]
    rows = MOE_BLOCK
    n_tok = h2v.shape[0]
    slot = b % 2

    def weight_copies(e):
        return (pltpu.make_async_copy(wg_hbm.at[e], wgf, wsem.at[0]),
                pltpu.make_async_copy(wu_hbm.at[e], wuf, wsem.at[1]),
                pltpu.make_async_copy(wd_hbm.at[e], wdf, wsem.at[2]))

    def gather_row(blk, r, half):
        pair = jnp.maximum(pair_ref[blk * rows + r], 0)
        xg[half, r] = h2v[jnp.where(pair >= n_tok, pair - n_tok, pair)]

    def row_write(blk, r, half, dummy):
        pair = pair_ref[blk * rows + r]
        dst = jnp.where((pair < 0) | dummy, TOP_K * n_tok + half * rows + r, pair)
        return pltpu.make_async_copy(ystage.at[half, r], y_hbm.at[dst], ysem.at[half])

    def wait_row_writes(half):
        pltpu.make_async_copy(ystage.at[half], y_hbm.at[pl.ds(0, rows)], ysem.at[half]).wait()

    weight_priority = 1

    @pl.when(b == 0)
    def _():
        h2_copy = pltpu.make_async_copy(h2_hbm, h2v, hsem.at[0])
        h2_copy.start()
        for cp in weight_copies(be_ref[0]):
            cp.start(priority=weight_priority)
        ystage[...] = jnp.zeros_like(ystage)
        h2_copy.wait()
        spare_fill = pltpu.make_async_copy(ystage.at[0], y_hbm.at[pl.ds(TOP_K * n_tok, rows)], hsem.at[0])
        spare_fill.start()
        spare_fill.wait()

        def body(r, carry):
            gather_row(0, r, 0)
            return carry
        lax.fori_loop(0, rows, body, 0)

    @pl.when(b < n_used)
    def _():
        first_of_expert = (b == 0) | (be_ref[b] != be_ref[jnp.maximum(b - 1, 0)])

        @pl.when(first_of_expert)
        def _():
            nxt = nx_ref[b]
            for m, (stage, dst) in enumerate(((wgf, wgb), (wuf, wub), (wdf, wdb))):
                weight_copies(be_ref[b])[m].wait()
                dst[...] = stage[...].astype(BF16)

                @pl.when(nxt >= 0)
                def _():
                    weight_copies(nxt)[m].start(priority=weight_priority)

        xb = xg[slot].reshape(rows, wgb.shape[0])
        prev_blk = jnp.maximum(b - 1, 0)
        for r in range(rows):
            row_write(prev_blk, r, 1 - slot, b == 0).start()
        for r in range(rows):
            gather_row(b + 1, r, 1 - slot)
        hg = _dot(xb, wgb[...])
        hu = _dot(xb, wub[...])
        act = (hg * _sigmoid(hg)) * hu
        y = _dot(act.astype(BF16), wdb[...]).astype(BF16)

        @pl.when(b >= 1)
        def _():
            wait_row_writes(slot)

        ystage[slot] = y.reshape(ystage.shape[1:])

    @pl.when(b == n_used)
    def _():
        def body(r, carry):
            row_write(b - 1, r, 1 - slot, False).start()
            return carry
        lax.fori_loop(0, rows, body, 0)
        wait_row_writes(slot)
        wait_row_writes(1 - slot)


def _experts(block_e, next_e, n_used, row_pair, h2c, wg, wu, wd):
    n_rows = row_pair.shape[0]
    n_blocks = n_rows // MOE_BLOCK
    n_tok = h2c.shape[0]
    _, d, de = wg.shape
    slab = (d // LANES, LANES)
    grid_spec = pltpu.PrefetchScalarGridSpec(
        num_scalar_prefetch=4,
        grid=(n_blocks,),
        in_specs=[pl.BlockSpec(memory_space=pl.ANY)] * 4,
        out_specs=pl.BlockSpec(memory_space=pl.ANY),
        scratch_shapes=[
            pltpu.VMEM((n_tok,) + slab, BF16),
            pltpu.VMEM((2, MOE_BLOCK) + slab, BF16),
            pltpu.VMEM((2, MOE_BLOCK) + slab, BF16),
            pltpu.VMEM((d, de), F32),
            pltpu.VMEM((d, de), F32),
            pltpu.VMEM((de, d), F32),
            pltpu.VMEM((d, de), BF16),
            pltpu.VMEM((d, de), BF16),
            pltpu.VMEM((de, d), BF16),
            pltpu.SemaphoreType.DMA((1,)),
            pltpu.SemaphoreType.DMA((3,)),
            pltpu.SemaphoreType.DMA((2,)),
        ],
    )
    return pl.pallas_call(
        _expert_kernel,
        grid_spec=grid_spec,
        out_shape=jax.ShapeDtypeStruct((TOP_K * n_tok + 2 * MOE_BLOCK,) + slab, BF16),
        compiler_params=_cparams(("arbitrary",), vmem=EXPERT_VMEM_LIMIT),
        name="experts",
    )(block_e, next_e, n_used, row_pair, h2c, wg, wu, wd)


def _combine_kernel(y0_ref, y1_ref, gate_ref, x1_ref, g_ref, o_ref):
    gates = gate_ref[...]
    y0 = y0_ref[...].reshape(x1_ref.shape).astype(F32)
    y1 = y1_ref[...].reshape(x1_ref.shape).astype(F32)
    x2 = x1_ref[...] + (y0 * gates[:, 0:1] + y1 * gates[:, 1:2])
    ms = jnp.mean(x2 * x2, axis=-1, keepdims=True)
    o_ref[...] = x2 * lax.rsqrt(ms + RMS_EPS) * g_ref[...]


def _combine(y2, gates, x1, g, tm):
    s, d = x1.shape
    slab = (d // LANES, LANES)
    n_steps = s // tm
    return pl.pallas_call(
        _combine_kernel,
        grid=(n_steps,),
        in_specs=[
            pl.BlockSpec((tm,) + slab, lambda i: (i, 0, 0)),
            pl.BlockSpec((tm,) + slab, lambda i: (n_steps + i, 0, 0)),
            pl.BlockSpec((tm, TOP_K), lambda i: (i, 0)),
            pl.BlockSpec((tm, d), lambda i: (i, 0)),
            pl.BlockSpec((1, d), lambda i: (0, 0)),
        ],
        out_specs=pl.BlockSpec((tm, d), lambda i: (i, 0)),
        out_shape=jax.ShapeDtypeStruct((s, d), F32),
        compiler_params=_cparams(("parallel",)),
        name="combine",
    )(y2, y2, gates, x1, g)


def _tile(n, pref):
    return pref if n % pref == 0 else n


def kernel(x, norm_mix_g, w_in, w_gla_a2, b_gla_a, gla_norm_g, attn_sinks, rel_bias_table, w_out,
           norm_ffn_g, w_router_group, b_router_group, w_router_expert, b_router_expert,
           w_expert_gate, w_expert_up, w_expert_down, norm_final_g):
    b, s, d = x.shape
    assert b == 1 and w_in.shape[0] == 1, "single batch, single layer"
    assert (s * TOP_K) % MOE_BLOCK == 0
    x2d = x.reshape(s, d)
    w_in_t = w_in[0].T

    w2 = jnp.pad(w_gla_a2[0], ((0, LANES - GLA_LOWRANK), (0, 0))).astype(BF16)
    h, log_a = _norm_loga(x2d, norm_mix_g[0].reshape(1, d), w_in_t, w2, b_gla_a[0].reshape(1, -1),
                          _tile(s, 512))
    proj = _in_proj(h, w_in_t, _tile(s, 4096), 512)

    bucket, is_prev = _folded_maps()
    bias = _rel_bias(rel_bias_table.T, jnp.asarray(bucket), jnp.asarray(is_prev))
    bias = bias.reshape(2, N_HEADS, WINDOW, WINDOW)
    o_attn = _swa(proj, attn_sinks[0], bias)
    o_gla = _gla(proj, log_a, gla_norm_g[0].reshape(1, -1), _tile(s, 256))

    wr = jnp.zeros((ROUTER_ROWS, d), F32)
    wr = wr.at[0:N_GROUPS].set(w_router_group[0].T).at[8:].set(w_router_expert[0].T).astype(BF16)
    x1, h2, logits_t = _merge_out(o_attn, o_gla, proj, x2d, w_out[0].astype(BF16),
                                  norm_ffn_g[0].reshape(1, d), wr, _tile(s, 256))

    bg = jnp.pad(b_router_group[0], (0, 8 - N_GROUPS)).reshape(8, 1)
    idx, gate, counts = _route(logits_t, bg, b_router_expert[0].reshape(N_EXPERTS, 1), _tile(s, 512))
    counts = counts[:, 0].astype(jnp.int32)
    padded = (counts + MOE_BLOCK - 1) // MOE_BLOCK * MOE_BLOCK
    pend = jnp.cumsum(padded)
    pstart = pend - padded
    expert = idx[0:TOP_K]
    eids = jnp.arange(N_EXPERTS, dtype=jnp.int32)[:, None, None]
    dest = jnp.sum(jnp.where(expert[None] == eids, pstart[:, None, None], 0), axis=0) + idx[TOP_K:2 * TOP_K]
    n_pairs = s * TOP_K
    n_blocks = (n_pairs + N_EXPERTS * (MOE_BLOCK - 1) + MOE_BLOCK - 1) // MOE_BLOCK
    n_used = (pend[-1] // MOE_BLOCK).reshape(1)
    block_start = jnp.arange(n_blocks, dtype=jnp.int32) * MOE_BLOCK
    block_e = jnp.minimum(jnp.sum((pend[None, :] <= block_start[:, None]).astype(jnp.int32), axis=1),
                          N_EXPERTS - 1)
    row_pair = jnp.full((n_blocks * MOE_BLOCK,), -1, jnp.int32).at[dest.reshape(-1)].set(
        jnp.arange(n_pairs, dtype=jnp.int32), unique_indices=True)

    e_col = jnp.arange(N_EXPERTS, dtype=jnp.int32)[:, None]
    e_row = jnp.arange(N_EXPERTS, dtype=jnp.int32)[None, :]
    later = (e_row > e_col) & (counts > 0)[None, :]
    next_nonempty = jnp.min(jnp.where(later, e_row, N_EXPERTS), axis=1)
    next_nonempty = jnp.where(next_nonempty < N_EXPERTS, next_nonempty, -1)
    next_e = jnp.sum(jnp.where(block_e[:, None] == e_row, next_nonempty[None, :], 0), axis=1)
    y2 = _experts(block_e, next_e, n_used, row_pair, h2, w_expert_gate[0], w_expert_up[0], w_expert_down[0])
    out = _combine(y2, gate[0:TOP_K].T, x1, norm_final_g.reshape(1, d), _tile(s, 256))
    return out.reshape(b, s, d)
```

```python
import functools
import math

import numpy as np
import jax
import jax.numpy as jnp
from jax import lax
from jax.experimental import pallas as pl
from jax.experimental.pallas import tpu as pltpu

F32 = jnp.float32
BF16 = jnp.bfloat16

N_HEADS = 32
N_KV_HEADS = 4
HEAD_DIM = 64
WINDOW = 128
N_BUCKETS = 32
MAX_DISTANCE = 128
GLA_HEADS = 4
GLA_DK = 256
GLA_DV = 512
GLA_LOWRANK = 16
GLA_TAU = 16.0
GLA_CHUNK = 64
N_GROUPS = 4
EXPERTS_PER_GROUP = 8
N_EXPERTS = 32
TOP_K = 2
MOE_BLOCK = 128
RMS_EPS = 1e-6
NEG_INF = -1e30

COL_GA = 8704
COL_GATES = 8720
MAIN_COLS = 8704
GATE_COLS = 4096
COL_AQ = GATE_COLS + 0
COL_AK = GATE_COLS + 2048
COL_AV = GATE_COLS + 2304
COL_GQ = GATE_COLS + 2560
COL_GK = GATE_COLS + 3584
COL_GV = GATE_COLS + 4608
COL_GR = GATE_COLS + 6656

LANES = 128
VMEM_LIMIT = 56 * 1024 * 1024
EXPERT_VMEM_LIMIT = 60 * 1024 * 1024


def _cparams(sem, vmem=VMEM_LIMIT):
    return pltpu.CompilerParams(dimension_semantics=sem, vmem_limit_bytes=vmem)


def _split3(x):
    hi = x.astype(BF16)
    r1 = x - hi.astype(F32)
    mid = r1.astype(BF16)
    lo = (r1 - mid.astype(F32)).astype(BF16)
    return hi, mid, lo


def _sigmoid(x):
    return 0.5 * jnp.tanh(0.5 * x) + 0.5


def _dot(a, b):
    return jnp.dot(a, b, preferred_element_type=F32)


def _dot_nt(a, b):
    return lax.dot_general(a, b, (((1,), (1,)), ((), ())), preferred_element_type=F32)


def _dot_tn(a, b):
    return lax.dot_general(a, b, (((0,), (0,)), ((), ())), preferred_element_type=F32)


def _norm_loga_kernel(x_ref, g_ref, wga_ref, w2_ref, b_ref, h_ref, la_ref):
    x = x_ref[...]
    ms = jnp.mean(x * x, axis=-1, keepdims=True)
    hb = (x * lax.rsqrt(ms + RMS_EPS) * g_ref[...]).astype(BF16)
    h_ref[...] = hb
    row = lax.broadcasted_iota(jnp.int32, wga_ref.shape, 0)
    wga = jnp.where(row < GLA_LOWRANK, wga_ref[...], 0.0).astype(BF16)
    ga = _dot_nt(hb, wga)
    z = _dot(ga.astype(BF16), w2_ref[...]) + b_ref[...]
    la_ref[...] = (jnp.minimum(z, 0.0) - jnp.log1p(jnp.exp(-jnp.abs(z)))) * (1.0 / GLA_TAU)


def _norm_loga(x, g, wga, w2, b, tm):
    s, d = x.shape
    nq = w2.shape[1]
    return pl.pallas_call(
        _norm_loga_kernel,
        grid=(s // tm,),
        in_specs=[
            pl.BlockSpec((tm, d), lambda i: (i, 0)),
            pl.BlockSpec((1, d), lambda i: (0, 0)),
            pl.BlockSpec((LANES, d), lambda i: (COL_GA // LANES, 0)),
            pl.BlockSpec((LANES, nq), lambda i: (0, 0)),
            pl.BlockSpec((1, nq), lambda i: (0, 0)),
        ],
        out_specs=[
            pl.BlockSpec((tm, d), lambda i: (i, 0)),
            pl.BlockSpec((tm, nq), lambda i: (i, 0)),
        ],
        out_shape=[
            jax.ShapeDtypeStruct((s, d), BF16),
            jax.ShapeDtypeStruct((s, nq), F32),
        ],
        compiler_params=_cparams(("parallel",)),
        name="norm_loga",
    )(x, g, wga, w2, b)


def _in_proj_kernel(h_hbm, wt_hbm, eg_hbm, eu_hbm, ed_hbm, o_ref, egb_hbm, eub_hbm, edb_hbm,
                    h_vmem, w_stage, wbf, cin, cout, sem, csem_in, csem_out, *, n_main, n_tiles, tn):
    j = pl.program_id(0)
    i = pl.program_id(1)
    tm = o_ref.shape[0]
    step = j * pl.num_programs(1) + i
    n_steps = n_tiles * pl.num_programs(1)
    n_conv = 3 * eg_hbm.shape[0]
    d, de = eg_hbm.shape[1:]
    buf = step % 2

    def w_copy(jj):
        row0 = jnp.where(jj < n_main, jj * tn, COL_GATES + (jj - n_main) * tn)
        return pltpu.make_async_copy(wt_hbm.at[pl.ds(pl.multiple_of(row0, 8), tn)], w_stage, sem.at[1])

    def conv_copies(kind, e, b_, to_hbm):
        src_t, dst_t = ((eg_hbm, egb_hbm), (eu_hbm, eub_hbm), (ed_hbm, edb_hbm))[kind]
        stage, hbm, s_ = (cout, dst_t, csem_out) if to_hbm else (cin, src_t, csem_in)
        if kind < 2:
            pairs = [(hbm.at[e], stage.at[b_])]
        else:
            pairs = [(hbm.at[e, :, pl.ds(c * de, de)], stage.at[b_, pl.ds(c * de, de)]) for c in range(d // de)]
        return [pltpu.make_async_copy(v, h_, s_.at[b_]) if to_hbm else pltpu.make_async_copy(h_, v, s_.at[b_])
                for h_, v in pairs]

    def start_conv(st, to_hbm):
        for kind in range(3):
            @pl.when(st % 3 == kind)
            def _():
                for cp in conv_copies(kind, st // 3, st % 2, to_hbm):
                    cp.start()

    @pl.when(step == 0)
    def _():
        h_copy = pltpu.make_async_copy(h_hbm, h_vmem, sem.at[0])
        h_copy.start()
        w_copy(0).start()
        start_conv(step, False)
        h_copy.wait()

    @pl.when(step < n_conv)
    def _():
        pltpu.make_async_copy(eg_hbm.at[0], cin.at[buf], csem_in.at[buf]).wait()

    @pl.when((step >= 2) & (step < n_conv + 2))
    def _():
        pltpu.make_async_copy(cout.at[buf], egb_hbm.at[0], csem_out.at[buf]).wait()

    @pl.when(step + 1 < n_conv)
    def _():
        start_conv(step + 1, False)

    @pl.when(i == 0)
    def _():
        w_copy(j).wait()
        wbf[...] = w_stage[...].astype(BF16)

        @pl.when(j + 1 < n_tiles)
        def _():
            w_copy(j + 1).start()

    cout[buf] = cin[buf].astype(BF16)
    h = h_vmem[pl.ds(pl.multiple_of(i * tm, tm), tm), :]
    o_ref[...] = _dot_nt(h, wbf[...]).astype(o_ref.dtype)

    @pl.when(step < n_conv)
    def _():
        start_conv(step, True)


def _in_proj(h, w_t, eg, eu, ed, tm, tn):
    s, d = h.shape
    n_main = MAIN_COLS // tn
    n_gate = GATE_COLS // tn
    n_tiles = n_main + n_gate
    n_exp, _, de = eg.shape
    assert n_tiles * (s // tm) >= 3 * n_exp + 2 and d % de == 0
    kern = functools.partial(_in_proj_kernel, n_main=n_main, n_tiles=n_tiles, tn=tn)
    any_spec = pl.BlockSpec(memory_space=pl.ANY)
    return pl.pallas_call(
        kern,
        grid=(n_tiles, s // tm),
        in_specs=[any_spec] * 5,
        out_specs=[pl.BlockSpec((tm, tn), lambda j, i: (i, (j + n_gate) % n_tiles)), any_spec, any_spec, any_spec],
        out_shape=[jax.ShapeDtypeStruct((s, n_tiles * tn), BF16),
                   jax.ShapeDtypeStruct(eg.shape, BF16), jax.ShapeDtypeStruct(eu.shape, BF16),
                   jax.ShapeDtypeStruct(ed.shape, BF16)],
        scratch_shapes=[
            pltpu.VMEM((s, d), BF16),
            pltpu.VMEM((tn, d), F32),
            pltpu.VMEM((tn, d), BF16),
            pltpu.VMEM((2, d, de), F32),
            pltpu.VMEM((2, d, de), BF16),
            pltpu.SemaphoreType.DMA((2,)),
            pltpu.SemaphoreType.DMA((2,)),
            pltpu.SemaphoreType.DMA((2,)),
        ],
        compiler_params=_cparams(("arbitrary", "arbitrary"), vmem=EXPERT_VMEM_LIMIT),
        name="in_proj",
    )(h, w_t, eg, eu, ed)


def _folded_maps():
    j = np.arange(WINDOW)[:, None]
    i = np.arange(WINDOW)[None, :]
    n = (i - j) % WINDOW
    max_exact = N_BUCKETS // 2
    ratio = np.maximum(n, max_exact).astype(np.float32) / np.float32(max_exact)
    large = max_exact + (np.log(ratio) / np.float32(math.log(MAX_DISTANCE / max_exact))
                         * (N_BUCKETS - max_exact)).astype(np.int32)
    large = np.minimum(large, N_BUCKETS - 1)
    bucket = np.where(n < max_exact, n, large).astype(np.int32).reshape(1, -1)
    is_prev = (j > i).astype(np.int32).reshape(1, -1)
    return bucket, is_prev


def _rel_bias_kernel(tab_ref, bucket_ref, prev_ref, o_ref):
    nb = tab_ref.shape[1]
    width = bucket_ref.shape[1]
    onehot = (lax.broadcasted_iota(jnp.int32, (nb, width), 0) == bucket_ref[...]).astype(BF16)
    hi, mid, lo = _split3(tab_ref[...])
    bias = _dot(hi, onehot) + _dot(mid, onehot) + _dot(lo, onehot)
    o_ref[0] = bias
    o_ref[1] = jnp.where(prev_ref[...] > 0, NEG_INF, bias)


def _rel_bias(table_t, bucket, is_prev):
    nh = table_t.shape[0]
    width = bucket.shape[1]
    return pl.pallas_call(
        _rel_bias_kernel,
        out_shape=jax.ShapeDtypeStruct((2, nh, width), F32),
        compiler_params=pltpu.CompilerParams(vmem_limit_bytes=VMEM_LIMIT),
        name="rel_bias",
    )(table_t, bucket, is_prev)


SWA_SCORES_AHEAD = 10


def _swa_kernel(sink_ref, q_ref, kc_ref, kp_ref, vc_ref, vp_ref, bias_ref, o_ref):
    w = WINDOW
    hd = HEAD_DIM
    gq = N_HEADS // N_KV_HEADS
    is_prev = (lax.broadcasted_iota(jnp.int32, (w, w), 0) > lax.broadcasted_iota(jnp.int32, (w, w), 1))
    zeros = jnp.zeros((2 * w, hd), BF16)
    scale = HEAD_DIM ** -0.5
    k_pl, v_pl = [], []
    for g in range(N_KV_HEADS):
        cs = slice(g * hd, (g + 1) * hd)
        k_g = jnp.concatenate([kp_ref[:, cs], kc_ref[:, cs]], axis=0) * scale
        v_g = jnp.concatenate([vp_ref[:, cs], vc_ref[:, cs]], axis=0)
        k_pl.append((jnp.concatenate([k_g, zeros], axis=1), jnp.concatenate([zeros, k_g], axis=1)))
        v_pl.append((jnp.concatenate([v_g, zeros], axis=1), jnp.concatenate([zeros, v_g], axis=1)))

    def scores(hh):
        pair = hh - hh % 2
        return _dot_nt(k_pl[hh // gq][hh % 2], q_ref[:, pair * hd:(pair + 2) * hd])

    def attend(hh, st2):
        st = jnp.where(is_prev, st2[:w], st2[w:]) + bias_ref[hh]
        sink = sink_ref[hh]
        m = jnp.maximum(jnp.max(st, axis=0, keepdims=True), sink)
        p = jnp.exp(st - m)
        recip = 1.0 / (jnp.sum(p, axis=0, keepdims=True) + jnp.exp(sink - m))
        probs = p * recip
        p_cat = jnp.concatenate([jnp.where(is_prev, probs, 0.0), jnp.where(is_prev, 0.0, probs)], axis=0)
        return _dot_tn(p_cat.astype(BF16), v_pl[hh // gq][hh % 2])

    ahead = SWA_SCORES_AHEAD
    st2 = {hh: scores(hh) for hh in range(ahead)}
    acc = None
    for hh in range(N_HEADS):
        if hh + ahead < N_HEADS:
            st2[hh + ahead] = scores(hh + ahead)
        part = attend(hh, st2.pop(hh))
        if hh % 2 == 0:
            acc = part
        else:
            o_ref[:, (hh - 1) * hd:(hh + 1) * hd] = (acc + part).astype(o_ref.dtype)


def _swa(proj, sinks, bias):
    s = proj.shape[0]
    w = WINDOW
    nb = s // w
    kvw = N_KV_HEADS * HEAD_DIM
    dq = N_HEADS * HEAD_DIM
    prev = lambda n: jnp.maximum(n - 1, 0)
    return pl.pallas_call(
        _swa_kernel,
        grid=(nb,),
        in_specs=[
            pl.BlockSpec(memory_space=pltpu.SMEM),
            pl.BlockSpec((w, dq), lambda n: (n, COL_AQ // dq)),
            pl.BlockSpec((w, kvw), lambda n: (n, COL_AK // kvw)),
            pl.BlockSpec((w, kvw), lambda n: (prev(n), COL_AK // kvw)),
            pl.BlockSpec((w, kvw), lambda n: (n, COL_AV // kvw)),
            pl.BlockSpec((w, kvw), lambda n: (prev(n), COL_AV // kvw)),
            pl.BlockSpec((None, N_HEADS, w, w), lambda n: (jnp.where(n == 0, 1, 0), 0, 0, 0)),
        ],
        out_specs=pl.BlockSpec((w, dq), lambda n: (n, 0)),
        out_shape=jax.ShapeDtypeStruct((s, dq), BF16),
        compiler_params=_cparams(("parallel",)),
        name="swa",
    )(sinks, proj, proj, proj, proj, proj, bias)


def _gla_kernel(*refs):
    nh = GLA_HEADS
    q_refs, k_refs, v_refs, gr_refs = (refs[i * nh:(i + 1) * nh] for i in range(4))
    la_ref, gn_ref, o_ref, state_ref = refs[4 * nh:]
    c = GLA_CHUNK
    rows, dk = q_refs[0].shape
    dv = v_refs[0].shape[1]

    @pl.when(pl.program_id(0) == 0)
    def _():
        state_ref[...] = jnp.zeros_like(state_ref)

    la_parts = _split3(la_ref[...])
    ri = lax.broadcasted_iota(jnp.int32, (rows, rows), 0)
    ci = lax.broadcasted_iota(jnp.int32, (rows, rows), 1)
    tri = ((ri // c == ci // c) & (ri >= ci)).astype(BF16)
    cum_all = _dot(tri, la_parts[0]) + _dot(tri, la_parts[1]) + _dot(tri, la_parts[2])
    causal = (lax.broadcasted_iota(jnp.int32, (c, c), 0) >= lax.broadcasted_iota(jnp.int32, (c, c), 1))
    gn = gn_ref[...]
    n_chunks = rows // c

    o_intra, q_in, upd, decay = {}, {}, {}, {}
    for h in range(nh):
        for ch in range(n_chunks):
            sl = slice(ch * c, (ch + 1) * c)
            cum_c = cum_all[sl, h * dk:(h + 1) * dk]
            ref = cum_c[c // 2 - 1:c // 2]
            last = cum_c[c - 1:c]
            q = q_refs[h][sl, :].astype(F32) * (dk ** -0.5)
            k = k_refs[h][sl, :].astype(F32)
            v = v_refs[h][sl, :]
            q_intra = (q * jnp.exp(cum_c - ref)).astype(BF16)
            k_intra = (k * jnp.exp(ref - cum_c)).astype(BF16)
            a = jnp.where(causal, _dot_nt(q_intra, k_intra), 0.0)
            o_intra[h, ch] = _dot(a.astype(BF16), v)
            q_in[h, ch] = (q * jnp.exp(cum_c)).astype(BF16)
            k_dec = (k * jnp.exp(last - cum_c)).astype(BF16)
            upd[h, ch] = _dot_tn(k_dec, v)
            decay[h, ch] = jnp.broadcast_to(jnp.exp(last), (LANES, dk)).T

    state = [state_ref[h] for h in range(nh)]
    for ch in range(n_chunks):
        sl = slice(ch * c, (ch + 1) * c)
        for h in range(nh):
            o = o_intra[h, ch] + _dot(q_in[h, ch], state[h].astype(BF16))
            state[h] = jnp.concatenate(
                [state[h][:, jv * LANES:(jv + 1) * LANES] * decay[h, ch]
                 + upd[h, ch][:, jv * LANES:(jv + 1) * LANES] for jv in range(dv // LANES)], axis=1)
            ms = jnp.mean(o * o, axis=-1, keepdims=True)
            on = o * lax.rsqrt(ms + RMS_EPS) * gn
            gr = gr_refs[h][sl, :].astype(F32)
            o_ref[sl, h * dv:(h + 1) * dv] = (on * (gr * _sigmoid(gr))).astype(o_ref.dtype)
    for h in range(nh):
        state_ref[h] = state[h]


def _gla(proj, la, gn, rows):
    s = proj.shape[0]
    dk, dv = GLA_DK, GLA_DV
    nh = GLA_HEADS

    def head_specs(width, col0):
        return [pl.BlockSpec((rows, width), functools.partial(lambda i, h: (i, col0 // width + h), h=h))
                for h in range(nh)]

    in_specs = (head_specs(dk, COL_GQ) + head_specs(dk, COL_GK) + head_specs(dv, COL_GV)
                + head_specs(dv, COL_GR)
                + [pl.BlockSpec((rows, nh * dk), lambda i: (i, 0)), pl.BlockSpec((1, dv), lambda i: (0, 0))])
    return pl.pallas_call(
        _gla_kernel,
        grid=(s // rows,),
        in_specs=in_specs,
        out_specs=pl.BlockSpec((rows, nh * dv), lambda i: (i, 0)),
        out_shape=jax.ShapeDtypeStruct((s, nh * dv), BF16),
        scratch_shapes=[pltpu.VMEM((nh, dk, dv), F32)],
        compiler_params=_cparams(("arbitrary",)),
        name="gla",
    )(*([proj] * (4 * nh)), la, gn)


MERGE_SUB_ROWS = 128


def _merge_out_kernel(oa_ref, og_ref, ga_ref, gg_ref, x_ref, wo_ref, g_ref, wr_ref,
                      x1_ref, h2_ref, lt_ref):
    sub = MERGE_SUB_ROWS
    tiles = [slice(t * sub, (t + 1) * sub) for t in range(x_ref.shape[0] // sub)]
    merged = [(_sigmoid(ga_ref[rs, :].astype(F32)) * oa_ref[rs, :].astype(F32)
               + _sigmoid(gg_ref[rs, :].astype(F32)) * og_ref[rs, :].astype(F32)).astype(BF16) for rs in tiles]
    projected = [_dot(m, wo_ref[...]) for m in merged]
    for rs, y in zip(tiles, projected):
        x1 = x_ref[rs, :] + y
        x1_ref[rs, :] = x1
        ms = jnp.mean(x1 * x1, axis=-1, keepdims=True)
        h2b = (x1 * lax.rsqrt(ms + RMS_EPS) * g_ref[...]).astype(BF16)
        h2_ref[rs] = h2b.reshape((sub,) + h2_ref.shape[1:])
        lt_ref[:, rs] = _dot_nt(wr_ref[...], h2b)


def _merge_out(o_attn, o_gla, gates, x, wo, g, wr, tm):
    s, d = x.shape
    nr = wr.shape[0]
    row = lambda i: (i, 0)
    return pl.pallas_call(
        _merge_out_kernel,
        grid=(s // tm,),
        in_specs=[
            pl.BlockSpec((tm, d), row),
            pl.BlockSpec((tm, d), row),
            pl.BlockSpec((tm, d), lambda i: (i, 0)),
            pl.BlockSpec((tm, d), lambda i: (i, 1)),
            pl.BlockSpec((tm, d), row),
            pl.BlockSpec((d, d), lambda i: (0, 0)),
            pl.BlockSpec((1, d), lambda i: (0, 0)),
            pl.BlockSpec((nr, d), lambda i: (0, 0)),
        ],
        out_specs=[
            pl.BlockSpec((tm, d), row),
            pl.BlockSpec((tm, d // LANES, LANES), lambda i: (i, 0, 0)),
            pl.BlockSpec((nr, tm), lambda i: (0, i)),
        ],
        out_shape=[
            jax.ShapeDtypeStruct((s, d), F32),
            jax.ShapeDtypeStruct((s, d // LANES, LANES), BF16),
            jax.ShapeDtypeStruct((nr, s), F32),
        ],
        compiler_params=_cparams(("parallel",)),
        name="merge_out",
    )(o_attn, o_gla, gates, gates, x, wo, g, wr)


ROUTER_ROWS = 8 + N_EXPERTS


def _route_kernel(lt_ref, bg_ref, be_ref, idx_ref, gate_ref, cnt_ref, carry_ref):
    tb = lt_ref.shape[1]
    epg = EXPERTS_PER_GROUP

    @pl.when(pl.program_id(0) == 0)
    def _():
        carry_ref[...] = jnp.zeros_like(carry_ref)

    logits = lt_ref[...]
    gl = logits[0:N_GROUPS]
    g_exp = jnp.exp(gl - jnp.max(gl, axis=0, keepdims=True))
    g_prob = g_exp / jnp.sum(g_exp, axis=0, keepdims=True)
    gb = gl + bg_ref[0:N_GROUPS]
    rowg = lax.broadcasted_iota(jnp.int32, (N_GROUPS, tb), 0)
    g_idx = jnp.min(jnp.where(gb == jnp.max(gb, axis=0, keepdims=True), rowg, N_GROUPS),
                    axis=0, keepdims=True)
    p_group = jnp.sum(jnp.where(rowg == g_idx, g_prob, 0.0), axis=0, keepdims=True)

    el = logits[8:8 + epg]
    eb = jnp.broadcast_to(be_ref[0:epg], (epg, tb))
    for g in range(1, N_GROUPS):
        pick = g_idx == g
        el = jnp.where(pick, logits[8 + g * epg:8 + (g + 1) * epg], el)
        eb = jnp.where(pick, be_ref[g * epg:(g + 1) * epg], eb)
    e_exp = jnp.exp(el - jnp.max(el, axis=0, keepdims=True))
    e_prob = e_exp / jnp.sum(e_exp, axis=0, keepdims=True)
    score = el + eb
    rowe = lax.broadcasted_iota(jnp.int32, (epg, tb), 0)
    i1 = jnp.min(jnp.where(score == jnp.max(score, axis=0, keepdims=True), rowe, epg),
                 axis=0, keepdims=True)
    score2 = jnp.where(rowe == i1, -jnp.inf, score)
    i2 = jnp.min(jnp.where(score2 == jnp.max(score2, axis=0, keepdims=True), rowe, epg),
                 axis=0, keepdims=True)
    q1 = jnp.sum(jnp.where(rowe == i1, e_prob, 0.0), axis=0, keepdims=True)
    q2 = jnp.sum(jnp.where(rowe == i2, e_prob, 0.0), axis=0, keepdims=True)
    qs = q1 + q2
    gate1 = p_group * q1 / qs
    gate2 = p_group * q2 / qs
    e1 = g_idx * epg + i1
    e2 = g_idx * epg + i2

    rowx = lax.broadcasted_iota(jnp.int32, (N_EXPERTS, tb), 0)
    hit1 = rowx == e1
    hit2 = rowx == e2
    member = (hit1 | hit2).astype(BF16)
    before = (lax.broadcasted_iota(jnp.int32, (tb, tb), 0)
              < lax.broadcasted_iota(jnp.int32, (tb, tb), 1)).astype(BF16)
    carry = carry_ref[...]
    count = _dot(member, before) + carry[:, 0:1]
    r1 = jnp.sum(jnp.where(hit1, count, 0.0), axis=0, keepdims=True).astype(jnp.int32)
    r2 = jnp.sum(jnp.where(hit2, count, 0.0), axis=0, keepdims=True).astype(jnp.int32)
    carry = carry + jnp.sum(member.astype(F32), axis=1, keepdims=True)
    carry_ref[...] = carry
    cnt_ref[...] = carry

    row8 = lax.broadcasted_iota(jnp.int32, (8, tb), 0)
    idx_ref[...] = jnp.where(row8 == 0, e1, jnp.where(row8 == 1, e2,
                             jnp.where(row8 == 2, r1, jnp.where(row8 == 3, r2, 0))))
    gate_ref[...] = jnp.where(row8 == 0, gate1, jnp.where(row8 == 1, gate2, 0.0))


def _route(lt, bg, be, tb):
    nr, s = lt.shape
    return pl.pallas_call(
        _route_kernel,
        grid=(s // tb,),
        in_specs=[
            pl.BlockSpec((nr, tb), lambda i: (0, i)),
            pl.BlockSpec((8, 1), lambda i: (0, 0)),
            pl.BlockSpec((N_EXPERTS, 1), lambda i: (0, 0)),
        ],
        out_specs=[
            pl.BlockSpec((8, tb), lambda i: (0, i)),
            pl.BlockSpec((8, tb), lambda i: (0, i)),
            pl.BlockSpec((N_EXPERTS, LANES), lambda i: (0, 0)),
        ],
        out_shape=[
            jax.ShapeDtypeStruct((8, s), jnp.int32),
            jax.ShapeDtypeStruct((8, s), F32),
            jax.ShapeDtypeStruct((N_EXPERTS, LANES), F32),
        ],
        scratch_shapes=[pltpu.VMEM((N_EXPERTS, LANES), F32)],
        compiler_params=_cparams(("arbitrary",)),
        name="route",
    )(lt, bg, be)


def _expert_kernel(be_ref, nx_ref, ws_ref, nu_ref, pair_ref, h2_hbm, wg_hbm, wu_hbm, wd_hbm, y_hbm,
                   h2v, xg, ystage, wgb, wub, wdb, hsem, wsem, ysem):
    b = pl.program_id(0)
    n_used = nu_ref[0]
    rows = MOE_BLOCK
    n_tok = h2v.shape[0]
    slot = b % 2

    def weight_copies(e, half):
        return (pltpu.make_async_copy(wg_hbm.at[e], wgb.at[half], wsem.at[half, 0]),
                pltpu.make_async_copy(wu_hbm.at[e], wub.at[half], wsem.at[half, 1]),
                pltpu.make_async_copy(wd_hbm.at[e], wdb.at[half], wsem.at[half, 2]))

    def gather_row(blk, r, half):
        pair = jnp.maximum(pair_ref[blk * rows + r], 0)
        xg[half, r] = h2v[jnp.where(pair >= n_tok, pair - n_tok, pair)]

    def row_write(blk, r, half, dummy):
        pair = pair_ref[blk * rows + r]
        dst = jnp.where((pair < 0) | dummy, TOP_K * n_tok + half * rows + r, pair)
        return pltpu.make_async_copy(ystage.at[half, r], y_hbm.at[dst], ysem.at[half])

    def wait_row_writes(half):
        pltpu.make_async_copy(ystage.at[half], y_hbm.at[pl.ds(0, rows)], ysem.at[half]).wait()

    weight_priority = 1

    @pl.when(b == 0)
    def _():
        h2_copy = pltpu.make_async_copy(h2_hbm, h2v, hsem.at[0])
        h2_copy.start()
        for cp in weight_copies(be_ref[0], ws_ref[0]):
            cp.start(priority=weight_priority)
        ystage[...] = jnp.zeros_like(ystage)
        h2_copy.wait()
        spare_fill = pltpu.make_async_copy(ystage.at[0], y_hbm.at[pl.ds(TOP_K * n_tok, rows)], hsem.at[0])
        spare_fill.start()
        spare_fill.wait()

        def body(r, carry):
            gather_row(0, r, 0)
            return carry
        lax.fori_loop(0, rows, body, 0)

    @pl.when(b < n_used)
    def _():
        first_of_expert = (b == 0) | (be_ref[b] != be_ref[jnp.maximum(b - 1, 0)])
        wh = ws_ref[b]

        @pl.when(first_of_expert)
        def _():
            nxt = nx_ref[b]

            @pl.when(nxt >= 0)
            def _():
                for cp in weight_copies(nxt, 1 - wh):
                    cp.start(priority=weight_priority)

            for cp in weight_copies(be_ref[b], wh):
                cp.wait()

        xb = xg[slot].reshape(rows, wgb.shape[1])
        prev_blk = jnp.maximum(b - 1, 0)
        for r in range(rows):
            row_write(prev_blk, r, 1 - slot, b == 0).start()
        for r in range(rows):
            gather_row(b + 1, r, 1 - slot)
        hg = _dot(xb, wgb[wh])
        hu = _dot(xb, wub[wh])
        act = (hg * _sigmoid(hg)) * hu
        y = _dot(act.astype(BF16), wdb[wh]).astype(BF16)

        @pl.when(b >= 1)
        def _():
            wait_row_writes(slot)

        ystage[slot] = y.reshape(ystage.shape[1:])

    @pl.when(b == n_used)
    def _():
        def body(r, carry):
            row_write(b - 1, r, 1 - slot, False).start()
            return carry
        lax.fori_loop(0, rows, body, 0)
        wait_row_writes(slot)
        wait_row_writes(1 - slot)


def _experts(block_e, next_e, w_half, n_used, row_pair, h2c, wg, wu, wd):
    n_rows = row_pair.shape[0]
    n_blocks = n_rows // MOE_BLOCK
    n_tok = h2c.shape[0]
    _, d, de = wg.shape
    slab = (d // LANES, LANES)
    grid_spec = pltpu.PrefetchScalarGridSpec(
        num_scalar_prefetch=5,
        grid=(n_blocks,),
        in_specs=[pl.BlockSpec(memory_space=pl.ANY)] * 4,
        out_specs=pl.BlockSpec(memory_space=pl.ANY),
        scratch_shapes=[
            pltpu.VMEM((n_tok,) + slab, BF16),
            pltpu.VMEM((2, MOE_BLOCK) + slab, BF16),
            pltpu.VMEM((2, MOE_BLOCK) + slab, BF16),
            pltpu.VMEM((2, d, de), BF16),
            pltpu.VMEM((2, d, de), BF16),
            pltpu.VMEM((2, de, d), BF16),
            pltpu.SemaphoreType.DMA((1,)),
            pltpu.SemaphoreType.DMA((2, 3)),
            pltpu.SemaphoreType.DMA((2,)),
        ],
    )
    return pl.pallas_call(
        _expert_kernel,
        grid_spec=grid_spec,
        out_shape=jax.ShapeDtypeStruct((TOP_K * n_tok + 2 * MOE_BLOCK,) + slab, BF16),
        compiler_params=_cparams(("arbitrary",), vmem=EXPERT_VMEM_LIMIT),
        name="experts",
    )(block_e, next_e, w_half, n_used, row_pair, h2c, wg, wu, wd)


def _combine_kernel(y0_ref, y1_ref, gate_ref, x1_ref, g_ref, o_ref):
    gates = gate_ref[...]
    y0 = y0_ref[...].reshape(x1_ref.shape).astype(F32)
    y1 = y1_ref[...].reshape(x1_ref.shape).astype(F32)
    x2 = x1_ref[...] + (y0 * gates[:, 0:1] + y1 * gates[:, 1:2])
    ms = jnp.mean(x2 * x2, axis=-1, keepdims=True)
    o_ref[...] = x2 * lax.rsqrt(ms + RMS_EPS) * g_ref[...]


def _combine(y2, gates, x1, g, tm):
    s, d = x1.shape
    slab = (d // LANES, LANES)
    n_steps = s // tm
    return pl.pallas_call(
        _combine_kernel,
        grid=(n_steps,),
        in_specs=[
            pl.BlockSpec((tm,) + slab, lambda i: (i, 0, 0)),
            pl.BlockSpec((tm,) + slab, lambda i: (n_steps + i, 0, 0)),
            pl.BlockSpec((tm, TOP_K), lambda i: (i, 0)),
            pl.BlockSpec((tm, d), lambda i: (i, 0)),
            pl.BlockSpec((1, d), lambda i: (0, 0)),
        ],
        out_specs=pl.BlockSpec((tm, d), lambda i: (i, 0)),
        out_shape=jax.ShapeDtypeStruct((s, d), F32),
        compiler_params=_cparams(("parallel",)),
        name="combine",
    )(y2, y2, gates, x1, g)


def _tile(n, pref):
    return pref if n % pref == 0 else n


def kernel(x, norm_mix_g, w_in, w_gla_a2, b_gla_a, gla_norm_g, attn_sinks, rel_bias_table, w_out,
           norm_ffn_g, w_router_group, b_router_group, w_router_expert, b_router_expert,
           w_expert_gate, w_expert_up, w_expert_down, norm_final_g):
    b, s, d = x.shape
    assert b == 1 and w_in.shape[0] == 1, "single batch, single layer"
    assert (s * TOP_K) % MOE_BLOCK == 0
    x2d = x.reshape(s, d)
    w_in_t = w_in[0].T

    w2 = jnp.pad(w_gla_a2[0], ((0, LANES - GLA_LOWRANK), (0, 0))).astype(BF16)
    h, log_a = _norm_loga(x2d, norm_mix_g[0].reshape(1, d), w_in_t, w2, b_gla_a[0].reshape(1, -1),
                          _tile(s, 512))
    proj_tn = 512
    n_proj_tiles = (MAIN_COLS + GATE_COLS) // proj_tn
    proj_tm = next(t for t in (2048, 1024, 512, 256, 128, 64, 32, 16)
                   if s % t == 0 and n_proj_tiles * (s // t) >= 3 * N_EXPERTS + 2)
    proj, wg_b, wu_b, wd_b = _in_proj(h, w_in_t, w_expert_gate[0], w_expert_up[0], w_expert_down[0],
                                      proj_tm, proj_tn)

    bucket, is_prev = _folded_maps()
    bias = _rel_bias(rel_bias_table.T, jnp.asarray(bucket), jnp.asarray(is_prev))
    bias = bias.reshape(2, N_HEADS, WINDOW, WINDOW)
    o_attn = _swa(proj, attn_sinks[0], bias)
    o_gla = _gla(proj, log_a, gla_norm_g[0].reshape(1, -1), _tile(s, 256))

    wr = jnp.zeros((ROUTER_ROWS, d), F32)
    wr = wr.at[0:N_GROUPS].set(w_router_group[0].T).at[8:].set(w_router_expert[0].T).astype(BF16)
    x1, h2, logits_t = _merge_out(o_attn, o_gla, proj, x2d, w_out[0].astype(BF16),
                                  norm_ffn_g[0].reshape(1, d), wr, _tile(s, 256))

    bg = jnp.pad(b_router_group[0], (0, 8 - N_GROUPS)).reshape(8, 1)
    idx, gate, counts = _route(logits_t, bg, b_router_expert[0].reshape(N_EXPERTS, 1), _tile(s, 512))
    counts = counts[:, 0].astype(jnp.int32)
    padded = (counts + MOE_BLOCK - 1) // MOE_BLOCK * MOE_BLOCK
    pend = jnp.cumsum(padded)
    pstart = pend - padded
    expert = idx[0:TOP_K]
    eids = jnp.arange(N_EXPERTS, dtype=jnp.int32)[:, None, None]
    dest = jnp.sum(jnp.where(expert[None] == eids, pstart[:, None, None], 0), axis=0) + idx[TOP_K:2 * TOP_K]
    n_pairs = s * TOP_K
    n_blocks = (n_pairs + N_EXPERTS * (MOE_BLOCK - 1) + MOE_BLOCK - 1) // MOE_BLOCK
    n_used = (pend[-1] // MOE_BLOCK).reshape(1)
    block_start = jnp.arange(n_blocks, dtype=jnp.int32) * MOE_BLOCK
    block_e = jnp.minimum(jnp.sum((pend[None, :] <= block_start[:, None]).astype(jnp.int32), axis=1),
                          N_EXPERTS - 1)
    row_pair = jnp.full((n_blocks * MOE_BLOCK,), -1, jnp.int32).at[dest.reshape(-1)].set(
        jnp.arange(n_pairs, dtype=jnp.int32), unique_indices=True)

    e_col = jnp.arange(N_EXPERTS, dtype=jnp.int32)[:, None]
    e_row = jnp.arange(N_EXPERTS, dtype=jnp.int32)[None, :]
    later = (e_row > e_col) & (counts > 0)[None, :]
    next_nonempty = jnp.min(jnp.where(later, e_row, N_EXPERTS), axis=1)
    next_nonempty = jnp.where(next_nonempty < N_EXPERTS, next_nonempty, -1)
    next_e = jnp.sum(jnp.where(block_e[:, None] == e_row, next_nonempty[None, :], 0), axis=1)
    parity = (jnp.cumsum((counts > 0).astype(jnp.int32)) - 1) % 2
    w_half = jnp.sum(jnp.where(block_e[:, None] == e_row, parity[None, :], 0), axis=1)
    y2 = _experts(block_e, next_e, w_half, n_used, row_pair, h2, wg_b, wu_b, wd_b)
    out = _combine(y2, gate[0:TOP_K].T, x1, norm_final_g.reshape(1, d), _tile(s, 256))
    return out.reshape(b, s, d)
```

```python
import functools
import math

import numpy as np
import jax
import jax.numpy as jnp
from jax import lax
from jax.experimental import pallas as pl
from jax.experimental.pallas import tpu as pltpu

F32 = jnp.float32
BF16 = jnp.bfloat16

N_HEADS = 32
N_KV_HEADS = 4
HEAD_DIM = 64
WINDOW = 128
N_BUCKETS = 32
MAX_DISTANCE = 128
GLA_HEADS = 4
GLA_DK = 256
GLA_DV = 512
GLA_LOWRANK = 16
GLA_TAU = 16.0
GLA_CHUNK = 64
N_GROUPS = 4
EXPERTS_PER_GROUP = 8
N_EXPERTS = 32
TOP_K = 2
MOE_BLOCK = 128
RMS_EPS = 1e-6
NEG_INF = -1e30

COL_GA = 8704
COL_GATES = 8720
MAIN_COLS = 8704
GATE_COLS = 4096
COL_AQ = GATE_COLS + 0
COL_AK = GATE_COLS + 2048
COL_AV = GATE_COLS + 2304
COL_GQ = GATE_COLS + 2560
COL_GK = GATE_COLS + 3584
COL_GV = GATE_COLS + 4608
COL_GR = GATE_COLS + 6656

LANES = 128
VMEM_LIMIT = 56 * 1024 * 1024
EXPERT_VMEM_LIMIT = 60 * 1024 * 1024


def _cparams(sem, vmem=VMEM_LIMIT):
    return pltpu.CompilerParams(dimension_semantics=sem, vmem_limit_bytes=vmem)


def _split3(x):
    hi = x.astype(BF16)
    r1 = x - hi.astype(F32)
    mid = r1.astype(BF16)
    lo = (r1 - mid.astype(F32)).astype(BF16)
    return hi, mid, lo


def _sigmoid(x):
    return 0.5 * jnp.tanh(0.5 * x) + 0.5


def _dot(a, b):
    return jnp.dot(a, b, preferred_element_type=F32)


def _dot_nt(a, b):
    return lax.dot_general(a, b, (((1,), (1,)), ((), ())), preferred_element_type=F32)


def _dot_tn(a, b):
    return lax.dot_general(a, b, (((0,), (0,)), ((), ())), preferred_element_type=F32)


def _norm_loga_kernel(x_ref, g_ref, wga_ref, w2_ref, b_ref, h_ref, la_ref):
    x = x_ref[...]
    ms = jnp.mean(x * x, axis=-1, keepdims=True)
    hb = (x * lax.rsqrt(ms + RMS_EPS) * g_ref[...]).astype(BF16)
    h_ref[...] = hb
    row = lax.broadcasted_iota(jnp.int32, wga_ref.shape, 0)
    wga = jnp.where(row < GLA_LOWRANK, wga_ref[...], 0.0).astype(BF16)
    ga = _dot_nt(hb, wga)
    z = _dot(ga.astype(BF16), w2_ref[...]) + b_ref[...]
    la_ref[...] = (jnp.minimum(z, 0.0) - jnp.log1p(jnp.exp(-jnp.abs(z)))) * (1.0 / GLA_TAU)


def _norm_loga(x, g, wga, w2, b, tm):
    s, d = x.shape
    nq = w2.shape[1]
    return pl.pallas_call(
        _norm_loga_kernel,
        grid=(s // tm,),
        in_specs=[
            pl.BlockSpec((tm, d), lambda i: (i, 0)),
            pl.BlockSpec((1, d), lambda i: (0, 0)),
            pl.BlockSpec((LANES, d), lambda i: (COL_GA // LANES, 0)),
            pl.BlockSpec((LANES, nq), lambda i: (0, 0)),
            pl.BlockSpec((1, nq), lambda i: (0, 0)),
        ],
        out_specs=[
            pl.BlockSpec((tm, d), lambda i: (i, 0)),
            pl.BlockSpec((tm, nq), lambda i: (i, 0)),
        ],
        out_shape=[
            jax.ShapeDtypeStruct((s, d), BF16),
            jax.ShapeDtypeStruct((s, nq), F32),
        ],
        compiler_params=_cparams(("parallel",)),
        name="norm_loga",
    )(x, g, wga, w2, b)


def _in_proj_kernel(h_hbm, wt_hbm, o_ref, h_vmem, w_stage, wbf, sem, *, n_main, n_tiles, tn):
    j = pl.program_id(0)
    i = pl.program_id(1)
    tm = o_ref.shape[0]

    def w_copy(jj):
        row0 = jnp.where(jj < n_main, jj * tn, COL_GATES + (jj - n_main) * tn)
        return pltpu.make_async_copy(wt_hbm.at[pl.ds(pl.multiple_of(row0, 8), tn)], w_stage, sem.at[1])

    @pl.when((j == 0) & (i == 0))
    def _():
        h_copy = pltpu.make_async_copy(h_hbm, h_vmem, sem.at[0])
        h_copy.start()
        w_copy(0).start()
        h_copy.wait()

    @pl.when(i == 0)
    def _():
        w_copy(j).wait()
        wbf[...] = w_stage[...].astype(BF16)

        @pl.when(j + 1 < n_tiles)
        def _():
            w_copy(j + 1).start()

    h = h_vmem[pl.ds(pl.multiple_of(i * tm, tm), tm), :]
    o_ref[...] = _dot_nt(h, wbf[...]).astype(o_ref.dtype)


def _in_proj(h, w_t, tm, tn):
    s, d = h.shape
    n_main = MAIN_COLS // tn
    n_gate = GATE_COLS // tn
    n_tiles = n_main + n_gate
    kern = functools.partial(_in_proj_kernel, n_main=n_main, n_tiles=n_tiles, tn=tn)
    return pl.pallas_call(
        kern,
        grid=(n_tiles, s // tm),
        in_specs=[pl.BlockSpec(memory_space=pl.ANY), pl.BlockSpec(memory_space=pl.ANY)],
        out_specs=pl.BlockSpec((tm, tn), lambda j, i: (i, (j + n_gate) % n_tiles)),
        out_shape=jax.ShapeDtypeStruct((s, n_tiles * tn), BF16),
        scratch_shapes=[
            pltpu.VMEM((s, d), BF16),
            pltpu.VMEM((tn, d), F32),
            pltpu.VMEM((tn, d), BF16),
            pltpu.SemaphoreType.DMA((2,)),
        ],
        compiler_params=_cparams(("arbitrary", "arbitrary")),
        name="in_proj",
    )(h, w_t)


def _folded_maps():
    j = np.arange(WINDOW)[:, None]
    i = np.arange(WINDOW)[None, :]
    n = (i - j) % WINDOW
    max_exact = N_BUCKETS // 2
    ratio = np.maximum(n, max_exact).astype(np.float32) / np.float32(max_exact)
    large = max_exact + (np.log(ratio) / np.float32(math.log(MAX_DISTANCE / max_exact))
                         * (N_BUCKETS - max_exact)).astype(np.int32)
    large = np.minimum(large, N_BUCKETS - 1)
    bucket = np.where(n < max_exact, n, large).astype(np.int32).reshape(1, -1)
    is_prev = (j > i).astype(np.int32).reshape(1, -1)
    return bucket, is_prev


def _rel_bias_kernel(tab_ref, bucket_ref, prev_ref, o_ref):
    nb = tab_ref.shape[1]
    width = bucket_ref.shape[1]
    onehot = (lax.broadcasted_iota(jnp.int32, (nb, width), 0) == bucket_ref[...]).astype(BF16)
    hi, mid, lo = _split3(tab_ref[...])
    bias = _dot(hi, onehot) + _dot(mid, onehot) + _dot(lo, onehot)
    o_ref[0] = bias
    o_ref[1] = jnp.where(prev_ref[...] > 0, NEG_INF, bias)


def _rel_bias(table_t, bucket, is_prev):
    nh = table_t.shape[0]
    width = bucket.shape[1]
    return pl.pallas_call(
        _rel_bias_kernel,
        out_shape=jax.ShapeDtypeStruct((2, nh, width), F32),
        compiler_params=pltpu.CompilerParams(vmem_limit_bytes=VMEM_LIMIT),
        name="rel_bias",
    )(table_t, bucket, is_prev)


SWA_SCORES_AHEAD = 10


def _swa_kernel(sink_ref, q_ref, kc_ref, kp_ref, vc_ref, vp_ref, bias_ref, o_ref):
    w = WINDOW
    hd = HEAD_DIM
    gq = N_HEADS // N_KV_HEADS
    is_prev = (lax.broadcasted_iota(jnp.int32, (w, w), 0) > lax.broadcasted_iota(jnp.int32, (w, w), 1))
    zeros = jnp.zeros((2 * w, hd), BF16)
    scale = HEAD_DIM ** -0.5
    k_pl, v_pl = [], []
    for g in range(N_KV_HEADS):
        cs = slice(g * hd, (g + 1) * hd)
        k_g = jnp.concatenate([kp_ref[:, cs], kc_ref[:, cs]], axis=0) * scale
        v_g = jnp.concatenate([vp_ref[:, cs], vc_ref[:, cs]], axis=0)
        k_pl.append((jnp.concatenate([k_g, zeros], axis=1), jnp.concatenate([zeros, k_g], axis=1)))
        v_pl.append((jnp.concatenate([v_g, zeros], axis=1), jnp.concatenate([zeros, v_g], axis=1)))

    def scores(hh):
        pair = hh - hh % 2
        return _dot_nt(k_pl[hh // gq][hh % 2], q_ref[:, pair * hd:(pair + 2) * hd])

    def attend(hh, st2):
        st = jnp.where(is_prev, st2[:w], st2[w:]) + bias_ref[hh]
        sink = sink_ref[hh]
        m = jnp.maximum(jnp.max(st, axis=0, keepdims=True), sink)
        p = jnp.exp(st - m)
        recip = 1.0 / (jnp.sum(p, axis=0, keepdims=True) + jnp.exp(sink - m))
        probs = p * recip
        p_cat = jnp.concatenate([jnp.where(is_prev, probs, 0.0), jnp.where(is_prev, 0.0, probs)], axis=0)
        return _dot_tn(p_cat.astype(BF16), v_pl[hh // gq][hh % 2])

    ahead = SWA_SCORES_AHEAD
    st2 = {hh: scores(hh) for hh in range(ahead)}
    acc = None
    for hh in range(N_HEADS):
        if hh + ahead < N_HEADS:
            st2[hh + ahead] = scores(hh + ahead)
        part = attend(hh, st2.pop(hh))
        if hh % 2 == 0:
            acc = part
        else:
            o_ref[:, (hh - 1) * hd:(hh + 1) * hd] = (acc + part).astype(o_ref.dtype)


def _swa(proj, sinks, bias):
    s = proj.shape[0]
    w = WINDOW
    nb = s // w
    kvw = N_KV_HEADS * HEAD_DIM
    dq = N_HEADS * HEAD_DIM
    prev = lambda n: jnp.maximum(n - 1, 0)
    return pl.pallas_call(
        _swa_kernel,
        grid=(nb,),
        in_specs=[
            pl.BlockSpec(memory_space=pltpu.SMEM),
            pl.BlockSpec((w, dq), lambda n: (n, COL_AQ // dq)),
            pl.BlockSpec((w, kvw), lambda n: (n, COL_AK // kvw)),
            pl.BlockSpec((w, kvw), lambda n: (prev(n), COL_AK // kvw)),
            pl.BlockSpec((w, kvw), lambda n: (n, COL_AV // kvw)),
            pl.BlockSpec((w, kvw), lambda n: (prev(n), COL_AV // kvw)),
            pl.BlockSpec((None, N_HEADS, w, w), lambda n: (jnp.where(n == 0, 1, 0), 0, 0, 0)),
        ],
        out_specs=pl.BlockSpec((w, dq), lambda n: (n, 0)),
        out_shape=jax.ShapeDtypeStruct((s, dq), BF16),
        compiler_params=_cparams(("parallel",)),
        name="swa",
    )(sinks, proj, proj, proj, proj, proj, bias)


def _gla_kernel(*refs):
    nh = GLA_HEADS
    q_refs, k_refs, v_refs, gr_refs = (refs[i * nh:(i + 1) * nh] for i in range(4))
    la_ref, gn_ref, o_ref, state_ref = refs[4 * nh:]
    c = GLA_CHUNK
    rows, dk = q_refs[0].shape
    dv = v_refs[0].shape[1]

    @pl.when(pl.program_id(0) == 0)
    def _():
        state_ref[...] = jnp.zeros_like(state_ref)

    la_parts = _split3(la_ref[...])
    ri = lax.broadcasted_iota(jnp.int32, (rows, rows), 0)
    ci = lax.broadcasted_iota(jnp.int32, (rows, rows), 1)
    tri = ((ri // c == ci // c) & (ri >= ci)).astype(BF16)
    cum_all = _dot(tri, la_parts[0]) + _dot(tri, la_parts[1]) + _dot(tri, la_parts[2])
    causal = (lax.broadcasted_iota(jnp.int32, (c, c), 0) >= lax.broadcasted_iota(jnp.int32, (c, c), 1))
    gn = gn_ref[...]
    n_chunks = rows // c

    o_intra, q_in, upd, decay = {}, {}, {}, {}
    for h in range(nh):
        for ch in range(n_chunks):
            sl = slice(ch * c, (ch + 1) * c)
            cum_c = cum_all[sl, h * dk:(h + 1) * dk]
            ref = cum_c[c // 2 - 1:c // 2]
            last = cum_c[c - 1:c]
            q = q_refs[h][sl, :].astype(F32) * (dk ** -0.5)
            k = k_refs[h][sl, :].astype(F32)
            v = v_refs[h][sl, :]
            q_intra = (q * jnp.exp(cum_c - ref)).astype(BF16)
            k_intra = (k * jnp.exp(ref - cum_c)).astype(BF16)
            a = jnp.where(causal, _dot_nt(q_intra, k_intra), 0.0)
            o_intra[h, ch] = _dot(a.astype(BF16), v)
            q_in[h, ch] = (q * jnp.exp(cum_c)).astype(BF16)
            k_dec = (k * jnp.exp(last - cum_c)).astype(BF16)
            upd[h, ch] = _dot_tn(k_dec, v)
            decay[h, ch] = jnp.broadcast_to(jnp.exp(last), (LANES, dk)).T

    state = [state_ref[h] for h in range(nh)]
    for ch in range(n_chunks):
        sl = slice(ch * c, (ch + 1) * c)
        for h in range(nh):
            o = o_intra[h, ch] + _dot(q_in[h, ch], state[h].astype(BF16))
            state[h] = jnp.concatenate(
                [state[h][:, jv * LANES:(jv + 1) * LANES] * decay[h, ch]
                 + upd[h, ch][:, jv * LANES:(jv + 1) * LANES] for jv in range(dv // LANES)], axis=1)
            ms = jnp.mean(o * o, axis=-1, keepdims=True)
            on = o * lax.rsqrt(ms + RMS_EPS) * gn
            gr = gr_refs[h][sl, :].astype(F32)
            o_ref[sl, h * dv:(h + 1) * dv] = (on * (gr * _sigmoid(gr))).astype(o_ref.dtype)
    for h in range(nh):
        state_ref[h] = state[h]


def _gla(proj, la, gn, rows):
    s = proj.shape[0]
    dk, dv = GLA_DK, GLA_DV
    nh = GLA_HEADS

    def head_specs(width, col0):
        return [pl.BlockSpec((rows, width), functools.partial(lambda i, h: (i, col0 // width + h), h=h))
                for h in range(nh)]

    in_specs = (head_specs(dk, COL_GQ) + head_specs(dk, COL_GK) + head_specs(dv, COL_GV)
                + head_specs(dv, COL_GR)
                + [pl.BlockSpec((rows, nh * dk), lambda i: (i, 0)), pl.BlockSpec((1, dv), lambda i: (0, 0))])
    return pl.pallas_call(
        _gla_kernel,
        grid=(s // rows,),
        in_specs=in_specs,
        out_specs=pl.BlockSpec((rows, nh * dv), lambda i: (i, 0)),
        out_shape=jax.ShapeDtypeStruct((s, nh * dv), BF16),
        scratch_shapes=[pltpu.VMEM((nh, dk, dv), F32)],
        compiler_params=_cparams(("arbitrary",)),
        name="gla",
    )(*([proj] * (4 * nh)), la, gn)


MERGE_SUB_ROWS = 128


def _merge_out_kernel(oa_ref, og_ref, ga_ref, gg_ref, x_ref, wo_ref, g_ref, wr_ref,
                      x1_ref, h2_ref, lt_ref):
    sub = MERGE_SUB_ROWS
    tiles = [slice(t * sub, (t + 1) * sub) for t in range(x_ref.shape[0] // sub)]
    merged = [(_sigmoid(ga_ref[rs, :].astype(F32)) * oa_ref[rs, :].astype(F32)
               + _sigmoid(gg_ref[rs, :].astype(F32)) * og_ref[rs, :].astype(F32)).astype(BF16) for rs in tiles]
    projected = [_dot(m, wo_ref[...]) for m in merged]
    for rs, y in zip(tiles, projected):
        x1 = x_ref[rs, :] + y
        x1_ref[rs, :] = x1
        ms = jnp.mean(x1 * x1, axis=-1, keepdims=True)
        h2b = (x1 * lax.rsqrt(ms + RMS_EPS) * g_ref[...]).astype(BF16)
        h2_ref[rs] = h2b.reshape((sub,) + h2_ref.shape[1:])
        lt_ref[:, rs] = _dot_nt(wr_ref[...], h2b)


def _merge_out(o_attn, o_gla, gates, x, wo, g, wr, tm):
    s, d = x.shape
    nr = wr.shape[0]
    row = lambda i: (i, 0)
    return pl.pallas_call(
        _merge_out_kernel,
        grid=(s // tm,),
        in_specs=[
            pl.BlockSpec((tm, d), row),
            pl.BlockSpec((tm, d), row),
            pl.BlockSpec((tm, d), lambda i: (i, 0)),
            pl.BlockSpec((tm, d), lambda i: (i, 1)),
            pl.BlockSpec((tm, d), row),
            pl.BlockSpec((d, d), lambda i: (0, 0)),
            pl.BlockSpec((1, d), lambda i: (0, 0)),
            pl.BlockSpec((nr, d), lambda i: (0, 0)),
        ],
        out_specs=[
            pl.BlockSpec((tm, d), row),
            pl.BlockSpec((tm, d // LANES, LANES), lambda i: (i, 0, 0)),
            pl.BlockSpec((nr, tm), lambda i: (0, i)),
        ],
        out_shape=[
            jax.ShapeDtypeStruct((s, d), F32),
            jax.ShapeDtypeStruct((s, d // LANES, LANES), BF16),
            jax.ShapeDtypeStruct((nr, s), F32),
        ],
        compiler_params=_cparams(("parallel",)),
        name="merge_out",
    )(o_attn, o_gla, gates, gates, x, wo, g, wr)


ROUTER_ROWS = 8 + N_EXPERTS


def _route_kernel(lt_ref, bg_ref, be_ref, idx_ref, gate_ref, cnt_ref, carry_ref):
    tb = lt_ref.shape[1]
    epg = EXPERTS_PER_GROUP

    @pl.when(pl.program_id(0) == 0)
    def _():
        carry_ref[...] = jnp.zeros_like(carry_ref)

    logits = lt_ref[...]
    gl = logits[0:N_GROUPS]
    g_exp = jnp.exp(gl - jnp.max(gl, axis=0, keepdims=True))
    g_prob = g_exp / jnp.sum(g_exp, axis=0, keepdims=True)
    gb = gl + bg_ref[0:N_GROUPS]
    rowg = lax.broadcasted_iota(jnp.int32, (N_GROUPS, tb), 0)
    g_idx = jnp.min(jnp.where(gb == jnp.max(gb, axis=0, keepdims=True), rowg, N_GROUPS),
                    axis=0, keepdims=True)
    p_group = jnp.sum(jnp.where(rowg == g_idx, g_prob, 0.0), axis=0, keepdims=True)

    el = logits[8:8 + epg]
    eb = jnp.broadcast_to(be_ref[0:epg], (epg, tb))
    for g in range(1, N_GROUPS):
        pick = g_idx == g
        el = jnp.where(pick, logits[8 + g * epg:8 + (g + 1) * epg], el)
        eb = jnp.where(pick, be_ref[g * epg:(g + 1) * epg], eb)
    e_exp = jnp.exp(el - jnp.max(el, axis=0, keepdims=True))
    e_prob = e_exp / jnp.sum(e_exp, axis=0, keepdims=True)
    score = el + eb
    rowe = lax.broadcasted_iota(jnp.int32, (epg, tb), 0)
    i1 = jnp.min(jnp.where(score == jnp.max(score, axis=0, keepdims=True), rowe, epg),
                 axis=0, keepdims=True)
    score2 = jnp.where(rowe == i1, -jnp.inf, score)
    i2 = jnp.min(jnp.where(score2 == jnp.max(score2, axis=0, keepdims=True), rowe, epg),
                 axis=0, keepdims=True)
    q1 = jnp.sum(jnp.where(rowe == i1, e_prob, 0.0), axis=0, keepdims=True)
    q2 = jnp.sum(jnp.where(rowe == i2, e_prob, 0.0), axis=0, keepdims=True)
    qs = q1 + q2
    gate1 = p_group * q1 / qs
    gate2 = p_group * q2 / qs
    e1 = g_idx * epg + i1
    e2 = g_idx * epg + i2

    rowx = lax.broadcasted_iota(jnp.int32, (N_EXPERTS, tb), 0)
    hit1 = rowx == e1
    hit2 = rowx == e2
    member = (hit1 | hit2).astype(BF16)
    before = (lax.broadcasted_iota(jnp.int32, (tb, tb), 0)
              < lax.broadcasted_iota(jnp.int32, (tb, tb), 1)).astype(BF16)
    carry = carry_ref[...]
    count = _dot(member, before) + carry[:, 0:1]
    r1 = jnp.sum(jnp.where(hit1, count, 0.0), axis=0, keepdims=True).astype(jnp.int32)
    r2 = jnp.sum(jnp.where(hit2, count, 0.0), axis=0, keepdims=True).astype(jnp.int32)
    carry = carry + jnp.sum(member.astype(F32), axis=1, keepdims=True)
    carry_ref[...] = carry
    cnt_ref[...] = carry

    row8 = lax.broadcasted_iota(jnp.int32, (8, tb), 0)
    idx_ref[...] = jnp.where(row8 == 0, e1, jnp.where(row8 == 1, e2,
                             jnp.where(row8 == 2, r1, jnp.where(row8 == 3, r2, 0))))
    gate_ref[...] = jnp.where(row8 == 0, gate1, jnp.where(row8 == 1, gate2, 0.0))


def _route(lt, bg, be, tb):
    nr, s = lt.shape
    return pl.pallas_call(
        _route_kernel,
        grid=(s // tb,),
        in_specs=[
            pl.BlockSpec((nr, tb), lambda i: (0, i)),
            pl.BlockSpec((8, 1), lambda i: (0, 0)),
            pl.BlockSpec((N_EXPERTS, 1), lambda i: (0, 0)),
        ],
        out_specs=[
            pl.BlockSpec((8, tb), lambda i: (0, i)),
            pl.BlockSpec((8, tb), lambda i: (0, i)),
            pl.BlockSpec((N_EXPERTS, LANES), lambda i: (0, 0)),
        ],
        out_shape=[
            jax.ShapeDtypeStruct((8, s), jnp.int32),
            jax.ShapeDtypeStruct((8, s), F32),
            jax.ShapeDtypeStruct((N_EXPERTS, LANES), F32),
        ],
        scratch_shapes=[pltpu.VMEM((N_EXPERTS, LANES), F32)],
        compiler_params=_cparams(("arbitrary",)),
        name="route",
    )(lt, bg, be)


def _expert_kernel(be_ref, nx_ref, nu_ref, dest_ref, h2_hbm, wg_hbm, wu_hbm, wd_hbm, y_hbm,
                   pair_ref, h2v, xg, ystage, wgf, wuf, wdf, wgb, wub, wdb, hsem, wsem, ysem):
    n_used = nu_ref[0]
    rows = MOE_BLOCK
    n_tok = h2v.shape[0]

    def weight_copy(e, m):
        src, stage = ((wg_hbm, wgf), (wu_hbm, wuf), (wd_hbm, wdf))[m]
        return pltpu.make_async_copy(src.at[e], stage, wsem.at[m])

    def gather_row(blk, r, half):
        pair = jnp.maximum(pair_ref[blk * rows + r], 0)
        xg[half, r] = h2v[jnp.where(pair >= n_tok, pair - n_tok, pair)]

    def row_write(blk, r, half, dummy):
        pair = pair_ref[blk * rows + r]
        dst = jnp.where((pair < 0) | dummy, TOP_K * n_tok + half * rows + r, pair)
        return pltpu.make_async_copy(ystage.at[half, r], y_hbm.at[dst], ysem.at[half])

    def wait_row_writes(half):
        pltpu.make_async_copy(ystage.at[half], y_hbm.at[pl.ds(0, rows)], ysem.at[half]).wait()

    weight_priority = 1

    h2_copy = pltpu.make_async_copy(h2_hbm, h2v, hsem.at[0])
    h2_copy.start()
    for m in range(3):
        weight_copy(be_ref[0], m).start(priority=weight_priority)
    ystage[...] = jnp.zeros_like(ystage)

    def clear(r, carry):
        pair_ref[r] = -1
        return carry
    lax.fori_loop(0, pair_ref.shape[0], clear, 0, unroll=8)

    def place(p, carry):
        pair_ref[dest_ref[p]] = p
        return carry
    lax.fori_loop(0, dest_ref.shape[0], place, 0, unroll=8)

    h2_copy.wait()
    spare_fill = pltpu.make_async_copy(ystage.at[0], y_hbm.at[pl.ds(TOP_K * n_tok, rows)], hsem.at[0])
    spare_fill.start()
    spare_fill.wait()

    def first_gather(r, carry):
        gather_row(0, r, 0)
        return carry
    lax.fori_loop(0, rows, first_gather, 0)

    def block(b, carry):
        slot = b % 2
        first_of_expert = (b == 0) | (be_ref[b] != be_ref[jnp.maximum(b - 1, 0)])

        @pl.when(first_of_expert)
        def _():
            nxt = nx_ref[b]
            for m, (stage, dst) in enumerate(((wgf, wgb), (wuf, wub), (wdf, wdb))):
                weight_copy(be_ref[b], m).wait()
                dst[...] = stage[...].astype(BF16)

                @pl.when(nxt >= 0)
                def _():
                    weight_copy(nxt, m).start(priority=weight_priority)

        xb = xg[slot].reshape(rows, wgb.shape[0])
        prev_blk = jnp.maximum(b - 1, 0)
        for r in range(rows):
            row_write(prev_blk, r, 1 - slot, b == 0).start()
        for r in range(rows):
            gather_row(b + 1, r, 1 - slot)
        hg = _dot(xb, wgb[...])
        hu = _dot(xb, wub[...])
        act = (hg * _sigmoid(hg)) * hu
        y = _dot(act.astype(BF16), wdb[...]).astype(BF16)

        @pl.when(b >= 1)
        def _():
            wait_row_writes(slot)

        ystage[slot] = y.reshape(ystage.shape[1:])
        return carry

    lax.fori_loop(0, n_used, block, 0)

    last_half = (n_used - 1) % 2

    def last_writes(r, carry):
        row_write(n_used - 1, r, last_half, False).start()
        return carry
    lax.fori_loop(0, rows, last_writes, 0)
    wait_row_writes(1 - last_half)
    wait_row_writes(last_half)


def _experts(block_e, next_e, n_used, dest, h2c, wg, wu, wd):
    n_rows = block_e.shape[0] * MOE_BLOCK
    n_tok = h2c.shape[0]
    _, d, de = wg.shape
    slab = (d // LANES, LANES)
    grid_spec = pltpu.PrefetchScalarGridSpec(
        num_scalar_prefetch=4,
        grid=(1,),
        in_specs=[pl.BlockSpec(memory_space=pl.ANY)] * 4,
        out_specs=pl.BlockSpec(memory_space=pl.ANY),
        scratch_shapes=[
            pltpu.SMEM((n_rows,), jnp.int32),
            pltpu.VMEM((n_tok,) + slab, BF16),
            pltpu.VMEM((2, MOE_BLOCK) + slab, BF16),
            pltpu.VMEM((2, MOE_BLOCK) + slab, BF16),
            pltpu.VMEM((d, de), F32),
            pltpu.VMEM((d, de), F32),
            pltpu.VMEM((de, d), F32),
            pltpu.VMEM((d, de), BF16),
            pltpu.VMEM((d, de), BF16),
            pltpu.VMEM((de, d), BF16),
            pltpu.SemaphoreType.DMA((1,)),
            pltpu.SemaphoreType.DMA((3,)),
            pltpu.SemaphoreType.DMA((2,)),
        ],
    )
    return pl.pallas_call(
        _expert_kernel,
        grid_spec=grid_spec,
        out_shape=jax.ShapeDtypeStruct((TOP_K * n_tok + 2 * MOE_BLOCK,) + slab, BF16),
        compiler_params=_cparams(("arbitrary",), vmem=EXPERT_VMEM_LIMIT),
        name="experts",
    )(block_e, next_e, n_used, dest, h2c, wg, wu, wd)


def _combine_kernel(y0_ref, y1_ref, gate_ref, x1_ref, g_ref, o_ref):
    gates = gate_ref[...]
    y0 = y0_ref[...].reshape(x1_ref.shape).astype(F32)
    y1 = y1_ref[...].reshape(x1_ref.shape).astype(F32)
    x2 = x1_ref[...] + (y0 * gates[:, 0:1] + y1 * gates[:, 1:2])
    ms = jnp.mean(x2 * x2, axis=-1, keepdims=True)
    o_ref[...] = x2 * lax.rsqrt(ms + RMS_EPS) * g_ref[...]


def _combine(y2, gates, x1, g, tm):
    s, d = x1.shape
    slab = (d // LANES, LANES)
    n_steps = s // tm
    return pl.pallas_call(
        _combine_kernel,
        grid=(n_steps,),
        in_specs=[
            pl.BlockSpec((tm,) + slab, lambda i: (i, 0, 0)),
            pl.BlockSpec((tm,) + slab, lambda i: (n_steps + i, 0, 0)),
            pl.BlockSpec((tm, TOP_K), lambda i: (i, 0)),
            pl.BlockSpec((tm, d), lambda i: (i, 0)),
            pl.BlockSpec((1, d), lambda i: (0, 0)),
        ],
        out_specs=pl.BlockSpec((tm, d), lambda i: (i, 0)),
        out_shape=jax.ShapeDtypeStruct((s, d), F32),
        compiler_params=_cparams(("parallel",)),
        name="combine",
    )(y2, y2, gates, x1, g)


def _tile(n, pref):
    return pref if n % pref == 0 else n


def kernel(x, norm_mix_g, w_in, w_gla_a2, b_gla_a, gla_norm_g, attn_sinks, rel_bias_table, w_out,
           norm_ffn_g, w_router_group, b_router_group, w_router_expert, b_router_expert,
           w_expert_gate, w_expert_up, w_expert_down, norm_final_g):
    b, s, d = x.shape
    assert b == 1 and w_in.shape[0] == 1, "single batch, single layer"
    assert (s * TOP_K) % MOE_BLOCK == 0
    x2d = x.reshape(s, d)
    w_in_t = w_in[0].T

    w2 = jnp.pad(w_gla_a2[0], ((0, LANES - GLA_LOWRANK), (0, 0))).astype(BF16)
    h, log_a = _norm_loga(x2d, norm_mix_g[0].reshape(1, d), w_in_t, w2, b_gla_a[0].reshape(1, -1),
                          _tile(s, 512))
    proj = _in_proj(h, w_in_t, _tile(s, 4096), 512)

    bucket, is_prev = _folded_maps()
    bias = _rel_bias(rel_bias_table.T, jnp.asarray(bucket), jnp.asarray(is_prev))
    bias = bias.reshape(2, N_HEADS, WINDOW, WINDOW)
    o_attn = _swa(proj, attn_sinks[0], bias)
    o_gla = _gla(proj, log_a, gla_norm_g[0].reshape(1, -1), _tile(s, 256))

    wr = jnp.zeros((ROUTER_ROWS, d), F32)
    wr = wr.at[0:N_GROUPS].set(w_router_group[0].T).at[8:].set(w_router_expert[0].T).astype(BF16)
    x1, h2, logits_t = _merge_out(o_attn, o_gla, proj, x2d, w_out[0].astype(BF16),
                                  norm_ffn_g[0].reshape(1, d), wr, _tile(s, 256))

    bg = jnp.pad(b_router_group[0], (0, 8 - N_GROUPS)).reshape(8, 1)
    idx, gate, counts = _route(logits_t, bg, b_router_expert[0].reshape(N_EXPERTS, 1), _tile(s, 512))
    counts = counts[:, 0].astype(jnp.int32)
    padded = (counts + MOE_BLOCK - 1) // MOE_BLOCK * MOE_BLOCK
    pend = jnp.cumsum(padded)
    pstart = pend - padded
    expert = idx[0:TOP_K]
    eids = jnp.arange(N_EXPERTS, dtype=jnp.int32)[:, None, None]
    dest = jnp.sum(jnp.where(expert[None] == eids, pstart[:, None, None], 0), axis=0) + idx[TOP_K:2 * TOP_K]
    n_pairs = s * TOP_K
    n_blocks = (n_pairs + N_EXPERTS * (MOE_BLOCK - 1) + MOE_BLOCK - 1) // MOE_BLOCK
    n_used = (pend[-1] // MOE_BLOCK).reshape(1)
    block_start = jnp.arange(n_blocks, dtype=jnp.int32) * MOE_BLOCK
    block_e = jnp.minimum(jnp.sum((pend[None, :] <= block_start[:, None]).astype(jnp.int32), axis=1),
                          N_EXPERTS - 1)

    e_col = jnp.arange(N_EXPERTS, dtype=jnp.int32)[:, None]
    e_row = jnp.arange(N_EXPERTS, dtype=jnp.int32)[None, :]
    later = (e_row > e_col) & (counts > 0)[None, :]
    next_nonempty = jnp.min(jnp.where(later, e_row, N_EXPERTS), axis=1)
    next_nonempty = jnp.where(next_nonempty < N_EXPERTS, next_nonempty, -1)
    next_e = jnp.sum(jnp.where(block_e[:, None] == e_row, next_nonempty[None, :], 0), axis=1)
    y2 = _experts(block_e, next_e, n_used, dest.reshape(-1), h2,
                  w_expert_gate[0], w_expert_up[0], w_expert_down[0])
    out = _combine(y2, gate[0:TOP_K].T, x1, norm_final_g.reshape(1, d), _tile(s, 512))
    return out.reshape(b, s, d)
```

```python
import functools
import math

import numpy as np
import jax
import jax.numpy as jnp
from jax import lax
from jax.experimental import pallas as pl
from jax.experimental.pallas import tpu as pltpu

F32 = jnp.float32
BF16 = jnp.bfloat16

N_HEADS = 32
N_KV_HEADS = 4
HEAD_DIM = 64
WINDOW = 128
N_BUCKETS = 32
MAX_DISTANCE = 128
GLA_HEADS = 4
GLA_DK = 256
GLA_DV = 512
GLA_LOWRANK = 16
GLA_TAU = 16.0
GLA_CHUNK = 64
N_GROUPS = 4
EXPERTS_PER_GROUP = 8
N_EXPERTS = 32
TOP_K = 2
MOE_BLOCK = 128
RMS_EPS = 1e-6
NEG_INF = -1e30

COL_GA = 8704
COL_GATES = 8720
MAIN_COLS = 8704
GATE_COLS = 4096
COL_AQ = GATE_COLS + 0
COL_AK = GATE_COLS + 2048
COL_AV = GATE_COLS + 2304
COL_GQ = GATE_COLS + 2560
COL_GK = GATE_COLS + 3584
COL_GV = GATE_COLS + 4608
COL_GR = GATE_COLS + 6656

LANES = 128
VMEM_LIMIT = 56 * 1024 * 1024
EXPERT_VMEM_LIMIT = 60 * 1024 * 1024


def _cparams(sem, vmem=VMEM_LIMIT):
    return pltpu.CompilerParams(dimension_semantics=sem, vmem_limit_bytes=vmem)


def _split3(x):
    hi = x.astype(BF16)
    r1 = x - hi.astype(F32)
    mid = r1.astype(BF16)
    lo = (r1 - mid.astype(F32)).astype(BF16)
    return hi, mid, lo


def _sigmoid(x):
    return 0.5 * jnp.tanh(0.5 * x) + 0.5


def _dot(a, b):
    return jnp.dot(a, b, preferred_element_type=F32)


def _dot_nt(a, b):
    return lax.dot_general(a, b, (((1,), (1,)), ((), ())), preferred_element_type=F32)


def _dot_tn(a, b):
    return lax.dot_general(a, b, (((0,), (0,)), ((), ())), preferred_element_type=F32)


def _norm_loga_kernel(x_ref, g_ref, wga_ref, w2_ref, b_ref, h_ref, la_ref):
    x = x_ref[...]
    ms = jnp.mean(x * x, axis=-1, keepdims=True)
    hb = (x * lax.rsqrt(ms + RMS_EPS) * g_ref[...]).astype(BF16)
    h_ref[...] = hb
    row = lax.broadcasted_iota(jnp.int32, wga_ref.shape, 0)
    wga = jnp.where(row < GLA_LOWRANK, wga_ref[...], 0.0).astype(BF16)
    ga = _dot_nt(hb, wga)
    z = _dot(ga.astype(BF16), w2_ref[...]) + b_ref[...]
    la_ref[...] = (jnp.minimum(z, 0.0) - jnp.log1p(jnp.exp(-jnp.abs(z)))) * (1.0 / GLA_TAU)


def _norm_loga(x, g, wga, w2, b, tm):
    s, d = x.shape
    nq = w2.shape[1]
    return pl.pallas_call(
        _norm_loga_kernel,
        grid=(s // tm,),
        in_specs=[
            pl.BlockSpec((tm, d), lambda i: (i, 0)),
            pl.BlockSpec((1, d), lambda i: (0, 0)),
            pl.BlockSpec((LANES, d), lambda i: (COL_GA // LANES, 0)),
            pl.BlockSpec((LANES, nq), lambda i: (0, 0)),
            pl.BlockSpec((1, nq), lambda i: (0, 0)),
        ],
        out_specs=[
            pl.BlockSpec((tm, d), lambda i: (i, 0)),
            pl.BlockSpec((tm, nq), lambda i: (i, 0)),
        ],
        out_shape=[
            jax.ShapeDtypeStruct((s, d), BF16),
            jax.ShapeDtypeStruct((s, nq), F32),
        ],
        compiler_params=_cparams(("parallel",)),
        name="norm_loga",
    )(x, g, wga, w2, b)


def _in_proj_kernel(h_hbm, wt_hbm, o_ref, h_vmem, w_stage, wbf, sem, *, n_main, n_tiles, tn):
    j = pl.program_id(0)
    i = pl.program_id(1)
    tm = o_ref.shape[0]

    def w_copy(jj):
        row0 = jnp.where(jj < n_main, jj * tn, COL_GATES + (jj - n_main) * tn)
        return pltpu.make_async_copy(wt_hbm.at[pl.ds(pl.multiple_of(row0, 8), tn)], w_stage, sem.at[1])

    @pl.when((j == 0) & (i == 0))
    def _():
        h_copy = pltpu.make_async_copy(h_hbm, h_vmem, sem.at[0])
        h_copy.start()
        w_copy(0).start()
        h_copy.wait()

    @pl.when(i == 0)
    def _():
        w_copy(j).wait()
        wbf[...] = w_stage[...].astype(BF16)

        @pl.when(j + 1 < n_tiles)
        def _():
            w_copy(j + 1).start()

    h = h_vmem[pl.ds(pl.multiple_of(i * tm, tm), tm), :]
    o_ref[...] = _dot_nt(h, wbf[...]).astype(o_ref.dtype)


def _in_proj(h, w_t, tm, tn):
    s, d = h.shape
    n_main = MAIN_COLS // tn
    n_gate = GATE_COLS // tn
    n_tiles = n_main + n_gate
    kern = functools.partial(_in_proj_kernel, n_main=n_main, n_tiles=n_tiles, tn=tn)
    return pl.pallas_call(
        kern,
        grid=(n_tiles, s // tm),
        in_specs=[pl.BlockSpec(memory_space=pl.ANY), pl.BlockSpec(memory_space=pl.ANY)],
        out_specs=pl.BlockSpec((tm, tn), lambda j, i: (i, (j + n_gate) % n_tiles)),
        out_shape=jax.ShapeDtypeStruct((s, n_tiles * tn), BF16),
        scratch_shapes=[
            pltpu.VMEM((s, d), BF16),
            pltpu.VMEM((tn, d), F32),
            pltpu.VMEM((tn, d), BF16),
            pltpu.SemaphoreType.DMA((2,)),
        ],
        compiler_params=_cparams(("arbitrary", "arbitrary")),
        name="in_proj",
    )(h, w_t)


def _folded_maps():
    j = np.arange(WINDOW)[:, None]
    i = np.arange(WINDOW)[None, :]
    n = (i - j) % WINDOW
    max_exact = N_BUCKETS // 2
    ratio = np.maximum(n, max_exact).astype(np.float32) / np.float32(max_exact)
    large = max_exact + (np.log(ratio) / np.float32(math.log(MAX_DISTANCE / max_exact))
                         * (N_BUCKETS - max_exact)).astype(np.int32)
    large = np.minimum(large, N_BUCKETS - 1)
    bucket = np.where(n < max_exact, n, large).astype(np.int32).reshape(1, -1)
    is_prev = (j > i).astype(np.int32).reshape(1, -1)
    return bucket, is_prev


def _rel_bias_kernel(tab_ref, bucket_ref, prev_ref, o_ref):
    nb = tab_ref.shape[1]
    width = bucket_ref.shape[1]
    onehot = (lax.broadcasted_iota(jnp.int32, (nb, width), 0) == bucket_ref[...]).astype(BF16)
    hi, mid, lo = _split3(tab_ref[...])
    bias = _dot(hi, onehot) + _dot(mid, onehot) + _dot(lo, onehot)
    o_ref[0] = bias
    o_ref[1] = jnp.where(prev_ref[...] > 0, NEG_INF, bias)


def _rel_bias(table_t, bucket, is_prev):
    nh = table_t.shape[0]
    width = bucket.shape[1]
    return pl.pallas_call(
        _rel_bias_kernel,
        out_shape=jax.ShapeDtypeStruct((2, nh, width), F32),
        compiler_params=pltpu.CompilerParams(vmem_limit_bytes=VMEM_LIMIT),
        name="rel_bias",
    )(table_t, bucket, is_prev)


SWA_SCORES_AHEAD = 10


def _swa_kernel(sink_ref, q_ref, kc_ref, kp_ref, vc_ref, vp_ref, bias_ref, o_ref):
    w = WINDOW
    hd = HEAD_DIM
    gq = N_HEADS // N_KV_HEADS
    is_prev = (lax.broadcasted_iota(jnp.int32, (w, w), 0) > lax.broadcasted_iota(jnp.int32, (w, w), 1))
    zeros = jnp.zeros((2 * w, hd), BF16)
    scale = HEAD_DIM ** -0.5
    k_pl, v_pl = [], []
    for g in range(N_KV_HEADS):
        cs = slice(g * hd, (g + 1) * hd)
        k_g = jnp.concatenate([kp_ref[:, cs], kc_ref[:, cs]], axis=0) * scale
        v_g = jnp.concatenate([vp_ref[:, cs], vc_ref[:, cs]], axis=0)
        k_pl.append((jnp.concatenate([k_g, zeros], axis=1), jnp.concatenate([zeros, k_g], axis=1)))
        v_pl.append((jnp.concatenate([v_g, zeros], axis=1), jnp.concatenate([zeros, v_g], axis=1)))

    def scores(hh):
        pair = hh - hh % 2
        return _dot_nt(k_pl[hh // gq][hh % 2], q_ref[:, pair * hd:(pair + 2) * hd])

    def attend(hh, st2):
        st = jnp.where(is_prev, st2[:w], st2[w:]) + bias_ref[hh]
        sink = sink_ref[hh]
        m = jnp.maximum(jnp.max(st, axis=0, keepdims=True), sink)
        p = jnp.exp(st - m)
        recip = 1.0 / (jnp.sum(p, axis=0, keepdims=True) + jnp.exp(sink - m))
        probs = p * recip
        p_cat = jnp.concatenate([jnp.where(is_prev, probs, 0.0), jnp.where(is_prev, 0.0, probs)], axis=0)
        return _dot_tn(p_cat.astype(BF16), v_pl[hh // gq][hh % 2])

    ahead = SWA_SCORES_AHEAD
    st2 = {hh: scores(hh) for hh in range(ahead)}
    acc = None
    for hh in range(N_HEADS):
        if hh + ahead < N_HEADS:
            st2[hh + ahead] = scores(hh + ahead)
        part = attend(hh, st2.pop(hh))
        if hh % 2 == 0:
            acc = part
        else:
            o_ref[:, (hh - 1) * hd:(hh + 1) * hd] = (acc + part).astype(o_ref.dtype)


def _swa(proj, sinks, bias):
    s = proj.shape[0]
    w = WINDOW
    nb = s // w
    kvw = N_KV_HEADS * HEAD_DIM
    dq = N_HEADS * HEAD_DIM
    prev = lambda n: jnp.maximum(n - 1, 0)
    return pl.pallas_call(
        _swa_kernel,
        grid=(nb,),
        in_specs=[
            pl.BlockSpec(memory_space=pltpu.SMEM),
            pl.BlockSpec((w, dq), lambda n: (n, COL_AQ // dq)),
            pl.BlockSpec((w, kvw), lambda n: (n, COL_AK // kvw)),
            pl.BlockSpec((w, kvw), lambda n: (prev(n), COL_AK // kvw)),
            pl.BlockSpec((w, kvw), lambda n: (n, COL_AV // kvw)),
            pl.BlockSpec((w, kvw), lambda n: (prev(n), COL_AV // kvw)),
            pl.BlockSpec((None, N_HEADS, w, w), lambda n: (jnp.where(n == 0, 1, 0), 0, 0, 0)),
        ],
        out_specs=pl.BlockSpec((w, dq), lambda n: (n, 0)),
        out_shape=jax.ShapeDtypeStruct((s, dq), BF16),
        compiler_params=_cparams(("parallel",)),
        name="swa",
    )(sinks, proj, proj, proj, proj, proj, bias)


def _gla_kernel(*refs):
    nh = GLA_HEADS
    q_refs, k_refs, v_refs, gr_refs = (refs[i * nh:(i + 1) * nh] for i in range(4))
    la_ref, gn_ref, o_ref, state_ref = refs[4 * nh:]
    c = GLA_CHUNK
    rows, dk = q_refs[0].shape
    dv = v_refs[0].shape[1]

    @pl.when(pl.program_id(0) == 0)
    def _():
        state_ref[...] = jnp.zeros_like(state_ref)

    la_parts = _split3(la_ref[...])
    ri = lax.broadcasted_iota(jnp.int32, (rows, rows), 0)
    ci = lax.broadcasted_iota(jnp.int32, (rows, rows), 1)
    tri = ((ri // c == ci // c) & (ri >= ci)).astype(BF16)
    cum_all = _dot(tri, la_parts[0]) + _dot(tri, la_parts[1]) + _dot(tri, la_parts[2])
    causal = (lax.broadcasted_iota(jnp.int32, (c, c), 0) >= lax.broadcasted_iota(jnp.int32, (c, c), 1))
    gn = gn_ref[...]
    n_chunks = rows // c

    o_intra, q_in, upd, decay = {}, {}, {}, {}
    for h in range(nh):
        for ch in range(n_chunks):
            sl = slice(ch * c, (ch + 1) * c)
            cum_c = cum_all[sl, h * dk:(h + 1) * dk]
            ref = cum_c[c // 2 - 1:c // 2]
            last = cum_c[c - 1:c]
            q = q_refs[h][sl, :].astype(F32) * (dk ** -0.5)
            k = k_refs[h][sl, :].astype(F32)
            v = v_refs[h][sl, :]
            q_intra = (q * jnp.exp(cum_c - ref)).astype(BF16)
            k_intra = (k * jnp.exp(ref - cum_c)).astype(BF16)
            a = jnp.where(causal, _dot_nt(q_intra, k_intra), 0.0)
            o_intra[h, ch] = _dot(a.astype(BF16), v)
            q_in[h, ch] = (q * jnp.exp(cum_c)).astype(BF16)
            k_dec = (k * jnp.exp(last - cum_c)).astype(BF16)
            upd[h, ch] = _dot_tn(k_dec, v)
            decay[h, ch] = jnp.broadcast_to(jnp.exp(last), (LANES, dk)).T

    state = [state_ref[h] for h in range(nh)]
    for ch in range(n_chunks):
        sl = slice(ch * c, (ch + 1) * c)
        for h in range(nh):
            o = o_intra[h, ch] + _dot(q_in[h, ch], state[h].astype(BF16))
            state[h] = jnp.concatenate(
                [state[h][:, jv * LANES:(jv + 1) * LANES] * decay[h, ch]
                 + upd[h, ch][:, jv * LANES:(jv + 1) * LANES] for jv in range(dv // LANES)], axis=1)
            ms = jnp.mean(o * o, axis=-1, keepdims=True)
            on = o * lax.rsqrt(ms + RMS_EPS) * gn
            gr = gr_refs[h][sl, :].astype(F32)
            o_ref[sl, h * dv:(h + 1) * dv] = (on * (gr * _sigmoid(gr))).astype(o_ref.dtype)
    for h in range(nh):
        state_ref[h] = state[h]


def _gla(proj, la, gn, rows):
    s = proj.shape[0]
    dk, dv = GLA_DK, GLA_DV
    nh = GLA_HEADS

    def head_specs(width, col0):
        return [pl.BlockSpec((rows, width), functools.partial(lambda i, h: (i, col0 // width + h), h=h))
                for h in range(nh)]

    in_specs = (head_specs(dk, COL_GQ) + head_specs(dk, COL_GK) + head_specs(dv, COL_GV)
                + head_specs(dv, COL_GR)
                + [pl.BlockSpec((rows, nh * dk), lambda i: (i, 0)), pl.BlockSpec((1, dv), lambda i: (0, 0))])
    return pl.pallas_call(
        _gla_kernel,
        grid=(s // rows,),
        in_specs=in_specs,
        out_specs=pl.BlockSpec((rows, nh * dv), lambda i: (i, 0)),
        out_shape=jax.ShapeDtypeStruct((s, nh * dv), BF16),
        scratch_shapes=[pltpu.VMEM((nh, dk, dv), F32)],
        compiler_params=_cparams(("arbitrary",)),
        name="gla",
    )(*([proj] * (4 * nh)), la, gn)


MERGE_SUB_ROWS = 128


def _merge_out_kernel(oa_ref, og_ref, ga_ref, gg_ref, x_ref, wo_ref, g_ref, wr_ref,
                      x1_ref, h2_ref, lt_ref):
    sub = MERGE_SUB_ROWS
    tiles = [slice(t * sub, (t + 1) * sub) for t in range(x_ref.shape[0] // sub)]
    merged = [(_sigmoid(ga_ref[rs, :].astype(F32)) * oa_ref[rs, :].astype(F32)
               + _sigmoid(gg_ref[rs, :].astype(F32)) * og_ref[rs, :].astype(F32)).astype(BF16) for rs in tiles]
    projected = [_dot(m, wo_ref[...]) for m in merged]
    for rs, y in zip(tiles, projected):
        x1 = x_ref[rs, :] + y
        x1_ref[rs, :] = x1
        ms = jnp.mean(x1 * x1, axis=-1, keepdims=True)
        h2b = (x1 * lax.rsqrt(ms + RMS_EPS) * g_ref[...]).astype(BF16)
        h2_ref[rs] = h2b.reshape((sub,) + h2_ref.shape[1:])
        lt_ref[:, rs] = _dot_nt(wr_ref[...], h2b)


def _merge_out(o_attn, o_gla, gates, x, wo, g, wr, tm):
    s, d = x.shape
    nr = wr.shape[0]
    row = lambda i: (i, 0)
    return pl.pallas_call(
        _merge_out_kernel,
        grid=(s // tm,),
        in_specs=[
            pl.BlockSpec((tm, d), row),
            pl.BlockSpec((tm, d), row),
            pl.BlockSpec((tm, d), lambda i: (i, 0)),
            pl.BlockSpec((tm, d), lambda i: (i, 1)),
            pl.BlockSpec((tm, d), row),
            pl.BlockSpec((d, d), lambda i: (0, 0), pipeline_mode=pl.Buffered(1)),
            pl.BlockSpec((1, d), lambda i: (0, 0)),
            pl.BlockSpec((nr, d), lambda i: (0, 0)),
        ],
        out_specs=[
            pl.BlockSpec((tm, d), row),
            pl.BlockSpec((tm, d // LANES, LANES), lambda i: (i, 0, 0)),
            pl.BlockSpec((nr, tm), lambda i: (0, i)),
        ],
        out_shape=[
            jax.ShapeDtypeStruct((s, d), F32),
            jax.ShapeDtypeStruct((s, d // LANES, LANES), BF16),
            jax.ShapeDtypeStruct((nr, s), F32),
        ],
        compiler_params=_cparams(("parallel",)),
        name="merge_out",
    )(o_attn, o_gla, gates, gates, x, wo, g, wr)


ROUTER_ROWS = 8 + N_EXPERTS


def _route_kernel(lt_ref, bg_ref, be_ref, idx_ref, gate_ref, cnt_ref, carry_ref):
    tb = lt_ref.shape[1]
    epg = EXPERTS_PER_GROUP

    @pl.when(pl.program_id(0) == 0)
    def _():
        carry_ref[...] = jnp.zeros_like(carry_ref)

    logits = lt_ref[...]
    gl = logits[0:N_GROUPS]
    g_exp = jnp.exp(gl - jnp.max(gl, axis=0, keepdims=True))
    g_prob = g_exp / jnp.sum(g_exp, axis=0, keepdims=True)
    gb = gl + bg_ref[0:N_GROUPS]
    rowg = lax.broadcasted_iota(jnp.int32, (N_GROUPS, tb), 0)
    g_idx = jnp.min(jnp.where(gb == jnp.max(gb, axis=0, keepdims=True), rowg, N_GROUPS),
                    axis=0, keepdims=True)
    p_group = jnp.sum(jnp.where(rowg == g_idx, g_prob, 0.0), axis=0, keepdims=True)

    el = logits[8:8 + epg]
    eb = jnp.broadcast_to(be_ref[0:epg], (epg, tb))
    for g in range(1, N_GROUPS):
        pick = g_idx == g
        el = jnp.where(pick, logits[8 + g * epg:8 + (g + 1) * epg], el)
        eb = jnp.where(pick, be_ref[g * epg:(g + 1) * epg], eb)
    e_exp = jnp.exp(el - jnp.max(el, axis=0, keepdims=True))
    e_prob = e_exp / jnp.sum(e_exp, axis=0, keepdims=True)
    score = el + eb
    rowe = lax.broadcasted_iota(jnp.int32, (epg, tb), 0)
    i1 = jnp.min(jnp.where(score == jnp.max(score, axis=0, keepdims=True), rowe, epg),
                 axis=0, keepdims=True)
    score2 = jnp.where(rowe == i1, -jnp.inf, score)
    i2 = jnp.min(jnp.where(score2 == jnp.max(score2, axis=0, keepdims=True), rowe, epg),
                 axis=0, keepdims=True)
    q1 = jnp.sum(jnp.where(rowe == i1, e_prob, 0.0), axis=0, keepdims=True)
    q2 = jnp.sum(jnp.where(rowe == i2, e_prob, 0.0), axis=0, keepdims=True)
    qs = q1 + q2
    gate1 = p_group * q1 / qs
    gate2 = p_group * q2 / qs
    e1 = g_idx * epg + i1
    e2 = g_idx * epg + i2

    rowx = lax.broadcasted_iota(jnp.int32, (N_EXPERTS, tb), 0)
    hit1 = rowx == e1
    hit2 = rowx == e2
    member = (hit1 | hit2).astype(BF16)
    before = (lax.broadcasted_iota(jnp.int32, (tb, tb), 0)
              < lax.broadcasted_iota(jnp.int32, (tb, tb), 1)).astype(BF16)
    carry = carry_ref[...]
    count = _dot(member, before) + carry[:, 0:1]
    r1 = jnp.sum(jnp.where(hit1, count, 0.0), axis=0, keepdims=True).astype(jnp.int32)
    r2 = jnp.sum(jnp.where(hit2, count, 0.0), axis=0, keepdims=True).astype(jnp.int32)
    carry = carry + jnp.sum(member.astype(F32), axis=1, keepdims=True)
    carry_ref[...] = carry
    cnt_ref[...] = carry

    row8 = lax.broadcasted_iota(jnp.int32, (8, tb), 0)
    idx_ref[...] = jnp.where(row8 == 0, e1, jnp.where(row8 == 1, e2,
                             jnp.where(row8 == 2, r1, jnp.where(row8 == 3, r2, 0))))
    gate_ref[...] = jnp.where(row8 == 0, gate1, jnp.where(row8 == 1, gate2, 0.0))


def _route(lt, bg, be, tb):
    nr, s = lt.shape
    return pl.pallas_call(
        _route_kernel,
        grid=(s // tb,),
        in_specs=[
            pl.BlockSpec((nr, tb), lambda i: (0, i)),
            pl.BlockSpec((8, 1), lambda i: (0, 0)),
            pl.BlockSpec((N_EXPERTS, 1), lambda i: (0, 0)),
        ],
        out_specs=[
            pl.BlockSpec((8, tb), lambda i: (0, i)),
            pl.BlockSpec((8, tb), lambda i: (0, i)),
            pl.BlockSpec((N_EXPERTS, LANES), lambda i: (0, 0)),
        ],
        out_shape=[
            jax.ShapeDtypeStruct((8, s), jnp.int32),
            jax.ShapeDtypeStruct((8, s), F32),
            jax.ShapeDtypeStruct((N_EXPERTS, LANES), F32),
        ],
        scratch_shapes=[pltpu.VMEM((N_EXPERTS, LANES), F32)],
        compiler_params=_cparams(("arbitrary",)),
        name="route",
    )(lt, bg, be)


def _expert_kernel(be_ref, nx_ref, nu_ref, dest_ref, neg_hbm, h2_hbm, wg_hbm, wu_hbm, wd_hbm, y_hbm,
                   pair_ref, h2v, xg, ystage, wgf, wuf, wdf, wgb, wub, wdb, hsem, wsem, ysem):
    n_used = nu_ref[0]
    rows = MOE_BLOCK
    n_tok = h2v.shape[0]

    def weight_copy(e, m):
        src, stage = ((wg_hbm, wgf), (wu_hbm, wuf), (wd_hbm, wdf))[m]
        return pltpu.make_async_copy(src.at[e], stage, wsem.at[m])

    def gather_row(blk, r, half):
        pair = jnp.maximum(pair_ref[blk * rows + r], 0)
        xg[half, r] = h2v[jnp.where(pair >= n_tok, pair - n_tok, pair)]

    def row_write(blk, r, half, dummy):
        pair = pair_ref[blk * rows + r]
        dst = jnp.where((pair < 0) | dummy, TOP_K * n_tok + half * rows + r, pair)
        return pltpu.make_async_copy(ystage.at[half, r], y_hbm.at[dst], ysem.at[half])

    def wait_row_writes(half):
        pltpu.make_async_copy(ystage.at[half], y_hbm.at[pl.ds(0, rows)], ysem.at[half]).wait()

    weight_priority = 1

    fill = pltpu.make_async_copy(neg_hbm, pair_ref, hsem.at[1])
    fill.start()
    h2_copy = pltpu.make_async_copy(h2_hbm, h2v, hsem.at[0])
    h2_copy.start()
    for m in range(3):
        weight_copy(be_ref[0], m).start(priority=weight_priority)
    ystage[...] = jnp.zeros_like(ystage)

    fill.wait()

    def place(p, carry):
        pair_ref[dest_ref[p]] = p
        return carry
    lax.fori_loop(0, dest_ref.shape[0], place, 0, unroll=8)

    h2_copy.wait()
    spare_fill = pltpu.make_async_copy(ystage.at[0], y_hbm.at[pl.ds(TOP_K * n_tok, rows)], hsem.at[0])
    spare_fill.start()
    spare_fill.wait()

    def first_gather(r, carry):
        gather_row(0, r, 0)
        return carry
    lax.fori_loop(0, rows, first_gather, 0)

    def block(b, carry):
        slot = b % 2
        first_of_expert = (b == 0) | (be_ref[b] != be_ref[jnp.maximum(b - 1, 0)])

        @pl.when(first_of_expert)
        def _():
            nxt = nx_ref[b]
            for m, (stage, dst) in enumerate(((wgf, wgb), (wuf, wub), (wdf, wdb))):
                weight_copy(be_ref[b], m).wait()
                dst[...] = stage[...].astype(BF16)

                @pl.when(nxt >= 0)
                def _():
                    weight_copy(nxt, m).start(priority=weight_priority)

        xb = xg[slot].reshape(rows, wgb.shape[0])
        prev_blk = jnp.maximum(b - 1, 0)
        for r in range(rows):
            row_write(prev_blk, r, 1 - slot, b == 0).start(priority=r % 2)
        for r in range(rows):
            gather_row(b + 1, r, 1 - slot)
        hg = _dot(xb, wgb[...])
        hu = _dot(xb, wub[...])
        act = (hg * _sigmoid(hg)) * hu
        y = _dot(act.astype(BF16), wdb[...]).astype(BF16)

        @pl.when(b >= 1)
        def _():
            wait_row_writes(slot)

        ystage[slot] = y.reshape(ystage.shape[1:])
        return carry

    lax.fori_loop(0, n_used, block, 0)

    last_half = (n_used - 1) % 2

    def last_writes(r, carry):
        row_write(n_used - 1, r, last_half, False).start()
        return carry
    lax.fori_loop(0, rows, last_writes, 0)
    wait_row_writes(1 - last_half)
    wait_row_writes(last_half)


def _experts(block_e, next_e, n_used, dest, h2c, wg, wu, wd):
    n_rows = block_e.shape[0] * MOE_BLOCK
    n_tok = h2c.shape[0]
    _, d, de = wg.shape
    slab = (d // LANES, LANES)
    grid_spec = pltpu.PrefetchScalarGridSpec(
        num_scalar_prefetch=4,
        grid=(1,),
        in_specs=[pl.BlockSpec(memory_space=pl.ANY)] * 5,
        out_specs=pl.BlockSpec(memory_space=pl.ANY),
        scratch_shapes=[
            pltpu.SMEM((n_rows,), jnp.int32),
            pltpu.VMEM((n_tok,) + slab, BF16),
            pltpu.VMEM((2, MOE_BLOCK) + slab, BF16),
            pltpu.VMEM((2, MOE_BLOCK) + slab, BF16),
            pltpu.VMEM((d, de), F32),
            pltpu.VMEM((d, de), F32),
            pltpu.VMEM((de, d), F32),
            pltpu.VMEM((d, de), BF16),
            pltpu.VMEM((d, de), BF16),
            pltpu.VMEM((de, d), BF16),
            pltpu.SemaphoreType.DMA((2,)),
            pltpu.SemaphoreType.DMA((3,)),
            pltpu.SemaphoreType.DMA((2,)),
        ],
    )
    padding_marks = jnp.full((n_rows,), -1, jnp.int32)
    return pl.pallas_call(
        _expert_kernel,
        grid_spec=grid_spec,
        out_shape=jax.ShapeDtypeStruct((TOP_K * n_tok + 2 * MOE_BLOCK,) + slab, BF16),
        compiler_params=_cparams(("arbitrary",), vmem=EXPERT_VMEM_LIMIT),
        name="experts",
    )(block_e, next_e, n_used, dest, padding_marks, h2c, wg, wu, wd)


def _combine_kernel(y0_ref, y1_ref, gate_ref, x1_ref, g_ref, o_ref):
    gates = gate_ref[...]
    y0 = y0_ref[...].reshape(x1_ref.shape).astype(F32)
    y1 = y1_ref[...].reshape(x1_ref.shape).astype(F32)
    x2 = x1_ref[...] + (y0 * gates[:, 0:1] + y1 * gates[:, 1:2])
    ms = jnp.mean(x2 * x2, axis=-1, keepdims=True)
    o_ref[...] = x2 * lax.rsqrt(ms + RMS_EPS) * g_ref[...]


def _combine(y2, gates, x1, g, tm):
    s, d = x1.shape
    slab = (d // LANES, LANES)
    n_steps = s // tm
    return pl.pallas_call(
        _combine_kernel,
        grid=(n_steps,),
        in_specs=[
            pl.BlockSpec((tm,) + slab, lambda i: (i, 0, 0)),
            pl.BlockSpec((tm,) + slab, lambda i: (n_steps + i, 0, 0)),
            pl.BlockSpec((tm, TOP_K), lambda i: (i, 0)),
            pl.BlockSpec((tm, d), lambda i: (i, 0)),
            pl.BlockSpec((1, d), lambda i: (0, 0)),
        ],
        out_specs=pl.BlockSpec((tm, d), lambda i: (i, 0)),
        out_shape=jax.ShapeDtypeStruct((s, d), F32),
        compiler_params=_cparams(("parallel",)),
        name="combine",
    )(y2, y2, gates, x1, g)


def _tile(n, pref):
    return pref if n % pref == 0 else n


def kernel(x, norm_mix_g, w_in, w_gla_a2, b_gla_a, gla_norm_g, attn_sinks, rel_bias_table, w_out,
           norm_ffn_g, w_router_group, b_router_group, w_router_expert, b_router_expert,
           w_expert_gate, w_expert_up, w_expert_down, norm_final_g):
    b, s, d = x.shape
    assert b == 1 and w_in.shape[0] == 1, "single batch, single layer"
    assert (s * TOP_K) % MOE_BLOCK == 0
    x2d = x.reshape(s, d)
    w_in_t = w_in[0].T

    w2 = jnp.pad(w_gla_a2[0], ((0, LANES - GLA_LOWRANK), (0, 0))).astype(BF16)
    h, log_a = _norm_loga(x2d, norm_mix_g[0].reshape(1, d), w_in_t, w2, b_gla_a[0].reshape(1, -1),
                          _tile(s, 512))
    proj = _in_proj(h, w_in_t, _tile(s, 4096), 512)

    bucket, is_prev = _folded_maps()
    bias = _rel_bias(rel_bias_table.T, jnp.asarray(bucket), jnp.asarray(is_prev))
    bias = bias.reshape(2, N_HEADS, WINDOW, WINDOW)
    o_attn = _swa(proj, attn_sinks[0], bias)
    o_gla = _gla(proj, log_a, gla_norm_g[0].reshape(1, -1), _tile(s, 256))

    wr = jnp.zeros((ROUTER_ROWS, d), F32)
    wr = wr.at[0:N_GROUPS].set(w_router_group[0].T).at[8:].set(w_router_expert[0].T).astype(BF16)
    x1, h2, logits_t = _merge_out(o_attn, o_gla, proj, x2d, w_out[0].astype(BF16),
                                  norm_ffn_g[0].reshape(1, d), wr, _tile(s, 512))

    bg = jnp.pad(b_router_group[0], (0, 8 - N_GROUPS)).reshape(8, 1)
    idx, gate, counts = _route(logits_t, bg, b_router_expert[0].reshape(N_EXPERTS, 1), _tile(s, 512))
    counts = counts[:, 0].astype(jnp.int32)
    padded = (counts + MOE_BLOCK - 1) // MOE_BLOCK * MOE_BLOCK
    pend = jnp.cumsum(padded)
    pstart = pend - padded
    expert = idx[0:TOP_K]
    eids = jnp.arange(N_EXPERTS, dtype=jnp.int32)[:, None, None]
    dest = jnp.sum(jnp.where(expert[None] == eids, pstart[:, None, None], 0), axis=0) + idx[TOP_K:2 * TOP_K]
    n_pairs = s * TOP_K
    n_blocks = (n_pairs + N_EXPERTS * (MOE_BLOCK - 1) + MOE_BLOCK - 1) // MOE_BLOCK
    n_used = (pend[-1] // MOE_BLOCK).reshape(1)
    block_start = jnp.arange(n_blocks, dtype=jnp.int32) * MOE_BLOCK
    block_e = jnp.minimum(jnp.sum((pend[None, :] <= block_start[:, None]).astype(jnp.int32), axis=1),
                          N_EXPERTS - 1)

    e_col = jnp.arange(N_EXPERTS, dtype=jnp.int32)[:, None]
    e_row = jnp.arange(N_EXPERTS, dtype=jnp.int32)[None, :]
    later = (e_row > e_col) & (counts > 0)[None, :]
    next_nonempty = jnp.min(jnp.where(later, e_row, N_EXPERTS), axis=1)
    next_nonempty = jnp.where(next_nonempty < N_EXPERTS, next_nonempty, -1)
    next_e = jnp.sum(jnp.where(block_e[:, None] == e_row, next_nonempty[None, :], 0), axis=1)
    y2 = _experts(block_e, next_e, n_used, dest.reshape(-1), h2,
                  w_expert_gate[0], w_expert_up[0], w_expert_down[0])
    out = _combine(y2, gate[0:TOP_K].T, x1, norm_final_g.reshape(1, d), _tile(s, 512))
    return out.reshape(b, s, d)
```

```python
import functools
import math

import numpy as np
import jax
import jax.numpy as jnp
from jax import lax
from jax.experimental import pallas as pl
from jax.experimental.pallas import tpu as pltpu

F32 = jnp.float32
BF16 = jnp.bfloat16

N_HEADS = 32
N_KV_HEADS = 4
HEAD_DIM = 64
WINDOW = 128
N_BUCKETS = 32
MAX_DISTANCE = 128
GLA_HEADS = 4
GLA_DK = 256
GLA_DV = 512
GLA_LOWRANK = 16
GLA_TAU = 16.0
GLA_CHUNK = 64
N_GROUPS = 4
EXPERTS_PER_GROUP = 8
N_EXPERTS = 32
TOP_K = 2
MOE_BLOCK = 128
RMS_EPS = 1e-6
NEG_INF = -1e30

COL_GA = 8704
COL_GATES = 8720
MAIN_COLS = 8704
GATE_COLS = 4096
COL_AQ = GATE_COLS + 0
COL_AK = GATE_COLS + 2048
COL_AV = GATE_COLS + 2304
COL_GQ = GATE_COLS + 2560
COL_GK = GATE_COLS + 3584
COL_GV = GATE_COLS + 4608
COL_GR = GATE_COLS + 6656

LANES = 128
VMEM_LIMIT = 56 * 1024 * 1024
EXPERT_VMEM_LIMIT = 60 * 1024 * 1024


def _cparams(sem, vmem=VMEM_LIMIT):
    return pltpu.CompilerParams(dimension_semantics=sem, vmem_limit_bytes=vmem)


def _split3(x):
    hi = x.astype(BF16)
    r1 = x - hi.astype(F32)
    mid = r1.astype(BF16)
    lo = (r1 - mid.astype(F32)).astype(BF16)
    return hi, mid, lo


def _sigmoid(x):
    return 0.5 * jnp.tanh(0.5 * x) + 0.5


def _dot(a, b):
    return jnp.dot(a, b, preferred_element_type=F32)


def _dot_nt(a, b):
    return lax.dot_general(a, b, (((1,), (1,)), ((), ())), preferred_element_type=F32)


def _dot_tn(a, b):
    return lax.dot_general(a, b, (((0,), (0,)), ((), ())), preferred_element_type=F32)


def _norm_loga_kernel(x_ref, g_ref, wga_ref, w2_ref, b_ref, h_ref, la_ref):
    x = x_ref[...]
    ms = jnp.mean(x * x, axis=-1, keepdims=True)
    hb = (x * lax.rsqrt(ms + RMS_EPS) * g_ref[...]).astype(BF16)
    h_ref[...] = hb
    row = lax.broadcasted_iota(jnp.int32, wga_ref.shape, 0)
    wga = jnp.where(row < GLA_LOWRANK, wga_ref[...], 0.0).astype(BF16)
    ga = _dot_nt(hb, wga)
    z = _dot(ga.astype(BF16), w2_ref[...]) + b_ref[...]
    la_ref[...] = (jnp.minimum(z, 0.0) - jnp.log1p(jnp.exp(-jnp.abs(z)))) * (1.0 / GLA_TAU)


def _norm_loga(x, g, wga, w2, b, tm):
    s, d = x.shape
    nq = w2.shape[1]
    return pl.pallas_call(
        _norm_loga_kernel,
        grid=(s // tm,),
        in_specs=[
            pl.BlockSpec((tm, d), lambda i: (i, 0)),
            pl.BlockSpec((1, d), lambda i: (0, 0)),
            pl.BlockSpec((LANES, d), lambda i: (COL_GA // LANES, 0)),
            pl.BlockSpec((LANES, nq), lambda i: (0, 0)),
            pl.BlockSpec((1, nq), lambda i: (0, 0)),
        ],
        out_specs=[
            pl.BlockSpec((tm, d), lambda i: (i, 0)),
            pl.BlockSpec((tm, nq), lambda i: (i, 0)),
        ],
        out_shape=[
            jax.ShapeDtypeStruct((s, d), BF16),
            jax.ShapeDtypeStruct((s, nq), F32),
        ],
        compiler_params=_cparams(("parallel",)),
        name="norm_loga",
    )(x, g, wga, w2, b)


def _in_proj_kernel(h_hbm, wt_hbm, o_ref, h_vmem, w_stage, wbf, sem, *, n_main, n_tiles, tn):
    j = pl.program_id(0)
    i = pl.program_id(1)
    tm = o_ref.shape[0]

    def w_copy(jj):
        row0 = jnp.where(jj < n_main, jj * tn, COL_GATES + (jj - n_main) * tn)
        return pltpu.make_async_copy(wt_hbm.at[pl.ds(pl.multiple_of(row0, 8), tn)], w_stage, sem.at[1])

    @pl.when((j == 0) & (i == 0))
    def _():
        h_copy = pltpu.make_async_copy(h_hbm, h_vmem, sem.at[0])
        h_copy.start()
        w_copy(0).start()
        h_copy.wait()

    @pl.when(i == 0)
    def _():
        w_copy(j).wait()
        wbf[...] = w_stage[...].astype(BF16)

        @pl.when(j + 1 < n_tiles)
        def _():
            w_copy(j + 1).start()

    h = h_vmem[pl.ds(pl.multiple_of(i * tm, tm), tm), :]
    o_ref[...] = _dot_nt(h, wbf[...]).astype(o_ref.dtype)


def _in_proj(h, w_t, tm, tn):
    s, d = h.shape
    n_main = MAIN_COLS // tn
    n_gate = GATE_COLS // tn
    n_tiles = n_main + n_gate
    kern = functools.partial(_in_proj_kernel, n_main=n_main, n_tiles=n_tiles, tn=tn)
    return pl.pallas_call(
        kern,
        grid=(n_tiles, s // tm),
        in_specs=[pl.BlockSpec(memory_space=pl.ANY), pl.BlockSpec(memory_space=pl.ANY)],
        out_specs=pl.BlockSpec((tm, tn), lambda j, i: (i, (j + n_gate) % n_tiles)),
        out_shape=jax.ShapeDtypeStruct((s, n_tiles * tn), BF16),
        scratch_shapes=[
            pltpu.VMEM((s, d), BF16),
            pltpu.VMEM((tn, d), F32),
            pltpu.VMEM((tn, d), BF16),
            pltpu.SemaphoreType.DMA((2,)),
        ],
        compiler_params=_cparams(("arbitrary", "arbitrary")),
        name="in_proj",
    )(h, w_t)


def _folded_maps():
    j = np.arange(WINDOW)[:, None]
    i = np.arange(WINDOW)[None, :]
    n = (i - j) % WINDOW
    max_exact = N_BUCKETS // 2
    ratio = np.maximum(n, max_exact).astype(np.float32) / np.float32(max_exact)
    large = max_exact + (np.log(ratio) / np.float32(math.log(MAX_DISTANCE / max_exact))
                         * (N_BUCKETS - max_exact)).astype(np.int32)
    large = np.minimum(large, N_BUCKETS - 1)
    bucket = np.where(n < max_exact, n, large).astype(np.int32).reshape(1, -1)
    is_prev = (j > i).astype(np.int32).reshape(1, -1)
    return bucket, is_prev


def _rel_bias_kernel(tab_ref, bucket_ref, prev_ref, o_ref):
    nb = tab_ref.shape[1]
    width = bucket_ref.shape[1]
    onehot = (lax.broadcasted_iota(jnp.int32, (nb, width), 0) == bucket_ref[...]).astype(BF16)
    hi, mid, lo = _split3(tab_ref[...])
    bias = _dot(hi, onehot) + _dot(mid, onehot) + _dot(lo, onehot)
    tile = o_ref.shape[1:]
    o_ref[0] = bias.reshape(tile)
    o_ref[1] = jnp.where(prev_ref[...] > 0, NEG_INF, bias).reshape(tile)


def _rel_bias(table_t, bucket, is_prev):
    nh = table_t.shape[0]
    width = bucket.shape[1]
    return pl.pallas_call(
        _rel_bias_kernel,
        out_shape=jax.ShapeDtypeStruct((2, nh, WINDOW, WINDOW), F32),
        compiler_params=pltpu.CompilerParams(vmem_limit_bytes=VMEM_LIMIT),
        name="rel_bias",
    )(table_t, bucket, is_prev)


SWA_SCORES_AHEAD = 10
SWA_BLOCKS_PER_STEP = 4


def _swa_kernel(sink_ref, q_ref, kc_ref, kp_ref, vc_ref, vp_ref, bias0_ref, bias_ref, o_ref):
    w = WINDOW
    hd = HEAD_DIM
    gq = N_HEADS // N_KV_HEADS
    n_sub = q_ref.shape[0] // w
    is_prev = (lax.broadcasted_iota(jnp.int32, (w, w), 0) > lax.broadcasted_iota(jnp.int32, (w, w), 1))
    zeros = jnp.zeros((2 * w, hd), BF16)
    scale = HEAD_DIM ** -0.5
    k_pl, v_pl = {}, {}
    for sub in range(n_sub):
        for g in range(N_KV_HEADS):
            cs = slice(g * hd, (g + 1) * hd)
            if sub == 0:
                k_g = jnp.concatenate([kp_ref[:, cs], kc_ref[0:w, cs]], axis=0) * scale
                v_g = jnp.concatenate([vp_ref[:, cs], vc_ref[0:w, cs]], axis=0)
            else:
                k_g = kc_ref[(sub - 1) * w:(sub + 1) * w, cs] * scale
                v_g = vc_ref[(sub - 1) * w:(sub + 1) * w, cs]
            k_pl[sub, g] = (jnp.concatenate([k_g, zeros], axis=1), jnp.concatenate([zeros, k_g], axis=1))
            v_pl[sub, g] = (jnp.concatenate([v_g, zeros], axis=1), jnp.concatenate([zeros, v_g], axis=1))

    def scores(sub, hh):
        pair = hh - hh % 2
        return _dot_nt(k_pl[sub, hh // gq][hh % 2], q_ref[sub * w:(sub + 1) * w, pair * hd:(pair + 2) * hd])

    def attend(sub, hh, st2):
        bias = bias0_ref if sub == 0 else bias_ref
        st = jnp.where(is_prev, st2[:w], st2[w:]) + bias[hh]
        sink = sink_ref[hh]
        m = jnp.maximum(jnp.max(st, axis=0, keepdims=True), sink)
        p = jnp.exp(st - m)
        recip = 1.0 / (jnp.sum(p, axis=0, keepdims=True) + jnp.exp(sink - m))
        probs = p * recip
        p_cat = jnp.concatenate([jnp.where(is_prev, probs, 0.0), jnp.where(is_prev, 0.0, probs)], axis=0)
        return _dot_tn(p_cat.astype(BF16), v_pl[sub, hh // gq][hh % 2])

    chains = [(sub, hh) for sub in range(n_sub) for hh in range(N_HEADS)]
    ahead = SWA_SCORES_AHEAD
    st2 = {c: scores(*c) for c in chains[:ahead]}
    acc = None
    for n, (sub, hh) in enumerate(chains):
        if n + ahead < len(chains):
            st2[chains[n + ahead]] = scores(*chains[n + ahead])
        part = attend(sub, hh, st2.pop((sub, hh)))
        if hh % 2 == 0:
            acc = part
        else:
            o_ref[sub * w:(sub + 1) * w, (hh - 1) * hd:(hh + 1) * hd] = (acc + part).astype(o_ref.dtype)


def _swa(proj, sinks, bias):
    s = proj.shape[0]
    w = WINDOW
    nb = s // w
    n_sub = SWA_BLOCKS_PER_STEP if nb % SWA_BLOCKS_PER_STEP == 0 else 1
    kvw = N_KV_HEADS * HEAD_DIM
    dq = N_HEADS * HEAD_DIM
    rows = n_sub * w
    prev = lambda n: jnp.maximum(n * n_sub - 1, 0)
    return pl.pallas_call(
        _swa_kernel,
        grid=(nb // n_sub,),
        in_specs=[
            pl.BlockSpec(memory_space=pltpu.SMEM),
            pl.BlockSpec((rows, dq), lambda n: (n, COL_AQ // dq)),
            pl.BlockSpec((rows, kvw), lambda n: (n, COL_AK // kvw)),
            pl.BlockSpec((w, kvw), lambda n: (prev(n), COL_AK // kvw)),
            pl.BlockSpec((rows, kvw), lambda n: (n, COL_AV // kvw)),
            pl.BlockSpec((w, kvw), lambda n: (prev(n), COL_AV // kvw)),
            pl.BlockSpec((None, N_HEADS, w, w), lambda n: (jnp.where(n == 0, 1, 0), 0, 0, 0)),
            pl.BlockSpec((None, N_HEADS, w, w), lambda n: (0, 0, 0, 0)),
        ],
        out_specs=pl.BlockSpec((rows, dq), lambda n: (n, 0)),
        out_shape=jax.ShapeDtypeStruct((s, dq), BF16),
        compiler_params=_cparams(("parallel",)),
        name="swa",
    )(sinks, proj, proj, proj, proj, proj, bias, bias)


def _gla_kernel(*refs):
    nh = GLA_HEADS
    q_refs, k_refs, v_refs, gr_refs = (refs[i * nh:(i + 1) * nh] for i in range(4))
    la_ref, gn_ref, o_ref, state_ref = refs[4 * nh:]
    c = GLA_CHUNK
    rows, dk = q_refs[0].shape
    dv = v_refs[0].shape[1]

    @pl.when(pl.program_id(0) == 0)
    def _():
        state_ref[...] = jnp.zeros_like(state_ref)

    la_parts = _split3(la_ref[...])
    ri = lax.broadcasted_iota(jnp.int32, (rows, rows), 0)
    ci = lax.broadcasted_iota(jnp.int32, (rows, rows), 1)
    tri = ((ri // c == ci // c) & (ri >= ci)).astype(BF16)
    cum_all = _dot(tri, la_parts[0]) + _dot(tri, la_parts[1]) + _dot(tri, la_parts[2])
    causal = (lax.broadcasted_iota(jnp.int32, (c, c), 0) >= lax.broadcasted_iota(jnp.int32, (c, c), 1))
    gn = gn_ref[...]
    n_chunks = rows // c

    o_intra, q_in, upd, decay = {}, {}, {}, {}
    for h in range(nh):
        for ch in range(n_chunks):
            sl = slice(ch * c, (ch + 1) * c)
            cum_c = cum_all[sl, h * dk:(h + 1) * dk]
            ref = cum_c[c // 2 - 1:c // 2]
            last = cum_c[c - 1:c]
            q = q_refs[h][sl, :].astype(F32) * (dk ** -0.5)
            k = k_refs[h][sl, :].astype(F32)
            v = v_refs[h][sl, :]
            q_intra = (q * jnp.exp(cum_c - ref)).astype(BF16)
            k_intra = (k * jnp.exp(ref - cum_c)).astype(BF16)
            a = jnp.where(causal, _dot_nt(q_intra, k_intra), 0.0)
            o_intra[h, ch] = _dot(a.astype(BF16), v)
            q_in[h, ch] = (q * jnp.exp(cum_c)).astype(BF16)
            k_dec = (k * jnp.exp(last - cum_c)).astype(BF16)
            upd[h, ch] = _dot_tn(k_dec, v)
            decay[h, ch] = jnp.broadcast_to(jnp.exp(last), (LANES, dk)).T

    state = [state_ref[h] for h in range(nh)]
    for ch in range(n_chunks):
        sl = slice(ch * c, (ch + 1) * c)
        for h in range(nh):
            o = o_intra[h, ch] + _dot(q_in[h, ch], state[h].astype(BF16))
            state[h] = jnp.concatenate(
                [state[h][:, jv * LANES:(jv + 1) * LANES] * decay[h, ch]
                 + upd[h, ch][:, jv * LANES:(jv + 1) * LANES] for jv in range(dv // LANES)], axis=1)
            ms = jnp.mean(o * o, axis=-1, keepdims=True)
            on = o * lax.rsqrt(ms + RMS_EPS) * gn
            gr = gr_refs[h][sl, :].astype(F32)
            o_ref[sl, h * dv:(h + 1) * dv] = (on * (gr * _sigmoid(gr))).astype(o_ref.dtype)
    for h in range(nh):
        state_ref[h] = state[h]


def _gla(proj, la, gn, rows):
    s = proj.shape[0]
    dk, dv = GLA_DK, GLA_DV
    nh = GLA_HEADS

    def head_specs(width, col0):
        return [pl.BlockSpec((rows, width), functools.partial(lambda i, h: (i, col0 // width + h), h=h))
                for h in range(nh)]

    in_specs = (head_specs(dk, COL_GQ) + head_specs(dk, COL_GK) + head_specs(dv, COL_GV)
                + head_specs(dv, COL_GR)
                + [pl.BlockSpec((rows, nh * dk), lambda i: (i, 0)), pl.BlockSpec((1, dv), lambda i: (0, 0))])
    return pl.pallas_call(
        _gla_kernel,
        grid=(s // rows,),
        in_specs=in_specs,
        out_specs=pl.BlockSpec((rows, nh * dv), lambda i: (i, 0)),
        out_shape=jax.ShapeDtypeStruct((s, nh * dv), BF16),
        scratch_shapes=[pltpu.VMEM((nh, dk, dv), F32)],
        compiler_params=_cparams(("arbitrary",)),
        name="gla",
    )(*([proj] * (4 * nh)), la, gn)


MERGE_SUB_ROWS = 128


def _merge_out_kernel(oa_ref, og_ref, ga_ref, gg_ref, x_ref, wo_ref, g_ref, wr_ref,
                      x1_ref, h2_ref, lt_ref):
    sub = MERGE_SUB_ROWS
    tiles = [slice(t * sub, (t + 1) * sub) for t in range(x_ref.shape[0] // sub)]
    merged = [(_sigmoid(ga_ref[rs, :].astype(F32)) * oa_ref[rs, :].astype(F32)
               + _sigmoid(gg_ref[rs, :].astype(F32)) * og_ref[rs, :].astype(F32)).astype(BF16) for rs in tiles]
    projected = [_dot(m, wo_ref[...]) for m in merged]
    for rs, y in zip(tiles, projected):
        x1 = x_ref[rs, :] + y
        x1_ref[rs, :] = x1
        ms = jnp.mean(x1 * x1, axis=-1, keepdims=True)
        h2b = (x1 * lax.rsqrt(ms + RMS_EPS) * g_ref[...]).astype(BF16)
        h2_ref[rs] = h2b.reshape((sub,) + h2_ref.shape[1:])
        lt_ref[:, rs] = _dot_nt(wr_ref[...], h2b)


def _merge_out(o_attn, o_gla, gates, x, wo, g, wr, tm):
    s, d = x.shape
    nr = wr.shape[0]
    row = lambda i: (i, 0)
    return pl.pallas_call(
        _merge_out_kernel,
        grid=(s // tm,),
        in_specs=[
            pl.BlockSpec((tm, d), row),
            pl.BlockSpec((tm, d), row),
            pl.BlockSpec((tm, d), lambda i: (i, 0)),
            pl.BlockSpec((tm, d), lambda i: (i, 1)),
            pl.BlockSpec((tm, d), row),
            pl.BlockSpec((d, d), lambda i: (0, 0), pipeline_mode=pl.Buffered(1)),
            pl.BlockSpec((1, d), lambda i: (0, 0)),
            pl.BlockSpec((nr, d), lambda i: (0, 0)),
        ],
        out_specs=[
            pl.BlockSpec((tm, d), row),
            pl.BlockSpec((tm, d // LANES, LANES), lambda i: (i, 0, 0)),
            pl.BlockSpec((nr, tm), lambda i: (0, i)),
        ],
        out_shape=[
            jax.ShapeDtypeStruct((s, d), F32),
            jax.ShapeDtypeStruct((s, d // LANES, LANES), BF16),
            jax.ShapeDtypeStruct((nr, s), F32),
        ],
        compiler_params=_cparams(("parallel",)),
        name="merge_out",
    )(o_attn, o_gla, gates, gates, x, wo, g, wr)


ROUTER_ROWS = 8 + N_EXPERTS


def _route_kernel(lt_ref, bg_ref, be_ref, idx_ref, gate_ref, cnt_ref, carry_ref):
    tb = lt_ref.shape[1]
    epg = EXPERTS_PER_GROUP

    @pl.when(pl.program_id(0) == 0)
    def _():
        carry_ref[...] = jnp.zeros_like(carry_ref)

    logits = lt_ref[...]
    gl = logits[0:N_GROUPS]
    g_exp = jnp.exp(gl - jnp.max(gl, axis=0, keepdims=True))
    g_prob = g_exp / jnp.sum(g_exp, axis=0, keepdims=True)
    gb = gl + bg_ref[0:N_GROUPS]
    rowg = lax.broadcasted_iota(jnp.int32, (N_GROUPS, tb), 0)
    g_idx = jnp.min(jnp.where(gb == jnp.max(gb, axis=0, keepdims=True), rowg, N_GROUPS),
                    axis=0, keepdims=True)
    p_group = jnp.sum(jnp.where(rowg == g_idx, g_prob, 0.0), axis=0, keepdims=True)

    el = logits[8:8 + epg]
    eb = jnp.broadcast_to(be_ref[0:epg], (epg, tb))
    for g in range(1, N_GROUPS):
        pick = g_idx == g
        el = jnp.where(pick, logits[8 + g * epg:8 + (g + 1) * epg], el)
        eb = jnp.where(pick, be_ref[g * epg:(g + 1) * epg], eb)
    e_exp = jnp.exp(el - jnp.max(el, axis=0, keepdims=True))
    e_prob = e_exp / jnp.sum(e_exp, axis=0, keepdims=True)
    score = el + eb
    rowe = lax.broadcasted_iota(jnp.int32, (epg, tb), 0)
    i1 = jnp.min(jnp.where(score == jnp.max(score, axis=0, keepdims=True), rowe, epg),
                 axis=0, keepdims=True)
    score2 = jnp.where(rowe == i1, -jnp.inf, score)
    i2 = jnp.min(jnp.where(score2 == jnp.max(score2, axis=0, keepdims=True), rowe, epg),
                 axis=0, keepdims=True)
    q1 = jnp.sum(jnp.where(rowe == i1, e_prob, 0.0), axis=0, keepdims=True)
    q2 = jnp.sum(jnp.where(rowe == i2, e_prob, 0.0), axis=0, keepdims=True)
    qs = q1 + q2
    gate1 = p_group * q1 / qs
    gate2 = p_group * q2 / qs
    e1 = g_idx * epg + i1
    e2 = g_idx * epg + i2

    rowx = lax.broadcasted_iota(jnp.int32, (N_EXPERTS, tb), 0)
    hit1 = rowx == e1
    hit2 = rowx == e2
    member = (hit1 | hit2).astype(BF16)
    before = (lax.broadcasted_iota(jnp.int32, (tb, tb), 0)
              < lax.broadcasted_iota(jnp.int32, (tb, tb), 1)).astype(BF16)
    carry = carry_ref[...]
    count = _dot(member, before) + carry[:, 0:1]
    r1 = jnp.sum(jnp.where(hit1, count, 0.0), axis=0, keepdims=True).astype(jnp.int32)
    r2 = jnp.sum(jnp.where(hit2, count, 0.0), axis=0, keepdims=True).astype(jnp.int32)
    carry = carry + jnp.sum(member.astype(F32), axis=1, keepdims=True)
    carry_ref[...] = carry
    cnt_ref[...] = carry

    row8 = lax.broadcasted_iota(jnp.int32, (8, tb), 0)
    idx_ref[...] = jnp.where(row8 == 0, e1, jnp.where(row8 == 1, e2,
                             jnp.where(row8 == 2, r1, jnp.where(row8 == 3, r2, 0))))
    gate_ref[...] = jnp.where(row8 == 0, gate1, jnp.where(row8 == 1, gate2, 0.0))


def _route(lt, bg, be, tb):
    nr, s = lt.shape
    return pl.pallas_call(
        _route_kernel,
        grid=(s // tb,),
        in_specs=[
            pl.BlockSpec((nr, tb), lambda i: (0, i)),
            pl.BlockSpec((8, 1), lambda i: (0, 0)),
            pl.BlockSpec((N_EXPERTS, 1), lambda i: (0, 0)),
        ],
        out_specs=[
            pl.BlockSpec((8, tb), lambda i: (0, i)),
            pl.BlockSpec((8, tb), lambda i: (0, i)),
            pl.BlockSpec((N_EXPERTS, LANES), lambda i: (0, 0)),
        ],
        out_shape=[
            jax.ShapeDtypeStruct((8, s), jnp.int32),
            jax.ShapeDtypeStruct((8, s), F32),
            jax.ShapeDtypeStruct((N_EXPERTS, LANES), F32),
        ],
        scratch_shapes=[pltpu.VMEM((N_EXPERTS, LANES), F32)],
        compiler_params=_cparams(("arbitrary",)),
        name="route",
    )(lt, bg, be)


def _expert_kernel(be_ref, nx_ref, nu_ref, dest_ref, neg_hbm, h2_hbm, wg_hbm, wu_hbm, wd_hbm, y_hbm,
                   pair_ref, h2v, xg, ystage, wgf, wuf, wdf, wgb, wub, wdb, hsem, wsem, ysem):
    n_used = nu_ref[0]
    rows = MOE_BLOCK
    n_tok = h2v.shape[0]

    def weight_copy(e, m):
        src, stage = ((wg_hbm, wgf), (wu_hbm, wuf), (wd_hbm, wdf))[m]
        return pltpu.make_async_copy(src.at[e], stage, wsem.at[m])

    def gather_row(blk, r, half):
        pair = jnp.maximum(pair_ref[blk * rows + r], 0)
        xg[half, r] = h2v[jnp.where(pair >= n_tok, pair - n_tok, pair)]

    def row_write(blk, r, half, dummy):
        pair = pair_ref[blk * rows + r]
        dst = jnp.where((pair < 0) | dummy, TOP_K * n_tok + half * rows + r, pair)
        return pltpu.make_async_copy(ystage.at[half, r], y_hbm.at[dst], ysem.at[half])

    def wait_row_writes(half):
        pltpu.make_async_copy(ystage.at[half], y_hbm.at[pl.ds(0, rows)], ysem.at[half]).wait()

    weight_priority = 1

    fill = pltpu.make_async_copy(neg_hbm, pair_ref, hsem.at[1])
    fill.start()
    h2_copy = pltpu.make_async_copy(h2_hbm, h2v, hsem.at[0])
    h2_copy.start()
    for m in range(3):
        weight_copy(be_ref[0], m).start(priority=weight_priority)
    ystage[...] = jnp.zeros_like(ystage)

    fill.wait()

    def place(p, carry):
        pair_ref[dest_ref[p]] = p
        return carry
    lax.fori_loop(0, dest_ref.shape[0], place, 0, unroll=8)

    h2_copy.wait()
    spare_fill = pltpu.make_async_copy(ystage.at[0], y_hbm.at[pl.ds(TOP_K * n_tok, rows)], hsem.at[0])
    spare_fill.start()
    spare_fill.wait()

    def first_gather(r, carry):
        gather_row(0, r, 0)
        return carry
    lax.fori_loop(0, rows, first_gather, 0)

    def block(b, carry):
        slot = b % 2
        first_of_expert = (b == 0) | (be_ref[b] != be_ref[jnp.maximum(b - 1, 0)])

        @pl.when(first_of_expert)
        def _():
            nxt = nx_ref[b]
            for m, (stage, dst) in enumerate(((wgf, wgb), (wuf, wub), (wdf, wdb))):
                weight_copy(be_ref[b], m).wait()
                dst[...] = stage[...].astype(BF16)

                @pl.when(nxt >= 0)
                def _():
                    weight_copy(nxt, m).start(priority=weight_priority)

        xb = xg[slot].reshape(rows, wgb.shape[0])
        prev_blk = jnp.maximum(b - 1, 0)
        for r in range(rows):
            row_write(prev_blk, r, 1 - slot, b == 0).start(priority=r % 2)
        for r in range(rows):
            gather_row(b + 1, r, 1 - slot)
        hg = _dot(xb, wgb[...])
        hu = _dot(xb, wub[...])
        act = (hg * _sigmoid(hg)) * hu
        y = _dot(act.astype(BF16), wdb[...]).astype(BF16)

        @pl.when(b >= 1)
        def _():
            wait_row_writes(slot)

        ystage[slot] = y.reshape(ystage.shape[1:])
        return carry

    lax.fori_loop(0, n_used, block, 0)

    last_half = (n_used - 1) % 2

    def last_writes(r, carry):
        row_write(n_used - 1, r, last_half, False).start()
        return carry
    lax.fori_loop(0, rows, last_writes, 0)
    wait_row_writes(1 - last_half)
    wait_row_writes(last_half)


def _experts(block_e, next_e, n_used, dest, h2c, wg, wu, wd):
    n_rows = block_e.shape[0] * MOE_BLOCK
    n_tok = h2c.shape[0]
    _, d, de = wg.shape
    slab = (d // LANES, LANES)
    grid_spec = pltpu.PrefetchScalarGridSpec(
        num_scalar_prefetch=4,
        grid=(1,),
        in_specs=[pl.BlockSpec(memory_space=pl.ANY)] * 5,
        out_specs=pl.BlockSpec(memory_space=pl.ANY),
        scratch_shapes=[
            pltpu.SMEM((n_rows,), jnp.int32),
            pltpu.VMEM((n_tok,) + slab, BF16),
            pltpu.VMEM((2, MOE_BLOCK) + slab, BF16),
            pltpu.VMEM((2, MOE_BLOCK) + slab, BF16),
            pltpu.VMEM((d, de), F32),
            pltpu.VMEM((d, de), F32),
            pltpu.VMEM((de, d), F32),
            pltpu.VMEM((d, de), BF16),
            pltpu.VMEM((d, de), BF16),
            pltpu.VMEM((de, d), BF16),
            pltpu.SemaphoreType.DMA((2,)),
            pltpu.SemaphoreType.DMA((3,)),
            pltpu.SemaphoreType.DMA((2,)),
        ],
    )
    padding_marks = jnp.full((n_rows,), -1, jnp.int32)
    return pl.pallas_call(
        _expert_kernel,
        grid_spec=grid_spec,
        out_shape=jax.ShapeDtypeStruct((TOP_K * n_tok + 2 * MOE_BLOCK,) + slab, BF16),
        compiler_params=_cparams(("arbitrary",), vmem=EXPERT_VMEM_LIMIT),
        name="experts",
    )(block_e, next_e, n_used, dest, padding_marks, h2c, wg, wu, wd)


def _combine_kernel(y0_ref, y1_ref, gate_ref, x1_ref, g_ref, o_ref):
    gates = gate_ref[...]
    y0 = y0_ref[...].reshape(x1_ref.shape).astype(F32)
    y1 = y1_ref[...].reshape(x1_ref.shape).astype(F32)
    x2 = x1_ref[...] + (y0 * gates[:, 0:1] + y1 * gates[:, 1:2])
    ms = jnp.mean(x2 * x2, axis=-1, keepdims=True)
    o_ref[...] = x2 * lax.rsqrt(ms + RMS_EPS) * g_ref[...]


def _combine(y2, gates, x1, g, tm):
    s, d = x1.shape
    slab = (d // LANES, LANES)
    n_steps = s // tm
    return pl.pallas_call(
        _combine_kernel,
        grid=(n_steps,),
        in_specs=[
            pl.BlockSpec((tm,) + slab, lambda i: (i, 0, 0)),
            pl.BlockSpec((tm,) + slab, lambda i: (n_steps + i, 0, 0)),
            pl.BlockSpec((tm, TOP_K), lambda i: (i, 0)),
            pl.BlockSpec((tm, d), lambda i: (i, 0)),
            pl.BlockSpec((1, d), lambda i: (0, 0)),
        ],
        out_specs=pl.BlockSpec((tm, d), lambda i: (i, 0)),
        out_shape=jax.ShapeDtypeStruct((s, d), F32),
        compiler_params=_cparams(("parallel",)),
        name="combine",
    )(y2, y2, gates, x1, g)


def _tile(n, pref):
    return pref if n % pref == 0 else n


def kernel(x, norm_mix_g, w_in, w_gla_a2, b_gla_a, gla_norm_g, attn_sinks, rel_bias_table, w_out,
           norm_ffn_g, w_router_group, b_router_group, w_router_expert, b_router_expert,
           w_expert_gate, w_expert_up, w_expert_down, norm_final_g):
    b, s, d = x.shape
    assert b == 1 and w_in.shape[0] == 1, "single batch, single layer"
    assert (s * TOP_K) % MOE_BLOCK == 0
    x2d = x.reshape(s, d)
    w_in_t = w_in[0].T

    w2 = jnp.pad(w_gla_a2[0], ((0, LANES - GLA_LOWRANK), (0, 0))).astype(BF16)
    h, log_a = _norm_loga(x2d, norm_mix_g[0].reshape(1, d), w_in_t, w2, b_gla_a[0].reshape(1, -1),
                          _tile(s, 512))
    proj = _in_proj(h, w_in_t, _tile(s, 4096), 512)

    bucket, is_prev = _folded_maps()
    bias = _rel_bias(rel_bias_table.T, jnp.asarray(bucket), jnp.asarray(is_prev))
    o_attn = _swa(proj, attn_sinks[0], bias)
    o_gla = _gla(proj, log_a, gla_norm_g[0].reshape(1, -1), _tile(s, 256))

    wr = jnp.zeros((ROUTER_ROWS, d), F32)
    wr = wr.at[0:N_GROUPS].set(w_router_group[0].T).at[8:].set(w_router_expert[0].T).astype(BF16)
    x1, h2, logits_t = _merge_out(o_attn, o_gla, proj, x2d, w_out[0].astype(BF16),
                                  norm_ffn_g[0].reshape(1, d), wr, _tile(s, 512))

    bg = jnp.pad(b_router_group[0], (0, 8 - N_GROUPS)).reshape(8, 1)
    idx, gate, counts = _route(logits_t, bg, b_router_expert[0].reshape(N_EXPERTS, 1), _tile(s, 512))
    counts = counts[:, 0].astype(jnp.int32)
    padded = (counts + MOE_BLOCK - 1) // MOE_BLOCK * MOE_BLOCK
    pend = jnp.cumsum(padded)
    pstart = pend - padded
    expert = idx[0:TOP_K]
    eids = jnp.arange(N_EXPERTS, dtype=jnp.int32)[:, None, None]
    dest = jnp.sum(jnp.where(expert[None] == eids, pstart[:, None, None], 0), axis=0) + idx[TOP_K:2 * TOP_K]
    n_pairs = s * TOP_K
    n_blocks = (n_pairs + N_EXPERTS * (MOE_BLOCK - 1) + MOE_BLOCK - 1) // MOE_BLOCK
    n_used = (pend[-1] // MOE_BLOCK).reshape(1)
    block_start = jnp.arange(n_blocks, dtype=jnp.int32) * MOE_BLOCK
    block_e = jnp.minimum(jnp.sum((pend[None, :] <= block_start[:, None]).astype(jnp.int32), axis=1),
                          N_EXPERTS - 1)

    e_col = jnp.arange(N_EXPERTS, dtype=jnp.int32)[:, None]
    e_row = jnp.arange(N_EXPERTS, dtype=jnp.int32)[None, :]
    later = (e_row > e_col) & (counts > 0)[None, :]
    next_nonempty = jnp.min(jnp.where(later, e_row, N_EXPERTS), axis=1)
    next_nonempty = jnp.where(next_nonempty < N_EXPERTS, next_nonempty, -1)
    next_e = jnp.sum(jnp.where(block_e[:, None] == e_row, next_nonempty[None, :], 0), axis=1)
    y2 = _experts(block_e, next_e, n_used, dest.reshape(-1), h2,
                  w_expert_gate[0], w_expert_up[0], w_expert_down[0])
    out = _combine(y2, gate[0:TOP_K].T, x1, norm_final_g.reshape(1, d), _tile(s, 512))
    return out.reshape(b, s, d)
```

```python
import functools
import math

import numpy as np
import jax
import jax.numpy as jnp
from jax import lax
from jax.experimental import pallas as pl
from jax.experimental.pallas import tpu as pltpu

F32 = jnp.float32
BF16 = jnp.bfloat16

N_HEADS = 32
N_KV_HEADS = 4
HEAD_DIM = 64
WINDOW = 128
N_BUCKETS = 32
MAX_DISTANCE = 128
GLA_HEADS = 4
GLA_DK = 256
GLA_DV = 512
GLA_LOWRANK = 16
GLA_TAU = 16.0
GLA_CHUNK = 64
N_GROUPS = 4
EXPERTS_PER_GROUP = 8
N_EXPERTS = 32
TOP_K = 2
MOE_BLOCK = 128
RMS_EPS = 1e-6
NEG_INF = -1e30

COL_GA = 8704
COL_GATES = 8720
MAIN_COLS = 8704
GATE_COLS = 4096
COL_AQ = GATE_COLS + 0
COL_AK = GATE_COLS + 2048
COL_AV = GATE_COLS + 2304
COL_GQ = GATE_COLS + 2560
COL_GK = GATE_COLS + 3584
COL_GV = GATE_COLS + 4608
COL_GR = GATE_COLS + 6656

LANES = 128
VMEM_LIMIT = 56 * 1024 * 1024
EXPERT_VMEM_LIMIT = 60 * 1024 * 1024


def _cparams(sem, vmem=VMEM_LIMIT):
    return pltpu.CompilerParams(dimension_semantics=sem, vmem_limit_bytes=vmem)


def _split3(x):
    hi = x.astype(BF16)
    r1 = x - hi.astype(F32)
    mid = r1.astype(BF16)
    lo = (r1 - mid.astype(F32)).astype(BF16)
    return hi, mid, lo


def _sigmoid(x):
    return 0.5 * jnp.tanh(0.5 * x) + 0.5


def _dot(a, b):
    return jnp.dot(a, b, preferred_element_type=F32)


def _dot_nt(a, b):
    return lax.dot_general(a, b, (((1,), (1,)), ((), ())), preferred_element_type=F32)


def _dot_tn(a, b):
    return lax.dot_general(a, b, (((0,), (0,)), ((), ())), preferred_element_type=F32)


def _norm_loga_kernel(x_ref, g_ref, wga_ref, w2_ref, b_ref, h_ref, la_ref):
    x = x_ref[...]
    ms = jnp.mean(x * x, axis=-1, keepdims=True)
    hb = (x * lax.rsqrt(ms + RMS_EPS) * g_ref[...]).astype(BF16)
    h_ref[...] = hb
    row = lax.broadcasted_iota(jnp.int32, wga_ref.shape, 0)
    wga = jnp.where(row < GLA_LOWRANK, wga_ref[...], 0.0).astype(BF16)
    ga = _dot_nt(hb, wga)
    z = _dot(ga.astype(BF16), w2_ref[...]) + b_ref[...]
    la_ref[...] = (jnp.minimum(z, 0.0) - jnp.log1p(jnp.exp(-jnp.abs(z)))) * (1.0 / GLA_TAU)


def _norm_loga(x, g, wga, w2, b, tm):
    s, d = x.shape
    nq = w2.shape[1]
    return pl.pallas_call(
        _norm_loga_kernel,
        grid=(s // tm,),
        in_specs=[
            pl.BlockSpec((tm, d), lambda i: (i, 0)),
            pl.BlockSpec((1, d), lambda i: (0, 0)),
            pl.BlockSpec((LANES, d), lambda i: (COL_GA // LANES, 0)),
            pl.BlockSpec((LANES, nq), lambda i: (0, 0)),
            pl.BlockSpec((1, nq), lambda i: (0, 0)),
        ],
        out_specs=[
            pl.BlockSpec((tm, d), lambda i: (i, 0)),
            pl.BlockSpec((tm, nq), lambda i: (i, 0)),
        ],
        out_shape=[
            jax.ShapeDtypeStruct((s, d), BF16),
            jax.ShapeDtypeStruct((s, nq), F32),
        ],
        compiler_params=_cparams(("parallel",)),
        name="norm_loga",
    )(x, g, wga, w2, b)


def _in_proj_kernel(h_hbm, wt_hbm, o_ref, h_vmem, w_stage, wbf, sem, *, n_main, n_tiles, tn):
    j = pl.program_id(0)
    i = pl.program_id(1)
    tm = o_ref.shape[0]

    def w_copy(jj):
        row0 = jnp.where(jj < n_main, jj * tn, COL_GATES + (jj - n_main) * tn)
        return pltpu.make_async_copy(wt_hbm.at[pl.ds(pl.multiple_of(row0, 8), tn)], w_stage, sem.at[1])

    @pl.when((j == 0) & (i == 0))
    def _():
        h_copy = pltpu.make_async_copy(h_hbm, h_vmem, sem.at[0])
        h_copy.start()
        w_copy(0).start()
        h_copy.wait()

    @pl.when(i == 0)
    def _():
        w_copy(j).wait()
        wbf[...] = w_stage[...].astype(BF16)

        @pl.when(j + 1 < n_tiles)
        def _():
            w_copy(j + 1).start()

    h = h_vmem[pl.ds(pl.multiple_of(i * tm, tm), tm), :]
    o_ref[...] = _dot_nt(h, wbf[...]).astype(o_ref.dtype)


def _in_proj(h, w_t, tm, tn):
    s, d = h.shape
    n_main = MAIN_COLS // tn
    n_gate = GATE_COLS // tn
    n_tiles = n_main + n_gate
    kern = functools.partial(_in_proj_kernel, n_main=n_main, n_tiles=n_tiles, tn=tn)
    return pl.pallas_call(
        kern,
        grid=(n_tiles, s // tm),
        in_specs=[pl.BlockSpec(memory_space=pl.ANY), pl.BlockSpec(memory_space=pl.ANY)],
        out_specs=pl.BlockSpec((tm, tn), lambda j, i: (i, (j + n_gate) % n_tiles)),
        out_shape=jax.ShapeDtypeStruct((s, n_tiles * tn), BF16),
        scratch_shapes=[
            pltpu.VMEM((s, d), BF16),
            pltpu.VMEM((tn, d), F32),
            pltpu.VMEM((tn, d), BF16),
            pltpu.SemaphoreType.DMA((2,)),
        ],
        compiler_params=_cparams(("arbitrary", "arbitrary")),
        name="in_proj",
    )(h, w_t)


def _folded_maps():
    j = np.arange(WINDOW)[:, None]
    i = np.arange(WINDOW)[None, :]
    n = (i - j) % WINDOW
    max_exact = N_BUCKETS // 2
    ratio = np.maximum(n, max_exact).astype(np.float32) / np.float32(max_exact)
    large = max_exact + (np.log(ratio) / np.float32(math.log(MAX_DISTANCE / max_exact))
                         * (N_BUCKETS - max_exact)).astype(np.int32)
    large = np.minimum(large, N_BUCKETS - 1)
    bucket = np.where(n < max_exact, n, large).astype(np.int32).reshape(1, -1)
    is_prev = (j > i).astype(np.int32).reshape(1, -1)
    return bucket, is_prev


def _rel_bias_kernel(tab_ref, bucket_ref, prev_ref, o_ref):
    nb = tab_ref.shape[1]
    width = bucket_ref.shape[1]
    onehot = (lax.broadcasted_iota(jnp.int32, (nb, width), 0) == bucket_ref[...]).astype(BF16)
    hi, mid, lo = _split3(tab_ref[...])
    bias = _dot(hi, onehot) + _dot(mid, onehot) + _dot(lo, onehot)
    tile = o_ref.shape[1:]
    o_ref[0] = bias.reshape(tile)
    o_ref[1] = jnp.where(prev_ref[...] > 0, NEG_INF, bias).reshape(tile)


def _rel_bias(table_t, bucket, is_prev):
    nh = table_t.shape[0]
    width = bucket.shape[1]
    return pl.pallas_call(
        _rel_bias_kernel,
        out_shape=jax.ShapeDtypeStruct((2, nh, WINDOW, WINDOW), F32),
        compiler_params=pltpu.CompilerParams(vmem_limit_bytes=VMEM_LIMIT),
        name="rel_bias",
    )(table_t, bucket, is_prev)


SWA_SCORES_AHEAD = 10
SWA_BLOCKS_PER_STEP = 4


def _swa_kernel(sink_ref, q_ref, kc_ref, kp_ref, vc_ref, vp_ref, bias0_ref, bias_ref, o_ref):
    w = WINDOW
    hd = HEAD_DIM
    gq = N_HEADS // N_KV_HEADS
    n_sub = q_ref.shape[0] // w
    is_prev = (lax.broadcasted_iota(jnp.int32, (w, w), 0) > lax.broadcasted_iota(jnp.int32, (w, w), 1))
    zeros = jnp.zeros((2 * w, hd), BF16)
    scale = HEAD_DIM ** -0.5
    k_pl, v_pl = {}, {}
    for sub in range(n_sub):
        for g in range(N_KV_HEADS):
            cs = slice(g * hd, (g + 1) * hd)
            if sub == 0:
                k_g = jnp.concatenate([kp_ref[:, cs], kc_ref[0:w, cs]], axis=0) * scale
                v_g = jnp.concatenate([vp_ref[:, cs], vc_ref[0:w, cs]], axis=0)
            else:
                k_g = kc_ref[(sub - 1) * w:(sub + 1) * w, cs] * scale
                v_g = vc_ref[(sub - 1) * w:(sub + 1) * w, cs]
            k_pl[sub, g] = (jnp.concatenate([k_g, zeros], axis=1), jnp.concatenate([zeros, k_g], axis=1))
            v_pl[sub, g] = (jnp.concatenate([v_g, zeros], axis=1), jnp.concatenate([zeros, v_g], axis=1))

    def scores(sub, hh):
        pair = hh - hh % 2
        return _dot_nt(k_pl[sub, hh // gq][hh % 2], q_ref[sub * w:(sub + 1) * w, pair * hd:(pair + 2) * hd])

    def attend(sub, hh, st2):
        bias = bias0_ref if sub == 0 else bias_ref
        st = jnp.where(is_prev, st2[:w], st2[w:]) + bias[hh]
        sink = sink_ref[hh]
        m = jnp.maximum(jnp.max(st, axis=0, keepdims=True), sink)
        p = jnp.exp(st - m)
        recip = 1.0 / (jnp.sum(p, axis=0, keepdims=True) + jnp.exp(sink - m))
        probs = p * recip
        p_cat = jnp.concatenate([jnp.where(is_prev, probs, 0.0), jnp.where(is_prev, 0.0, probs)], axis=0)
        return _dot_tn(p_cat.astype(BF16), v_pl[sub, hh // gq][hh % 2])

    chains = [(sub, hh) for sub in range(n_sub) for hh in range(N_HEADS)]
    ahead = SWA_SCORES_AHEAD
    st2 = {c: scores(*c) for c in chains[:ahead]}
    acc = None
    for n, (sub, hh) in enumerate(chains):
        if n + ahead < len(chains):
            st2[chains[n + ahead]] = scores(*chains[n + ahead])
        part = attend(sub, hh, st2.pop((sub, hh)))
        if hh % 2 == 0:
            acc = part
        else:
            o_ref[sub * w:(sub + 1) * w, (hh - 1) * hd:(hh + 1) * hd] = (acc + part).astype(o_ref.dtype)


def _swa(proj, sinks, bias):
    s = proj.shape[0]
    w = WINDOW
    nb = s // w
    n_sub = SWA_BLOCKS_PER_STEP if nb % SWA_BLOCKS_PER_STEP == 0 else 1
    kvw = N_KV_HEADS * HEAD_DIM
    dq = N_HEADS * HEAD_DIM
    rows = n_sub * w
    prev = lambda n: jnp.maximum(n * n_sub - 1, 0)
    return pl.pallas_call(
        _swa_kernel,
        grid=(nb // n_sub,),
        in_specs=[
            pl.BlockSpec(memory_space=pltpu.SMEM),
            pl.BlockSpec((rows, dq), lambda n: (n, COL_AQ // dq)),
            pl.BlockSpec((rows, kvw), lambda n: (n, COL_AK // kvw)),
            pl.BlockSpec((w, kvw), lambda n: (prev(n), COL_AK // kvw)),
            pl.BlockSpec((rows, kvw), lambda n: (n, COL_AV // kvw)),
            pl.BlockSpec((w, kvw), lambda n: (prev(n), COL_AV // kvw)),
            pl.BlockSpec((None, N_HEADS, w, w), lambda n: (jnp.where(n == 0, 1, 0), 0, 0, 0)),
            pl.BlockSpec((None, N_HEADS, w, w), lambda n: (0, 0, 0, 0)),
        ],
        out_specs=pl.BlockSpec((rows, dq), lambda n: (n, 0)),
        out_shape=jax.ShapeDtypeStruct((s, dq), BF16),
        compiler_params=_cparams(("parallel",)),
        name="swa",
    )(sinks, proj, proj, proj, proj, proj, bias, bias)


GLA_CUM_ROWS = 256


def _gla_kernel(*refs):
    nh = GLA_HEADS
    q_refs, k_refs, v_refs, gr_refs = (refs[i * nh:(i + 1) * nh] for i in range(4))
    la_ref, gn_ref, o_ref, state_ref = refs[4 * nh:]
    c = GLA_CHUNK
    rows, dk = q_refs[0].shape
    dv = v_refs[0].shape[1]

    @pl.when(pl.program_id(0) == 0)
    def _():
        state_ref[...] = jnp.zeros_like(state_ref)

    cr = min(rows, GLA_CUM_ROWS)
    ri = lax.broadcasted_iota(jnp.int32, (cr, cr), 0)
    ci = lax.broadcasted_iota(jnp.int32, (cr, cr), 1)
    tri = ((ri // c == ci // c) & (ri >= ci)).astype(BF16)
    cum_parts = []
    for t in range(rows // cr):
        la_parts = _split3(la_ref[t * cr:(t + 1) * cr, :])
        cum_parts.append(_dot(tri, la_parts[0]) + _dot(tri, la_parts[1]) + _dot(tri, la_parts[2]))
    cum_all = jnp.concatenate(cum_parts, axis=0)
    causal = (lax.broadcasted_iota(jnp.int32, (c, c), 0) >= lax.broadcasted_iota(jnp.int32, (c, c), 1))
    gn = gn_ref[...]
    n_chunks = rows // c

    o_intra, q_in, upd, decay = {}, {}, {}, {}
    for h in range(nh):
        for ch in range(n_chunks):
            sl = slice(ch * c, (ch + 1) * c)
            cum_c = cum_all[sl, h * dk:(h + 1) * dk]
            ref = cum_c[c // 2 - 1:c // 2]
            last = cum_c[c - 1:c]
            q = q_refs[h][sl, :].astype(F32) * (dk ** -0.5)
            k = k_refs[h][sl, :].astype(F32)
            v = v_refs[h][sl, :]
            q_intra = (q * jnp.exp(cum_c - ref)).astype(BF16)
            k_intra = (k * jnp.exp(ref - cum_c)).astype(BF16)
            a = jnp.where(causal, _dot_nt(q_intra, k_intra), 0.0)
            o_intra[h, ch] = _dot(a.astype(BF16), v)
            q_in[h, ch] = (q * jnp.exp(cum_c)).astype(BF16)
            k_dec = (k * jnp.exp(last - cum_c)).astype(BF16)
            upd[h, ch] = _dot_tn(k_dec, v)
            decay[h, ch] = jnp.broadcast_to(jnp.exp(last), (LANES, dk)).T

    state = [state_ref[h] for h in range(nh)]
    for ch in range(n_chunks):
        sl = slice(ch * c, (ch + 1) * c)
        for h in range(nh):
            o = o_intra[h, ch] + _dot(q_in[h, ch], state[h].astype(BF16))
            state[h] = jnp.concatenate(
                [state[h][:, jv * LANES:(jv + 1) * LANES] * decay[h, ch]
                 + upd[h, ch][:, jv * LANES:(jv + 1) * LANES] for jv in range(dv // LANES)], axis=1)
            ms = jnp.mean(o * o, axis=-1, keepdims=True)
            on = o * lax.rsqrt(ms + RMS_EPS) * gn
            gr = gr_refs[h][sl, :].astype(F32)
            o_ref[sl, h * dv:(h + 1) * dv] = (on * (gr * _sigmoid(gr))).astype(o_ref.dtype)
    for h in range(nh):
        state_ref[h] = state[h]


def _gla(proj, la, gn, rows):
    s = proj.shape[0]
    dk, dv = GLA_DK, GLA_DV
    nh = GLA_HEADS

    def head_specs(width, col0):
        return [pl.BlockSpec((rows, width), functools.partial(lambda i, h: (i, col0 // width + h), h=h))
                for h in range(nh)]

    in_specs = (head_specs(dk, COL_GQ) + head_specs(dk, COL_GK) + head_specs(dv, COL_GV)
                + head_specs(dv, COL_GR)
                + [pl.BlockSpec((rows, nh * dk), lambda i: (i, 0)), pl.BlockSpec((1, dv), lambda i: (0, 0))])
    return pl.pallas_call(
        _gla_kernel,
        grid=(s // rows,),
        in_specs=in_specs,
        out_specs=pl.BlockSpec((rows, nh * dv), lambda i: (i, 0)),
        out_shape=jax.ShapeDtypeStruct((s, nh * dv), BF16),
        scratch_shapes=[pltpu.VMEM((nh, dk, dv), F32)],
        compiler_params=_cparams(("arbitrary",)),
        name="gla",
    )(*([proj] * (4 * nh)), la, gn)


MERGE_SUB_ROWS = 128


def _merge_out_kernel(oa_ref, og_ref, ga_ref, gg_ref, x_ref, wo_ref, g_ref, wr_ref,
                      x1_ref, h2_ref, lt_ref):
    sub = MERGE_SUB_ROWS
    tiles = [slice(t * sub, (t + 1) * sub) for t in range(x_ref.shape[0] // sub)]
    merged = [(_sigmoid(ga_ref[rs, :].astype(F32)) * oa_ref[rs, :].astype(F32)
               + _sigmoid(gg_ref[rs, :].astype(F32)) * og_ref[rs, :].astype(F32)).astype(BF16) for rs in tiles]
    projected = [_dot(m, wo_ref[...]) for m in merged]
    for rs, y in zip(tiles, projected):
        x1 = x_ref[rs, :] + y
        x1_ref[rs, :] = x1
        ms = jnp.mean(x1 * x1, axis=-1, keepdims=True)
        h2b = (x1 * lax.rsqrt(ms + RMS_EPS) * g_ref[...]).astype(BF16)
        h2_ref[rs] = h2b.reshape((sub,) + h2_ref.shape[1:])
        lt_ref[:, rs] = _dot_nt(wr_ref[...], h2b)


def _merge_out(o_attn, o_gla, gates, x, wo, g, wr, tm):
    s, d = x.shape
    nr = wr.shape[0]
    row = lambda i: (i, 0)
    return pl.pallas_call(
        _merge_out_kernel,
        grid=(s // tm,),
        in_specs=[
            pl.BlockSpec((tm, d), row),
            pl.BlockSpec((tm, d), row),
            pl.BlockSpec((tm, d), lambda i: (i, 0)),
            pl.BlockSpec((tm, d), lambda i: (i, 1)),
            pl.BlockSpec((tm, d), row),
            pl.BlockSpec((d, d), lambda i: (0, 0), pipeline_mode=pl.Buffered(1)),
            pl.BlockSpec((1, d), lambda i: (0, 0)),
            pl.BlockSpec((nr, d), lambda i: (0, 0)),
        ],
        out_specs=[
            pl.BlockSpec((tm, d), row),
            pl.BlockSpec((tm, d // LANES, LANES), lambda i: (i, 0, 0)),
            pl.BlockSpec((nr, tm), lambda i: (0, i)),
        ],
        out_shape=[
            jax.ShapeDtypeStruct((s, d), F32),
            jax.ShapeDtypeStruct((s, d // LANES, LANES), BF16),
            jax.ShapeDtypeStruct((nr, s), F32),
        ],
        compiler_params=_cparams(("parallel",)),
        name="merge_out",
    )(o_attn, o_gla, gates, gates, x, wo, g, wr)


ROUTER_ROWS = 8 + N_EXPERTS


def _route_kernel(lt_ref, bg_ref, be_ref, idx_ref, gate_ref, cnt_ref, carry_ref):
    tb = lt_ref.shape[1]
    epg = EXPERTS_PER_GROUP

    @pl.when(pl.program_id(0) == 0)
    def _():
        carry_ref[...] = jnp.zeros_like(carry_ref)

    logits = lt_ref[...]
    gl = logits[0:N_GROUPS]
    g_exp = jnp.exp(gl - jnp.max(gl, axis=0, keepdims=True))
    g_prob = g_exp / jnp.sum(g_exp, axis=0, keepdims=True)
    gb = gl + bg_ref[0:N_GROUPS]
    rowg = lax.broadcasted_iota(jnp.int32, (N_GROUPS, tb), 0)
    g_idx = jnp.min(jnp.where(gb == jnp.max(gb, axis=0, keepdims=True), rowg, N_GROUPS),
                    axis=0, keepdims=True)
    p_group = jnp.sum(jnp.where(rowg == g_idx, g_prob, 0.0), axis=0, keepdims=True)

    el = logits[8:8 + epg]
    eb = jnp.broadcast_to(be_ref[0:epg], (epg, tb))
    for g in range(1, N_GROUPS):
        pick = g_idx == g
        el = jnp.where(pick, logits[8 + g * epg:8 + (g + 1) * epg], el)
        eb = jnp.where(pick, be_ref[g * epg:(g + 1) * epg], eb)
    e_exp = jnp.exp(el - jnp.max(el, axis=0, keepdims=True))
    e_prob = e_exp / jnp.sum(e_exp, axis=0, keepdims=True)
    score = el + eb
    rowe = lax.broadcasted_iota(jnp.int32, (epg, tb), 0)
    i1 = jnp.min(jnp.where(score == jnp.max(score, axis=0, keepdims=True), rowe, epg),
                 axis=0, keepdims=True)
    score2 = jnp.where(rowe == i1, -jnp.inf, score)
    i2 = jnp.min(jnp.where(score2 == jnp.max(score2, axis=0, keepdims=True), rowe, epg),
                 axis=0, keepdims=True)
    q1 = jnp.sum(jnp.where(rowe == i1, e_prob, 0.0), axis=0, keepdims=True)
    q2 = jnp.sum(jnp.where(rowe == i2, e_prob, 0.0), axis=0, keepdims=True)
    qs = q1 + q2
    gate1 = p_group * q1 / qs
    gate2 = p_group * q2 / qs
    e1 = g_idx * epg + i1
    e2 = g_idx * epg + i2

    rowx = lax.broadcasted_iota(jnp.int32, (N_EXPERTS, tb), 0)
    hit1 = rowx == e1
    hit2 = rowx == e2
    member = (hit1 | hit2).astype(BF16)
    before = (lax.broadcasted_iota(jnp.int32, (tb, tb), 0)
              < lax.broadcasted_iota(jnp.int32, (tb, tb), 1)).astype(BF16)
    carry = carry_ref[...]
    count = _dot(member, before) + carry[:, 0:1]
    r1 = jnp.sum(jnp.where(hit1, count, 0.0), axis=0, keepdims=True).astype(jnp.int32)
    r2 = jnp.sum(jnp.where(hit2, count, 0.0), axis=0, keepdims=True).astype(jnp.int32)
    carry = carry + jnp.sum(member.astype(F32), axis=1, keepdims=True)
    carry_ref[...] = carry
    cnt_ref[...] = carry

    row8 = lax.broadcasted_iota(jnp.int32, (8, tb), 0)
    idx_ref[...] = jnp.where(row8 == 0, e1, jnp.where(row8 == 1, e2,
                             jnp.where(row8 == 2, r1, jnp.where(row8 == 3, r2, 0))))
    gate_ref[...] = jnp.where(row8 == 0, gate1, jnp.where(row8 == 1, gate2, 0.0))


def _route(lt, bg, be, tb):
    nr, s = lt.shape
    return pl.pallas_call(
        _route_kernel,
        grid=(s // tb,),
        in_specs=[
            pl.BlockSpec((nr, tb), lambda i: (0, i)),
            pl.BlockSpec((8, 1), lambda i: (0, 0)),
            pl.BlockSpec((N_EXPERTS, 1), lambda i: (0, 0)),
        ],
        out_specs=[
            pl.BlockSpec((8, tb), lambda i: (0, i)),
            pl.BlockSpec((8, tb), lambda i: (0, i)),
            pl.BlockSpec((N_EXPERTS, LANES), lambda i: (0, 0)),
        ],
        out_shape=[
            jax.ShapeDtypeStruct((8, s), jnp.int32),
            jax.ShapeDtypeStruct((8, s), F32),
            jax.ShapeDtypeStruct((N_EXPERTS, LANES), F32),
        ],
        scratch_shapes=[pltpu.VMEM((N_EXPERTS, LANES), F32)],
        compiler_params=_cparams(("arbitrary",)),
        name="route",
    )(lt, bg, be)


def _expert_kernel(be_ref, nx_ref, nu_ref, dest_ref, neg_hbm, h2_hbm, wg_hbm, wu_hbm, wd_hbm, y_hbm,
                   pair_ref, h2v, xg, ystage, wgf, wuf, wdf, wgb, wub, wdb, hsem, wsem, ysem):
    n_used = nu_ref[0]
    rows = MOE_BLOCK
    n_tok = h2v.shape[0]

    def weight_copy(e, m):
        src, stage = ((wg_hbm, wgf), (wu_hbm, wuf), (wd_hbm, wdf))[m]
        return pltpu.make_async_copy(src.at[e], stage, wsem.at[m])

    def gather_row(blk, r, half):
        pair = jnp.maximum(pair_ref[blk * rows + r], 0)
        xg[half, r] = h2v[jnp.where(pair >= n_tok, pair - n_tok, pair)]

    def row_write(blk, r, half, dummy):
        pair = pair_ref[blk * rows + r]
        dst = jnp.where((pair < 0) | dummy, TOP_K * n_tok + half * rows + r, pair)
        return pltpu.make_async_copy(ystage.at[half, r], y_hbm.at[dst], ysem.at[half])

    def wait_row_writes(half):
        pltpu.make_async_copy(ystage.at[half], y_hbm.at[pl.ds(0, rows)], ysem.at[half]).wait()

    weight_priority = 1

    fill = pltpu.make_async_copy(neg_hbm, pair_ref, hsem.at[1])
    fill.start()
    h2_copy = pltpu.make_async_copy(h2_hbm, h2v, hsem.at[0])
    h2_copy.start()
    for m in range(3):
        weight_copy(be_ref[0], m).start(priority=weight_priority)
    ystage[...] = jnp.zeros_like(ystage)

    fill.wait()

    def place(p, carry):
        pair_ref[dest_ref[p]] = p
        return carry
    lax.fori_loop(0, dest_ref.shape[0], place, 0, unroll=8)

    h2_copy.wait()
    spare_fill = pltpu.make_async_copy(ystage.at[0], y_hbm.at[pl.ds(TOP_K * n_tok, rows)], hsem.at[0])
    spare_fill.start()
    spare_fill.wait()

    def first_gather(r, carry):
        gather_row(0, r, 0)
        return carry
    lax.fori_loop(0, rows, first_gather, 0)

    def block(b, carry):
        slot = b % 2
        first_of_expert = (b == 0) | (be_ref[b] != be_ref[jnp.maximum(b - 1, 0)])

        @pl.when(first_of_expert)
        def _():
            nxt = nx_ref[b]
            for m, (stage, dst) in enumerate(((wgf, wgb), (wuf, wub), (wdf, wdb))):
                weight_copy(be_ref[b], m).wait()
                dst[...] = stage[...].astype(BF16)

                @pl.when(nxt >= 0)
                def _():
                    weight_copy(nxt, m).start(priority=weight_priority)

        xb = xg[slot].reshape(rows, wgb.shape[0])
        prev_blk = jnp.maximum(b - 1, 0)
        for r in range(rows):
            row_write(prev_blk, r, 1 - slot, b == 0).start(priority=r % 2)
        for r in range(rows):
            gather_row(b + 1, r, 1 - slot)
        hg = _dot(xb, wgb[...])
        hu = _dot(xb, wub[...])
        act = (hg * _sigmoid(hg)) * hu
        y = _dot(act.astype(BF16), wdb[...]).astype(BF16)

        @pl.when(b >= 1)
        def _():
            wait_row_writes(slot)

        ystage[slot] = y.reshape(ystage.shape[1:])
        return carry

    lax.fori_loop(0, n_used, block, 0)

    last_half = (n_used - 1) % 2

    def last_writes(r, carry):
        row_write(n_used - 1, r, last_half, False).start()
        return carry
    lax.fori_loop(0, rows, last_writes, 0)
    wait_row_writes(1 - last_half)
    wait_row_writes(last_half)


def _experts(block_e, next_e, n_used, dest, h2c, wg, wu, wd):
    n_rows = block_e.shape[0] * MOE_BLOCK
    n_tok = h2c.shape[0]
    _, d, de = wg.shape
    slab = (d // LANES, LANES)
    grid_spec = pltpu.PrefetchScalarGridSpec(
        num_scalar_prefetch=4,
        grid=(1,),
        in_specs=[pl.BlockSpec(memory_space=pl.ANY)] * 5,
        out_specs=pl.BlockSpec(memory_space=pl.ANY),
        scratch_shapes=[
            pltpu.SMEM((n_rows,), jnp.int32),
            pltpu.VMEM((n_tok,) + slab, BF16),
            pltpu.VMEM((2, MOE_BLOCK) + slab, BF16),
            pltpu.VMEM((2, MOE_BLOCK) + slab, BF16),
            pltpu.VMEM((d, de), F32),
            pltpu.VMEM((d, de), F32),
            pltpu.VMEM((de, d), F32),
            pltpu.VMEM((d, de), BF16),
            pltpu.VMEM((d, de), BF16),
            pltpu.VMEM((de, d), BF16),
            pltpu.SemaphoreType.DMA((2,)),
            pltpu.SemaphoreType.DMA((3,)),
            pltpu.SemaphoreType.DMA((2,)),
        ],
    )
    padding_marks = jnp.full((n_rows,), -1, jnp.int32)
    return pl.pallas_call(
        _expert_kernel,
        grid_spec=grid_spec,
        out_shape=jax.ShapeDtypeStruct((TOP_K * n_tok + 2 * MOE_BLOCK,) + slab, BF16),
        compiler_params=_cparams(("arbitrary",), vmem=EXPERT_VMEM_LIMIT),
        name="experts",
    )(block_e, next_e, n_used, dest, padding_marks, h2c, wg, wu, wd)


def _combine_kernel(y0_ref, y1_ref, gate_ref, x1_ref, g_ref, o_ref):
    gates = gate_ref[...]
    y0 = y0_ref[...].reshape(x1_ref.shape).astype(F32)
    y1 = y1_ref[...].reshape(x1_ref.shape).astype(F32)
    x2 = x1_ref[...] + (y0 * gates[:, 0:1] + y1 * gates[:, 1:2])
    ms = jnp.mean(x2 * x2, axis=-1, keepdims=True)
    o_ref[...] = x2 * lax.rsqrt(ms + RMS_EPS) * g_ref[...]


def _combine(y2, gates, x1, g, tm):
    s, d = x1.shape
    slab = (d // LANES, LANES)
    n_steps = s // tm
    return pl.pallas_call(
        _combine_kernel,
        grid=(n_steps,),
        in_specs=[
            pl.BlockSpec((tm,) + slab, lambda i: (i, 0, 0)),
            pl.BlockSpec((tm,) + slab, lambda i: (n_steps + i, 0, 0)),
            pl.BlockSpec((tm, TOP_K), lambda i: (i, 0)),
            pl.BlockSpec((tm, d), lambda i: (i, 0)),
            pl.BlockSpec((1, d), lambda i: (0, 0)),
        ],
        out_specs=pl.BlockSpec((tm, d), lambda i: (i, 0)),
        out_shape=jax.ShapeDtypeStruct((s, d), F32),
        compiler_params=_cparams(("parallel",)),
        name="combine",
    )(y2, y2, gates, x1, g)


def _tile(n, pref):
    return pref if n % pref == 0 else n


def kernel(x, norm_mix_g, w_in, w_gla_a2, b_gla_a, gla_norm_g, attn_sinks, rel_bias_table, w_out,
           norm_ffn_g, w_router_group, b_router_group, w_router_expert, b_router_expert,
           w_expert_gate, w_expert_up, w_expert_down, norm_final_g):
    b, s, d = x.shape
    assert b == 1 and w_in.shape[0] == 1, "single batch, single layer"
    assert (s * TOP_K) % MOE_BLOCK == 0
    x2d = x.reshape(s, d)
    w_in_t = w_in[0].T

    w2 = jnp.pad(w_gla_a2[0], ((0, LANES - GLA_LOWRANK), (0, 0))).astype(BF16)
    h, log_a = _norm_loga(x2d, norm_mix_g[0].reshape(1, d), w_in_t, w2, b_gla_a[0].reshape(1, -1),
                          _tile(s, 1024))
    proj = _in_proj(h, w_in_t, _tile(s, 4096), 512)

    bucket, is_prev = _folded_maps()
    bias = _rel_bias(rel_bias_table.T, jnp.asarray(bucket), jnp.asarray(is_prev))
    o_attn = _swa(proj, attn_sinks[0], bias)
    o_gla = _gla(proj, log_a, gla_norm_g[0].reshape(1, -1), _tile(s, 512))

    wr = jnp.zeros((ROUTER_ROWS, d), F32)
    wr = wr.at[0:N_GROUPS].set(w_router_group[0].T).at[8:].set(w_router_expert[0].T).astype(BF16)
    x1, h2, logits_t = _merge_out(o_attn, o_gla, proj, x2d, w_out[0].astype(BF16),
                                  norm_ffn_g[0].reshape(1, d), wr, _tile(s, 512))

    bg = jnp.pad(b_router_group[0], (0, 8 - N_GROUPS)).reshape(8, 1)
    idx, gate, counts = _route(logits_t, bg, b_router_expert[0].reshape(N_EXPERTS, 1), _tile(s, 512))
    counts = counts[:, 0].astype(jnp.int32)
    padded = (counts + MOE_BLOCK - 1) // MOE_BLOCK * MOE_BLOCK
    pend = jnp.cumsum(padded)
    pstart = pend - padded
    expert = idx[0:TOP_K]
    eids = jnp.arange(N_EXPERTS, dtype=jnp.int32)[:, None, None]
    dest = jnp.sum(jnp.where(expert[None] == eids, pstart[:, None, None], 0), axis=0) + idx[TOP_K:2 * TOP_K]
    n_pairs = s * TOP_K
    n_blocks = (n_pairs + N_EXPERTS * (MOE_BLOCK - 1) + MOE_BLOCK - 1) // MOE_BLOCK
    n_used = (pend[-1] // MOE_BLOCK).reshape(1)
    block_start = jnp.arange(n_blocks, dtype=jnp.int32) * MOE_BLOCK
    block_e = jnp.minimum(jnp.sum((pend[None, :] <= block_start[:, None]).astype(jnp.int32), axis=1),
                          N_EXPERTS - 1)

    e_col = jnp.arange(N_EXPERTS, dtype=jnp.int32)[:, None]
    e_row = jnp.arange(N_EXPERTS, dtype=jnp.int32)[None, :]
    later = (e_row > e_col) & (counts > 0)[None, :]
    next_nonempty = jnp.min(jnp.where(later, e_row, N_EXPERTS), axis=1)
    next_nonempty = jnp.where(next_nonempty < N_EXPERTS, next_nonempty, -1)
    next_e = jnp.sum(jnp.where(block_e[:, None] == e_row, next_nonempty[None, :], 0), axis=1)
    y2 = _experts(block_e, next_e, n_used, dest.reshape(-1), h2,
                  w_expert_gate[0], w_expert_up[0], w_expert_down[0])
    out = _combine(y2, gate[0:TOP_K].T, x1, norm_final_g.reshape(1, d), _tile(s, 512))
    return out.reshape(b, s, d)
```

```python
import functools
import math

import numpy as np
import jax
import jax.numpy as jnp
from jax import lax
from jax.experimental import pallas as pl
from jax.experimental.pallas import tpu as pltpu

F32 = jnp.float32
BF16 = jnp.bfloat16

N_HEADS = 32
N_KV_HEADS = 4
HEAD_DIM = 64
WINDOW = 128
N_BUCKETS = 32
MAX_DISTANCE = 128
GLA_HEADS = 4
GLA_DK = 256
GLA_DV = 512
GLA_LOWRANK = 16
GLA_TAU = 16.0
GLA_CHUNK = 64
N_GROUPS = 4
EXPERTS_PER_GROUP = 8
N_EXPERTS = 32
TOP_K = 2
MOE_BLOCK = 256
RMS_EPS = 1e-6
NEG_INF = -1e30

COL_GA = 8704
COL_GATES = 8720
MAIN_COLS = 8704
GATE_COLS = 4096
COL_AQ = GATE_COLS + 0
COL_AK = GATE_COLS + 2048
COL_AV = GATE_COLS + 2304
COL_GQ = GATE_COLS + 2560
COL_GK = GATE_COLS + 3584
COL_GV = GATE_COLS + 4608
COL_GR = GATE_COLS + 6656

LANES = 128
VMEM_LIMIT = 56 * 1024 * 1024
EXPERT_VMEM_LIMIT = 60 * 1024 * 1024


def _cparams(sem, vmem=VMEM_LIMIT):
    return pltpu.CompilerParams(dimension_semantics=sem, vmem_limit_bytes=vmem)


def _split3(x):
    hi = x.astype(BF16)
    r1 = x - hi.astype(F32)
    mid = r1.astype(BF16)
    lo = (r1 - mid.astype(F32)).astype(BF16)
    return hi, mid, lo


def _sigmoid(x):
    return 0.5 * jnp.tanh(0.5 * x) + 0.5


def _dot(a, b):
    return jnp.dot(a, b, preferred_element_type=F32)


def _dot_nt(a, b):
    return lax.dot_general(a, b, (((1,), (1,)), ((), ())), preferred_element_type=F32)


def _dot_tn(a, b):
    return lax.dot_general(a, b, (((0,), (0,)), ((), ())), preferred_element_type=F32)


def _norm_loga_kernel(x_ref, g_ref, wga_ref, w2_ref, b_ref, h_ref, la_ref):
    x = x_ref[...]
    ms = jnp.mean(x * x, axis=-1, keepdims=True)
    hb = (x * lax.rsqrt(ms + RMS_EPS) * g_ref[...]).astype(BF16)
    h_ref[...] = hb
    row = lax.broadcasted_iota(jnp.int32, wga_ref.shape, 0)
    wga = jnp.where(row < GLA_LOWRANK, wga_ref[...], 0.0).astype(BF16)
    ga = _dot_nt(hb, wga)
    z = _dot(ga.astype(BF16), w2_ref[...]) + b_ref[...]
    la_ref[...] = (jnp.minimum(z, 0.0) - jnp.log1p(jnp.exp(-jnp.abs(z)))) * (1.0 / GLA_TAU)


def _norm_loga(x, g, wga, w2, b, tm):
    s, d = x.shape
    nq = w2.shape[1]
    return pl.pallas_call(
        _norm_loga_kernel,
        grid=(s // tm,),
        in_specs=[
            pl.BlockSpec((tm, d), lambda i: (i, 0)),
            pl.BlockSpec((1, d), lambda i: (0, 0)),
            pl.BlockSpec((LANES, d), lambda i: (COL_GA // LANES, 0)),
            pl.BlockSpec((LANES, nq), lambda i: (0, 0)),
            pl.BlockSpec((1, nq), lambda i: (0, 0)),
        ],
        out_specs=[
            pl.BlockSpec((tm, d), lambda i: (i, 0)),
            pl.BlockSpec((tm, nq), lambda i: (i, 0)),
        ],
        out_shape=[
            jax.ShapeDtypeStruct((s, d), BF16),
            jax.ShapeDtypeStruct((s, nq), F32),
        ],
        compiler_params=_cparams(("parallel",)),
        name="norm_loga",
    )(x, g, wga, w2, b)


def _in_proj_kernel(h_hbm, wt_hbm, o_ref, h_vmem, w_stage, wbf, sem, *, n_main, n_tiles, tn):
    j = pl.program_id(0)
    i = pl.program_id(1)
    tm = o_ref.shape[0]

    def w_copy(jj):
        row0 = jnp.where(jj < n_main, jj * tn, COL_GATES + (jj - n_main) * tn)
        return pltpu.make_async_copy(wt_hbm.at[pl.ds(pl.multiple_of(row0, 8), tn)], w_stage, sem.at[1])

    @pl.when((j == 0) & (i == 0))
    def _():
        h_copy = pltpu.make_async_copy(h_hbm, h_vmem, sem.at[0])
        h_copy.start()
        w_copy(0).start()
        h_copy.wait()

    @pl.when(i == 0)
    def _():
        w_copy(j).wait()
        wbf[...] = w_stage[...].astype(BF16)

        @pl.when(j + 1 < n_tiles)
        def _():
            w_copy(j + 1).start()

    h = h_vmem[pl.ds(pl.multiple_of(i * tm, tm), tm), :]
    o_ref[...] = _dot_nt(h, wbf[...]).astype(o_ref.dtype)


def _in_proj(h, w_t, tm, tn):
    s, d = h.shape
    n_main = MAIN_COLS // tn
    n_gate = GATE_COLS // tn
    n_tiles = n_main + n_gate
    kern = functools.partial(_in_proj_kernel, n_main=n_main, n_tiles=n_tiles, tn=tn)
    return pl.pallas_call(
        kern,
        grid=(n_tiles, s // tm),
        in_specs=[pl.BlockSpec(memory_space=pl.ANY), pl.BlockSpec(memory_space=pl.ANY)],
        out_specs=pl.BlockSpec((tm, tn), lambda j, i: (i, (j + n_gate) % n_tiles)),
        out_shape=jax.ShapeDtypeStruct((s, n_tiles * tn), BF16),
        scratch_shapes=[
            pltpu.VMEM((s, d), BF16),
            pltpu.VMEM((tn, d), F32),
            pltpu.VMEM((tn, d), BF16),
            pltpu.SemaphoreType.DMA((2,)),
        ],
        compiler_params=_cparams(("arbitrary", "arbitrary")),
        name="in_proj",
    )(h, w_t)


def _folded_maps():
    j = np.arange(WINDOW)[:, None]
    i = np.arange(WINDOW)[None, :]
    n = (i - j) % WINDOW
    max_exact = N_BUCKETS // 2
    ratio = np.maximum(n, max_exact).astype(np.float32) / np.float32(max_exact)
    large = max_exact + (np.log(ratio) / np.float32(math.log(MAX_DISTANCE / max_exact))
                         * (N_BUCKETS - max_exact)).astype(np.int32)
    large = np.minimum(large, N_BUCKETS - 1)
    bucket = np.where(n < max_exact, n, large).astype(np.int32).reshape(1, -1)
    is_prev = (j > i).astype(np.int32).reshape(1, -1)
    return bucket, is_prev


def _rel_bias_kernel(tab_ref, bucket_ref, prev_ref, o_ref):
    nb = tab_ref.shape[1]
    width = bucket_ref.shape[1]
    onehot = (lax.broadcasted_iota(jnp.int32, (nb, width), 0) == bucket_ref[...]).astype(BF16)
    hi, mid, lo = _split3(tab_ref[...])
    bias = _dot(hi, onehot) + _dot(mid, onehot) + _dot(lo, onehot)
    tile = o_ref.shape[1:]
    o_ref[0] = bias.reshape(tile)
    o_ref[1] = jnp.where(prev_ref[...] > 0, NEG_INF, bias).reshape(tile)


def _rel_bias(table_t, bucket, is_prev):
    nh = table_t.shape[0]
    width = bucket.shape[1]
    return pl.pallas_call(
        _rel_bias_kernel,
        out_shape=jax.ShapeDtypeStruct((2, nh, WINDOW, WINDOW), F32),
        compiler_params=pltpu.CompilerParams(vmem_limit_bytes=VMEM_LIMIT),
        name="rel_bias",
    )(table_t, bucket, is_prev)


SWA_SCORES_AHEAD = 10
SWA_BLOCKS_PER_STEP = 4


def _swa_kernel(sink_ref, q_ref, kc_ref, kp_ref, vc_ref, vp_ref, bias0_ref, bias_ref, o_ref):
    w = WINDOW
    hd = HEAD_DIM
    gq = N_HEADS // N_KV_HEADS
    n_sub = q_ref.shape[0] // w
    is_prev = (lax.broadcasted_iota(jnp.int32, (w, w), 0) > lax.broadcasted_iota(jnp.int32, (w, w), 1))
    zeros = jnp.zeros((2 * w, hd), BF16)
    scale = HEAD_DIM ** -0.5
    k_pl, v_pl = {}, {}
    for sub in range(n_sub):
        for g in range(N_KV_HEADS):
            cs = slice(g * hd, (g + 1) * hd)
            if sub == 0:
                k_g = jnp.concatenate([kp_ref[:, cs], kc_ref[0:w, cs]], axis=0) * scale
                v_g = jnp.concatenate([vp_ref[:, cs], vc_ref[0:w, cs]], axis=0)
            else:
                k_g = kc_ref[(sub - 1) * w:(sub + 1) * w, cs] * scale
                v_g = vc_ref[(sub - 1) * w:(sub + 1) * w, cs]
            k_pl[sub, g] = (jnp.concatenate([k_g, zeros], axis=1), jnp.concatenate([zeros, k_g], axis=1))
            v_pl[sub, g] = (jnp.concatenate([v_g, zeros], axis=1), jnp.concatenate([zeros, v_g], axis=1))

    def scores(sub, hh):
        pair = hh - hh % 2
        return _dot_nt(k_pl[sub, hh // gq][hh % 2], q_ref[sub * w:(sub + 1) * w, pair * hd:(pair + 2) * hd])

    def attend(sub, hh, st2):
        bias = bias0_ref if sub == 0 else bias_ref
        st = jnp.where(is_prev, st2[:w], st2[w:]) + bias[hh]
        sink = sink_ref[hh]
        m = jnp.maximum(jnp.max(st, axis=0, keepdims=True), sink)
        p = jnp.exp(st - m)
        recip = 1.0 / (jnp.sum(p, axis=0, keepdims=True) + jnp.exp(sink - m))
        probs = p * recip
        p_cat = jnp.concatenate([jnp.where(is_prev, probs, 0.0), jnp.where(is_prev, 0.0, probs)], axis=0)
        return _dot_tn(p_cat.astype(BF16), v_pl[sub, hh // gq][hh % 2])

    chains = [(sub, hh) for sub in range(n_sub) for hh in range(N_HEADS)]
    ahead = SWA_SCORES_AHEAD
    st2 = {c: scores(*c) for c in chains[:ahead]}
    acc = None
    for n, (sub, hh) in enumerate(chains):
        if n + ahead < len(chains):
            st2[chains[n + ahead]] = scores(*chains[n + ahead])
        part = attend(sub, hh, st2.pop((sub, hh)))
        if hh % 2 == 0:
            acc = part
        else:
            o_ref[sub * w:(sub + 1) * w, (hh - 1) * hd:(hh + 1) * hd] = (acc + part).astype(o_ref.dtype)


def _swa(proj, sinks, bias):
    s = proj.shape[0]
    w = WINDOW
    nb = s // w
    n_sub = SWA_BLOCKS_PER_STEP if nb % SWA_BLOCKS_PER_STEP == 0 else 1
    kvw = N_KV_HEADS * HEAD_DIM
    dq = N_HEADS * HEAD_DIM
    rows = n_sub * w
    prev = lambda n: jnp.maximum(n * n_sub - 1, 0)
    return pl.pallas_call(
        _swa_kernel,
        grid=(nb // n_sub,),
        in_specs=[
            pl.BlockSpec(memory_space=pltpu.SMEM),
            pl.BlockSpec((rows, dq), lambda n: (n, COL_AQ // dq)),
            pl.BlockSpec((rows, kvw), lambda n: (n, COL_AK // kvw)),
            pl.BlockSpec((w, kvw), lambda n: (prev(n), COL_AK // kvw)),
            pl.BlockSpec((rows, kvw), lambda n: (n, COL_AV // kvw)),
            pl.BlockSpec((w, kvw), lambda n: (prev(n), COL_AV // kvw)),
            pl.BlockSpec((None, N_HEADS, w, w), lambda n: (jnp.where(n == 0, 1, 0), 0, 0, 0)),
            pl.BlockSpec((None, N_HEADS, w, w), lambda n: (0, 0, 0, 0)),
        ],
        out_specs=pl.BlockSpec((rows, dq), lambda n: (n, 0)),
        out_shape=jax.ShapeDtypeStruct((s, dq), BF16),
        compiler_params=_cparams(("parallel",)),
        name="swa",
    )(sinks, proj, proj, proj, proj, proj, bias, bias)


GLA_CUM_ROWS = 256


def _gla_kernel(*refs):
    nh = GLA_HEADS
    q_refs, k_refs, v_refs, gr_refs = (refs[i * nh:(i + 1) * nh] for i in range(4))
    la_ref, gn_ref, o_ref, state_ref = refs[4 * nh:]
    c = GLA_CHUNK
    rows, dk = q_refs[0].shape
    dv = v_refs[0].shape[1]

    @pl.when(pl.program_id(0) == 0)
    def _():
        state_ref[...] = jnp.zeros_like(state_ref)

    cr = min(rows, GLA_CUM_ROWS)
    ri = lax.broadcasted_iota(jnp.int32, (cr, cr), 0)
    ci = lax.broadcasted_iota(jnp.int32, (cr, cr), 1)
    tri = ((ri // c == ci // c) & (ri >= ci)).astype(BF16)
    cum_parts = []
    for t in range(rows // cr):
        la_parts = _split3(la_ref[t * cr:(t + 1) * cr, :])
        cum_parts.append(_dot(tri, la_parts[0]) + _dot(tri, la_parts[1]) + _dot(tri, la_parts[2]))
    cum_all = jnp.concatenate(cum_parts, axis=0)
    causal = (lax.broadcasted_iota(jnp.int32, (c, c), 0) >= lax.broadcasted_iota(jnp.int32, (c, c), 1))
    gn = gn_ref[...]
    n_chunks = rows // c

    o_intra, q_in, upd, decay = {}, {}, {}, {}
    for h in range(nh):
        for ch in range(n_chunks):
            sl = slice(ch * c, (ch + 1) * c)
            cum_c = cum_all[sl, h * dk:(h + 1) * dk]
            ref = cum_c[c // 2 - 1:c // 2]
            last = cum_c[c - 1:c]
            q = q_refs[h][sl, :].astype(F32) * (dk ** -0.5)
            k = k_refs[h][sl, :].astype(F32)
            v = v_refs[h][sl, :]
            q_intra = (q * jnp.exp(cum_c - ref)).astype(BF16)
            k_intra = (k * jnp.exp(ref - cum_c)).astype(BF16)
            a = jnp.where(causal, _dot_nt(q_intra, k_intra), 0.0)
            o_intra[h, ch] = _dot(a.astype(BF16), v)
            q_in[h, ch] = (q * jnp.exp(cum_c)).astype(BF16)
            k_dec = (k * jnp.exp(last - cum_c)).astype(BF16)
            upd[h, ch] = _dot_tn(k_dec, v)
            decay[h, ch] = jnp.broadcast_to(jnp.exp(last), (LANES, dk)).T

    state = [state_ref[h] for h in range(nh)]
    for ch in range(n_chunks):
        sl = slice(ch * c, (ch + 1) * c)
        for h in range(nh):
            o = o_intra[h, ch] + _dot(q_in[h, ch], state[h].astype(BF16))
            state[h] = jnp.concatenate(
                [state[h][:, jv * LANES:(jv + 1) * LANES] * decay[h, ch]
                 + upd[h, ch][:, jv * LANES:(jv + 1) * LANES] for jv in range(dv // LANES)], axis=1)
            ms = jnp.mean(o * o, axis=-1, keepdims=True)
            on = o * lax.rsqrt(ms + RMS_EPS) * gn
            gr = gr_refs[h][sl, :].astype(F32)
            o_ref[sl, h * dv:(h + 1) * dv] = (on * (gr * _sigmoid(gr))).astype(o_ref.dtype)
    for h in range(nh):
        state_ref[h] = state[h]


def _gla(proj, la, gn, rows):
    s = proj.shape[0]
    dk, dv = GLA_DK, GLA_DV
    nh = GLA_HEADS

    def head_specs(width, col0):
        return [pl.BlockSpec((rows, width), functools.partial(lambda i, h: (i, col0 // width + h), h=h))
                for h in range(nh)]

    in_specs = (head_specs(dk, COL_GQ) + head_specs(dk, COL_GK) + head_specs(dv, COL_GV)
                + head_specs(dv, COL_GR)
                + [pl.BlockSpec((rows, nh * dk), lambda i: (i, 0)), pl.BlockSpec((1, dv), lambda i: (0, 0))])
    return pl.pallas_call(
        _gla_kernel,
        grid=(s // rows,),
        in_specs=in_specs,
        out_specs=pl.BlockSpec((rows, nh * dv), lambda i: (i, 0)),
        out_shape=jax.ShapeDtypeStruct((s, nh * dv), BF16),
        scratch_shapes=[pltpu.VMEM((nh, dk, dv), F32)],
        compiler_params=_cparams(("arbitrary",)),
        name="gla",
    )(*([proj] * (4 * nh)), la, gn)


MERGE_SUB_ROWS = 128


def _merge_out_kernel(oa_ref, og_ref, ga_ref, gg_ref, x_ref, wo_ref, g_ref, wr_ref,
                      x1_ref, h2_ref, lt_ref):
    sub = MERGE_SUB_ROWS
    tiles = [slice(t * sub, (t + 1) * sub) for t in range(x_ref.shape[0] // sub)]
    merged = [(_sigmoid(ga_ref[rs, :].astype(F32)) * oa_ref[rs, :].astype(F32)
               + _sigmoid(gg_ref[rs, :].astype(F32)) * og_ref[rs, :].astype(F32)).astype(BF16) for rs in tiles]
    projected = [_dot(m, wo_ref[...]) for m in merged]
    for rs, y in zip(tiles, projected):
        x1 = x_ref[rs, :] + y
        x1_ref[rs, :] = x1
        ms = jnp.mean(x1 * x1, axis=-1, keepdims=True)
        h2b = (x1 * lax.rsqrt(ms + RMS_EPS) * g_ref[...]).astype(BF16)
        h2_ref[rs] = h2b.reshape((sub,) + h2_ref.shape[1:])
        lt_ref[:, rs] = _dot_nt(wr_ref[...], h2b)


def _merge_out(o_attn, o_gla, gates, x, wo, g, wr, tm):
    s, d = x.shape
    nr = wr.shape[0]
    row = lambda i: (i, 0)
    return pl.pallas_call(
        _merge_out_kernel,
        grid=(s // tm,),
        in_specs=[
            pl.BlockSpec((tm, d), row),
            pl.BlockSpec((tm, d), row),
            pl.BlockSpec((tm, d), lambda i: (i, 0)),
            pl.BlockSpec((tm, d), lambda i: (i, 1)),
            pl.BlockSpec((tm, d), row),
            pl.BlockSpec((d, d), lambda i: (0, 0), pipeline_mode=pl.Buffered(1)),
            pl.BlockSpec((1, d), lambda i: (0, 0)),
            pl.BlockSpec((nr, d), lambda i: (0, 0)),
        ],
        out_specs=[
            pl.BlockSpec((tm, d), row),
            pl.BlockSpec((tm, d // LANES, LANES), lambda i: (i, 0, 0)),
            pl.BlockSpec((nr, tm), lambda i: (0, i)),
        ],
        out_shape=[
            jax.ShapeDtypeStruct((s, d), F32),
            jax.ShapeDtypeStruct((s, d // LANES, LANES), BF16),
            jax.ShapeDtypeStruct((nr, s), F32),
        ],
        compiler_params=_cparams(("parallel",)),
        name="merge_out",
    )(o_attn, o_gla, gates, gates, x, wo, g, wr)


ROUTER_ROWS = 8 + N_EXPERTS


def _route_kernel(lt_ref, bg_ref, be_ref, idx_ref, gate_ref, cnt_ref, carry_ref):
    tb = lt_ref.shape[1]
    epg = EXPERTS_PER_GROUP

    @pl.when(pl.program_id(0) == 0)
    def _():
        carry_ref[...] = jnp.zeros_like(carry_ref)

    logits = lt_ref[...]
    gl = logits[0:N_GROUPS]
    g_exp = jnp.exp(gl - jnp.max(gl, axis=0, keepdims=True))
    g_prob = g_exp / jnp.sum(g_exp, axis=0, keepdims=True)
    gb = gl + bg_ref[0:N_GROUPS]
    rowg = lax.broadcasted_iota(jnp.int32, (N_GROUPS, tb), 0)
    g_idx = jnp.min(jnp.where(gb == jnp.max(gb, axis=0, keepdims=True), rowg, N_GROUPS),
                    axis=0, keepdims=True)
    p_group = jnp.sum(jnp.where(rowg == g_idx, g_prob, 0.0), axis=0, keepdims=True)

    el = logits[8:8 + epg]
    eb = jnp.broadcast_to(be_ref[0:epg], (epg, tb))
    for g in range(1, N_GROUPS):
        pick = g_idx == g
        el = jnp.where(pick, logits[8 + g * epg:8 + (g + 1) * epg], el)
        eb = jnp.where(pick, be_ref[g * epg:(g + 1) * epg], eb)
    e_exp = jnp.exp(el - jnp.max(el, axis=0, keepdims=True))
    e_prob = e_exp / jnp.sum(e_exp, axis=0, keepdims=True)
    score = el + eb
    rowe = lax.broadcasted_iota(jnp.int32, (epg, tb), 0)
    i1 = jnp.min(jnp.where(score == jnp.max(score, axis=0, keepdims=True), rowe, epg),
                 axis=0, keepdims=True)
    score2 = jnp.where(rowe == i1, -jnp.inf, score)
    i2 = jnp.min(jnp.where(score2 == jnp.max(score2, axis=0, keepdims=True), rowe, epg),
                 axis=0, keepdims=True)
    q1 = jnp.sum(jnp.where(rowe == i1, e_prob, 0.0), axis=0, keepdims=True)
    q2 = jnp.sum(jnp.where(rowe == i2, e_prob, 0.0), axis=0, keepdims=True)
    qs = q1 + q2
    gate1 = p_group * q1 / qs
    gate2 = p_group * q2 / qs
    e1 = g_idx * epg + i1
    e2 = g_idx * epg + i2

    rowx = lax.broadcasted_iota(jnp.int32, (N_EXPERTS, tb), 0)
    hit1 = rowx == e1
    hit2 = rowx == e2
    member = (hit1 | hit2).astype(BF16)
    before = (lax.broadcasted_iota(jnp.int32, (tb, tb), 0)
              < lax.broadcasted_iota(jnp.int32, (tb, tb), 1)).astype(BF16)
    carry = carry_ref[...]
    count = _dot(member, before) + carry[:, 0:1]
    r1 = jnp.sum(jnp.where(hit1, count, 0.0), axis=0, keepdims=True).astype(jnp.int32)
    r2 = jnp.sum(jnp.where(hit2, count, 0.0), axis=0, keepdims=True).astype(jnp.int32)
    carry = carry + jnp.sum(member.astype(F32), axis=1, keepdims=True)
    carry_ref[...] = carry
    cnt_ref[...] = carry

    row8 = lax.broadcasted_iota(jnp.int32, (8, tb), 0)
    idx_ref[...] = jnp.where(row8 == 0, e1, jnp.where(row8 == 1, e2,
                             jnp.where(row8 == 2, r1, jnp.where(row8 == 3, r2, 0))))
    gate_ref[...] = jnp.where(row8 == 0, gate1, jnp.where(row8 == 1, gate2, 0.0))


def _route(lt, bg, be, tb):
    nr, s = lt.shape
    return pl.pallas_call(
        _route_kernel,
        grid=(s // tb,),
        in_specs=[
            pl.BlockSpec((nr, tb), lambda i: (0, i)),
            pl.BlockSpec((8, 1), lambda i: (0, 0)),
            pl.BlockSpec((N_EXPERTS, 1), lambda i: (0, 0)),
        ],
        out_specs=[
            pl.BlockSpec((8, tb), lambda i: (0, i)),
            pl.BlockSpec((8, tb), lambda i: (0, i)),
            pl.BlockSpec((N_EXPERTS, LANES), lambda i: (0, 0)),
        ],
        out_shape=[
            jax.ShapeDtypeStruct((8, s), jnp.int32),
            jax.ShapeDtypeStruct((8, s), F32),
            jax.ShapeDtypeStruct((N_EXPERTS, LANES), F32),
        ],
        scratch_shapes=[pltpu.VMEM((N_EXPERTS, LANES), F32)],
        compiler_params=_cparams(("arbitrary",)),
        name="route",
    )(lt, bg, be)


def _expert_kernel(be_ref, nx_ref, nu_ref, dest_ref, neg_hbm, h2_hbm, wg_hbm, wu_hbm, wd_hbm, y_hbm,
                   pair_ref, h2v, xg, ystage, wgf, wuf, wdf, wgb, wub, wdb, hsem, wsem, ysem):
    n_used = nu_ref[0]
    rows = MOE_BLOCK
    n_tok = h2v.shape[0]

    def weight_copy(e, m):
        src, stage = ((wg_hbm, wgf), (wu_hbm, wuf), (wd_hbm, wdf))[m]
        return pltpu.make_async_copy(src.at[e], stage, wsem.at[m])

    def gather_row(blk, r, half):
        pair = jnp.maximum(pair_ref[blk * rows + r], 0)
        xg[half, r] = h2v[jnp.where(pair >= n_tok, pair - n_tok, pair)]

    def row_write(blk, r, half, dummy):
        pair = pair_ref[blk * rows + r]
        dst = jnp.where((pair < 0) | dummy, TOP_K * n_tok + half * rows + r, pair)
        return pltpu.make_async_copy(ystage.at[half, r], y_hbm.at[dst], ysem.at[half])

    def wait_row_writes(half):
        pltpu.make_async_copy(ystage.at[half], y_hbm.at[pl.ds(0, rows)], ysem.at[half]).wait()

    weight_priority = 1

    fill = pltpu.make_async_copy(neg_hbm, pair_ref, hsem.at[1])
    fill.start()
    h2_copy = pltpu.make_async_copy(h2_hbm, h2v, hsem.at[0])
    h2_copy.start()
    for m in range(3):
        weight_copy(be_ref[0], m).start(priority=weight_priority)
    ystage[...] = jnp.zeros_like(ystage)

    fill.wait()

    def place(p, carry):
        pair_ref[dest_ref[p]] = p
        return carry
    lax.fori_loop(0, dest_ref.shape[0], place, 0, unroll=8)

    h2_copy.wait()
    spare_fill = pltpu.make_async_copy(ystage.at[0], y_hbm.at[pl.ds(TOP_K * n_tok, rows)], hsem.at[0])
    spare_fill.start()
    spare_fill.wait()

    def first_gather(r, carry):
        gather_row(0, r, 0)
        return carry
    lax.fori_loop(0, rows, first_gather, 0)

    def block(b, carry):
        slot = b % 2
        first_of_expert = (b == 0) | (be_ref[b] != be_ref[jnp.maximum(b - 1, 0)])

        @pl.when(first_of_expert)
        def _():
            nxt = nx_ref[b]
            for m, (stage, dst) in enumerate(((wgf, wgb), (wuf, wub), (wdf, wdb))):
                weight_copy(be_ref[b], m).wait()
                dst[...] = stage[...].astype(BF16)

                @pl.when(nxt >= 0)
                def _():
                    weight_copy(nxt, m).start(priority=weight_priority)

        xb = xg[slot].reshape(rows, wgb.shape[0])
        prev_blk = jnp.maximum(b - 1, 0)
        for r in range(rows):
            row_write(prev_blk, r, 1 - slot, b == 0).start(priority=r % 2)
        for r in range(rows):
            gather_row(b + 1, r, 1 - slot)
        hg = _dot(xb, wgb[...])
        hu = _dot(xb, wub[...])
        act = (hg * _sigmoid(hg)) * hu
        y = _dot(act.astype(BF16), wdb[...]).astype(BF16)

        @pl.when(b >= 1)
        def _():
            wait_row_writes(slot)

        ystage[slot] = y.reshape(ystage.shape[1:])
        return carry

    lax.fori_loop(0, n_used, block, 0)

    last_half = (n_used - 1) % 2

    def last_writes(r, carry):
        row_write(n_used - 1, r, last_half, False).start()
        return carry
    lax.fori_loop(0, rows, last_writes, 0)
    wait_row_writes(1 - last_half)
    wait_row_writes(last_half)


def _experts(block_e, next_e, n_used, dest, h2c, wg, wu, wd):
    n_rows = block_e.shape[0] * MOE_BLOCK
    n_tok = h2c.shape[0]
    _, d, de = wg.shape
    slab = (d // LANES, LANES)
    grid_spec = pltpu.PrefetchScalarGridSpec(
        num_scalar_prefetch=4,
        grid=(1,),
        in_specs=[pl.BlockSpec(memory_space=pl.ANY)] * 5,
        out_specs=pl.BlockSpec(memory_space=pl.ANY),
        scratch_shapes=[
            pltpu.SMEM((n_rows,), jnp.int32),
            pltpu.VMEM((n_tok,) + slab, BF16),
            pltpu.VMEM((2, MOE_BLOCK) + slab, BF16),
            pltpu.VMEM((2, MOE_BLOCK) + slab, BF16),
            pltpu.VMEM((d, de), F32),
            pltpu.VMEM((d, de), F32),
            pltpu.VMEM((de, d), F32),
            pltpu.VMEM((d, de), BF16),
            pltpu.VMEM((d, de), BF16),
            pltpu.VMEM((de, d), BF16),
            pltpu.SemaphoreType.DMA((2,)),
            pltpu.SemaphoreType.DMA((3,)),
            pltpu.SemaphoreType.DMA((2,)),
        ],
    )
    padding_marks = jnp.full((n_rows,), -1, jnp.int32)
    return pl.pallas_call(
        _expert_kernel,
        grid_spec=grid_spec,
        out_shape=jax.ShapeDtypeStruct((TOP_K * n_tok + 2 * MOE_BLOCK,) + slab, BF16),
        compiler_params=_cparams(("arbitrary",), vmem=EXPERT_VMEM_LIMIT),
        name="experts",
    )(block_e, next_e, n_used, dest, padding_marks, h2c, wg, wu, wd)


def _combine_kernel(y0_ref, y1_ref, gate_ref, x1_ref, g_ref, o_ref):
    gates = gate_ref[...]
    y0 = y0_ref[...].reshape(x1_ref.shape).astype(F32)
    y1 = y1_ref[...].reshape(x1_ref.shape).astype(F32)
    x2 = x1_ref[...] + (y0 * gates[:, 0:1] + y1 * gates[:, 1:2])
    ms = jnp.mean(x2 * x2, axis=-1, keepdims=True)
    o_ref[...] = x2 * lax.rsqrt(ms + RMS_EPS) * g_ref[...]


def _combine(y2, gates, x1, g, tm):
    s, d = x1.shape
    slab = (d // LANES, LANES)
    n_steps = s // tm
    return pl.pallas_call(
        _combine_kernel,
        grid=(n_steps,),
        in_specs=[
            pl.BlockSpec((tm,) + slab, lambda i: (i, 0, 0)),
            pl.BlockSpec((tm,) + slab, lambda i: (n_steps + i, 0, 0)),
            pl.BlockSpec((tm, TOP_K), lambda i: (i, 0)),
            pl.BlockSpec((tm, d), lambda i: (i, 0)),
            pl.BlockSpec((1, d), lambda i: (0, 0)),
        ],
        out_specs=pl.BlockSpec((tm, d), lambda i: (i, 0)),
        out_shape=jax.ShapeDtypeStruct((s, d), F32),
        compiler_params=_cparams(("parallel",)),
        name="combine",
    )(y2, y2, gates, x1, g)


def _tile(n, pref):
    return pref if n % pref == 0 else n


def kernel(x, norm_mix_g, w_in, w_gla_a2, b_gla_a, gla_norm_g, attn_sinks, rel_bias_table, w_out,
           norm_ffn_g, w_router_group, b_router_group, w_router_expert, b_router_expert,
           w_expert_gate, w_expert_up, w_expert_down, norm_final_g):
    b, s, d = x.shape
    assert b == 1 and w_in.shape[0] == 1, "single batch, single layer"
    assert (s * TOP_K) % MOE_BLOCK == 0
    x2d = x.reshape(s, d)
    w_in_t = w_in[0].T

    w2 = jnp.pad(w_gla_a2[0], ((0, LANES - GLA_LOWRANK), (0, 0))).astype(BF16)
    h, log_a = _norm_loga(x2d, norm_mix_g[0].reshape(1, d), w_in_t, w2, b_gla_a[0].reshape(1, -1),
                          _tile(s, 1024))
    proj = _in_proj(h, w_in_t, _tile(s, 4096), 512)

    bucket, is_prev = _folded_maps()
    bias = _rel_bias(rel_bias_table.T, jnp.asarray(bucket), jnp.asarray(is_prev))
    o_attn = _swa(proj, attn_sinks[0], bias)
    o_gla = _gla(proj, log_a, gla_norm_g[0].reshape(1, -1), _tile(s, 512))

    wr = jnp.zeros((ROUTER_ROWS, d), F32)
    wr = wr.at[0:N_GROUPS].set(w_router_group[0].T).at[8:].set(w_router_expert[0].T).astype(BF16)
    x1, h2, logits_t = _merge_out(o_attn, o_gla, proj, x2d, w_out[0].astype(BF16),
                                  norm_ffn_g[0].reshape(1, d), wr, _tile(s, 512))

    bg = jnp.pad(b_router_group[0], (0, 8 - N_GROUPS)).reshape(8, 1)
    idx, gate, counts = _route(logits_t, bg, b_router_expert[0].reshape(N_EXPERTS, 1), _tile(s, 512))
    counts = counts[:, 0].astype(jnp.int32)
    padded = (counts + MOE_BLOCK - 1) // MOE_BLOCK * MOE_BLOCK
    pend = jnp.cumsum(padded)
    pstart = pend - padded
    expert = idx[0:TOP_K]
    eids = jnp.arange(N_EXPERTS, dtype=jnp.int32)[:, None, None]
    dest = jnp.sum(jnp.where(expert[None] == eids, pstart[:, None, None], 0), axis=0) + idx[TOP_K:2 * TOP_K]
    n_pairs = s * TOP_K
    n_blocks = (n_pairs + N_EXPERTS * (MOE_BLOCK - 1) + MOE_BLOCK - 1) // MOE_BLOCK
    n_used = (pend[-1] // MOE_BLOCK).reshape(1)
    block_start = jnp.arange(n_blocks, dtype=jnp.int32) * MOE_BLOCK
    block_e = jnp.minimum(jnp.sum((pend[None, :] <= block_start[:, None]).astype(jnp.int32), axis=1),
                          N_EXPERTS - 1)

    e_col = jnp.arange(N_EXPERTS, dtype=jnp.int32)[:, None]
    e_row = jnp.arange(N_EXPERTS, dtype=jnp.int32)[None, :]
    later = (e_row > e_col) & (counts > 0)[None, :]
    next_nonempty = jnp.min(jnp.where(later, e_row, N_EXPERTS), axis=1)
    next_nonempty = jnp.where(next_nonempty < N_EXPERTS, next_nonempty, -1)
    next_e = jnp.sum(jnp.where(block_e[:, None] == e_row, next_nonempty[None, :], 0), axis=1)
    y2 = _experts(block_e, next_e, n_used, dest.reshape(-1), h2,
                  w_expert_gate[0], w_expert_up[0], w_expert_down[0])
    out = _combine(y2, gate[0:TOP_K].T, x1, norm_final_g.reshape(1, d), _tile(s, 512))
    return out.reshape(b, s, d)
```

```python
import functools
import math

import numpy as np
import jax
import jax.numpy as jnp
from jax import lax
from jax.experimental import pallas as pl
from jax.experimental.pallas import tpu as pltpu

F32 = jnp.float32
BF16 = jnp.bfloat16

N_HEADS = 32
N_KV_HEADS = 4
HEAD_DIM = 64
WINDOW = 128
N_BUCKETS = 32
MAX_DISTANCE = 128
GLA_HEADS = 4
GLA_DK = 256
GLA_DV = 512
GLA_LOWRANK = 16
GLA_TAU = 16.0
GLA_CHUNK = 64
N_GROUPS = 4
EXPERTS_PER_GROUP = 8
N_EXPERTS = 32
TOP_K = 2
MOE_BLOCK = 256
RMS_EPS = 1e-6
NEG_INF = -1e30

COL_GA = 8704
COL_GATES = 8720
MAIN_COLS = 8704
GATE_COLS = 4096
COL_AQ = GATE_COLS + 0
COL_AK = GATE_COLS + 2048
COL_AV = GATE_COLS + 2304
COL_GQ = GATE_COLS + 2560
COL_GK = GATE_COLS + 3584
COL_GV = GATE_COLS + 4608
COL_GR = GATE_COLS + 6656

LANES = 128
VMEM_LIMIT = 56 * 1024 * 1024
EXPERT_VMEM_LIMIT = 60 * 1024 * 1024


def _cparams(sem, vmem=VMEM_LIMIT):
    return pltpu.CompilerParams(dimension_semantics=sem, vmem_limit_bytes=vmem)


def _split3(x):
    hi = x.astype(BF16)
    r1 = x - hi.astype(F32)
    mid = r1.astype(BF16)
    lo = (r1 - mid.astype(F32)).astype(BF16)
    return hi, mid, lo


def _sigmoid(x):
    return 0.5 * jnp.tanh(0.5 * x) + 0.5


def _dot(a, b):
    return jnp.dot(a, b, preferred_element_type=F32)


def _dot_nt(a, b):
    return lax.dot_general(a, b, (((1,), (1,)), ((), ())), preferred_element_type=F32)


def _dot_tn(a, b):
    return lax.dot_general(a, b, (((0,), (0,)), ((), ())), preferred_element_type=F32)


def _norm_loga_kernel(x_ref, g_ref, wga_ref, w2_ref, b_ref, h_ref, la_ref):
    x = x_ref[...]
    ms = jnp.mean(x * x, axis=-1, keepdims=True)
    hb = (x * lax.rsqrt(ms + RMS_EPS) * g_ref[...]).astype(BF16)
    h_ref[...] = hb
    row = lax.broadcasted_iota(jnp.int32, wga_ref.shape, 0)
    wga = jnp.where(row < GLA_LOWRANK, wga_ref[...], 0.0).astype(BF16)
    ga = _dot_nt(hb, wga)
    z = _dot(ga.astype(BF16), w2_ref[...]) + b_ref[...]
    la_ref[...] = (jnp.minimum(z, 0.0) - jnp.log1p(jnp.exp(-jnp.abs(z)))) * (1.0 / GLA_TAU)


def _norm_loga(x, g, wga, w2, b, tm):
    s, d = x.shape
    nq = w2.shape[1]
    return pl.pallas_call(
        _norm_loga_kernel,
        grid=(s // tm,),
        in_specs=[
            pl.BlockSpec((tm, d), lambda i: (i, 0)),
            pl.BlockSpec((1, d), lambda i: (0, 0)),
            pl.BlockSpec((LANES, d), lambda i: (COL_GA // LANES, 0)),
            pl.BlockSpec((LANES, nq), lambda i: (0, 0)),
            pl.BlockSpec((1, nq), lambda i: (0, 0)),
        ],
        out_specs=[
            pl.BlockSpec((tm, d), lambda i: (i, 0)),
            pl.BlockSpec((tm, nq), lambda i: (i, 0)),
        ],
        out_shape=[
            jax.ShapeDtypeStruct((s, d), BF16),
            jax.ShapeDtypeStruct((s, nq), F32),
        ],
        compiler_params=_cparams(("parallel",)),
        name="norm_loga",
    )(x, g, wga, w2, b)


def _in_proj_kernel(h_hbm, wt_hbm, o_ref, h_vmem, w_stage, wbf, sem, *, n_main, n_tiles, tn):
    j = pl.program_id(0)
    i = pl.program_id(1)
    n_i = pl.num_programs(1)
    tm = o_ref.shape[0]
    cur, nxt = j % 2, (j + 1) % 2

    def w_copy(jj):
        row0 = jnp.where(jj < n_main, jj * tn, COL_GATES + (jj - n_main) * tn)
        return pltpu.make_async_copy(wt_hbm.at[pl.ds(pl.multiple_of(row0, 8), tn)], w_stage.at[jj % 2],
                                     sem.at[1 + jj % 2])

    @pl.when((j == 0) & (i == 0))
    def _():
        h_copy = pltpu.make_async_copy(h_hbm, h_vmem, sem.at[0])
        h_copy.start()
        w_copy(0).start()
        w_copy(1).start()
        w_copy(0).wait()
        wbf[0] = w_stage[0].astype(BF16)
        if n_tiles > 2:
            w_copy(2).start()
        h_copy.wait()

    @pl.when(i == 0)
    def _():
        @pl.when(j + 1 < n_tiles)
        def _():
            w_copy(j + 1).wait()

        @pl.when((j >= 1) & (j + 2 < n_tiles))
        def _():
            w_copy(j + 2).start()

    h = h_vmem[pl.ds(pl.multiple_of(i * tm, tm), tm), :]
    o_ref[...] = _dot_nt(h, wbf[cur]).astype(o_ref.dtype)

    rows_per_step = tn // n_i
    rs = pl.ds(pl.multiple_of(i * rows_per_step, rows_per_step), rows_per_step)
    wbf[nxt, rs, :] = w_stage[nxt, rs, :].astype(BF16)


def _in_proj(h, w_t, tm, tn):
    s, d = h.shape
    n_main = MAIN_COLS // tn
    n_gate = GATE_COLS // tn
    n_tiles = n_main + n_gate
    kern = functools.partial(_in_proj_kernel, n_main=n_main, n_tiles=n_tiles, tn=tn)
    return pl.pallas_call(
        kern,
        grid=(n_tiles, s // tm),
        in_specs=[pl.BlockSpec(memory_space=pl.ANY), pl.BlockSpec(memory_space=pl.ANY)],
        out_specs=pl.BlockSpec((tm, tn), lambda j, i: (i, (j + n_gate) % n_tiles)),
        out_shape=jax.ShapeDtypeStruct((s, n_tiles * tn), BF16),
        scratch_shapes=[
            pltpu.VMEM((s, d), BF16),
            pltpu.VMEM((2, tn, d), F32),
            pltpu.VMEM((2, tn, d), BF16),
            pltpu.SemaphoreType.DMA((3,)),
        ],
        compiler_params=_cparams(("arbitrary", "arbitrary"), vmem=EXPERT_VMEM_LIMIT),
        name="in_proj",
    )(h, w_t)


def _folded_maps():
    j = np.arange(WINDOW)[:, None]
    i = np.arange(WINDOW)[None, :]
    n = (i - j) % WINDOW
    max_exact = N_BUCKETS // 2
    ratio = np.maximum(n, max_exact).astype(np.float32) / np.float32(max_exact)
    large = max_exact + (np.log(ratio) / np.float32(math.log(MAX_DISTANCE / max_exact))
                         * (N_BUCKETS - max_exact)).astype(np.int32)
    large = np.minimum(large, N_BUCKETS - 1)
    bucket = np.where(n < max_exact, n, large).astype(np.int32).reshape(1, -1)
    is_prev = (j > i).astype(np.int32).reshape(1, -1)
    return bucket, is_prev


def _rel_bias_kernel(tab_ref, bucket_ref, prev_ref, o_ref):
    nb = tab_ref.shape[1]
    width = bucket_ref.shape[1]
    onehot = (lax.broadcasted_iota(jnp.int32, (nb, width), 0) == bucket_ref[...]).astype(BF16)
    hi, mid, lo = _split3(tab_ref[...])
    bias = _dot(hi, onehot) + _dot(mid, onehot) + _dot(lo, onehot)
    tile = o_ref.shape[1:]
    o_ref[0] = bias.reshape(tile)
    o_ref[1] = jnp.where(prev_ref[...] > 0, NEG_INF, bias).reshape(tile)


def _rel_bias(table_t, bucket, is_prev):
    nh = table_t.shape[0]
    width = bucket.shape[1]
    return pl.pallas_call(
        _rel_bias_kernel,
        out_shape=jax.ShapeDtypeStruct((2, nh, WINDOW, WINDOW), F32),
        compiler_params=pltpu.CompilerParams(vmem_limit_bytes=VMEM_LIMIT),
        name="rel_bias",
    )(table_t, bucket, is_prev)


SWA_SCORES_AHEAD = 12
SWA_BLOCKS_PER_STEP = 8


def _swa_kernel(sink_ref, q_ref, kc_ref, kp_ref, vc_ref, vp_ref, bias0_ref, bias_ref, o_ref):
    w = WINDOW
    hd = HEAD_DIM
    gq = N_HEADS // N_KV_HEADS
    n_sub = q_ref.shape[0] // w
    is_prev = (lax.broadcasted_iota(jnp.int32, (w, w), 0) > lax.broadcasted_iota(jnp.int32, (w, w), 1))
    zeros = jnp.zeros((2 * w, hd), BF16)
    scale = HEAD_DIM ** -0.5
    k_pl, v_pl = {}, {}
    for sub in range(n_sub):
        for g in range(N_KV_HEADS):
            cs = slice(g * hd, (g + 1) * hd)
            if sub == 0:
                k_g = jnp.concatenate([kp_ref[:, cs], kc_ref[0:w, cs]], axis=0) * scale
                v_g = jnp.concatenate([vp_ref[:, cs], vc_ref[0:w, cs]], axis=0)
            else:
                k_g = kc_ref[(sub - 1) * w:(sub + 1) * w, cs] * scale
                v_g = vc_ref[(sub - 1) * w:(sub + 1) * w, cs]
            k_pl[sub, g] = (jnp.concatenate([k_g, zeros], axis=1), jnp.concatenate([zeros, k_g], axis=1))
            v_pl[sub, g] = (jnp.concatenate([v_g, zeros], axis=1), jnp.concatenate([zeros, v_g], axis=1))

    def scores(sub, hh):
        pair = hh - hh % 2
        return _dot_nt(k_pl[sub, hh // gq][hh % 2], q_ref[sub * w:(sub + 1) * w, pair * hd:(pair + 2) * hd])

    def attend(sub, hh, st2):
        bias = bias0_ref if sub == 0 else bias_ref
        st = jnp.where(is_prev, st2[:w], st2[w:]) + bias[hh]
        sink = sink_ref[hh]
        m = jnp.maximum(jnp.max(st, axis=0, keepdims=True), sink)
        p = jnp.exp(st - m)
        recip = 1.0 / (jnp.sum(p, axis=0, keepdims=True) + jnp.exp(sink - m))
        probs = p * recip
        p_cat = jnp.concatenate([jnp.where(is_prev, probs, 0.0), jnp.where(is_prev, 0.0, probs)], axis=0)
        return _dot_tn(p_cat.astype(BF16), v_pl[sub, hh // gq][hh % 2])

    chains = [(sub, hh) for sub in range(n_sub) for hh in range(N_HEADS)]
    ahead = SWA_SCORES_AHEAD
    st2 = {c: scores(*c) for c in chains[:ahead]}
    acc = None
    for n, (sub, hh) in enumerate(chains):
        if n + ahead < len(chains):
            st2[chains[n + ahead]] = scores(*chains[n + ahead])
        part = attend(sub, hh, st2.pop((sub, hh)))
        if hh % 2 == 0:
            acc = part
        else:
            o_ref[sub * w:(sub + 1) * w, (hh - 1) * hd:(hh + 1) * hd] = (acc + part).astype(o_ref.dtype)


def _swa(proj, sinks, bias):
    s = proj.shape[0]
    w = WINDOW
    nb = s // w
    n_sub = SWA_BLOCKS_PER_STEP if nb % SWA_BLOCKS_PER_STEP == 0 else 1
    kvw = N_KV_HEADS * HEAD_DIM
    dq = N_HEADS * HEAD_DIM
    rows = n_sub * w
    prev = lambda n: jnp.maximum(n * n_sub - 1, 0)
    return pl.pallas_call(
        _swa_kernel,
        grid=(nb // n_sub,),
        in_specs=[
            pl.BlockSpec(memory_space=pltpu.SMEM),
            pl.BlockSpec((rows, dq), lambda n: (n, COL_AQ // dq)),
            pl.BlockSpec((rows, kvw), lambda n: (n, COL_AK // kvw)),
            pl.BlockSpec((w, kvw), lambda n: (prev(n), COL_AK // kvw)),
            pl.BlockSpec((rows, kvw), lambda n: (n, COL_AV // kvw)),
            pl.BlockSpec((w, kvw), lambda n: (prev(n), COL_AV // kvw)),
            pl.BlockSpec((None, N_HEADS, w, w), lambda n: (jnp.where(n == 0, 1, 0), 0, 0, 0)),
            pl.BlockSpec((None, N_HEADS, w, w), lambda n: (0, 0, 0, 0)),
        ],
        out_specs=pl.BlockSpec((rows, dq), lambda n: (n, 0)),
        out_shape=jax.ShapeDtypeStruct((s, dq), BF16),
        compiler_params=_cparams(("parallel",)),
        name="swa",
    )(sinks, proj, proj, proj, proj, proj, bias, bias)


GLA_CUM_ROWS = 256


def _gla_kernel(*refs):
    nh = GLA_HEADS
    q_refs, k_refs, v_refs, gr_refs = (refs[i * nh:(i + 1) * nh] for i in range(4))
    la_ref, gn_ref, o_ref, state_ref = refs[4 * nh:]
    c = GLA_CHUNK
    rows, dk = q_refs[0].shape
    dv = v_refs[0].shape[1]

    @pl.when(pl.program_id(0) == 0)
    def _():
        state_ref[...] = jnp.zeros_like(state_ref)

    cr = min(rows, GLA_CUM_ROWS)
    ri = lax.broadcasted_iota(jnp.int32, (cr, cr), 0)
    ci = lax.broadcasted_iota(jnp.int32, (cr, cr), 1)
    tri = ((ri // c == ci // c) & (ri >= ci)).astype(BF16)
    cum_parts = []
    for t in range(rows // cr):
        la_parts = _split3(la_ref[t * cr:(t + 1) * cr, :])
        cum_parts.append(_dot(tri, la_parts[0]) + _dot(tri, la_parts[1]) + _dot(tri, la_parts[2]))
    cum_all = jnp.concatenate(cum_parts, axis=0)
    causal = (lax.broadcasted_iota(jnp.int32, (c, c), 0) >= lax.broadcasted_iota(jnp.int32, (c, c), 1))
    gn = gn_ref[...]
    n_chunks = rows // c

    o_intra, q_in, upd, decay = {}, {}, {}, {}
    for h in range(nh):
        for ch in range(n_chunks):
            sl = slice(ch * c, (ch + 1) * c)
            cum_c = cum_all[sl, h * dk:(h + 1) * dk]
            ref = cum_c[c // 2 - 1:c // 2]
            last = cum_c[c - 1:c]
            q = q_refs[h][sl, :].astype(F32) * (dk ** -0.5)
            k = k_refs[h][sl, :].astype(F32)
            v = v_refs[h][sl, :]
            q_intra = (q * jnp.exp(cum_c - ref)).astype(BF16)
            k_intra = (k * jnp.exp(ref - cum_c)).astype(BF16)
            a = jnp.where(causal, _dot_nt(q_intra, k_intra), 0.0)
            o_intra[h, ch] = _dot(a.astype(BF16), v)
            q_in[h, ch] = (q * jnp.exp(cum_c)).astype(BF16)
            k_dec = (k * jnp.exp(last - cum_c)).astype(BF16)
            upd[h, ch] = _dot_tn(k_dec, v)
            decay[h, ch] = jnp.broadcast_to(jnp.exp(last), (LANES, dk)).T

    state = [state_ref[h] for h in range(nh)]
    for ch in range(n_chunks):
        sl = slice(ch * c, (ch + 1) * c)
        for h in range(nh):
            o = o_intra[h, ch] + _dot(q_in[h, ch], state[h].astype(BF16))
            state[h] = jnp.concatenate(
                [state[h][:, jv * LANES:(jv + 1) * LANES] * decay[h, ch]
                 + upd[h, ch][:, jv * LANES:(jv + 1) * LANES] for jv in range(dv // LANES)], axis=1)
            ms = jnp.mean(o * o, axis=-1, keepdims=True)
            on = o * lax.rsqrt(ms + RMS_EPS) * gn
            gr = gr_refs[h][sl, :].astype(F32)
            o_ref[sl, h * dv:(h + 1) * dv] = (on * (gr * _sigmoid(gr))).astype(o_ref.dtype)
    for h in range(nh):
        state_ref[h] = state[h]


def _gla(proj, la, gn, rows):
    s = proj.shape[0]
    dk, dv = GLA_DK, GLA_DV
    nh = GLA_HEADS

    def head_specs(width, col0):
        return [pl.BlockSpec((rows, width), functools.partial(lambda i, h: (i, col0 // width + h), h=h))
                for h in range(nh)]

    in_specs = (head_specs(dk, COL_GQ) + head_specs(dk, COL_GK) + head_specs(dv, COL_GV)
                + head_specs(dv, COL_GR)
                + [pl.BlockSpec((rows, nh * dk), lambda i: (i, 0)), pl.BlockSpec((1, dv), lambda i: (0, 0))])
    return pl.pallas_call(
        _gla_kernel,
        grid=(s // rows,),
        in_specs=in_specs,
        out_specs=pl.BlockSpec((rows, nh * dv), lambda i: (i, 0)),
        out_shape=jax.ShapeDtypeStruct((s, nh * dv), BF16),
        scratch_shapes=[pltpu.VMEM((nh, dk, dv), F32)],
        compiler_params=_cparams(("arbitrary",)),
        name="gla",
    )(*([proj] * (4 * nh)), la, gn)


MERGE_SUB_ROWS = 128


def _merge_out_kernel(oa_ref, og_ref, ga_ref, gg_ref, x_ref, wo_ref, g_ref, wr_ref,
                      x1_ref, h2_ref, lt_ref):
    sub = MERGE_SUB_ROWS
    tiles = [slice(t * sub, (t + 1) * sub) for t in range(x_ref.shape[0] // sub)]
    merged = [(_sigmoid(ga_ref[rs, :].astype(F32)) * oa_ref[rs, :].astype(F32)
               + _sigmoid(gg_ref[rs, :].astype(F32)) * og_ref[rs, :].astype(F32)).astype(BF16) for rs in tiles]
    projected = [_dot(m, wo_ref[...]) for m in merged]
    for rs, y in zip(tiles, projected):
        x1 = x_ref[rs, :] + y
        x1_ref[rs, :] = x1
        ms = jnp.mean(x1 * x1, axis=-1, keepdims=True)
        h2b = (x1 * lax.rsqrt(ms + RMS_EPS) * g_ref[...]).astype(BF16)
        h2_ref[rs] = h2b.reshape((sub,) + h2_ref.shape[1:])
        lt_ref[:, rs] = _dot_nt(wr_ref[...], h2b)


def _merge_out(o_attn, o_gla, gates, x, wo, g, wr, tm):
    s, d = x.shape
    nr = wr.shape[0]
    row = lambda i: (i, 0)
    return pl.pallas_call(
        _merge_out_kernel,
        grid=(s // tm,),
        in_specs=[
            pl.BlockSpec((tm, d), row),
            pl.BlockSpec((tm, d), row),
            pl.BlockSpec((tm, d), lambda i: (i, 0)),
            pl.BlockSpec((tm, d), lambda i: (i, 1)),
            pl.BlockSpec((tm, d), row),
            pl.BlockSpec((d, d), lambda i: (0, 0), pipeline_mode=pl.Buffered(1)),
            pl.BlockSpec((1, d), lambda i: (0, 0)),
            pl.BlockSpec((nr, d), lambda i: (0, 0)),
        ],
        out_specs=[
            pl.BlockSpec((tm, d), row),
            pl.BlockSpec((tm, d // LANES, LANES), lambda i: (i, 0, 0)),
            pl.BlockSpec((nr, tm), lambda i: (0, i)),
        ],
        out_shape=[
            jax.ShapeDtypeStruct((s, d), F32),
            jax.ShapeDtypeStruct((s, d // LANES, LANES), BF16),
            jax.ShapeDtypeStruct((nr, s), F32),
        ],
        compiler_params=_cparams(("parallel",)),
        name="merge_out",
    )(o_attn, o_gla, gates, gates, x, wo, g, wr)


ROUTER_ROWS = 8 + N_EXPERTS


def _route_kernel(lt_ref, bg_ref, be_ref, idx_ref, gate_ref, cnt_ref, carry_ref):
    tb = lt_ref.shape[1]
    epg = EXPERTS_PER_GROUP

    @pl.when(pl.program_id(0) == 0)
    def _():
        carry_ref[...] = jnp.zeros_like(carry_ref)

    logits = lt_ref[...]
    gl = logits[0:N_GROUPS]
    g_exp = jnp.exp(gl - jnp.max(gl, axis=0, keepdims=True))
    g_prob = g_exp / jnp.sum(g_exp, axis=0, keepdims=True)
    gb = gl + bg_ref[0:N_GROUPS]
    rowg = lax.broadcasted_iota(jnp.int32, (N_GROUPS, tb), 0)
    g_idx = jnp.min(jnp.where(gb == jnp.max(gb, axis=0, keepdims=True), rowg, N_GROUPS),
                    axis=0, keepdims=True)
    p_group = jnp.sum(jnp.where(rowg == g_idx, g_prob, 0.0), axis=0, keepdims=True)

    el = logits[8:8 + epg]
    eb = jnp.broadcast_to(be_ref[0:epg], (epg, tb))
    for g in range(1, N_GROUPS):
        pick = g_idx == g
        el = jnp.where(pick, logits[8 + g * epg:8 + (g + 1) * epg], el)
        eb = jnp.where(pick, be_ref[g * epg:(g + 1) * epg], eb)
    e_exp = jnp.exp(el - jnp.max(el, axis=0, keepdims=True))
    e_prob = e_exp / jnp.sum(e_exp, axis=0, keepdims=True)
    score = el + eb
    rowe = lax.broadcasted_iota(jnp.int32, (epg, tb), 0)
    i1 = jnp.min(jnp.where(score == jnp.max(score, axis=0, keepdims=True), rowe, epg),
                 axis=0, keepdims=True)
    score2 = jnp.where(rowe == i1, -jnp.inf, score)
    i2 = jnp.min(jnp.where(score2 == jnp.max(score2, axis=0, keepdims=True), rowe, epg),
                 axis=0, keepdims=True)
    q1 = jnp.sum(jnp.where(rowe == i1, e_prob, 0.0), axis=0, keepdims=True)
    q2 = jnp.sum(jnp.where(rowe == i2, e_prob, 0.0), axis=0, keepdims=True)
    qs = q1 + q2
    gate1 = p_group * q1 / qs
    gate2 = p_group * q2 / qs
    e1 = g_idx * epg + i1
    e2 = g_idx * epg + i2

    rowx = lax.broadcasted_iota(jnp.int32, (N_EXPERTS, tb), 0)
    hit1 = rowx == e1
    hit2 = rowx == e2
    member = (hit1 | hit2).astype(BF16)
    before = (lax.broadcasted_iota(jnp.int32, (tb, tb), 0)
              < lax.broadcasted_iota(jnp.int32, (tb, tb), 1)).astype(BF16)
    carry = carry_ref[...]
    count = _dot(member, before) + carry[:, 0:1]
    r1 = jnp.sum(jnp.where(hit1, count, 0.0), axis=0, keepdims=True).astype(jnp.int32)
    r2 = jnp.sum(jnp.where(hit2, count, 0.0), axis=0, keepdims=True).astype(jnp.int32)
    carry = carry + jnp.sum(member.astype(F32), axis=1, keepdims=True)
    carry_ref[...] = carry
    cnt_ref[...] = carry

    row8 = lax.broadcasted_iota(jnp.int32, (8, tb), 0)
    idx_ref[...] = jnp.where(row8 == 0, e1, jnp.where(row8 == 1, e2,
                             jnp.where(row8 == 2, r1, jnp.where(row8 == 3, r2, 0))))
    gate_ref[...] = jnp.where(row8 == 0, gate1, jnp.where(row8 == 1, gate2, 0.0))


def _route(lt, bg, be, tb):
    nr, s = lt.shape
    return pl.pallas_call(
        _route_kernel,
        grid=(s // tb,),
        in_specs=[
            pl.BlockSpec((nr, tb), lambda i: (0, i)),
            pl.BlockSpec((8, 1), lambda i: (0, 0)),
            pl.BlockSpec((N_EXPERTS, 1), lambda i: (0, 0)),
        ],
        out_specs=[
            pl.BlockSpec((8, tb), lambda i: (0, i)),
            pl.BlockSpec((8, tb), lambda i: (0, i)),
            pl.BlockSpec((N_EXPERTS, LANES), lambda i: (0, 0)),
        ],
        out_shape=[
            jax.ShapeDtypeStruct((8, s), jnp.int32),
            jax.ShapeDtypeStruct((8, s), F32),
            jax.ShapeDtypeStruct((N_EXPERTS, LANES), F32),
        ],
        scratch_shapes=[pltpu.VMEM((N_EXPERTS, LANES), F32)],
        compiler_params=_cparams(("arbitrary",)),
        name="route",
    )(lt, bg, be)


def _expert_kernel(be_ref, nx_ref, nu_ref, dest_ref, neg_hbm, h2_hbm, wg_hbm, wu_hbm, wd_hbm, y_hbm,
                   pair_ref, h2v, xg, ystage, wgf, wuf, wdf, wgb, wub, wdb, hsem, wsem, ysem):
    n_used = nu_ref[0]
    rows = MOE_BLOCK
    n_tok = h2v.shape[0]

    def weight_copy(e, m):
        src, stage = ((wg_hbm, wgf), (wu_hbm, wuf), (wd_hbm, wdf))[m]
        return pltpu.make_async_copy(src.at[e], stage, wsem.at[m])

    def gather_row(blk, r, half):
        pair = jnp.maximum(pair_ref[blk * rows + r], 0)
        xg[half, r] = h2v[jnp.where(pair >= n_tok, pair - n_tok, pair)]

    def row_write(blk, r, half, dummy):
        pair = pair_ref[blk * rows + r]
        dst = jnp.where((pair < 0) | dummy, TOP_K * n_tok + half * rows + r, pair)
        return pltpu.make_async_copy(ystage.at[half, r], y_hbm.at[dst], ysem.at[half])

    def wait_row_writes(half):
        pltpu.make_async_copy(ystage.at[half], y_hbm.at[pl.ds(0, rows)], ysem.at[half]).wait()

    weight_priority = 1

    fill = pltpu.make_async_copy(neg_hbm, pair_ref, hsem.at[1])
    fill.start()
    h2_copy = pltpu.make_async_copy(h2_hbm, h2v, hsem.at[0])
    h2_copy.start()
    for m in range(3):
        weight_copy(be_ref[0], m).start(priority=weight_priority)
    ystage[...] = jnp.zeros_like(ystage)

    fill.wait()

    def place(p, carry):
        pair_ref[dest_ref[p]] = p
        return carry
    lax.fori_loop(0, dest_ref.shape[0], place, 0, unroll=8)

    h2_copy.wait()
    spare_fill = pltpu.make_async_copy(ystage.at[0], y_hbm.at[pl.ds(TOP_K * n_tok, rows)], hsem.at[0])
    spare_fill.start()
    spare_fill.wait()

    def first_gather(r, carry):
        gather_row(0, r, 0)
        return carry
    lax.fori_loop(0, rows, first_gather, 0)

    def block(b, carry):
        slot = b % 2
        first_of_expert = (b == 0) | (be_ref[b] != be_ref[jnp.maximum(b - 1, 0)])

        @pl.when(first_of_expert)
        def _():
            nxt = nx_ref[b]
            for m, (stage, dst) in enumerate(((wgf, wgb), (wuf, wub), (wdf, wdb))):
                weight_copy(be_ref[b], m).wait()
                dst[...] = stage[...].astype(BF16)

                @pl.when(nxt >= 0)
                def _():
                    weight_copy(nxt, m).start(priority=weight_priority)

        xb = xg[slot].reshape(rows, wgb.shape[0])
        prev_blk = jnp.maximum(b - 1, 0)
        for r in range(rows):
            row_write(prev_blk, r, 1 - slot, b == 0).start(priority=r % 2)
        for r in range(rows):
            gather_row(b + 1, r, 1 - slot)
        hg = _dot(xb, wgb[...])
        hu = _dot(xb, wub[...])
        act = (hg * _sigmoid(hg)) * hu
        y = _dot(act.astype(BF16), wdb[...]).astype(BF16)

        @pl.when(b >= 1)
        def _():
            wait_row_writes(slot)

        ystage[slot] = y.reshape(ystage.shape[1:])
        return carry

    lax.fori_loop(0, n_used, block, 0)

    last_half = (n_used - 1) % 2

    def last_writes(r, carry):
        row_write(n_used - 1, r, last_half, False).start()
        return carry
    lax.fori_loop(0, rows, last_writes, 0)
    wait_row_writes(1 - last_half)
    wait_row_writes(last_half)


def _experts(block_e, next_e, n_used, dest, h2c, wg, wu, wd):
    n_rows = block_e.shape[0] * MOE_BLOCK
    n_tok = h2c.shape[0]
    _, d, de = wg.shape
    slab = (d // LANES, LANES)
    grid_spec = pltpu.PrefetchScalarGridSpec(
        num_scalar_prefetch=4,
        grid=(1,),
        in_specs=[pl.BlockSpec(memory_space=pl.ANY)] * 5,
        out_specs=pl.BlockSpec(memory_space=pl.ANY),
        scratch_shapes=[
            pltpu.SMEM((n_rows,), jnp.int32),
            pltpu.VMEM((n_tok,) + slab, BF16),
            pltpu.VMEM((2, MOE_BLOCK) + slab, BF16),
            pltpu.VMEM((2, MOE_BLOCK) + slab, BF16),
            pltpu.VMEM((d, de), F32),
            pltpu.VMEM((d, de), F32),
            pltpu.VMEM((de, d), F32),
            pltpu.VMEM((d, de), BF16),
            pltpu.VMEM((d, de), BF16),
            pltpu.VMEM((de, d), BF16),
            pltpu.SemaphoreType.DMA((2,)),
            pltpu.SemaphoreType.DMA((3,)),
            pltpu.SemaphoreType.DMA((2,)),
        ],
    )
    padding_marks = jnp.full((n_rows,), -1, jnp.int32)
    return pl.pallas_call(
        _expert_kernel,
        grid_spec=grid_spec,
        out_shape=jax.ShapeDtypeStruct((TOP_K * n_tok + 2 * MOE_BLOCK,) + slab, BF16),
        compiler_params=_cparams(("arbitrary",), vmem=EXPERT_VMEM_LIMIT),
        name="experts",
    )(block_e, next_e, n_used, dest, padding_marks, h2c, wg, wu, wd)


def _combine_kernel(y0_ref, y1_ref, gate_ref, x1_ref, g_ref, o_ref):
    gates = gate_ref[...]
    y0 = y0_ref[...].reshape(x1_ref.shape).astype(F32)
    y1 = y1_ref[...].reshape(x1_ref.shape).astype(F32)
    x2 = x1_ref[...] + (y0 * gates[:, 0:1] + y1 * gates[:, 1:2])
    ms = jnp.mean(x2 * x2, axis=-1, keepdims=True)
    o_ref[...] = x2 * lax.rsqrt(ms + RMS_EPS) * g_ref[...]


def _combine(y2, gates, x1, g, tm):
    s, d = x1.shape
    slab = (d // LANES, LANES)
    n_steps = s // tm
    return pl.pallas_call(
        _combine_kernel,
        grid=(n_steps,),
        in_specs=[
            pl.BlockSpec((tm,) + slab, lambda i: (i, 0, 0)),
            pl.BlockSpec((tm,) + slab, lambda i: (n_steps + i, 0, 0)),
            pl.BlockSpec((tm, TOP_K), lambda i: (i, 0)),
            pl.BlockSpec((tm, d), lambda i: (i, 0)),
            pl.BlockSpec((1, d), lambda i: (0, 0)),
        ],
        out_specs=pl.BlockSpec((tm, d), lambda i: (i, 0)),
        out_shape=jax.ShapeDtypeStruct((s, d), F32),
        compiler_params=_cparams(("parallel",)),
        name="combine",
    )(y2, y2, gates, x1, g)


def _tile(n, pref):
    return pref if n % pref == 0 else n


def kernel(x, norm_mix_g, w_in, w_gla_a2, b_gla_a, gla_norm_g, attn_sinks, rel_bias_table, w_out,
           norm_ffn_g, w_router_group, b_router_group, w_router_expert, b_router_expert,
           w_expert_gate, w_expert_up, w_expert_down, norm_final_g):
    b, s, d = x.shape
    assert b == 1 and w_in.shape[0] == 1, "single batch, single layer"
    assert (s * TOP_K) % MOE_BLOCK == 0
    x2d = x.reshape(s, d)
    w_in_t = w_in[0].T

    w2 = jnp.pad(w_gla_a2[0], ((0, LANES - GLA_LOWRANK), (0, 0))).astype(BF16)
    h, log_a = _norm_loga(x2d, norm_mix_g[0].reshape(1, d), w_in_t, w2, b_gla_a[0].reshape(1, -1),
                          _tile(s, 1024))
    proj = _in_proj(h, w_in_t, _tile(s, 4096), 512)

    bucket, is_prev = _folded_maps()
    bias = _rel_bias(rel_bias_table.T, jnp.asarray(bucket), jnp.asarray(is_prev))
    o_attn = _swa(proj, attn_sinks[0], bias)
    o_gla = _gla(proj, log_a, gla_norm_g[0].reshape(1, -1), _tile(s, 256))

    wr = jnp.zeros((ROUTER_ROWS, d), F32)
    wr = wr.at[0:N_GROUPS].set(w_router_group[0].T).at[8:].set(w_router_expert[0].T).astype(BF16)
    x1, h2, logits_t = _merge_out(o_attn, o_gla, proj, x2d, w_out[0].astype(BF16),
                                  norm_ffn_g[0].reshape(1, d), wr, _tile(s, 512))

    bg = jnp.pad(b_router_group[0], (0, 8 - N_GROUPS)).reshape(8, 1)
    idx, gate, counts = _route(logits_t, bg, b_router_expert[0].reshape(N_EXPERTS, 1), _tile(s, 512))
    counts = counts[:, 0].astype(jnp.int32)
    padded = (counts + MOE_BLOCK - 1) // MOE_BLOCK * MOE_BLOCK
    pend = jnp.cumsum(padded)
    pstart = pend - padded
    expert = idx[0:TOP_K]
    eids = jnp.arange(N_EXPERTS, dtype=jnp.int32)[:, None, None]
    dest = jnp.sum(jnp.where(expert[None] == eids, pstart[:, None, None], 0), axis=0) + idx[TOP_K:2 * TOP_K]
    n_pairs = s * TOP_K
    n_blocks = (n_pairs + N_EXPERTS * (MOE_BLOCK - 1) + MOE_BLOCK - 1) // MOE_BLOCK
    n_used = (pend[-1] // MOE_BLOCK).reshape(1)
    block_start = jnp.arange(n_blocks, dtype=jnp.int32) * MOE_BLOCK
    block_e = jnp.minimum(jnp.sum((pend[None, :] <= block_start[:, None]).astype(jnp.int32), axis=1),
                          N_EXPERTS - 1)

    e_col = jnp.arange(N_EXPERTS, dtype=jnp.int32)[:, None]
    e_row = jnp.arange(N_EXPERTS, dtype=jnp.int32)[None, :]
    later = (e_row > e_col) & (counts > 0)[None, :]
    next_nonempty = jnp.min(jnp.where(later, e_row, N_EXPERTS), axis=1)
    next_nonempty = jnp.where(next_nonempty < N_EXPERTS, next_nonempty, -1)
    next_e = jnp.sum(jnp.where(block_e[:, None] == e_row, next_nonempty[None, :], 0), axis=1)
    y2 = _experts(block_e, next_e, n_used, dest.reshape(-1), h2,
                  w_expert_gate[0], w_expert_up[0], w_expert_down[0])
    out = _combine(y2, gate[0:TOP_K].T, x1, norm_final_g.reshape(1, d), _tile(s, 512))
    return out.reshape(b, s, d)
```

```python
import functools
import math

import numpy as np
import jax
import jax.numpy as jnp
from jax import lax
from jax.experimental import pallas as pl
from jax.experimental.pallas import tpu as pltpu

F32 = jnp.float32
BF16 = jnp.bfloat16

N_HEADS = 32
N_KV_HEADS = 4
HEAD_DIM = 64
WINDOW = 128
N_BUCKETS = 32
MAX_DISTANCE = 128
GLA_HEADS = 4
GLA_DK = 256
GLA_DV = 512
GLA_LOWRANK = 16
GLA_TAU = 16.0
GLA_CHUNK = 64
N_GROUPS = 4
EXPERTS_PER_GROUP = 8
N_EXPERTS = 32
TOP_K = 2
MOE_BLOCK = 256
RMS_EPS = 1e-6
NEG_INF = -1e30

COL_GA = 8704
COL_GATES = 8720
MAIN_COLS = 8704
GATE_COLS = 4096
COL_AQ = GATE_COLS + 0
COL_AK = GATE_COLS + 2048
COL_AV = GATE_COLS + 2304
COL_GQ = GATE_COLS + 2560
COL_GK = GATE_COLS + 3584
COL_GV = GATE_COLS + 4608
COL_GR = GATE_COLS + 6656

LANES = 128
VMEM_LIMIT = 56 * 1024 * 1024
RESIDENT_VMEM_LIMIT = 60 * 1024 * 1024


def _cparams(sem, vmem=VMEM_LIMIT):
    return pltpu.CompilerParams(dimension_semantics=sem, vmem_limit_bytes=vmem)


def _split3(x):
    hi = x.astype(BF16)
    r1 = x - hi.astype(F32)
    mid = r1.astype(BF16)
    lo = (r1 - mid.astype(F32)).astype(BF16)
    return hi, mid, lo


def _sigmoid(x):
    return 0.5 * jnp.tanh(0.5 * x) + 0.5


def _dot(a, b):
    return jnp.dot(a, b, preferred_element_type=F32)


def _dot_nt(a, b):
    return lax.dot_general(a, b, (((1,), (1,)), ((), ())), preferred_element_type=F32)


def _dot_tn(a, b):
    return lax.dot_general(a, b, (((0,), (0,)), ((), ())), preferred_element_type=F32)


def _norm_loga_kernel(x_ref, g_ref, wga_ref, w2_ref, b_ref, h_ref, la_ref):
    x = x_ref[...]
    ms = jnp.mean(x * x, axis=-1, keepdims=True)
    hb = (x * lax.rsqrt(ms + RMS_EPS) * g_ref[...]).astype(BF16)
    h_ref[...] = hb
    row = lax.broadcasted_iota(jnp.int32, wga_ref.shape, 0)
    wga = jnp.where(row < GLA_LOWRANK, wga_ref[...], 0.0).astype(BF16)
    ga = _dot_nt(hb, wga)
    z = _dot(ga.astype(BF16), w2_ref[...]) + b_ref[...]
    la_ref[...] = (jnp.minimum(z, 0.0) - jnp.log1p(jnp.exp(-jnp.abs(z)))) * (1.0 / GLA_TAU)


def _norm_loga(x, g, wga, w2, b, tm):
    s, d = x.shape
    nq = w2.shape[1]
    return pl.pallas_call(
        _norm_loga_kernel,
        grid=(s // tm,),
        in_specs=[
            pl.BlockSpec((tm, d), lambda i: (i, 0)),
            pl.BlockSpec((1, d), lambda i: (0, 0)),
            pl.BlockSpec((LANES, d), lambda i: (COL_GA // LANES, 0)),
            pl.BlockSpec((LANES, nq), lambda i: (0, 0)),
            pl.BlockSpec((1, nq), lambda i: (0, 0)),
        ],
        out_specs=[
            pl.BlockSpec((tm, d), lambda i: (i, 0)),
            pl.BlockSpec((tm, nq), lambda i: (i, 0)),
        ],
        out_shape=[
            jax.ShapeDtypeStruct((s, d), BF16),
            jax.ShapeDtypeStruct((s, nq), F32),
        ],
        compiler_params=_cparams(("parallel",)),
        name="norm_loga",
    )(x, g, wga, w2, b)


def _in_proj_kernel(h_hbm, wt_hbm, o_ref, h_vmem, w_stage, wbf, sem, *, n_main, n_tiles, tn):
    j = pl.program_id(0)
    i = pl.program_id(1)
    n_i = pl.num_programs(1)
    tm = o_ref.shape[0]
    cur, nxt = j % 2, (j + 1) % 2

    def w_copy(jj):
        row0 = jnp.where(jj < n_main, jj * tn, COL_GATES + (jj - n_main) * tn)
        return pltpu.make_async_copy(wt_hbm.at[pl.ds(pl.multiple_of(row0, 8), tn)], w_stage.at[jj % 2],
                                     sem.at[1 + jj % 2])

    @pl.when((j == 0) & (i == 0))
    def _():
        h_copy = pltpu.make_async_copy(h_hbm, h_vmem, sem.at[0])
        h_copy.start()
        w_copy(0).start()
        w_copy(1).start()
        w_copy(0).wait()
        wbf[0] = w_stage[0].astype(BF16)
        if n_tiles > 2:
            w_copy(2).start()
        h_copy.wait()

    @pl.when(i == 0)
    def _():
        @pl.when(j + 1 < n_tiles)
        def _():
            w_copy(j + 1).wait()

        @pl.when((j >= 1) & (j + 2 < n_tiles))
        def _():
            w_copy(j + 2).start()

    h = h_vmem[pl.ds(pl.multiple_of(i * tm, tm), tm), :]
    o_ref[...] = _dot_nt(h, wbf[cur]).astype(o_ref.dtype)

    rows_per_step = tn // n_i
    rs = pl.ds(pl.multiple_of(i * rows_per_step, rows_per_step), rows_per_step)
    wbf[nxt, rs, :] = w_stage[nxt, rs, :].astype(BF16)


def _in_proj(h, w_t, tm, tn):
    s, d = h.shape
    n_main = MAIN_COLS // tn
    n_gate = GATE_COLS // tn
    n_tiles = n_main + n_gate
    kern = functools.partial(_in_proj_kernel, n_main=n_main, n_tiles=n_tiles, tn=tn)
    return pl.pallas_call(
        kern,
        grid=(n_tiles, s // tm),
        in_specs=[pl.BlockSpec(memory_space=pl.ANY), pl.BlockSpec(memory_space=pl.ANY)],
        out_specs=pl.BlockSpec((tm, tn), lambda j, i: (i, (j + n_gate) % n_tiles)),
        out_shape=jax.ShapeDtypeStruct((s, n_tiles * tn), BF16),
        scratch_shapes=[
            pltpu.VMEM((s, d), BF16),
            pltpu.VMEM((2, tn, d), F32),
            pltpu.VMEM((2, tn, d), BF16),
            pltpu.SemaphoreType.DMA((3,)),
        ],
        compiler_params=_cparams(("arbitrary", "arbitrary"), vmem=RESIDENT_VMEM_LIMIT),
        name="in_proj",
    )(h, w_t)


def _folded_maps():
    j = np.arange(WINDOW)[:, None]
    i = np.arange(WINDOW)[None, :]
    n = (i - j) % WINDOW
    max_exact = N_BUCKETS // 2
    ratio = np.maximum(n, max_exact).astype(np.float32) / np.float32(max_exact)
    large = max_exact + (np.log(ratio) / np.float32(math.log(MAX_DISTANCE / max_exact))
                         * (N_BUCKETS - max_exact)).astype(np.int32)
    large = np.minimum(large, N_BUCKETS - 1)
    bucket = np.where(n < max_exact, n, large).astype(np.int32).reshape(1, -1)
    is_prev = (j > i).astype(np.int32).reshape(1, -1)
    return bucket, is_prev


def _rel_bias_kernel(tab_ref, bucket_ref, prev_ref, o_ref):
    nb = tab_ref.shape[1]
    width = bucket_ref.shape[1]
    onehot = (lax.broadcasted_iota(jnp.int32, (nb, width), 0) == bucket_ref[...]).astype(BF16)
    hi, mid, lo = _split3(tab_ref[...])
    bias = _dot(hi, onehot) + _dot(mid, onehot) + _dot(lo, onehot)
    tile = o_ref.shape[1:]
    o_ref[0] = bias.reshape(tile)
    o_ref[1] = jnp.where(prev_ref[...] > 0, NEG_INF, bias).reshape(tile)


def _rel_bias(table_t, bucket, is_prev):
    nh = table_t.shape[0]
    width = bucket.shape[1]
    return pl.pallas_call(
        _rel_bias_kernel,
        out_shape=jax.ShapeDtypeStruct((2, nh, WINDOW, WINDOW), F32),
        compiler_params=pltpu.CompilerParams(vmem_limit_bytes=VMEM_LIMIT),
        name="rel_bias",
    )(table_t, bucket, is_prev)


SWA_SCORES_AHEAD = 12
SWA_BLOCKS_PER_STEP = 8


def _swa_kernel(sink_ref, q_ref, kc_ref, kp_ref, vc_ref, vp_ref, bias0_ref, bias_ref, o_ref):
    w = WINDOW
    hd = HEAD_DIM
    gq = N_HEADS // N_KV_HEADS
    n_sub = q_ref.shape[0] // w
    is_prev = (lax.broadcasted_iota(jnp.int32, (w, w), 0) > lax.broadcasted_iota(jnp.int32, (w, w), 1))
    zeros = jnp.zeros((2 * w, hd), BF16)
    scale = HEAD_DIM ** -0.5
    k_pl, v_pl = {}, {}
    for sub in range(n_sub):
        for g in range(N_KV_HEADS):
            cs = slice(g * hd, (g + 1) * hd)
            if sub == 0:
                k_g = jnp.concatenate([kp_ref[:, cs], kc_ref[0:w, cs]], axis=0) * scale
                v_g = jnp.concatenate([vp_ref[:, cs], vc_ref[0:w, cs]], axis=0)
            else:
                k_g = kc_ref[(sub - 1) * w:(sub + 1) * w, cs] * scale
                v_g = vc_ref[(sub - 1) * w:(sub + 1) * w, cs]
            k_pl[sub, g] = (jnp.concatenate([k_g, zeros], axis=1), jnp.concatenate([zeros, k_g], axis=1))
            v_pl[sub, g] = (jnp.concatenate([v_g, zeros], axis=1), jnp.concatenate([zeros, v_g], axis=1))

    def scores(sub, hh):
        pair = hh - hh % 2
        return _dot_nt(k_pl[sub, hh // gq][hh % 2], q_ref[sub * w:(sub + 1) * w, pair * hd:(pair + 2) * hd])

    def attend(sub, hh, st2):
        bias = bias0_ref if sub == 0 else bias_ref
        st = jnp.where(is_prev, st2[:w], st2[w:]) + bias[hh]
        sink = sink_ref[hh]
        m = jnp.maximum(jnp.max(st, axis=0, keepdims=True), sink)
        p = jnp.exp(st - m)
        recip = 1.0 / (jnp.sum(p, axis=0, keepdims=True) + jnp.exp(sink - m))
        probs = p * recip
        p_cat = jnp.concatenate([jnp.where(is_prev, probs, 0.0), jnp.where(is_prev, 0.0, probs)], axis=0)
        return _dot_tn(p_cat.astype(BF16), v_pl[sub, hh // gq][hh % 2])

    chains = [(sub, hh) for sub in range(n_sub) for hh in range(N_HEADS)]
    ahead = SWA_SCORES_AHEAD
    st2 = {c: scores(*c) for c in chains[:ahead]}
    acc = None
    for n, (sub, hh) in enumerate(chains):
        if n + ahead < len(chains):
            st2[chains[n + ahead]] = scores(*chains[n + ahead])
        part = attend(sub, hh, st2.pop((sub, hh)))
        if hh % 2 == 0:
            acc = part
        else:
            o_ref[sub * w:(sub + 1) * w, (hh - 1) * hd:(hh + 1) * hd] = (acc + part).astype(o_ref.dtype)


def _swa(proj, sinks, bias):
    s = proj.shape[0]
    w = WINDOW
    nb = s // w
    n_sub = SWA_BLOCKS_PER_STEP if nb % SWA_BLOCKS_PER_STEP == 0 else 1
    kvw = N_KV_HEADS * HEAD_DIM
    dq = N_HEADS * HEAD_DIM
    rows = n_sub * w
    prev = lambda n: jnp.maximum(n * n_sub - 1, 0)
    return pl.pallas_call(
        _swa_kernel,
        grid=(nb // n_sub,),
        in_specs=[
            pl.BlockSpec(memory_space=pltpu.SMEM),
            pl.BlockSpec((rows, dq), lambda n: (n, COL_AQ // dq)),
            pl.BlockSpec((rows, kvw), lambda n: (n, COL_AK // kvw)),
            pl.BlockSpec((w, kvw), lambda n: (prev(n), COL_AK // kvw)),
            pl.BlockSpec((rows, kvw), lambda n: (n, COL_AV // kvw)),
            pl.BlockSpec((w, kvw), lambda n: (prev(n), COL_AV // kvw)),
            pl.BlockSpec((None, N_HEADS, w, w), lambda n: (jnp.where(n == 0, 1, 0), 0, 0, 0)),
            pl.BlockSpec((None, N_HEADS, w, w), lambda n: (0, 0, 0, 0)),
        ],
        out_specs=pl.BlockSpec((rows, dq), lambda n: (n, 0)),
        out_shape=jax.ShapeDtypeStruct((s, dq), BF16),
        compiler_params=_cparams(("parallel",)),
        name="swa",
    )(sinks, proj, proj, proj, proj, proj, bias, bias)


GLA_CUM_ROWS = 256


def _gla_kernel(*refs):
    nh = GLA_HEADS
    q_refs, k_refs, v_refs, gr_refs = (refs[i * nh:(i + 1) * nh] for i in range(4))
    la_ref, gn_ref, o_ref, state_ref = refs[4 * nh:]
    c = GLA_CHUNK
    rows, dk = q_refs[0].shape
    dv = v_refs[0].shape[1]

    @pl.when(pl.program_id(0) == 0)
    def _():
        state_ref[...] = jnp.zeros_like(state_ref)

    cr = min(rows, GLA_CUM_ROWS)
    ri = lax.broadcasted_iota(jnp.int32, (cr, cr), 0)
    ci = lax.broadcasted_iota(jnp.int32, (cr, cr), 1)
    tri = ((ri // c == ci // c) & (ri >= ci)).astype(BF16)
    cum_parts = []
    for t in range(rows // cr):
        la_parts = _split3(la_ref[t * cr:(t + 1) * cr, :])
        cum_parts.append(_dot(tri, la_parts[0]) + _dot(tri, la_parts[1]) + _dot(tri, la_parts[2]))
    cum_all = jnp.concatenate(cum_parts, axis=0)
    causal = (lax.broadcasted_iota(jnp.int32, (c, c), 0) >= lax.broadcasted_iota(jnp.int32, (c, c), 1))
    gn = gn_ref[...]
    n_chunks = rows // c

    o_intra, q_in, upd, decay = {}, {}, {}, {}
    for h in range(nh):
        for ch in range(n_chunks):
            sl = slice(ch * c, (ch + 1) * c)
            cum_c = cum_all[sl, h * dk:(h + 1) * dk]
            ref = cum_c[c // 2 - 1:c // 2]
            last = cum_c[c - 1:c]
            q = q_refs[h][sl, :].astype(F32) * (dk ** -0.5)
            k = k_refs[h][sl, :].astype(F32)
            v = v_refs[h][sl, :]
            q_intra = (q * jnp.exp(cum_c - ref)).astype(BF16)
            k_intra = (k * jnp.exp(ref - cum_c)).astype(BF16)
            a = jnp.where(causal, _dot_nt(q_intra, k_intra), 0.0)
            o_intra[h, ch] = _dot(a.astype(BF16), v)
            q_in[h, ch] = (q * jnp.exp(cum_c)).astype(BF16)
            k_dec = (k * jnp.exp(last - cum_c)).astype(BF16)
            upd[h, ch] = _dot_tn(k_dec, v)
            decay[h, ch] = jnp.broadcast_to(jnp.exp(last), (LANES, dk)).T

    state = [state_ref[h] for h in range(nh)]
    for ch in range(n_chunks):
        sl = slice(ch * c, (ch + 1) * c)
        for h in range(nh):
            o = o_intra[h, ch] + _dot(q_in[h, ch], state[h].astype(BF16))
            state[h] = jnp.concatenate(
                [state[h][:, jv * LANES:(jv + 1) * LANES] * decay[h, ch]
                 + upd[h, ch][:, jv * LANES:(jv + 1) * LANES] for jv in range(dv // LANES)], axis=1)
            ms = jnp.mean(o * o, axis=-1, keepdims=True)
            on = o * lax.rsqrt(ms + RMS_EPS) * gn
            gr = gr_refs[h][sl, :].astype(F32)
            o_ref[sl, h * dv:(h + 1) * dv] = (on * (gr * _sigmoid(gr))).astype(o_ref.dtype)
    for h in range(nh):
        state_ref[h] = state[h]


def _gla(proj, la, gn, rows):
    s = proj.shape[0]
    dk, dv = GLA_DK, GLA_DV
    nh = GLA_HEADS

    def head_specs(width, col0):
        return [pl.BlockSpec((rows, width), functools.partial(lambda i, h: (i, col0 // width + h), h=h))
                for h in range(nh)]

    in_specs = (head_specs(dk, COL_GQ) + head_specs(dk, COL_GK) + head_specs(dv, COL_GV)
                + head_specs(dv, COL_GR)
                + [pl.BlockSpec((rows, nh * dk), lambda i: (i, 0)), pl.BlockSpec((1, dv), lambda i: (0, 0))])
    return pl.pallas_call(
        _gla_kernel,
        grid=(s // rows,),
        in_specs=in_specs,
        out_specs=pl.BlockSpec((rows, nh * dv), lambda i: (i, 0)),
        out_shape=jax.ShapeDtypeStruct((s, nh * dv), BF16),
        scratch_shapes=[pltpu.VMEM((nh, dk, dv), F32)],
        compiler_params=_cparams(("arbitrary",)),
        name="gla",
    )(*([proj] * (4 * nh)), la, gn)


MERGE_SUB_ROWS = 128


def _merge_out_kernel(oa_ref, og_ref, ga_ref, gg_ref, x_ref, wo_ref, g_ref, wr_ref,
                      x1_ref, h2_ref, lt_ref):
    sub = MERGE_SUB_ROWS
    tiles = [slice(t * sub, (t + 1) * sub) for t in range(x_ref.shape[0] // sub)]
    merged = [(_sigmoid(ga_ref[rs, :].astype(F32)) * oa_ref[rs, :].astype(F32)
               + _sigmoid(gg_ref[rs, :].astype(F32)) * og_ref[rs, :].astype(F32)).astype(BF16) for rs in tiles]
    projected = [_dot(m, wo_ref[...]) for m in merged]
    for rs, y in zip(tiles, projected):
        x1 = x_ref[rs, :] + y
        x1_ref[rs, :] = x1
        ms = jnp.mean(x1 * x1, axis=-1, keepdims=True)
        h2b = (x1 * lax.rsqrt(ms + RMS_EPS) * g_ref[...]).astype(BF16)
        h2_ref[rs] = h2b.reshape((sub,) + h2_ref.shape[1:])
        lt_ref[:, rs] = _dot_nt(wr_ref[...], h2b)


def _merge_out(o_attn, o_gla, gates, x, wo, g, wr, tm):
    s, d = x.shape
    nr = wr.shape[0]
    row = lambda i: (i, 0)
    return pl.pallas_call(
        _merge_out_kernel,
        grid=(s // tm,),
        in_specs=[
            pl.BlockSpec((tm, d), row),
            pl.BlockSpec((tm, d), row),
            pl.BlockSpec((tm, d), lambda i: (i, 0)),
            pl.BlockSpec((tm, d), lambda i: (i, 1)),
            pl.BlockSpec((tm, d), row),
            pl.BlockSpec((d, d), lambda i: (0, 0), pipeline_mode=pl.Buffered(1)),
            pl.BlockSpec((1, d), lambda i: (0, 0)),
            pl.BlockSpec((nr, d), lambda i: (0, 0)),
        ],
        out_specs=[
            pl.BlockSpec((tm, d), row),
            pl.BlockSpec((tm, d // LANES, LANES), lambda i: (i, 0, 0)),
            pl.BlockSpec((nr, tm), lambda i: (0, i)),
        ],
        out_shape=[
            jax.ShapeDtypeStruct((s, d), F32),
            jax.ShapeDtypeStruct((s, d // LANES, LANES), BF16),
            jax.ShapeDtypeStruct((nr, s), F32),
        ],
        compiler_params=_cparams(("parallel",)),
        name="merge_out",
    )(o_attn, o_gla, gates, gates, x, wo, g, wr)


ROUTER_ROWS = 8 + N_EXPERTS


def _route_kernel(lt_ref, bg_ref, be_ref, idx_ref, gate_ref, cnt_ref, carry_ref):
    tb = lt_ref.shape[1]
    epg = EXPERTS_PER_GROUP

    @pl.when(pl.program_id(0) == 0)
    def _():
        carry_ref[...] = jnp.zeros_like(carry_ref)

    logits = lt_ref[...]
    gl = logits[0:N_GROUPS]
    g_exp = jnp.exp(gl - jnp.max(gl, axis=0, keepdims=True))
    g_prob = g_exp / jnp.sum(g_exp, axis=0, keepdims=True)
    gb = gl + bg_ref[0:N_GROUPS]
    rowg = lax.broadcasted_iota(jnp.int32, (N_GROUPS, tb), 0)
    g_idx = jnp.min(jnp.where(gb == jnp.max(gb, axis=0, keepdims=True), rowg, N_GROUPS),
                    axis=0, keepdims=True)
    p_group = jnp.sum(jnp.where(rowg == g_idx, g_prob, 0.0), axis=0, keepdims=True)

    el = logits[8:8 + epg]
    eb = jnp.broadcast_to(be_ref[0:epg], (epg, tb))
    for g in range(1, N_GROUPS):
        pick = g_idx == g
        el = jnp.where(pick, logits[8 + g * epg:8 + (g + 1) * epg], el)
        eb = jnp.where(pick, be_ref[g * epg:(g + 1) * epg], eb)
    e_exp = jnp.exp(el - jnp.max(el, axis=0, keepdims=True))
    e_prob = e_exp / jnp.sum(e_exp, axis=0, keepdims=True)
    score = el + eb
    rowe = lax.broadcasted_iota(jnp.int32, (epg, tb), 0)
    i1 = jnp.min(jnp.where(score == jnp.max(score, axis=0, keepdims=True), rowe, epg),
                 axis=0, keepdims=True)
    score2 = jnp.where(rowe == i1, -jnp.inf, score)
    i2 = jnp.min(jnp.where(score2 == jnp.max(score2, axis=0, keepdims=True), rowe, epg),
                 axis=0, keepdims=True)
    q1 = jnp.sum(jnp.where(rowe == i1, e_prob, 0.0), axis=0, keepdims=True)
    q2 = jnp.sum(jnp.where(rowe == i2, e_prob, 0.0), axis=0, keepdims=True)
    qs = q1 + q2
    gate1 = p_group * q1 / qs
    gate2 = p_group * q2 / qs
    e1 = g_idx * epg + i1
    e2 = g_idx * epg + i2

    rowx = lax.broadcasted_iota(jnp.int32, (N_EXPERTS, tb), 0)
    hit1 = rowx == e1
    hit2 = rowx == e2
    member = (hit1 | hit2).astype(BF16)
    before = (lax.broadcasted_iota(jnp.int32, (tb, tb), 0)
              < lax.broadcasted_iota(jnp.int32, (tb, tb), 1)).astype(BF16)
    carry = carry_ref[...]
    count = _dot(member, before) + carry[:, 0:1]
    r1 = jnp.sum(jnp.where(hit1, count, 0.0), axis=0, keepdims=True).astype(jnp.int32)
    r2 = jnp.sum(jnp.where(hit2, count, 0.0), axis=0, keepdims=True).astype(jnp.int32)
    carry = carry + jnp.sum(member.astype(F32), axis=1, keepdims=True)
    carry_ref[...] = carry
    cnt_ref[...] = carry

    row8 = lax.broadcasted_iota(jnp.int32, (8, tb), 0)
    idx_ref[...] = jnp.where(row8 == 0, e1, jnp.where(row8 == 1, e2,
                             jnp.where(row8 == 2, r1, jnp.where(row8 == 3, r2, 0))))
    gate_ref[...] = jnp.where(row8 == 0, gate1, jnp.where(row8 == 1, gate2, 0.0))


def _route(lt, bg, be, tb):
    nr, s = lt.shape
    return pl.pallas_call(
        _route_kernel,
        grid=(s // tb,),
        in_specs=[
            pl.BlockSpec((nr, tb), lambda i: (0, i)),
            pl.BlockSpec((8, 1), lambda i: (0, 0)),
            pl.BlockSpec((N_EXPERTS, 1), lambda i: (0, 0)),
        ],
        out_specs=[
            pl.BlockSpec((8, tb), lambda i: (0, i)),
            pl.BlockSpec((8, tb), lambda i: (0, i)),
            pl.BlockSpec((N_EXPERTS, LANES), lambda i: (0, 0)),
        ],
        out_shape=[
            jax.ShapeDtypeStruct((8, s), jnp.int32),
            jax.ShapeDtypeStruct((8, s), F32),
            jax.ShapeDtypeStruct((N_EXPERTS, LANES), F32),
        ],
        scratch_shapes=[pltpu.VMEM((N_EXPERTS, LANES), F32)],
        compiler_params=_cparams(("arbitrary",)),
        name="route",
    )(lt, bg, be)


def _expert_kernel(be_ref, nx_ref, nu_ref, dest_ref, neg_hbm, h2_hbm, wg_hbm, wu_hbm, wd_hbm, y_hbm,
                   pair_ref, h2v, xg, ystage, wgf, wuf, wdf, wgb, wub, wdb, hsem, wsem, ysem):
    n_used = nu_ref[0]
    rows = MOE_BLOCK
    n_tok = h2v.shape[0]

    def weight_copy(e, m):
        src, stage = ((wg_hbm, wgf), (wu_hbm, wuf), (wd_hbm, wdf))[m]
        return pltpu.make_async_copy(src.at[e], stage, wsem.at[m])

    def gather_row(blk, r, half):
        pair = jnp.maximum(pair_ref[blk * rows + r], 0)
        xg[half, r] = h2v[jnp.where(pair >= n_tok, pair - n_tok, pair)]

    def row_write(blk, r, half, dummy):
        pair = pair_ref[blk * rows + r]
        dst = jnp.where((pair < 0) | dummy, TOP_K * n_tok + half * rows + r, pair)
        return pltpu.make_async_copy(ystage.at[half, r], y_hbm.at[dst], ysem.at[half])

    def wait_row_writes(half):
        pltpu.make_async_copy(ystage.at[half], y_hbm.at[pl.ds(0, rows)], ysem.at[half]).wait()

    weight_priority = 1

    fill = pltpu.make_async_copy(neg_hbm, pair_ref, hsem.at[1])
    fill.start()
    h2_copy = pltpu.make_async_copy(h2_hbm, h2v, hsem.at[0])
    h2_copy.start()
    for m in range(3):
        weight_copy(be_ref[0], m).start(priority=weight_priority)
    ystage[...] = jnp.zeros_like(ystage)

    fill.wait()

    def place(p, carry):
        pair_ref[dest_ref[p]] = p
        return carry
    lax.fori_loop(0, dest_ref.shape[0], place, 0, unroll=8)

    h2_copy.wait()
    spare_fill = pltpu.make_async_copy(ystage.at[0], y_hbm.at[pl.ds(TOP_K * n_tok, rows)], hsem.at[0])
    spare_fill.start()
    spare_fill.wait()

    def first_gather(r, carry):
        gather_row(0, r, 0)
        return carry
    lax.fori_loop(0, rows, first_gather, 0)

    def block(b, carry):
        slot = b % 2
        first_of_expert = (b == 0) | (be_ref[b] != be_ref[jnp.maximum(b - 1, 0)])

        @pl.when(first_of_expert)
        def _():
            nxt = nx_ref[b]
            for m, (stage, dst) in enumerate(((wgf, wgb), (wuf, wub), (wdf, wdb))):
                weight_copy(be_ref[b], m).wait()
                dst[...] = stage[...].astype(BF16)

                @pl.when(nxt >= 0)
                def _():
                    weight_copy(nxt, m).start(priority=weight_priority)

        xb = xg[slot].reshape(rows, wgb.shape[0])
        prev_blk = jnp.maximum(b - 1, 0)
        for r in range(rows):
            row_write(prev_blk, r, 1 - slot, b == 0).start(priority=r % 2)
        for r in range(rows):
            gather_row(b + 1, r, 1 - slot)
        hg = _dot(xb, wgb[...])
        hu = _dot(xb, wub[...])
        act = (hg * _sigmoid(hg)) * hu
        y = _dot(act.astype(BF16), wdb[...]).astype(BF16)

        @pl.when(b >= 1)
        def _():
            wait_row_writes(slot)

        ystage[slot] = y.reshape(ystage.shape[1:])
        return carry

    lax.fori_loop(0, n_used, block, 0)

    last_half = (n_used - 1) % 2

    def last_writes(r, carry):
        row_write(n_used - 1, r, last_half, False).start()
        return carry
    lax.fori_loop(0, rows, last_writes, 0)
    wait_row_writes(1 - last_half)
    wait_row_writes(last_half)


def _experts(block_e, next_e, n_used, dest, h2c, wg, wu, wd):
    n_rows = block_e.shape[0] * MOE_BLOCK
    n_tok = h2c.shape[0]
    _, d, de = wg.shape
    slab = (d // LANES, LANES)
    grid_spec = pltpu.PrefetchScalarGridSpec(
        num_scalar_prefetch=4,
        grid=(1,),
        in_specs=[pl.BlockSpec(memory_space=pl.ANY)] * 5,
        out_specs=pl.BlockSpec(memory_space=pl.ANY),
        scratch_shapes=[
            pltpu.SMEM((n_rows,), jnp.int32),
            pltpu.VMEM((n_tok,) + slab, BF16),
            pltpu.VMEM((2, MOE_BLOCK) + slab, BF16),
            pltpu.VMEM((2, MOE_BLOCK) + slab, BF16),
            pltpu.VMEM((d, de), F32),
            pltpu.VMEM((d, de), F32),
            pltpu.VMEM((de, d), F32),
            pltpu.VMEM((d, de), BF16),
            pltpu.VMEM((d, de), BF16),
            pltpu.VMEM((de, d), BF16),
            pltpu.SemaphoreType.DMA((2,)),
            pltpu.SemaphoreType.DMA((3,)),
            pltpu.SemaphoreType.DMA((2,)),
        ],
    )
    padding_marks = jnp.full((n_rows,), -1, jnp.int32)
    return pl.pallas_call(
        _expert_kernel,
        grid_spec=grid_spec,
        out_shape=jax.ShapeDtypeStruct((TOP_K * n_tok + 2 * MOE_BLOCK,) + slab, BF16),
        compiler_params=_cparams(("arbitrary",), vmem=RESIDENT_VMEM_LIMIT),
        name="experts",
    )(block_e, next_e, n_used, dest, padding_marks, h2c, wg, wu, wd)


def _combine_kernel(y0_ref, y1_ref, gate_ref, x1_ref, g_ref, o_ref):
    gates = gate_ref[...]
    y0 = y0_ref[...].reshape(x1_ref.shape).astype(F32)
    y1 = y1_ref[...].reshape(x1_ref.shape).astype(F32)
    x2 = x1_ref[...] + (y0 * gates[:, 0:1] + y1 * gates[:, 1:2])
    ms = jnp.mean(x2 * x2, axis=-1, keepdims=True)
    o_ref[...] = x2 * lax.rsqrt(ms + RMS_EPS) * g_ref[...]


def _combine(y2, gates, x1, g, tm):
    s, d = x1.shape
    slab = (d // LANES, LANES)
    n_steps = s // tm
    return pl.pallas_call(
        _combine_kernel,
        grid=(n_steps,),
        in_specs=[
            pl.BlockSpec((tm,) + slab, lambda i: (i, 0, 0)),
            pl.BlockSpec((tm,) + slab, lambda i: (n_steps + i, 0, 0)),
            pl.BlockSpec((tm, TOP_K), lambda i: (i, 0)),
            pl.BlockSpec((tm, d), lambda i: (i, 0)),
            pl.BlockSpec((1, d), lambda i: (0, 0)),
        ],
        out_specs=pl.BlockSpec((tm, d), lambda i: (i, 0)),
        out_shape=jax.ShapeDtypeStruct((s, d), F32),
        compiler_params=_cparams(("parallel",)),
        name="combine",
    )(y2, y2, gates, x1, g)


def _tile(n, pref):
    return pref if n % pref == 0 else n


def kernel(x, norm_mix_g, w_in, w_gla_a2, b_gla_a, gla_norm_g, attn_sinks, rel_bias_table, w_out,
           norm_ffn_g, w_router_group, b_router_group, w_router_expert, b_router_expert,
           w_expert_gate, w_expert_up, w_expert_down, norm_final_g):
    b, s, d = x.shape
    assert b == 1 and w_in.shape[0] == 1, "single batch, single layer"
    assert (s * TOP_K) % MOE_BLOCK == 0
    x2d = x.reshape(s, d)
    w_in_t = w_in[0].T

    w2 = jnp.pad(w_gla_a2[0], ((0, LANES - GLA_LOWRANK), (0, 0))).astype(BF16)
    h, log_a = _norm_loga(x2d, norm_mix_g[0].reshape(1, d), w_in_t, w2, b_gla_a[0].reshape(1, -1),
                          _tile(s, 1024))
    proj = _in_proj(h, w_in_t, _tile(s, 4096), 512)

    bucket, is_prev = _folded_maps()
    bias = _rel_bias(rel_bias_table.T, jnp.asarray(bucket), jnp.asarray(is_prev))
    o_attn = _swa(proj, attn_sinks[0], bias)
    o_gla = _gla(proj, log_a, gla_norm_g[0].reshape(1, -1), _tile(s, 256))

    wr = jnp.zeros((ROUTER_ROWS, d), F32)
    wr = wr.at[0:N_GROUPS].set(w_router_group[0].T).at[8:].set(w_router_expert[0].T).astype(BF16)
    x1, h2, logits_t = _merge_out(o_attn, o_gla, proj, x2d, w_out[0].astype(BF16),
                                  norm_ffn_g[0].reshape(1, d), wr, _tile(s, 512))

    bg = jnp.pad(b_router_group[0], (0, 8 - N_GROUPS)).reshape(8, 1)
    idx, gate, counts = _route(logits_t, bg, b_router_expert[0].reshape(N_EXPERTS, 1), _tile(s, 512))
    counts = counts[:, 0].astype(jnp.int32)
    padded = (counts + MOE_BLOCK - 1) // MOE_BLOCK * MOE_BLOCK
    pend = jnp.cumsum(padded)
    pstart = pend - padded
    expert = idx[0:TOP_K]
    eids = jnp.arange(N_EXPERTS, dtype=jnp.int32)[:, None, None]
    dest = jnp.sum(jnp.where(expert[None] == eids, pstart[:, None, None], 0), axis=0) + idx[TOP_K:2 * TOP_K]
    n_pairs = s * TOP_K
    n_blocks = (n_pairs + N_EXPERTS * (MOE_BLOCK - 1) + MOE_BLOCK - 1) // MOE_BLOCK
    n_used = (pend[-1] // MOE_BLOCK).reshape(1)
    block_start = jnp.arange(n_blocks, dtype=jnp.int32) * MOE_BLOCK
    block_e = jnp.minimum(jnp.sum((pend[None, :] <= block_start[:, None]).astype(jnp.int32), axis=1),
                          N_EXPERTS - 1)

    e_col = jnp.arange(N_EXPERTS, dtype=jnp.int32)[:, None]
    e_row = jnp.arange(N_EXPERTS, dtype=jnp.int32)[None, :]
    later = (e_row > e_col) & (counts > 0)[None, :]
    next_nonempty = jnp.min(jnp.where(later, e_row, N_EXPERTS), axis=1)
    next_nonempty = jnp.where(next_nonempty < N_EXPERTS, next_nonempty, -1)
    next_e = jnp.sum(jnp.where(block_e[:, None] == e_row, next_nonempty[None, :], 0), axis=1)
    y2 = _experts(block_e, next_e, n_used, dest.reshape(-1), h2,
                  w_expert_gate[0], w_expert_up[0], w_expert_down[0])
    out = _combine(y2, gate[0:TOP_K].T, x1, norm_final_g.reshape(1, d), _tile(s, 512))
    return out.reshape(b, s, d)
```

```python
import functools
import math

import numpy as np
import jax
import jax.numpy as jnp
from jax import lax
from jax.experimental import pallas as pl
from jax.experimental.pallas import tpu as pltpu

F32 = jnp.float32
BF16 = jnp.bfloat16

N_HEADS = 32
N_KV_HEADS = 4
HEAD_DIM = 64
WINDOW = 128
N_BUCKETS = 32
MAX_DISTANCE = 128
GLA_HEADS = 4
GLA_DK = 256
GLA_DV = 512
GLA_LOWRANK = 16
GLA_TAU = 16.0
GLA_CHUNK = 64
N_GROUPS = 4
EXPERTS_PER_GROUP = 8
N_EXPERTS = 32
TOP_K = 2
MOE_BLOCK = 256
RMS_EPS = 1e-6
NEG_INF = -1e30

COL_GA = 8704
COL_GATES = 8720
MAIN_COLS = 8704
GATE_COLS = 4096
COL_AQ = GATE_COLS + 0
COL_AK = GATE_COLS + 2048
COL_AV = GATE_COLS + 2304
COL_GQ = GATE_COLS + 2560
COL_GK = GATE_COLS + 3584
COL_GV = GATE_COLS + 4608
COL_GR = GATE_COLS + 6656

LANES = 128
VMEM_LIMIT = 56 * 1024 * 1024
RESIDENT_VMEM_LIMIT = 60 * 1024 * 1024


def _cparams(sem, vmem=VMEM_LIMIT):
    return pltpu.CompilerParams(dimension_semantics=sem, vmem_limit_bytes=vmem)


def _split3(x):
    hi = x.astype(BF16)
    r1 = x - hi.astype(F32)
    mid = r1.astype(BF16)
    lo = (r1 - mid.astype(F32)).astype(BF16)
    return hi, mid, lo


def _sigmoid(x):
    return 0.5 * jnp.tanh(0.5 * x) + 0.5


def _dot(a, b):
    return jnp.dot(a, b, preferred_element_type=F32)


def _dot_nt(a, b):
    return lax.dot_general(a, b, (((1,), (1,)), ((), ())), preferred_element_type=F32)


def _dot_tn(a, b):
    return lax.dot_general(a, b, (((0,), (0,)), ((), ())), preferred_element_type=F32)


def _norm_loga_kernel(x_ref, g_ref, wga_ref, w2_ref, b_ref, h_ref, la_ref):
    x = x_ref[...]
    ms = jnp.mean(x * x, axis=-1, keepdims=True)
    hb = (x * lax.rsqrt(ms + RMS_EPS) * g_ref[...]).astype(BF16)
    h_ref[...] = hb
    row = lax.broadcasted_iota(jnp.int32, wga_ref.shape, 0)
    wga = jnp.where(row < GLA_LOWRANK, wga_ref[...], 0.0).astype(BF16)
    ga = _dot_nt(hb, wga)
    z = _dot(ga.astype(BF16), w2_ref[...]) + b_ref[...]
    la_ref[...] = (jnp.minimum(z, 0.0) - jnp.log1p(jnp.exp(-jnp.abs(z)))) * (1.0 / GLA_TAU)


def _norm_loga(x, g, wga, w2, b, tm):
    s, d = x.shape
    nq = w2.shape[1]
    return pl.pallas_call(
        _norm_loga_kernel,
        grid=(s // tm,),
        in_specs=[
            pl.BlockSpec((tm, d), lambda i: (i, 0)),
            pl.BlockSpec((1, d), lambda i: (0, 0)),
            pl.BlockSpec((LANES, d), lambda i: (COL_GA // LANES, 0)),
            pl.BlockSpec((LANES, nq), lambda i: (0, 0)),
            pl.BlockSpec((1, nq), lambda i: (0, 0)),
        ],
        out_specs=[
            pl.BlockSpec((tm, d), lambda i: (i, 0)),
            pl.BlockSpec((tm, nq), lambda i: (i, 0)),
        ],
        out_shape=[
            jax.ShapeDtypeStruct((s, d), BF16),
            jax.ShapeDtypeStruct((s, nq), F32),
        ],
        compiler_params=_cparams(("parallel",)),
        name="norm_loga",
    )(x, g, wga, w2, b)


def _in_proj_kernel(h_hbm, wt_hbm, o_ref, h_vmem, w_stage, wbf, sem, hsem, *, n_main, n_tiles, tn, n_row_tiles):
    j = pl.program_id(0)
    i = pl.program_id(1)
    n_i = pl.num_programs(1)
    tm = o_ref.shape[0]
    cur, nxt = j % 2, (j + 1) % 2

    def w_copy(jj):
        row0 = jnp.where(jj < n_main, jj * tn, COL_GATES + (jj - n_main) * tn)
        return pltpu.make_async_copy(wt_hbm.at[pl.ds(pl.multiple_of(row0, 8), tn)], w_stage.at[jj % 2],
                                     sem.at[jj % 2])

    def h_copy(ii):
        rs_ = pl.ds(pl.multiple_of(ii * tm, tm), tm)
        return pltpu.make_async_copy(h_hbm.at[rs_], h_vmem.at[rs_], hsem.at[ii])

    @pl.when((j == 0) & (i == 0))
    def _():
        for ii in range(n_row_tiles):
            h_copy(ii).start()
        w_copy(0).start()
        w_copy(1).start()
        w_copy(0).wait()
        wbf[0] = w_stage[0].astype(BF16)
        if n_tiles > 2:
            w_copy(2).start()

    @pl.when(j == 0)
    def _():
        h_copy(i).wait()

    @pl.when(i == 0)
    def _():
        @pl.when(j + 1 < n_tiles)
        def _():
            w_copy(j + 1).wait()

        @pl.when((j >= 1) & (j + 2 < n_tiles))
        def _():
            w_copy(j + 2).start()

    h = h_vmem[pl.ds(pl.multiple_of(i * tm, tm), tm), :]
    o_ref[...] = _dot_nt(h, wbf[cur]).astype(o_ref.dtype)

    rows_per_step = tn // n_i
    rs = pl.ds(pl.multiple_of(i * rows_per_step, rows_per_step), rows_per_step)
    wbf[nxt, rs, :] = w_stage[nxt, rs, :].astype(BF16)


def _in_proj(h, w_t, tm, tn):
    s, d = h.shape
    n_main = MAIN_COLS // tn
    n_gate = GATE_COLS // tn
    n_tiles = n_main + n_gate
    kern = functools.partial(_in_proj_kernel, n_main=n_main, n_tiles=n_tiles, tn=tn, n_row_tiles=s // tm)
    return pl.pallas_call(
        kern,
        grid=(n_tiles, s // tm),
        in_specs=[pl.BlockSpec(memory_space=pl.ANY), pl.BlockSpec(memory_space=pl.ANY)],
        out_specs=pl.BlockSpec((tm, tn), lambda j, i: (i, (j + n_gate) % n_tiles)),
        out_shape=jax.ShapeDtypeStruct((s, n_tiles * tn), BF16),
        scratch_shapes=[
            pltpu.VMEM((s, d), BF16),
            pltpu.VMEM((2, tn, d), F32),
            pltpu.VMEM((2, tn, d), BF16),
            pltpu.SemaphoreType.DMA((2,)),
            pltpu.SemaphoreType.DMA((s // tm,)),
        ],
        compiler_params=_cparams(("arbitrary", "arbitrary"), vmem=RESIDENT_VMEM_LIMIT),
        name="in_proj",
    )(h, w_t)


def _folded_maps():
    j = np.arange(WINDOW)[:, None]
    i = np.arange(WINDOW)[None, :]
    n = (i - j) % WINDOW
    max_exact = N_BUCKETS // 2
    ratio = np.maximum(n, max_exact).astype(np.float32) / np.float32(max_exact)
    large = max_exact + (np.log(ratio) / np.float32(math.log(MAX_DISTANCE / max_exact))
                         * (N_BUCKETS - max_exact)).astype(np.int32)
    large = np.minimum(large, N_BUCKETS - 1)
    bucket = np.where(n < max_exact, n, large).astype(np.int32).reshape(1, -1)
    is_prev = (j > i).astype(np.int32).reshape(1, -1)
    return bucket, is_prev


def _rel_bias_kernel(tab_ref, bucket_ref, prev_ref, o_ref):
    nb = tab_ref.shape[1]
    width = bucket_ref.shape[1]
    onehot = (lax.broadcasted_iota(jnp.int32, (nb, width), 0) == bucket_ref[...]).astype(BF16)
    hi, mid, lo = _split3(tab_ref[...])
    bias = _dot(hi, onehot) + _dot(mid, onehot) + _dot(lo, onehot)
    tile = o_ref.shape[1:]
    o_ref[0] = bias.reshape(tile)
    o_ref[1] = jnp.where(prev_ref[...] > 0, NEG_INF, bias).reshape(tile)


def _rel_bias(table_t, bucket, is_prev):
    nh = table_t.shape[0]
    width = bucket.shape[1]
    return pl.pallas_call(
        _rel_bias_kernel,
        out_shape=jax.ShapeDtypeStruct((2, nh, WINDOW, WINDOW), F32),
        compiler_params=pltpu.CompilerParams(vmem_limit_bytes=VMEM_LIMIT),
        name="rel_bias",
    )(table_t, bucket, is_prev)


SWA_SCORES_AHEAD = 12
SWA_BLOCKS_PER_STEP = 8


def _swa_kernel(sink_ref, q_ref, kc_ref, kp_ref, vc_ref, vp_ref, bias0_ref, bias_ref, o_ref):
    w = WINDOW
    hd = HEAD_DIM
    gq = N_HEADS // N_KV_HEADS
    n_sub = q_ref.shape[0] // w
    is_prev = (lax.broadcasted_iota(jnp.int32, (w, w), 0) > lax.broadcasted_iota(jnp.int32, (w, w), 1))
    zeros = jnp.zeros((2 * w, hd), BF16)
    scale = HEAD_DIM ** -0.5
    k_pl, v_pl = {}, {}
    for sub in range(n_sub):
        for g in range(N_KV_HEADS):
            cs = slice(g * hd, (g + 1) * hd)
            if sub == 0:
                k_g = jnp.concatenate([kp_ref[:, cs], kc_ref[0:w, cs]], axis=0) * scale
                v_g = jnp.concatenate([vp_ref[:, cs], vc_ref[0:w, cs]], axis=0)
            else:
                k_g = kc_ref[(sub - 1) * w:(sub + 1) * w, cs] * scale
                v_g = vc_ref[(sub - 1) * w:(sub + 1) * w, cs]
            k_pl[sub, g] = (jnp.concatenate([k_g, zeros], axis=1), jnp.concatenate([zeros, k_g], axis=1))
            v_pl[sub, g] = (jnp.concatenate([v_g, zeros], axis=1), jnp.concatenate([zeros, v_g], axis=1))

    def scores(sub, hh):
        pair = hh - hh % 2
        return _dot_nt(k_pl[sub, hh // gq][hh % 2], q_ref[sub * w:(sub + 1) * w, pair * hd:(pair + 2) * hd])

    def attend(sub, hh, st2):
        bias = bias0_ref if sub == 0 else bias_ref
        st = jnp.where(is_prev, st2[:w], st2[w:]) + bias[hh]
        sink = sink_ref[hh]
        m = jnp.maximum(jnp.max(st, axis=0, keepdims=True), sink)
        p = jnp.exp(st - m)
        recip = 1.0 / (jnp.sum(p, axis=0, keepdims=True) + jnp.exp(sink - m))
        probs = p * recip
        p_cat = jnp.concatenate([jnp.where(is_prev, probs, 0.0), jnp.where(is_prev, 0.0, probs)], axis=0)
        return _dot_tn(p_cat.astype(BF16), v_pl[sub, hh // gq][hh % 2])

    chains = [(sub, hh) for sub in range(n_sub) for hh in range(N_HEADS)]
    ahead = SWA_SCORES_AHEAD
    st2 = {c: scores(*c) for c in chains[:ahead]}
    acc = None
    for n, (sub, hh) in enumerate(chains):
        if n + ahead < len(chains):
            st2[chains[n + ahead]] = scores(*chains[n + ahead])
        part = attend(sub, hh, st2.pop((sub, hh)))
        if hh % 2 == 0:
            acc = part
        else:
            o_ref[sub * w:(sub + 1) * w, (hh - 1) * hd:(hh + 1) * hd] = (acc + part).astype(o_ref.dtype)


def _swa(proj, sinks, bias):
    s = proj.shape[0]
    w = WINDOW
    nb = s // w
    n_sub = SWA_BLOCKS_PER_STEP if nb % SWA_BLOCKS_PER_STEP == 0 else 1
    kvw = N_KV_HEADS * HEAD_DIM
    dq = N_HEADS * HEAD_DIM
    rows = n_sub * w
    prev = lambda n: jnp.maximum(n * n_sub - 1, 0)
    return pl.pallas_call(
        _swa_kernel,
        grid=(nb // n_sub,),
        in_specs=[
            pl.BlockSpec(memory_space=pltpu.SMEM),
            pl.BlockSpec((rows, dq), lambda n: (n, COL_AQ // dq)),
            pl.BlockSpec((rows, kvw), lambda n: (n, COL_AK // kvw)),
            pl.BlockSpec((w, kvw), lambda n: (prev(n), COL_AK // kvw)),
            pl.BlockSpec((rows, kvw), lambda n: (n, COL_AV // kvw)),
            pl.BlockSpec((w, kvw), lambda n: (prev(n), COL_AV // kvw)),
            pl.BlockSpec((None, N_HEADS, w, w), lambda n: (jnp.where(n == 0, 1, 0), 0, 0, 0)),
            pl.BlockSpec((None, N_HEADS, w, w), lambda n: (0, 0, 0, 0)),
        ],
        out_specs=pl.BlockSpec((rows, dq), lambda n: (n, 0)),
        out_shape=jax.ShapeDtypeStruct((s, dq), BF16),
        compiler_params=_cparams(("parallel",)),
        name="swa",
    )(sinks, proj, proj, proj, proj, proj, bias, bias)


GLA_CUM_ROWS = 256


def _gla_kernel(*refs):
    nh = GLA_HEADS
    q_refs, k_refs, v_refs, gr_refs = (refs[i * nh:(i + 1) * nh] for i in range(4))
    la_ref, gn_ref, o_ref, state_ref = refs[4 * nh:]
    c = GLA_CHUNK
    rows, dk = q_refs[0].shape
    dv = v_refs[0].shape[1]

    @pl.when(pl.program_id(0) == 0)
    def _():
        state_ref[...] = jnp.zeros_like(state_ref)

    cr = min(rows, GLA_CUM_ROWS)
    ri = lax.broadcasted_iota(jnp.int32, (cr, cr), 0)
    ci = lax.broadcasted_iota(jnp.int32, (cr, cr), 1)
    tri = ((ri // c == ci // c) & (ri >= ci)).astype(BF16)
    cum_parts = []
    for t in range(rows // cr):
        la_parts = _split3(la_ref[t * cr:(t + 1) * cr, :])
        cum_parts.append(_dot(tri, la_parts[0]) + _dot(tri, la_parts[1]) + _dot(tri, la_parts[2]))
    cum_all = jnp.concatenate(cum_parts, axis=0)
    causal = (lax.broadcasted_iota(jnp.int32, (c, c), 0) >= lax.broadcasted_iota(jnp.int32, (c, c), 1))
    gn = gn_ref[...]
    n_chunks = rows // c

    o_intra, q_in, upd, decay = {}, {}, {}, {}
    for h in range(nh):
        for ch in range(n_chunks):
            sl = slice(ch * c, (ch + 1) * c)
            cum_c = cum_all[sl, h * dk:(h + 1) * dk]
            ref = cum_c[c // 2 - 1:c // 2]
            last = cum_c[c - 1:c]
            q = q_refs[h][sl, :].astype(F32) * (dk ** -0.5)
            k = k_refs[h][sl, :].astype(F32)
            v = v_refs[h][sl, :]
            q_intra = (q * jnp.exp(cum_c - ref)).astype(BF16)
            k_intra = (k * jnp.exp(ref - cum_c)).astype(BF16)
            a = jnp.where(causal, _dot_nt(q_intra, k_intra), 0.0)
            o_intra[h, ch] = _dot(a.astype(BF16), v)
            q_in[h, ch] = (q * jnp.exp(cum_c)).astype(BF16)
            k_dec = (k * jnp.exp(last - cum_c)).astype(BF16)
            upd[h, ch] = _dot_tn(k_dec, v)
            decay[h, ch] = jnp.broadcast_to(jnp.exp(last), (LANES, dk)).T

    state = [state_ref[h] for h in range(nh)]
    for ch in range(n_chunks):
        sl = slice(ch * c, (ch + 1) * c)
        for h in range(nh):
            o = o_intra[h, ch] + _dot(q_in[h, ch], state[h].astype(BF16))
            state[h] = jnp.concatenate(
                [state[h][:, jv * LANES:(jv + 1) * LANES] * decay[h, ch]
                 + upd[h, ch][:, jv * LANES:(jv + 1) * LANES] for jv in range(dv // LANES)], axis=1)
            ms = jnp.mean(o * o, axis=-1, keepdims=True)
            on = o * lax.rsqrt(ms + RMS_EPS) * gn
            gr = gr_refs[h][sl, :].astype(F32)
            o_ref[sl, h * dv:(h + 1) * dv] = (on * (gr * _sigmoid(gr))).astype(o_ref.dtype)
    for h in range(nh):
        state_ref[h] = state[h]


def _gla(proj, la, gn, rows):
    s = proj.shape[0]
    dk, dv = GLA_DK, GLA_DV
    nh = GLA_HEADS

    def head_specs(width, col0):
        return [pl.BlockSpec((rows, width), functools.partial(lambda i, h: (i, col0 // width + h), h=h))
                for h in range(nh)]

    in_specs = (head_specs(dk, COL_GQ) + head_specs(dk, COL_GK) + head_specs(dv, COL_GV)
                + head_specs(dv, COL_GR)
                + [pl.BlockSpec((rows, nh * dk), lambda i: (i, 0)), pl.BlockSpec((1, dv), lambda i: (0, 0))])
    return pl.pallas_call(
        _gla_kernel,
        grid=(s // rows,),
        in_specs=in_specs,
        out_specs=pl.BlockSpec((rows, nh * dv), lambda i: (i, 0)),
        out_shape=jax.ShapeDtypeStruct((s, nh * dv), BF16),
        scratch_shapes=[pltpu.VMEM((nh, dk, dv), F32)],
        compiler_params=_cparams(("arbitrary",)),
        name="gla",
    )(*([proj] * (4 * nh)), la, gn)


MERGE_SUB_ROWS = 128


def _merge_out_kernel(oa_ref, og_ref, ga_ref, gg_ref, x_ref, wo_ref, g_ref, wr_ref, bg_ref, be_ref,
                      x1_ref, h2_ref, idx_ref, gate_ref, cnt_ref, lt_ref, carry_ref):
    sub = MERGE_SUB_ROWS
    tiles = [slice(t * sub, (t + 1) * sub) for t in range(x_ref.shape[0] // sub)]
    merged = [(_sigmoid(ga_ref[rs, :].astype(F32)) * oa_ref[rs, :].astype(F32)
               + _sigmoid(gg_ref[rs, :].astype(F32)) * og_ref[rs, :].astype(F32)).astype(BF16) for rs in tiles]
    projected = [_dot(m, wo_ref[...]) for m in merged]
    for rs, y in zip(tiles, projected):
        x1 = x_ref[rs, :] + y
        x1_ref[rs, :] = x1
        ms = jnp.mean(x1 * x1, axis=-1, keepdims=True)
        h2b = (x1 * lax.rsqrt(ms + RMS_EPS) * g_ref[...]).astype(BF16)
        h2_ref[rs] = h2b.reshape((sub,) + h2_ref.shape[1:])
        lt_ref[:, rs] = _dot_nt(wr_ref[...], h2b)
    _route_kernel(lt_ref, bg_ref, be_ref, idx_ref, gate_ref, cnt_ref, carry_ref)


def _merge_out(o_attn, o_gla, gates, x, wo, g, wr, bg, be, tm):
    s, d = x.shape
    nr = wr.shape[0]
    row = lambda i: (i, 0)
    return pl.pallas_call(
        _merge_out_kernel,
        grid=(s // tm,),
        in_specs=[
            pl.BlockSpec((tm, d), row),
            pl.BlockSpec((tm, d), row),
            pl.BlockSpec((tm, d), lambda i: (i, 0)),
            pl.BlockSpec((tm, d), lambda i: (i, 1)),
            pl.BlockSpec((tm, d), row),
            pl.BlockSpec((d, d), lambda i: (0, 0), pipeline_mode=pl.Buffered(1)),
            pl.BlockSpec((1, d), lambda i: (0, 0)),
            pl.BlockSpec((nr, d), lambda i: (0, 0)),
            pl.BlockSpec((8, 1), lambda i: (0, 0)),
            pl.BlockSpec((N_EXPERTS, 1), lambda i: (0, 0)),
        ],
        out_specs=[
            pl.BlockSpec((tm, d), row),
            pl.BlockSpec((tm, d // LANES, LANES), lambda i: (i, 0, 0)),
            pl.BlockSpec((8, tm), lambda i: (0, i)),
            pl.BlockSpec((8, tm), lambda i: (0, i)),
            pl.BlockSpec((N_EXPERTS, LANES), lambda i: (0, 0)),
        ],
        out_shape=[
            jax.ShapeDtypeStruct((s, d), F32),
            jax.ShapeDtypeStruct((s, d // LANES, LANES), BF16),
            jax.ShapeDtypeStruct((8, s), jnp.int32),
            jax.ShapeDtypeStruct((8, s), F32),
            jax.ShapeDtypeStruct((N_EXPERTS, LANES), F32),
        ],
        scratch_shapes=[pltpu.VMEM((nr, tm), F32), pltpu.VMEM((N_EXPERTS, LANES), F32)],
        compiler_params=_cparams(("arbitrary",)),
        name="merge_out",
    )(o_attn, o_gla, gates, gates, x, wo, g, wr, bg, be)


ROUTER_ROWS = 8 + N_EXPERTS


def _route_kernel(lt_ref, bg_ref, be_ref, idx_ref, gate_ref, cnt_ref, carry_ref):
    tb = lt_ref.shape[1]
    epg = EXPERTS_PER_GROUP

    @pl.when(pl.program_id(0) == 0)
    def _():
        carry_ref[...] = jnp.zeros_like(carry_ref)

    logits = lt_ref[...]
    gl = logits[0:N_GROUPS]
    g_exp = jnp.exp(gl - jnp.max(gl, axis=0, keepdims=True))
    g_prob = g_exp / jnp.sum(g_exp, axis=0, keepdims=True)
    gb = gl + bg_ref[0:N_GROUPS]
    rowg = lax.broadcasted_iota(jnp.int32, (N_GROUPS, tb), 0)
    g_idx = jnp.min(jnp.where(gb == jnp.max(gb, axis=0, keepdims=True), rowg, N_GROUPS),
                    axis=0, keepdims=True)
    p_group = jnp.sum(jnp.where(rowg == g_idx, g_prob, 0.0), axis=0, keepdims=True)

    el = logits[8:8 + epg]
    eb = jnp.broadcast_to(be_ref[0:epg], (epg, tb))
    for g in range(1, N_GROUPS):
        pick = g_idx == g
        el = jnp.where(pick, logits[8 + g * epg:8 + (g + 1) * epg], el)
        eb = jnp.where(pick, be_ref[g * epg:(g + 1) * epg], eb)
    e_exp = jnp.exp(el - jnp.max(el, axis=0, keepdims=True))
    e_prob = e_exp / jnp.sum(e_exp, axis=0, keepdims=True)
    score = el + eb
    rowe = lax.broadcasted_iota(jnp.int32, (epg, tb), 0)
    i1 = jnp.min(jnp.where(score == jnp.max(score, axis=0, keepdims=True), rowe, epg),
                 axis=0, keepdims=True)
    score2 = jnp.where(rowe == i1, -jnp.inf, score)
    i2 = jnp.min(jnp.where(score2 == jnp.max(score2, axis=0, keepdims=True), rowe, epg),
                 axis=0, keepdims=True)
    q1 = jnp.sum(jnp.where(rowe == i1, e_prob, 0.0), axis=0, keepdims=True)
    q2 = jnp.sum(jnp.where(rowe == i2, e_prob, 0.0), axis=0, keepdims=True)
    qs = q1 + q2
    gate1 = p_group * q1 / qs
    gate2 = p_group * q2 / qs
    e1 = g_idx * epg + i1
    e2 = g_idx * epg + i2

    rowx = lax.broadcasted_iota(jnp.int32, (N_EXPERTS, tb), 0)
    hit1 = rowx == e1
    hit2 = rowx == e2
    member = (hit1 | hit2).astype(BF16)
    before = (lax.broadcasted_iota(jnp.int32, (tb, tb), 0)
              < lax.broadcasted_iota(jnp.int32, (tb, tb), 1)).astype(BF16)
    carry = carry_ref[...]
    count = _dot(member, before) + carry[:, 0:1]
    r1 = jnp.sum(jnp.where(hit1, count, 0.0), axis=0, keepdims=True).astype(jnp.int32)
    r2 = jnp.sum(jnp.where(hit2, count, 0.0), axis=0, keepdims=True).astype(jnp.int32)
    carry = carry + jnp.sum(member.astype(F32), axis=1, keepdims=True)
    carry_ref[...] = carry
    cnt_ref[...] = carry

    row8 = lax.broadcasted_iota(jnp.int32, (8, tb), 0)
    idx_ref[...] = jnp.where(row8 == 0, e1, jnp.where(row8 == 1, e2,
                             jnp.where(row8 == 2, r1, jnp.where(row8 == 3, r2, 0))))
    gate_ref[...] = jnp.where(row8 == 0, gate1, jnp.where(row8 == 1, gate2, 0.0))


def _expert_kernel(be_ref, nx_ref, nu_ref, dest_ref, neg_hbm, h2_hbm, wg_hbm, wu_hbm, wd_hbm, y_hbm,
                   pair_ref, h2v, xg, ystage, wgf, wuf, wdf, wgb, wub, wdb, hsem, wsem, ysem):
    n_used = nu_ref[0]
    rows = MOE_BLOCK
    n_tok = h2v.shape[0]

    def weight_copy(e, m):
        src, stage = ((wg_hbm, wgf), (wu_hbm, wuf), (wd_hbm, wdf))[m]
        return pltpu.make_async_copy(src.at[e], stage, wsem.at[m])

    def gather_row(blk, r, half):
        pair = jnp.maximum(pair_ref[blk * rows + r], 0)
        xg[half, r] = h2v[jnp.where(pair >= n_tok, pair - n_tok, pair)]

    def row_write(blk, r, half, dummy):
        pair = pair_ref[blk * rows + r]
        dst = jnp.where((pair < 0) | dummy, TOP_K * n_tok + half * rows + r, pair)
        return pltpu.make_async_copy(ystage.at[half, r], y_hbm.at[dst], ysem.at[half])

    def wait_row_writes(half):
        pltpu.make_async_copy(ystage.at[half], y_hbm.at[pl.ds(0, rows)], ysem.at[half]).wait()

    weight_priority = 1

    fill = pltpu.make_async_copy(neg_hbm, pair_ref, hsem.at[1])
    fill.start()
    h2_copy = pltpu.make_async_copy(h2_hbm, h2v, hsem.at[0])
    h2_copy.start()
    for m in range(3):
        weight_copy(be_ref[0], m).start(priority=weight_priority)
    ystage[...] = jnp.zeros_like(ystage)

    fill.wait()

    def place(p, carry):
        pair_ref[dest_ref[p]] = p
        return carry
    lax.fori_loop(0, dest_ref.shape[0], place, 0, unroll=8)

    h2_copy.wait()
    spare_fill = pltpu.make_async_copy(ystage.at[0], y_hbm.at[pl.ds(TOP_K * n_tok, rows)], hsem.at[0])
    spare_fill.start()
    spare_fill.wait()

    def first_gather(r, carry):
        gather_row(0, r, 0)
        return carry
    lax.fori_loop(0, rows, first_gather, 0)

    def block(b, carry):
        slot = b % 2
        first_of_expert = (b == 0) | (be_ref[b] != be_ref[jnp.maximum(b - 1, 0)])

        @pl.when(first_of_expert)
        def _():
            nxt = nx_ref[b]
            for m, (stage, dst) in enumerate(((wgf, wgb), (wuf, wub), (wdf, wdb))):
                weight_copy(be_ref[b], m).wait()
                dst[...] = stage[...].astype(BF16)

                @pl.when(nxt >= 0)
                def _():
                    weight_copy(nxt, m).start(priority=weight_priority)

        xb = xg[slot].reshape(rows, wgb.shape[0])
        prev_blk = jnp.maximum(b - 1, 0)
        for r in range(rows):
            row_write(prev_blk, r, 1 - slot, b == 0).start(priority=r % 2)
        for r in range(rows):
            gather_row(b + 1, r, 1 - slot)
        hg = _dot(xb, wgb[...])
        hu = _dot(xb, wub[...])
        act = (hg * _sigmoid(hg)) * hu
        y = _dot(act.astype(BF16), wdb[...]).astype(BF16)

        @pl.when(b >= 1)
        def _():
            wait_row_writes(slot)

        ystage[slot] = y.reshape(ystage.shape[1:])
        return carry

    lax.fori_loop(0, n_used, block, 0)

    last_half = (n_used - 1) % 2

    def last_writes(r, carry):
        row_write(n_used - 1, r, last_half, False).start()
        return carry
    lax.fori_loop(0, rows, last_writes, 0)
    wait_row_writes(1 - last_half)
    wait_row_writes(last_half)


def _experts(block_e, next_e, n_used, dest, h2c, wg, wu, wd):
    n_rows = block_e.shape[0] * MOE_BLOCK
    n_tok = h2c.shape[0]
    _, d, de = wg.shape
    slab = (d // LANES, LANES)
    grid_spec = pltpu.PrefetchScalarGridSpec(
        num_scalar_prefetch=4,
        grid=(1,),
        in_specs=[pl.BlockSpec(memory_space=pl.ANY)] * 5,
        out_specs=pl.BlockSpec(memory_space=pl.ANY),
        scratch_shapes=[
            pltpu.SMEM((n_rows,), jnp.int32),
            pltpu.VMEM((n_tok,) + slab, BF16),
            pltpu.VMEM((2, MOE_BLOCK) + slab, BF16),
            pltpu.VMEM((2, MOE_BLOCK) + slab, BF16),
            pltpu.VMEM((d, de), F32),
            pltpu.VMEM((d, de), F32),
            pltpu.VMEM((de, d), F32),
            pltpu.VMEM((d, de), BF16),
            pltpu.VMEM((d, de), BF16),
            pltpu.VMEM((de, d), BF16),
            pltpu.SemaphoreType.DMA((2,)),
            pltpu.SemaphoreType.DMA((3,)),
            pltpu.SemaphoreType.DMA((2,)),
        ],
    )
    padding_marks = jnp.full((n_rows,), -1, jnp.int32)
    return pl.pallas_call(
        _expert_kernel,
        grid_spec=grid_spec,
        out_shape=jax.ShapeDtypeStruct((TOP_K * n_tok + 2 * MOE_BLOCK,) + slab, BF16),
        compiler_params=_cparams(("arbitrary",), vmem=RESIDENT_VMEM_LIMIT),
        name="experts",
    )(block_e, next_e, n_used, dest, padding_marks, h2c, wg, wu, wd)


def _combine_kernel(y0_ref, y1_ref, gate_ref, x1_ref, g_ref, o_ref):
    gates = gate_ref[...]
    y0 = y0_ref[...].reshape(x1_ref.shape).astype(F32)
    y1 = y1_ref[...].reshape(x1_ref.shape).astype(F32)
    x2 = x1_ref[...] + (y0 * gates[:, 0:1] + y1 * gates[:, 1:2])
    ms = jnp.mean(x2 * x2, axis=-1, keepdims=True)
    o_ref[...] = x2 * lax.rsqrt(ms + RMS_EPS) * g_ref[...]


def _combine(y2, gates, x1, g, tm):
    s, d = x1.shape
    slab = (d // LANES, LANES)
    n_steps = s // tm
    return pl.pallas_call(
        _combine_kernel,
        grid=(n_steps,),
        in_specs=[
            pl.BlockSpec((tm,) + slab, lambda i: (i, 0, 0)),
            pl.BlockSpec((tm,) + slab, lambda i: (n_steps + i, 0, 0)),
            pl.BlockSpec((tm, TOP_K), lambda i: (i, 0)),
            pl.BlockSpec((tm, d), lambda i: (i, 0)),
            pl.BlockSpec((1, d), lambda i: (0, 0)),
        ],
        out_specs=pl.BlockSpec((tm, d), lambda i: (i, 0)),
        out_shape=jax.ShapeDtypeStruct((s, d), F32),
        compiler_params=_cparams(("parallel",)),
        name="combine",
    )(y2, y2, gates, x1, g)


def _tile(n, pref):
    return pref if n % pref == 0 else n


def kernel(x, norm_mix_g, w_in, w_gla_a2, b_gla_a, gla_norm_g, attn_sinks, rel_bias_table, w_out,
           norm_ffn_g, w_router_group, b_router_group, w_router_expert, b_router_expert,
           w_expert_gate, w_expert_up, w_expert_down, norm_final_g):
    b, s, d = x.shape
    assert b == 1 and w_in.shape[0] == 1, "single batch, single layer"
    assert (s * TOP_K) % MOE_BLOCK == 0
    x2d = x.reshape(s, d)
    w_in_t = w_in[0].T

    w2 = jnp.pad(w_gla_a2[0], ((0, LANES - GLA_LOWRANK), (0, 0))).astype(BF16)
    h, log_a = _norm_loga(x2d, norm_mix_g[0].reshape(1, d), w_in_t, w2, b_gla_a[0].reshape(1, -1),
                          _tile(s, 1024))
    proj = _in_proj(h, w_in_t, _tile(s, 4096), 512)

    bucket, is_prev = _folded_maps()
    bias = _rel_bias(rel_bias_table.T, jnp.asarray(bucket), jnp.asarray(is_prev))
    o_attn = _swa(proj, attn_sinks[0], bias)
    o_gla = _gla(proj, log_a, gla_norm_g[0].reshape(1, -1), _tile(s, 256))

    wr = jnp.zeros((ROUTER_ROWS, d), F32)
    wr = wr.at[0:N_GROUPS].set(w_router_group[0].T).at[8:].set(w_router_expert[0].T).astype(BF16)
    bg = jnp.pad(b_router_group[0], (0, 8 - N_GROUPS)).reshape(8, 1)
    x1, h2, idx, gate, counts = _merge_out(o_attn, o_gla, proj, x2d, w_out[0].astype(BF16),
                                           norm_ffn_g[0].reshape(1, d), wr, bg,
                                           b_router_expert[0].reshape(N_EXPERTS, 1), _tile(s, 512))
    counts = counts[:, 0].astype(jnp.int32)
    padded = (counts + MOE_BLOCK - 1) // MOE_BLOCK * MOE_BLOCK
    pend = jnp.cumsum(padded)
    pstart = pend - padded
    expert = idx[0:TOP_K]
    eids = jnp.arange(N_EXPERTS, dtype=jnp.int32)[:, None, None]
    dest = jnp.sum(jnp.where(expert[None] == eids, pstart[:, None, None], 0), axis=0) + idx[TOP_K:2 * TOP_K]
    n_pairs = s * TOP_K
    n_blocks = (n_pairs + N_EXPERTS * (MOE_BLOCK - 1) + MOE_BLOCK - 1) // MOE_BLOCK
    n_used = (pend[-1] // MOE_BLOCK).reshape(1)
    block_start = jnp.arange(n_blocks, dtype=jnp.int32) * MOE_BLOCK
    block_e = jnp.minimum(jnp.sum((pend[None, :] <= block_start[:, None]).astype(jnp.int32), axis=1),
                          N_EXPERTS - 1)

    e_col = jnp.arange(N_EXPERTS, dtype=jnp.int32)[:, None]
    e_row = jnp.arange(N_EXPERTS, dtype=jnp.int32)[None, :]
    later = (e_row > e_col) & (counts > 0)[None, :]
    next_nonempty = jnp.min(jnp.where(later, e_row, N_EXPERTS), axis=1)
    next_nonempty = jnp.where(next_nonempty < N_EXPERTS, next_nonempty, -1)
    next_e = jnp.sum(jnp.where(block_e[:, None] == e_row, next_nonempty[None, :], 0), axis=1)
    y2 = _experts(block_e, next_e, n_used, dest.reshape(-1), h2,
                  w_expert_gate[0], w_expert_up[0], w_expert_down[0])
    out = _combine(y2, gate[0:TOP_K].T, x1, norm_final_g.reshape(1, d), _tile(s, 512))
    return out.reshape(b, s, d)
```

```python
import functools
import math

import numpy as np
import jax
import jax.numpy as jnp
from jax import lax
from jax.experimental import pallas as pl
from jax.experimental.pallas import tpu as pltpu

F32 = jnp.float32
BF16 = jnp.bfloat16

N_HEADS = 32
N_KV_HEADS = 4
HEAD_DIM = 64
WINDOW = 128
N_BUCKETS = 32
MAX_DISTANCE = 128
GLA_HEADS = 4
GLA_DK = 256
GLA_DV = 512
GLA_LOWRANK = 16
GLA_TAU = 16.0
GLA_CHUNK = 64
N_GROUPS = 4
EXPERTS_PER_GROUP = 8
N_EXPERTS = 32
TOP_K = 2
MOE_BLOCK = 256
RMS_EPS = 1e-6
NEG_INF = -1e30

COL_GA = 8704
COL_GATES = 8720
MAIN_COLS = 8704
GATE_COLS = 4096
COL_AQ = GATE_COLS + 0
COL_AK = GATE_COLS + 2048
COL_AV = GATE_COLS + 2304
COL_GQ = GATE_COLS + 2560
COL_GK = GATE_COLS + 3584
COL_GV = GATE_COLS + 4608
COL_GR = GATE_COLS + 6656

LANES = 128
VMEM_LIMIT = 56 * 1024 * 1024
RESIDENT_VMEM_LIMIT = 60 * 1024 * 1024


def _cparams(sem, vmem=VMEM_LIMIT):
    return pltpu.CompilerParams(dimension_semantics=sem, vmem_limit_bytes=vmem)


def _split3(x):
    hi = x.astype(BF16)
    r1 = x - hi.astype(F32)
    mid = r1.astype(BF16)
    lo = (r1 - mid.astype(F32)).astype(BF16)
    return hi, mid, lo


def _sigmoid(x):
    return 0.5 * jnp.tanh(0.5 * x) + 0.5


def _dot(a, b):
    return jnp.dot(a, b, preferred_element_type=F32)


def _dot_nt(a, b):
    return lax.dot_general(a, b, (((1,), (1,)), ((), ())), preferred_element_type=F32)


def _dot_tn(a, b):
    return lax.dot_general(a, b, (((0,), (0,)), ((), ())), preferred_element_type=F32)


def _norm_loga_kernel(x_ref, g_ref, wga_ref, w2_ref, b_ref, h_ref, la_ref):
    x = x_ref[...]
    ms = jnp.mean(x * x, axis=-1, keepdims=True)
    hb = (x * lax.rsqrt(ms + RMS_EPS) * g_ref[...]).astype(BF16)
    h_ref[...] = hb
    row = lax.broadcasted_iota(jnp.int32, wga_ref.shape, 0)
    wga = jnp.where(row < GLA_LOWRANK, wga_ref[...], 0.0).astype(BF16)
    ga = _dot_nt(hb, wga)
    z = _dot(ga.astype(BF16), w2_ref[...]) + b_ref[...]
    la_ref[...] = (jnp.minimum(z, 0.0) - jnp.log1p(jnp.exp(-jnp.abs(z)))) * (1.0 / GLA_TAU)


def _norm_loga(x, g, wga, w2, b, tm):
    s, d = x.shape
    nq = w2.shape[1]
    return pl.pallas_call(
        _norm_loga_kernel,
        grid=(s // tm,),
        in_specs=[
            pl.BlockSpec((tm, d), lambda i: (i, 0)),
            pl.BlockSpec((1, d), lambda i: (0, 0)),
            pl.BlockSpec((LANES, d), lambda i: (COL_GA // LANES, 0)),
            pl.BlockSpec((LANES, nq), lambda i: (0, 0)),
            pl.BlockSpec((1, nq), lambda i: (0, 0)),
        ],
        out_specs=[
            pl.BlockSpec((tm, d), lambda i: (i, 0)),
            pl.BlockSpec((tm, nq), lambda i: (i, 0)),
        ],
        out_shape=[
            jax.ShapeDtypeStruct((s, d), BF16),
            jax.ShapeDtypeStruct((s, nq), F32),
        ],
        compiler_params=_cparams(("parallel",)),
        name="norm_loga",
    )(x, g, wga, w2, b)


def _in_proj_kernel(h_hbm, wt_hbm, o_ref, h_vmem, w_stage, wbf, sem, hsem, *, n_main, n_tiles, tn, n_row_tiles):
    j = pl.program_id(0)
    i = pl.program_id(1)
    n_i = pl.num_programs(1)
    tm = o_ref.shape[0]
    cur, nxt = j % 2, (j + 1) % 2

    def w_copy(jj):
        row0 = jnp.where(jj < n_main, jj * tn, COL_GATES + (jj - n_main) * tn)
        return pltpu.make_async_copy(wt_hbm.at[pl.ds(pl.multiple_of(row0, 8), tn)], w_stage.at[jj % 2],
                                     sem.at[jj % 2])

    def h_copy(ii):
        rs_ = pl.ds(pl.multiple_of(ii * tm, tm), tm)
        return pltpu.make_async_copy(h_hbm.at[rs_], h_vmem.at[rs_], hsem.at[ii])

    @pl.when((j == 0) & (i == 0))
    def _():
        for ii in range(n_row_tiles):
            h_copy(ii).start()
        w_copy(0).start()
        w_copy(1).start()
        w_copy(0).wait()
        wbf[0] = w_stage[0].astype(BF16)
        if n_tiles > 2:
            w_copy(2).start()

    @pl.when(j == 0)
    def _():
        h_copy(i).wait()

    @pl.when(i == 0)
    def _():
        @pl.when(j + 1 < n_tiles)
        def _():
            w_copy(j + 1).wait()

        @pl.when((j >= 1) & (j + 2 < n_tiles))
        def _():
            w_copy(j + 2).start()

    h = h_vmem[pl.ds(pl.multiple_of(i * tm, tm), tm), :]
    o_ref[...] = _dot_nt(h, wbf[cur]).astype(o_ref.dtype)

    rows_per_step = tn // n_i
    rs = pl.ds(pl.multiple_of(i * rows_per_step, rows_per_step), rows_per_step)
    wbf[nxt, rs, :] = w_stage[nxt, rs, :].astype(BF16)


def _in_proj(h, w_t, tm, tn):
    s, d = h.shape
    n_main = MAIN_COLS // tn
    n_gate = GATE_COLS // tn
    n_tiles = n_main + n_gate
    kern = functools.partial(_in_proj_kernel, n_main=n_main, n_tiles=n_tiles, tn=tn, n_row_tiles=s // tm)
    return pl.pallas_call(
        kern,
        grid=(n_tiles, s // tm),
        in_specs=[pl.BlockSpec(memory_space=pl.ANY), pl.BlockSpec(memory_space=pl.ANY)],
        out_specs=pl.BlockSpec((tm, tn), lambda j, i: (i, (j + n_gate) % n_tiles)),
        out_shape=jax.ShapeDtypeStruct((s, n_tiles * tn), BF16),
        scratch_shapes=[
            pltpu.VMEM((s, d), BF16),
            pltpu.VMEM((2, tn, d), F32),
            pltpu.VMEM((2, tn, d), BF16),
            pltpu.SemaphoreType.DMA((2,)),
            pltpu.SemaphoreType.DMA((s // tm,)),
        ],
        compiler_params=_cparams(("arbitrary", "arbitrary"), vmem=RESIDENT_VMEM_LIMIT),
        name="in_proj",
    )(h, w_t)


def _folded_maps():
    j = np.arange(WINDOW)[:, None]
    i = np.arange(WINDOW)[None, :]
    n = (i - j) % WINDOW
    max_exact = N_BUCKETS // 2
    ratio = np.maximum(n, max_exact).astype(np.float32) / np.float32(max_exact)
    large = max_exact + (np.log(ratio) / np.float32(math.log(MAX_DISTANCE / max_exact))
                         * (N_BUCKETS - max_exact)).astype(np.int32)
    large = np.minimum(large, N_BUCKETS - 1)
    bucket = np.where(n < max_exact, n, large).astype(np.int32).reshape(1, -1)
    is_prev = (j > i).astype(np.int32).reshape(1, -1)
    return bucket, is_prev


def _rel_bias_kernel(tab_ref, bucket_ref, prev_ref, o_ref):
    nb = tab_ref.shape[1]
    width = bucket_ref.shape[1]
    onehot = (lax.broadcasted_iota(jnp.int32, (nb, width), 0) == bucket_ref[...]).astype(BF16)
    hi, mid, lo = _split3(tab_ref[...])
    bias = _dot(hi, onehot) + _dot(mid, onehot) + _dot(lo, onehot)
    tile = o_ref.shape[1:]
    o_ref[0] = bias.reshape(tile)
    o_ref[1] = jnp.where(prev_ref[...] > 0, NEG_INF, bias).reshape(tile)


SWA_SCORES_AHEAD = 12
SWA_BLOCKS_PER_STEP = 8


def _swa_kernel(sink_ref, q_ref, kc_ref, kp_ref, vc_ref, vp_ref, tab_ref, bucket_ref, prev_ref, o_ref, bias_s):
    w = WINDOW
    hd = HEAD_DIM
    gq = N_HEADS // N_KV_HEADS
    n_sub = q_ref.shape[0] // w

    @pl.when(pl.program_id(0) == 0)
    def _():
        _rel_bias_kernel(tab_ref, bucket_ref, prev_ref, bias_s)

    bias_ref = bias_s.at[0]
    bias0_ref = bias_s.at[jnp.where(pl.program_id(0) == 0, 1, 0)]
    is_prev = (lax.broadcasted_iota(jnp.int32, (w, w), 0) > lax.broadcasted_iota(jnp.int32, (w, w), 1))
    zeros = jnp.zeros((2 * w, hd), BF16)
    scale = HEAD_DIM ** -0.5
    k_pl, v_pl = {}, {}
    for sub in range(n_sub):
        for g in range(N_KV_HEADS):
            cs = slice(g * hd, (g + 1) * hd)
            if sub == 0:
                k_g = jnp.concatenate([kp_ref[:, cs], kc_ref[0:w, cs]], axis=0) * scale
                v_g = jnp.concatenate([vp_ref[:, cs], vc_ref[0:w, cs]], axis=0)
            else:
                k_g = kc_ref[(sub - 1) * w:(sub + 1) * w, cs] * scale
                v_g = vc_ref[(sub - 1) * w:(sub + 1) * w, cs]
            k_pl[sub, g] = (jnp.concatenate([k_g, zeros], axis=1), jnp.concatenate([zeros, k_g], axis=1))
            v_pl[sub, g] = (jnp.concatenate([v_g, zeros], axis=1), jnp.concatenate([zeros, v_g], axis=1))

    def scores(sub, hh):
        pair = hh - hh % 2
        return _dot_nt(k_pl[sub, hh // gq][hh % 2], q_ref[sub * w:(sub + 1) * w, pair * hd:(pair + 2) * hd])

    def attend(sub, hh, st2):
        bias = bias0_ref if sub == 0 else bias_ref
        st = jnp.where(is_prev, st2[:w], st2[w:]) + bias[hh]
        sink = sink_ref[hh]
        m = jnp.maximum(jnp.max(st, axis=0, keepdims=True), sink)
        p = jnp.exp(st - m)
        recip = 1.0 / (jnp.sum(p, axis=0, keepdims=True) + jnp.exp(sink - m))
        probs = p * recip
        p_cat = jnp.concatenate([jnp.where(is_prev, probs, 0.0), jnp.where(is_prev, 0.0, probs)], axis=0)
        return _dot_tn(p_cat.astype(BF16), v_pl[sub, hh // gq][hh % 2])

    chains = [(sub, hh) for sub in range(n_sub) for hh in range(N_HEADS)]
    ahead = SWA_SCORES_AHEAD
    st2 = {c: scores(*c) for c in chains[:ahead]}
    acc = None
    for n, (sub, hh) in enumerate(chains):
        if n + ahead < len(chains):
            st2[chains[n + ahead]] = scores(*chains[n + ahead])
        part = attend(sub, hh, st2.pop((sub, hh)))
        if hh % 2 == 0:
            acc = part
        else:
            o_ref[sub * w:(sub + 1) * w, (hh - 1) * hd:(hh + 1) * hd] = (acc + part).astype(o_ref.dtype)


def _swa(proj, sinks, table_t, bucket, is_prev):
    s = proj.shape[0]
    w = WINDOW
    nb = s // w
    n_sub = SWA_BLOCKS_PER_STEP if nb % SWA_BLOCKS_PER_STEP == 0 else 1
    kvw = N_KV_HEADS * HEAD_DIM
    dq = N_HEADS * HEAD_DIM
    rows = n_sub * w
    prev = lambda n: jnp.maximum(n * n_sub - 1, 0)
    return pl.pallas_call(
        _swa_kernel,
        grid=(nb // n_sub,),
        in_specs=[
            pl.BlockSpec(memory_space=pltpu.SMEM),
            pl.BlockSpec((rows, dq), lambda n: (n, COL_AQ // dq)),
            pl.BlockSpec((rows, kvw), lambda n: (n, COL_AK // kvw)),
            pl.BlockSpec((w, kvw), lambda n: (prev(n), COL_AK // kvw)),
            pl.BlockSpec((rows, kvw), lambda n: (n, COL_AV // kvw)),
            pl.BlockSpec((w, kvw), lambda n: (prev(n), COL_AV // kvw)),
            pl.BlockSpec(table_t.shape, lambda n: (0, 0)),
            pl.BlockSpec(bucket.shape, lambda n: (0, 0)),
            pl.BlockSpec(is_prev.shape, lambda n: (0, 0)),
        ],
        out_specs=pl.BlockSpec((rows, dq), lambda n: (n, 0)),
        out_shape=jax.ShapeDtypeStruct((s, dq), BF16),
        scratch_shapes=[pltpu.VMEM((2, N_HEADS, w, w), F32)],
        compiler_params=_cparams(("arbitrary",)),
        name="swa",
    )(sinks, proj, proj, proj, proj, proj, table_t, bucket, is_prev)


GLA_CUM_ROWS = 256


def _gla_kernel(*refs):
    nh = GLA_HEADS
    q_refs, k_refs, v_refs, gr_refs = (refs[i * nh:(i + 1) * nh] for i in range(4))
    la_ref, gn_ref, o_ref, state_ref = refs[4 * nh:]
    c = GLA_CHUNK
    rows, dk = q_refs[0].shape
    dv = v_refs[0].shape[1]

    @pl.when(pl.program_id(0) == 0)
    def _():
        state_ref[...] = jnp.zeros_like(state_ref)

    cr = min(rows, GLA_CUM_ROWS)
    ri = lax.broadcasted_iota(jnp.int32, (cr, cr), 0)
    ci = lax.broadcasted_iota(jnp.int32, (cr, cr), 1)
    tri = ((ri // c == ci // c) & (ri >= ci)).astype(BF16)
    cum_parts = []
    for t in range(rows // cr):
        la_parts = _split3(la_ref[t * cr:(t + 1) * cr, :])
        cum_parts.append(_dot(tri, la_parts[0]) + _dot(tri, la_parts[1]) + _dot(tri, la_parts[2]))
    cum_all = jnp.concatenate(cum_parts, axis=0)
    causal = (lax.broadcasted_iota(jnp.int32, (c, c), 0) >= lax.broadcasted_iota(jnp.int32, (c, c), 1))
    gn = gn_ref[...]
    n_chunks = rows // c

    o_intra, q_in, upd, decay = {}, {}, {}, {}
    for h in range(nh):
        for ch in range(n_chunks):
            sl = slice(ch * c, (ch + 1) * c)
            cum_c = cum_all[sl, h * dk:(h + 1) * dk]
            ref = cum_c[c // 2 - 1:c // 2]
            last = cum_c[c - 1:c]
            q = q_refs[h][sl, :].astype(F32) * (dk ** -0.5)
            k = k_refs[h][sl, :].astype(F32)
            v = v_refs[h][sl, :]
            q_intra = (q * jnp.exp(cum_c - ref)).astype(BF16)
            k_intra = (k * jnp.exp(ref - cum_c)).astype(BF16)
            a = jnp.where(causal, _dot_nt(q_intra, k_intra), 0.0)
            o_intra[h, ch] = _dot(a.astype(BF16), v)
            q_in[h, ch] = (q * jnp.exp(cum_c)).astype(BF16)
            k_dec = (k * jnp.exp(last - cum_c)).astype(BF16)
            upd[h, ch] = _dot_tn(k_dec, v)
            decay[h, ch] = jnp.broadcast_to(jnp.exp(last), (LANES, dk)).T

    state = [state_ref[h] for h in range(nh)]
    for ch in range(n_chunks):
        sl = slice(ch * c, (ch + 1) * c)
        for h in range(nh):
            o = o_intra[h, ch] + _dot(q_in[h, ch], state[h].astype(BF16))
            state[h] = jnp.concatenate(
                [state[h][:, jv * LANES:(jv + 1) * LANES] * decay[h, ch]
                 + upd[h, ch][:, jv * LANES:(jv + 1) * LANES] for jv in range(dv // LANES)], axis=1)
            ms = jnp.mean(o * o, axis=-1, keepdims=True)
            on = o * lax.rsqrt(ms + RMS_EPS) * gn
            gr = gr_refs[h][sl, :].astype(F32)
            o_ref[sl, h * dv:(h + 1) * dv] = (on * (gr * _sigmoid(gr))).astype(o_ref.dtype)
    for h in range(nh):
        state_ref[h] = state[h]


def _gla(proj, la, gn, rows):
    s = proj.shape[0]
    dk, dv = GLA_DK, GLA_DV
    nh = GLA_HEADS

    def head_specs(width, col0):
        return [pl.BlockSpec((rows, width), functools.partial(lambda i, h: (i, col0 // width + h), h=h))
                for h in range(nh)]

    in_specs = (head_specs(dk, COL_GQ) + head_specs(dk, COL_GK) + head_specs(dv, COL_GV)
                + head_specs(dv, COL_GR)
                + [pl.BlockSpec((rows, nh * dk), lambda i: (i, 0)), pl.BlockSpec((1, dv), lambda i: (0, 0))])
    return pl.pallas_call(
        _gla_kernel,
        grid=(s // rows,),
        in_specs=in_specs,
        out_specs=pl.BlockSpec((rows, nh * dv), lambda i: (i, 0)),
        out_shape=jax.ShapeDtypeStruct((s, nh * dv), BF16),
        scratch_shapes=[pltpu.VMEM((nh, dk, dv), F32)],
        compiler_params=_cparams(("arbitrary",)),
        name="gla",
    )(*([proj] * (4 * nh)), la, gn)


MERGE_SUB_ROWS = 128


def _merge_out_kernel(oa_ref, og_ref, ga_ref, gg_ref, x_ref, wo_ref, g_ref, wr_ref, bg_ref, be_ref,
                      x1_ref, h2_ref, idx_ref, gate_ref, cnt_ref, lt_ref, carry_ref):
    sub = MERGE_SUB_ROWS
    tiles = [slice(t * sub, (t + 1) * sub) for t in range(x_ref.shape[0] // sub)]
    merged = [(_sigmoid(ga_ref[rs, :].astype(F32)) * oa_ref[rs, :].astype(F32)
               + _sigmoid(gg_ref[rs, :].astype(F32)) * og_ref[rs, :].astype(F32)).astype(BF16) for rs in tiles]
    projected = [_dot(m, wo_ref[...]) for m in merged]
    for rs, y in zip(tiles, projected):
        x1 = x_ref[rs, :] + y
        x1_ref[rs, :] = x1
        ms = jnp.mean(x1 * x1, axis=-1, keepdims=True)
        h2b = (x1 * lax.rsqrt(ms + RMS_EPS) * g_ref[...]).astype(BF16)
        h2_ref[rs] = h2b.reshape((sub,) + h2_ref.shape[1:])
        lt_ref[:, rs] = _dot_nt(wr_ref[...], h2b)
    _route_kernel(lt_ref, bg_ref, be_ref, idx_ref, gate_ref, cnt_ref, carry_ref)


def _merge_out(o_attn, o_gla, gates, x, wo, g, wr, bg, be, tm):
    s, d = x.shape
    nr = wr.shape[0]
    row = lambda i: (i, 0)
    return pl.pallas_call(
        _merge_out_kernel,
        grid=(s // tm,),
        in_specs=[
            pl.BlockSpec((tm, d), row),
            pl.BlockSpec((tm, d), row),
            pl.BlockSpec((tm, d), lambda i: (i, 0)),
            pl.BlockSpec((tm, d), lambda i: (i, 1)),
            pl.BlockSpec((tm, d), row),
            pl.BlockSpec((d, d), lambda i: (0, 0), pipeline_mode=pl.Buffered(1)),
            pl.BlockSpec((1, d), lambda i: (0, 0)),
            pl.BlockSpec((nr, d), lambda i: (0, 0)),
            pl.BlockSpec((8, 1), lambda i: (0, 0)),
            pl.BlockSpec((N_EXPERTS, 1), lambda i: (0, 0)),
        ],
        out_specs=[
            pl.BlockSpec((tm, d), row),
            pl.BlockSpec((tm, d // LANES, LANES), lambda i: (i, 0, 0)),
            pl.BlockSpec((8, tm), lambda i: (0, i)),
            pl.BlockSpec((8, tm), lambda i: (0, i)),
            pl.BlockSpec((N_EXPERTS, LANES), lambda i: (0, 0)),
        ],
        out_shape=[
            jax.ShapeDtypeStruct((s, d), F32),
            jax.ShapeDtypeStruct((s, d // LANES, LANES), BF16),
            jax.ShapeDtypeStruct((8, s), jnp.int32),
            jax.ShapeDtypeStruct((8, s), F32),
            jax.ShapeDtypeStruct((N_EXPERTS, LANES), F32),
        ],
        scratch_shapes=[pltpu.VMEM((nr, tm), F32), pltpu.VMEM((N_EXPERTS, LANES), F32)],
        compiler_params=_cparams(("arbitrary",)),
        name="merge_out",
    )(o_attn, o_gla, gates, gates, x, wo, g, wr, bg, be)


ROUTER_ROWS = 8 + N_EXPERTS


def _route_kernel(lt_ref, bg_ref, be_ref, idx_ref, gate_ref, cnt_ref, carry_ref):
    tb = lt_ref.shape[1]
    epg = EXPERTS_PER_GROUP

    @pl.when(pl.program_id(0) == 0)
    def _():
        carry_ref[...] = jnp.zeros_like(carry_ref)

    logits = lt_ref[...]
    gl = logits[0:N_GROUPS]
    g_exp = jnp.exp(gl - jnp.max(gl, axis=0, keepdims=True))
    g_prob = g_exp / jnp.sum(g_exp, axis=0, keepdims=True)
    gb = gl + bg_ref[0:N_GROUPS]
    rowg = lax.broadcasted_iota(jnp.int32, (N_GROUPS, tb), 0)
    g_idx = jnp.min(jnp.where(gb == jnp.max(gb, axis=0, keepdims=True), rowg, N_GROUPS),
                    axis=0, keepdims=True)
    p_group = jnp.sum(jnp.where(rowg == g_idx, g_prob, 0.0), axis=0, keepdims=True)

    el = logits[8:8 + epg]
    eb = jnp.broadcast_to(be_ref[0:epg], (epg, tb))
    for g in range(1, N_GROUPS):
        pick = g_idx == g
        el = jnp.where(pick, logits[8 + g * epg:8 + (g + 1) * epg], el)
        eb = jnp.where(pick, be_ref[g * epg:(g + 1) * epg], eb)
    e_exp = jnp.exp(el - jnp.max(el, axis=0, keepdims=True))
    e_prob = e_exp / jnp.sum(e_exp, axis=0, keepdims=True)
    score = el + eb
    rowe = lax.broadcasted_iota(jnp.int32, (epg, tb), 0)
    i1 = jnp.min(jnp.where(score == jnp.max(score, axis=0, keepdims=True), rowe, epg),
                 axis=0, keepdims=True)
    score2 = jnp.where(rowe == i1, -jnp.inf, score)
    i2 = jnp.min(jnp.where(score2 == jnp.max(score2, axis=0, keepdims=True), rowe, epg),
                 axis=0, keepdims=True)
    q1 = jnp.sum(jnp.where(rowe == i1, e_prob, 0.0), axis=0, keepdims=True)
    q2 = jnp.sum(jnp.where(rowe == i2, e_prob, 0.0), axis=0, keepdims=True)
    qs = q1 + q2
    gate1 = p_group * q1 / qs
    gate2 = p_group * q2 / qs
    e1 = g_idx * epg + i1
    e2 = g_idx * epg + i2

    rowx = lax.broadcasted_iota(jnp.int32, (N_EXPERTS, tb), 0)
    hit1 = rowx == e1
    hit2 = rowx == e2
    member = (hit1 | hit2).astype(BF16)
    before = (lax.broadcasted_iota(jnp.int32, (tb, tb), 0)
              < lax.broadcasted_iota(jnp.int32, (tb, tb), 1)).astype(BF16)
    carry = carry_ref[...]
    count = _dot(member, before) + carry[:, 0:1]
    r1 = jnp.sum(jnp.where(hit1, count, 0.0), axis=0, keepdims=True).astype(jnp.int32)
    r2 = jnp.sum(jnp.where(hit2, count, 0.0), axis=0, keepdims=True).astype(jnp.int32)
    carry = carry + jnp.sum(member.astype(F32), axis=1, keepdims=True)
    carry_ref[...] = carry
    cnt_ref[...] = carry

    row8 = lax.broadcasted_iota(jnp.int32, (8, tb), 0)
    idx_ref[...] = jnp.where(row8 == 0, e1, jnp.where(row8 == 1, e2,
                             jnp.where(row8 == 2, r1, jnp.where(row8 == 3, r2, 0))))
    gate_ref[...] = jnp.where(row8 == 0, gate1, jnp.where(row8 == 1, gate2, 0.0))


def _expert_kernel(be_ref, nx_ref, nu_ref, dest_ref, neg_hbm, h2_hbm, wg_hbm, wu_hbm, wd_hbm, y_hbm,
                   pair_ref, h2v, xg, ystage, wgf, wuf, wdf, wgb, wub, wdb, hsem, wsem, ysem):
    n_used = nu_ref[0]
    rows = MOE_BLOCK
    n_tok = h2v.shape[0]

    def weight_copy(e, m):
        src, stage = ((wg_hbm, wgf), (wu_hbm, wuf), (wd_hbm, wdf))[m]
        return pltpu.make_async_copy(src.at[e], stage, wsem.at[m])

    def gather_row(blk, r, half):
        pair = jnp.maximum(pair_ref[blk * rows + r], 0)
        xg[half, r] = h2v[jnp.where(pair >= n_tok, pair - n_tok, pair)]

    def row_write(blk, r, half, dummy):
        pair = pair_ref[blk * rows + r]
        dst = jnp.where((pair < 0) | dummy, TOP_K * n_tok + half * rows + r, pair)
        return pltpu.make_async_copy(ystage.at[half, r], y_hbm.at[dst], ysem.at[half])

    def wait_row_writes(half):
        pltpu.make_async_copy(ystage.at[half], y_hbm.at[pl.ds(0, rows)], ysem.at[half]).wait()

    weight_priority = 1

    fill = pltpu.make_async_copy(neg_hbm, pair_ref, hsem.at[1])
    fill.start()
    h2_copy = pltpu.make_async_copy(h2_hbm, h2v, hsem.at[0])
    h2_copy.start()
    for m in range(3):
        weight_copy(be_ref[0], m).start(priority=weight_priority)
    ystage[...] = jnp.zeros_like(ystage)

    fill.wait()

    def place(p, carry):
        pair_ref[dest_ref[p]] = p
        return carry
    lax.fori_loop(0, dest_ref.shape[0], place, 0, unroll=8)

    h2_copy.wait()
    spare_fill = pltpu.make_async_copy(ystage.at[0], y_hbm.at[pl.ds(TOP_K * n_tok, rows)], hsem.at[0])
    spare_fill.start()
    spare_fill.wait()

    def first_gather(r, carry):
        gather_row(0, r, 0)
        return carry
    lax.fori_loop(0, rows, first_gather, 0)

    def block(b, carry):
        slot = b % 2
        first_of_expert = (b == 0) | (be_ref[b] != be_ref[jnp.maximum(b - 1, 0)])

        @pl.when(first_of_expert)
        def _():
            nxt = nx_ref[b]
            for m, (stage, dst) in enumerate(((wgf, wgb), (wuf, wub), (wdf, wdb))):
                weight_copy(be_ref[b], m).wait()
                dst[...] = stage[...].astype(BF16)

                @pl.when(nxt >= 0)
                def _():
                    weight_copy(nxt, m).start(priority=weight_priority)

        xb = xg[slot].reshape(rows, wgb.shape[0])
        prev_blk = jnp.maximum(b - 1, 0)
        for r in range(rows):
            row_write(prev_blk, r, 1 - slot, b == 0).start(priority=r % 2)
        for r in range(rows):
            gather_row(b + 1, r, 1 - slot)
        hg = _dot(xb, wgb[...])
        hu = _dot(xb, wub[...])
        act = (hg * _sigmoid(hg)) * hu
        y = _dot(act.astype(BF16), wdb[...]).astype(BF16)

        @pl.when(b >= 1)
        def _():
            wait_row_writes(slot)

        ystage[slot] = y.reshape(ystage.shape[1:])
        return carry

    lax.fori_loop(0, n_used, block, 0)

    last_half = (n_used - 1) % 2

    def last_writes(r, carry):
        row_write(n_used - 1, r, last_half, False).start()
        return carry
    lax.fori_loop(0, rows, last_writes, 0)
    wait_row_writes(1 - last_half)
    wait_row_writes(last_half)


def _experts(block_e, next_e, n_used, dest, h2c, wg, wu, wd):
    n_rows = block_e.shape[0] * MOE_BLOCK
    n_tok = h2c.shape[0]
    _, d, de = wg.shape
    slab = (d // LANES, LANES)
    grid_spec = pltpu.PrefetchScalarGridSpec(
        num_scalar_prefetch=4,
        grid=(1,),
        in_specs=[pl.BlockSpec(memory_space=pl.ANY)] * 5,
        out_specs=pl.BlockSpec(memory_space=pl.ANY),
        scratch_shapes=[
            pltpu.SMEM((n_rows,), jnp.int32),
            pltpu.VMEM((n_tok,) + slab, BF16),
            pltpu.VMEM((2, MOE_BLOCK) + slab, BF16),
            pltpu.VMEM((2, MOE_BLOCK) + slab, BF16),
            pltpu.VMEM((d, de), F32),
            pltpu.VMEM((d, de), F32),
            pltpu.VMEM((de, d), F32),
            pltpu.VMEM((d, de), BF16),
            pltpu.VMEM((d, de), BF16),
            pltpu.VMEM((de, d), BF16),
            pltpu.SemaphoreType.DMA((2,)),
            pltpu.SemaphoreType.DMA((3,)),
            pltpu.SemaphoreType.DMA((2,)),
        ],
    )
    padding_marks = jnp.full((n_rows,), -1, jnp.int32)
    return pl.pallas_call(
        _expert_kernel,
        grid_spec=grid_spec,
        out_shape=jax.ShapeDtypeStruct((TOP_K * n_tok + 2 * MOE_BLOCK,) + slab, BF16),
        compiler_params=_cparams(("arbitrary",), vmem=RESIDENT_VMEM_LIMIT),
        name="experts",
    )(block_e, next_e, n_used, dest, padding_marks, h2c, wg, wu, wd)


def _combine_kernel(y0_ref, y1_ref, gate_ref, x1_ref, g_ref, o_ref):
    gates = gate_ref[...]
    y0 = y0_ref[...].reshape(x1_ref.shape).astype(F32)
    y1 = y1_ref[...].reshape(x1_ref.shape).astype(F32)
    x2 = x1_ref[...] + (y0 * gates[:, 0:1] + y1 * gates[:, 1:2])
    ms = jnp.mean(x2 * x2, axis=-1, keepdims=True)
    o_ref[...] = x2 * lax.rsqrt(ms + RMS_EPS) * g_ref[...]


def _combine(y2, gates, x1, g, tm):
    s, d = x1.shape
    slab = (d // LANES, LANES)
    n_steps = s // tm
    return pl.pallas_call(
        _combine_kernel,
        grid=(n_steps,),
        in_specs=[
            pl.BlockSpec((tm,) + slab, lambda i: (i, 0, 0)),
            pl.BlockSpec((tm,) + slab, lambda i: (n_steps + i, 0, 0)),
            pl.BlockSpec((tm, TOP_K), lambda i: (i, 0)),
            pl.BlockSpec((tm, d), lambda i: (i, 0)),
            pl.BlockSpec((1, d), lambda i: (0, 0)),
        ],
        out_specs=pl.BlockSpec((tm, d), lambda i: (i, 0)),
        out_shape=jax.ShapeDtypeStruct((s, d), F32),
        compiler_params=_cparams(("parallel",)),
        name="combine",
    )(y2, y2, gates, x1, g)


def _tile(n, pref):
    return pref if n % pref == 0 else n


def kernel(x, norm_mix_g, w_in, w_gla_a2, b_gla_a, gla_norm_g, attn_sinks, rel_bias_table, w_out,
           norm_ffn_g, w_router_group, b_router_group, w_router_expert, b_router_expert,
           w_expert_gate, w_expert_up, w_expert_down, norm_final_g):
    b, s, d = x.shape
    assert b == 1 and w_in.shape[0] == 1, "single batch, single layer"
    assert (s * TOP_K) % MOE_BLOCK == 0
    x2d = x.reshape(s, d)
    w_in_t = w_in[0].T

    w2 = jnp.pad(w_gla_a2[0], ((0, LANES - GLA_LOWRANK), (0, 0))).astype(BF16)
    h, log_a = _norm_loga(x2d, norm_mix_g[0].reshape(1, d), w_in_t, w2, b_gla_a[0].reshape(1, -1),
                          _tile(s, 1024))
    proj = _in_proj(h, w_in_t, _tile(s, 4096), 512)

    bucket, is_prev = _folded_maps()
    o_attn = _swa(proj, attn_sinks[0], rel_bias_table.T, jnp.asarray(bucket), jnp.asarray(is_prev))
    o_gla = _gla(proj, log_a, gla_norm_g[0].reshape(1, -1), _tile(s, 256))

    wr = jnp.zeros((ROUTER_ROWS, d), F32)
    wr = wr.at[0:N_GROUPS].set(w_router_group[0].T).at[8:].set(w_router_expert[0].T).astype(BF16)
    bg = jnp.pad(b_router_group[0], (0, 8 - N_GROUPS)).reshape(8, 1)
    x1, h2, idx, gate, counts = _merge_out(o_attn, o_gla, proj, x2d, w_out[0].astype(BF16),
                                           norm_ffn_g[0].reshape(1, d), wr, bg,
                                           b_router_expert[0].reshape(N_EXPERTS, 1), _tile(s, 512))
    counts = counts[:, 0].astype(jnp.int32)
    padded = (counts + MOE_BLOCK - 1) // MOE_BLOCK * MOE_BLOCK
    pend = jnp.cumsum(padded)
    pstart = pend - padded
    expert = idx[0:TOP_K]
    eids = jnp.arange(N_EXPERTS, dtype=jnp.int32)[:, None, None]
    dest = jnp.sum(jnp.where(expert[None] == eids, pstart[:, None, None], 0), axis=0) + idx[TOP_K:2 * TOP_K]
    n_pairs = s * TOP_K
    n_blocks = (n_pairs + N_EXPERTS * (MOE_BLOCK - 1) + MOE_BLOCK - 1) // MOE_BLOCK
    n_used = (pend[-1] // MOE_BLOCK).reshape(1)
    block_start = jnp.arange(n_blocks, dtype=jnp.int32) * MOE_BLOCK
    block_e = jnp.minimum(jnp.sum((pend[None, :] <= block_start[:, None]).astype(jnp.int32), axis=1),
                          N_EXPERTS - 1)

    e_col = jnp.arange(N_EXPERTS, dtype=jnp.int32)[:, None]
    e_row = jnp.arange(N_EXPERTS, dtype=jnp.int32)[None, :]
    later = (e_row > e_col) & (counts > 0)[None, :]
    next_nonempty = jnp.min(jnp.where(later, e_row, N_EXPERTS), axis=1)
    next_nonempty = jnp.where(next_nonempty < N_EXPERTS, next_nonempty, -1)
    next_e = jnp.sum(jnp.where(block_e[:, None] == e_row, next_nonempty[None, :], 0), axis=1)
    y2 = _experts(block_e, next_e, n_used, dest.reshape(-1), h2,
                  w_expert_gate[0], w_expert_up[0], w_expert_down[0])
    out = _combine(y2, gate[0:TOP_K].T, x1, norm_final_g.reshape(1, d), _tile(s, 512))
    return out.reshape(b, s, d)
```

```python
import functools
import math

import numpy as np
import jax
import jax.numpy as jnp
from jax import lax
from jax.experimental import pallas as pl
from jax.experimental.pallas import tpu as pltpu

F32 = jnp.float32
BF16 = jnp.bfloat16

N_HEADS = 32
N_KV_HEADS = 4
HEAD_DIM = 64
WINDOW = 128
N_BUCKETS = 32
MAX_DISTANCE = 128
GLA_HEADS = 4
GLA_DK = 256
GLA_DV = 512
GLA_LOWRANK = 16
GLA_TAU = 16.0
GLA_CHUNK = 64
N_GROUPS = 4
EXPERTS_PER_GROUP = 8
N_EXPERTS = 32
TOP_K = 2
MOE_BLOCK = 256
RMS_EPS = 1e-6
NEG_INF = -1e30

COL_GA = 8704
COL_GATES = 8720
MAIN_COLS = 8704
GATE_COLS = 4096
COL_AQ = GATE_COLS + 0
COL_AK = GATE_COLS + 2048
COL_AV = GATE_COLS + 2304
COL_GQ = GATE_COLS + 2560
COL_GK = GATE_COLS + 3584
COL_GV = GATE_COLS + 4608
COL_GR = GATE_COLS + 6656

LANES = 128
VMEM_LIMIT = 56 * 1024 * 1024
RESIDENT_VMEM_LIMIT = 60 * 1024 * 1024


def _cparams(sem, vmem=VMEM_LIMIT):
    return pltpu.CompilerParams(dimension_semantics=sem, vmem_limit_bytes=vmem)


def _split3(x):
    hi = x.astype(BF16)
    r1 = x - hi.astype(F32)
    mid = r1.astype(BF16)
    lo = (r1 - mid.astype(F32)).astype(BF16)
    return hi, mid, lo


def _sigmoid(x):
    return 0.5 * jnp.tanh(0.5 * x) + 0.5


def _dot(a, b):
    return jnp.dot(a, b, preferred_element_type=F32)


def _dot_nt(a, b):
    return lax.dot_general(a, b, (((1,), (1,)), ((), ())), preferred_element_type=F32)


def _dot_tn(a, b):
    return lax.dot_general(a, b, (((0,), (0,)), ((), ())), preferred_element_type=F32)


def _norm_loga_kernel(x_ref, g_ref, wga_ref, w2_ref, b_ref, h_ref, la_ref):
    x = x_ref[...]
    ms = jnp.mean(x * x, axis=-1, keepdims=True)
    hb = (x * lax.rsqrt(ms + RMS_EPS) * g_ref[...]).astype(BF16)
    h_ref[...] = hb
    row = lax.broadcasted_iota(jnp.int32, wga_ref.shape, 0)
    wga = jnp.where(row < GLA_LOWRANK, wga_ref[...], 0.0).astype(BF16)
    ga = _dot_nt(hb, wga)
    z = _dot(ga.astype(BF16), w2_ref[...]) + b_ref[...]
    la_ref[...] = (jnp.minimum(z, 0.0) - jnp.log1p(jnp.exp(-jnp.abs(z)))) * (1.0 / GLA_TAU)


def _norm_loga(x, g, wga, w2, b, tm):
    s, d = x.shape
    nq = w2.shape[1]
    return pl.pallas_call(
        _norm_loga_kernel,
        grid=(s // tm,),
        in_specs=[
            pl.BlockSpec((tm, d), lambda i: (i, 0)),
            pl.BlockSpec((1, d), lambda i: (0, 0)),
            pl.BlockSpec((LANES, d), lambda i: (COL_GA // LANES, 0)),
            pl.BlockSpec((LANES, nq), lambda i: (0, 0)),
            pl.BlockSpec((1, nq), lambda i: (0, 0)),
        ],
        out_specs=[
            pl.BlockSpec((tm, d), lambda i: (i, 0)),
            pl.BlockSpec((tm, nq), lambda i: (i, 0)),
        ],
        out_shape=[
            jax.ShapeDtypeStruct((s, d), BF16),
            jax.ShapeDtypeStruct((s, nq), F32),
        ],
        compiler_params=_cparams(("parallel",)),
        name="norm_loga",
    )(x, g, wga, w2, b)


def _in_proj_kernel(h_hbm, wt_hbm, o_ref, h_vmem, w_stage, wbf, sem, hsem, *, n_main, n_tiles, tn, n_row_tiles):
    j = pl.program_id(0)
    i = pl.program_id(1)
    n_i = pl.num_programs(1)
    tm = o_ref.shape[0]
    cur, nxt = j % 2, (j + 1) % 2

    def w_copy(jj):
        row0 = jnp.where(jj < n_main, jj * tn, COL_GATES + (jj - n_main) * tn)
        return pltpu.make_async_copy(wt_hbm.at[pl.ds(pl.multiple_of(row0, 8), tn)], w_stage.at[jj % 2],
                                     sem.at[jj % 2])

    def h_copy(ii):
        rs_ = pl.ds(pl.multiple_of(ii * tm, tm), tm)
        return pltpu.make_async_copy(h_hbm.at[rs_], h_vmem.at[rs_], hsem.at[ii])

    @pl.when((j == 0) & (i == 0))
    def _():
        for ii in range(n_row_tiles):
            h_copy(ii).start()
        w_copy(0).start()
        w_copy(1).start()
        w_copy(0).wait()
        wbf[0] = w_stage[0].astype(BF16)
        if n_tiles > 2:
            w_copy(2).start()

    @pl.when(j == 0)
    def _():
        h_copy(i).wait()

    @pl.when(i == 0)
    def _():
        @pl.when(j + 1 < n_tiles)
        def _():
            w_copy(j + 1).wait()

        @pl.when((j >= 1) & (j + 2 < n_tiles))
        def _():
            w_copy(j + 2).start()

    h = h_vmem[pl.ds(pl.multiple_of(i * tm, tm), tm), :]
    o_ref[...] = _dot_nt(h, wbf[cur]).astype(o_ref.dtype)

    rows_per_step = tn // n_i
    rs = pl.ds(pl.multiple_of(i * rows_per_step, rows_per_step), rows_per_step)
    wbf[nxt, rs, :] = w_stage[nxt, rs, :].astype(BF16)


def _in_proj(h, w_t, tm, tn):
    s, d = h.shape
    n_main = MAIN_COLS // tn
    n_gate = GATE_COLS // tn
    n_tiles = n_main + n_gate
    kern = functools.partial(_in_proj_kernel, n_main=n_main, n_tiles=n_tiles, tn=tn, n_row_tiles=s // tm)
    return pl.pallas_call(
        kern,
        grid=(n_tiles, s // tm),
        in_specs=[pl.BlockSpec(memory_space=pl.ANY), pl.BlockSpec(memory_space=pl.ANY)],
        out_specs=pl.BlockSpec((tm, tn), lambda j, i: (i, (j + n_gate) % n_tiles)),
        out_shape=jax.ShapeDtypeStruct((s, n_tiles * tn), BF16),
        scratch_shapes=[
            pltpu.VMEM((s, d), BF16),
            pltpu.VMEM((2, tn, d), F32),
            pltpu.VMEM((2, tn, d), BF16),
            pltpu.SemaphoreType.DMA((2,)),
            pltpu.SemaphoreType.DMA((s // tm,)),
        ],
        compiler_params=_cparams(("arbitrary", "arbitrary"), vmem=RESIDENT_VMEM_LIMIT),
        name="in_proj",
    )(h, w_t)


def _folded_maps():
    j = np.arange(WINDOW)[:, None]
    i = np.arange(WINDOW)[None, :]
    n = (i - j) % WINDOW
    max_exact = N_BUCKETS // 2
    ratio = np.maximum(n, max_exact).astype(np.float32) / np.float32(max_exact)
    large = max_exact + (np.log(ratio) / np.float32(math.log(MAX_DISTANCE / max_exact))
                         * (N_BUCKETS - max_exact)).astype(np.int32)
    large = np.minimum(large, N_BUCKETS - 1)
    bucket = np.where(n < max_exact, n, large).astype(np.int32).reshape(1, -1)
    is_prev = (j > i).astype(np.int32).reshape(1, -1)
    return bucket, is_prev


def _rel_bias_kernel(tab_ref, bucket_ref, prev_ref, o_ref):
    nb = tab_ref.shape[1]
    width = bucket_ref.shape[1]
    onehot = (lax.broadcasted_iota(jnp.int32, (nb, width), 0) == bucket_ref[...]).astype(BF16)
    hi, mid, lo = _split3(tab_ref[...])
    bias = _dot(hi, onehot) + _dot(mid, onehot) + _dot(lo, onehot)
    tile = o_ref.shape[1:]
    o_ref[0] = bias.reshape(tile)
    o_ref[1] = jnp.where(prev_ref[...] > 0, NEG_INF, bias).reshape(tile)


SWA_SCORES_AHEAD = 12
SWA_BLOCKS_PER_STEP = 8


def _swa_kernel(sink_ref, q_ref, kc_ref, kp_ref, vc_ref, vp_ref, tab_ref, bucket_ref, prev_ref, o_ref, bias_s):
    w = WINDOW
    hd = HEAD_DIM
    gq = N_HEADS // N_KV_HEADS
    n_sub = q_ref.shape[0] // w

    @pl.when(pl.program_id(0) == 0)
    def _():
        _rel_bias_kernel(tab_ref, bucket_ref, prev_ref, bias_s)

    bias_ref = bias_s.at[0]
    bias0_ref = bias_s.at[jnp.where(pl.program_id(0) == 0, 1, 0)]
    is_prev = (lax.broadcasted_iota(jnp.int32, (w, w), 0) > lax.broadcasted_iota(jnp.int32, (w, w), 1))
    zeros = jnp.zeros((2 * w, hd), BF16)
    scale = HEAD_DIM ** -0.5
    k_pl, v_pl = {}, {}
    for sub in range(n_sub):
        for g in range(N_KV_HEADS):
            cs = slice(g * hd, (g + 1) * hd)
            if sub == 0:
                k_g = jnp.concatenate([kp_ref[:, cs], kc_ref[0:w, cs]], axis=0) * scale
                v_g = jnp.concatenate([vp_ref[:, cs], vc_ref[0:w, cs]], axis=0)
            else:
                k_g = kc_ref[(sub - 1) * w:(sub + 1) * w, cs] * scale
                v_g = vc_ref[(sub - 1) * w:(sub + 1) * w, cs]
            k_pl[sub, g] = (jnp.concatenate([k_g, zeros], axis=1), jnp.concatenate([zeros, k_g], axis=1))
            v_pl[sub, g] = (jnp.concatenate([v_g, zeros], axis=1), jnp.concatenate([zeros, v_g], axis=1))

    def scores(sub, hh):
        pair = hh - hh % 2
        return _dot_nt(k_pl[sub, hh // gq][hh % 2], q_ref[sub * w:(sub + 1) * w, pair * hd:(pair + 2) * hd])

    def attend(sub, hh, st2):
        bias = bias0_ref if sub == 0 else bias_ref
        st = jnp.where(is_prev, st2[:w], st2[w:]) + bias[hh]
        sink = sink_ref[hh]
        m = jnp.maximum(jnp.max(st, axis=0, keepdims=True), sink)
        p = jnp.exp(st - m)
        recip = 1.0 / (jnp.sum(p, axis=0, keepdims=True) + jnp.exp(sink - m))
        probs = p * recip
        p_cat = jnp.concatenate([jnp.where(is_prev, probs, 0.0), jnp.where(is_prev, 0.0, probs)], axis=0)
        return _dot_tn(p_cat.astype(BF16), v_pl[sub, hh // gq][hh % 2])

    chains = [(sub, hh) for sub in range(n_sub) for hh in range(N_HEADS)]
    ahead = SWA_SCORES_AHEAD
    st2 = {c: scores(*c) for c in chains[:ahead]}
    acc = None
    for n, (sub, hh) in enumerate(chains):
        if n + ahead < len(chains):
            st2[chains[n + ahead]] = scores(*chains[n + ahead])
        part = attend(sub, hh, st2.pop((sub, hh)))
        if hh % 2 == 0:
            acc = part
        else:
            o_ref[sub * w:(sub + 1) * w, (hh - 1) * hd:(hh + 1) * hd] = (acc + part).astype(o_ref.dtype)


def _swa(proj, sinks, table_t, bucket, is_prev):
    s = proj.shape[0]
    w = WINDOW
    nb = s // w
    n_sub = SWA_BLOCKS_PER_STEP if nb % SWA_BLOCKS_PER_STEP == 0 else 1
    kvw = N_KV_HEADS * HEAD_DIM
    dq = N_HEADS * HEAD_DIM
    rows = n_sub * w
    prev = lambda n: jnp.maximum(n * n_sub - 1, 0)
    return pl.pallas_call(
        _swa_kernel,
        grid=(nb // n_sub,),
        in_specs=[
            pl.BlockSpec(memory_space=pltpu.SMEM),
            pl.BlockSpec((rows, dq), lambda n: (n, COL_AQ // dq)),
            pl.BlockSpec((rows, kvw), lambda n: (n, COL_AK // kvw)),
            pl.BlockSpec((w, kvw), lambda n: (prev(n), COL_AK // kvw)),
            pl.BlockSpec((rows, kvw), lambda n: (n, COL_AV // kvw)),
            pl.BlockSpec((w, kvw), lambda n: (prev(n), COL_AV // kvw)),
            pl.BlockSpec(table_t.shape, lambda n: (0, 0)),
            pl.BlockSpec(bucket.shape, lambda n: (0, 0)),
            pl.BlockSpec(is_prev.shape, lambda n: (0, 0)),
        ],
        out_specs=pl.BlockSpec((rows, dq), lambda n: (n, 0)),
        out_shape=jax.ShapeDtypeStruct((s, dq), BF16),
        scratch_shapes=[pltpu.VMEM((2, N_HEADS, w, w), F32)],
        compiler_params=_cparams(("arbitrary",)),
        name="swa",
    )(sinks, proj, proj, proj, proj, proj, table_t, bucket, is_prev)


GLA_CUM_ROWS = 256


def _gla_kernel(*refs):
    nh = GLA_HEADS
    q_refs, k_refs, v_refs, gr_refs = (refs[i * nh:(i + 1) * nh] for i in range(4))
    la_ref, gn_ref, wo_ref, o_ref, wo_bf_ref, state_ref = refs[4 * nh:]
    wo_bf_ref[...] = wo_ref[...].astype(BF16)
    c = GLA_CHUNK
    rows, dk = q_refs[0].shape
    dv = v_refs[0].shape[1]

    @pl.when(pl.program_id(0) == 0)
    def _():
        state_ref[...] = jnp.zeros_like(state_ref)

    cr = min(rows, GLA_CUM_ROWS)
    ri = lax.broadcasted_iota(jnp.int32, (cr, cr), 0)
    ci = lax.broadcasted_iota(jnp.int32, (cr, cr), 1)
    tri = ((ri // c == ci // c) & (ri >= ci)).astype(BF16)
    cum_parts = []
    for t in range(rows // cr):
        la_parts = _split3(la_ref[t * cr:(t + 1) * cr, :])
        cum_parts.append(_dot(tri, la_parts[0]) + _dot(tri, la_parts[1]) + _dot(tri, la_parts[2]))
    cum_all = jnp.concatenate(cum_parts, axis=0)
    causal = (lax.broadcasted_iota(jnp.int32, (c, c), 0) >= lax.broadcasted_iota(jnp.int32, (c, c), 1))
    gn = gn_ref[...]
    n_chunks = rows // c

    o_intra, q_in, upd, decay = {}, {}, {}, {}
    for h in range(nh):
        for ch in range(n_chunks):
            sl = slice(ch * c, (ch + 1) * c)
            cum_c = cum_all[sl, h * dk:(h + 1) * dk]
            ref = cum_c[c // 2 - 1:c // 2]
            last = cum_c[c - 1:c]
            q = q_refs[h][sl, :].astype(F32) * (dk ** -0.5)
            k = k_refs[h][sl, :].astype(F32)
            v = v_refs[h][sl, :]
            q_intra = (q * jnp.exp(cum_c - ref)).astype(BF16)
            k_intra = (k * jnp.exp(ref - cum_c)).astype(BF16)
            a = jnp.where(causal, _dot_nt(q_intra, k_intra), 0.0)
            o_intra[h, ch] = _dot(a.astype(BF16), v)
            q_in[h, ch] = (q * jnp.exp(cum_c)).astype(BF16)
            k_dec = (k * jnp.exp(last - cum_c)).astype(BF16)
            upd[h, ch] = _dot_tn(k_dec, v)
            decay[h, ch] = jnp.broadcast_to(jnp.exp(last), (LANES, dk)).T

    state = [state_ref[h] for h in range(nh)]
    for ch in range(n_chunks):
        sl = slice(ch * c, (ch + 1) * c)
        for h in range(nh):
            o = o_intra[h, ch] + _dot(q_in[h, ch], state[h].astype(BF16))
            state[h] = jnp.concatenate(
                [state[h][:, jv * LANES:(jv + 1) * LANES] * decay[h, ch]
                 + upd[h, ch][:, jv * LANES:(jv + 1) * LANES] for jv in range(dv // LANES)], axis=1)
            ms = jnp.mean(o * o, axis=-1, keepdims=True)
            on = o * lax.rsqrt(ms + RMS_EPS) * gn
            gr = gr_refs[h][sl, :].astype(F32)
            o_ref[sl, h * dv:(h + 1) * dv] = (on * (gr * _sigmoid(gr))).astype(o_ref.dtype)
    for h in range(nh):
        state_ref[h] = state[h]


def _gla(proj, la, gn, w_out, rows):
    s = proj.shape[0]
    dk, dv = GLA_DK, GLA_DV
    nh = GLA_HEADS
    n_steps = s // rows
    wo_rows = w_out.shape[0] // n_steps
    assert wo_rows * n_steps == w_out.shape[0] and wo_rows % 16 == 0

    def head_specs(width, col0):
        return [pl.BlockSpec((rows, width), functools.partial(lambda i, h: (i, col0 // width + h), h=h))
                for h in range(nh)]

    in_specs = (head_specs(dk, COL_GQ) + head_specs(dk, COL_GK) + head_specs(dv, COL_GV)
                + head_specs(dv, COL_GR)
                + [pl.BlockSpec((rows, nh * dk), lambda i: (i, 0)), pl.BlockSpec((1, dv), lambda i: (0, 0)),
                   pl.BlockSpec((wo_rows, w_out.shape[1]), lambda i: (i, 0))])
    return pl.pallas_call(
        _gla_kernel,
        grid=(n_steps,),
        in_specs=in_specs,
        out_specs=[pl.BlockSpec((rows, nh * dv), lambda i: (i, 0)),
                   pl.BlockSpec((wo_rows, w_out.shape[1]), lambda i: (i, 0))],
        out_shape=[jax.ShapeDtypeStruct((s, nh * dv), BF16), jax.ShapeDtypeStruct(w_out.shape, BF16)],
        scratch_shapes=[pltpu.VMEM((nh, dk, dv), F32)],
        compiler_params=_cparams(("arbitrary",)),
        name="gla",
    )(*([proj] * (4 * nh)), la, gn, w_out)


MERGE_SUB_ROWS = 128


def _merge_out_kernel(oa_ref, og_ref, ga_ref, gg_ref, x_ref, wo_ref, g_ref, wr_ref, bg_ref, be_ref,
                      x1_ref, h2_ref, idx_ref, gate_ref, cnt_ref, lt_ref, carry_ref):
    sub = MERGE_SUB_ROWS
    tiles = [slice(t * sub, (t + 1) * sub) for t in range(x_ref.shape[0] // sub)]
    merged = [(_sigmoid(ga_ref[rs, :].astype(F32)) * oa_ref[rs, :].astype(F32)
               + _sigmoid(gg_ref[rs, :].astype(F32)) * og_ref[rs, :].astype(F32)).astype(BF16) for rs in tiles]
    projected = [_dot(m, wo_ref[...]) for m in merged]
    for rs, y in zip(tiles, projected):
        x1 = x_ref[rs, :] + y
        x1_ref[rs, :] = x1
        ms = jnp.mean(x1 * x1, axis=-1, keepdims=True)
        h2b = (x1 * lax.rsqrt(ms + RMS_EPS) * g_ref[...]).astype(BF16)
        h2_ref[rs] = h2b.reshape((sub,) + h2_ref.shape[1:])
        lt_ref[:, rs] = _dot_nt(wr_ref[...], h2b)
    _route_kernel(lt_ref, bg_ref, be_ref, idx_ref, gate_ref, cnt_ref, carry_ref)


def _merge_out(o_attn, o_gla, gates, x, wo, g, wr, bg, be, tm):
    s, d = x.shape
    nr = wr.shape[0]
    row = lambda i: (i, 0)
    return pl.pallas_call(
        _merge_out_kernel,
        grid=(s // tm,),
        in_specs=[
            pl.BlockSpec((tm, d), row),
            pl.BlockSpec((tm, d), row),
            pl.BlockSpec((tm, d), lambda i: (i, 0)),
            pl.BlockSpec((tm, d), lambda i: (i, 1)),
            pl.BlockSpec((tm, d), row),
            pl.BlockSpec((d, d), lambda i: (0, 0), pipeline_mode=pl.Buffered(1)),
            pl.BlockSpec((1, d), lambda i: (0, 0)),
            pl.BlockSpec((nr, d), lambda i: (0, 0)),
            pl.BlockSpec((8, 1), lambda i: (0, 0)),
            pl.BlockSpec((N_EXPERTS, 1), lambda i: (0, 0)),
        ],
        out_specs=[
            pl.BlockSpec((tm, d), row),
            pl.BlockSpec((tm, d // LANES, LANES), lambda i: (i, 0, 0)),
            pl.BlockSpec((8, tm), lambda i: (0, i)),
            pl.BlockSpec((8, tm), lambda i: (0, i)),
            pl.BlockSpec((N_EXPERTS, LANES), lambda i: (0, 0)),
        ],
        out_shape=[
            jax.ShapeDtypeStruct((s, d), F32),
            jax.ShapeDtypeStruct((s, d // LANES, LANES), BF16),
            jax.ShapeDtypeStruct((8, s), jnp.int32),
            jax.ShapeDtypeStruct((8, s), F32),
            jax.ShapeDtypeStruct((N_EXPERTS, LANES), F32),
        ],
        scratch_shapes=[pltpu.VMEM((nr, tm), F32), pltpu.VMEM((N_EXPERTS, LANES), F32)],
        compiler_params=_cparams(("arbitrary",)),
        name="merge_out",
    )(o_attn, o_gla, gates, gates, x, wo, g, wr, bg, be)


ROUTER_ROWS = 8 + N_EXPERTS


def _route_kernel(lt_ref, bg_ref, be_ref, idx_ref, gate_ref, cnt_ref, carry_ref):
    tb = lt_ref.shape[1]
    epg = EXPERTS_PER_GROUP

    @pl.when(pl.program_id(0) == 0)
    def _():
        carry_ref[...] = jnp.zeros_like(carry_ref)

    logits = lt_ref[...]
    gl = logits[0:N_GROUPS]
    g_exp = jnp.exp(gl - jnp.max(gl, axis=0, keepdims=True))
    g_prob = g_exp / jnp.sum(g_exp, axis=0, keepdims=True)
    gb = gl + bg_ref[0:N_GROUPS]
    rowg = lax.broadcasted_iota(jnp.int32, (N_GROUPS, tb), 0)
    g_idx = jnp.min(jnp.where(gb == jnp.max(gb, axis=0, keepdims=True), rowg, N_GROUPS),
                    axis=0, keepdims=True)
    p_group = jnp.sum(jnp.where(rowg == g_idx, g_prob, 0.0), axis=0, keepdims=True)

    el = logits[8:8 + epg]
    eb = jnp.broadcast_to(be_ref[0:epg], (epg, tb))
    for g in range(1, N_GROUPS):
        pick = g_idx == g
        el = jnp.where(pick, logits[8 + g * epg:8 + (g + 1) * epg], el)
        eb = jnp.where(pick, be_ref[g * epg:(g + 1) * epg], eb)
    e_exp = jnp.exp(el - jnp.max(el, axis=0, keepdims=True))
    e_prob = e_exp / jnp.sum(e_exp, axis=0, keepdims=True)
    score = el + eb
    rowe = lax.broadcasted_iota(jnp.int32, (epg, tb), 0)
    i1 = jnp.min(jnp.where(score == jnp.max(score, axis=0, keepdims=True), rowe, epg),
                 axis=0, keepdims=True)
    score2 = jnp.where(rowe == i1, -jnp.inf, score)
    i2 = jnp.min(jnp.where(score2 == jnp.max(score2, axis=0, keepdims=True), rowe, epg),
                 axis=0, keepdims=True)
    q1 = jnp.sum(jnp.where(rowe == i1, e_prob, 0.0), axis=0, keepdims=True)
    q2 = jnp.sum(jnp.where(rowe == i2, e_prob, 0.0), axis=0, keepdims=True)
    qs = q1 + q2
    gate1 = p_group * q1 / qs
    gate2 = p_group * q2 / qs
    e1 = g_idx * epg + i1
    e2 = g_idx * epg + i2

    rowx = lax.broadcasted_iota(jnp.int32, (N_EXPERTS, tb), 0)
    hit1 = rowx == e1
    hit2 = rowx == e2
    member = (hit1 | hit2).astype(BF16)
    before = (lax.broadcasted_iota(jnp.int32, (tb, tb), 0)
              < lax.broadcasted_iota(jnp.int32, (tb, tb), 1)).astype(BF16)
    carry = carry_ref[...]
    count = _dot(member, before) + carry[:, 0:1]
    r1 = jnp.sum(jnp.where(hit1, count, 0.0), axis=0, keepdims=True).astype(jnp.int32)
    r2 = jnp.sum(jnp.where(hit2, count, 0.0), axis=0, keepdims=True).astype(jnp.int32)
    carry = carry + jnp.sum(member.astype(F32), axis=1, keepdims=True)
    carry_ref[...] = carry
    cnt_ref[...] = carry

    row8 = lax.broadcasted_iota(jnp.int32, (8, tb), 0)
    idx_ref[...] = jnp.where(row8 == 0, e1, jnp.where(row8 == 1, e2,
                             jnp.where(row8 == 2, r1, jnp.where(row8 == 3, r2, 0))))
    gate_ref[...] = jnp.where(row8 == 0, gate1, jnp.where(row8 == 1, gate2, 0.0))


def _expert_kernel(be_ref, nx_ref, nu_ref, dest_ref, neg_hbm, h2_hbm, wg_hbm, wu_hbm, wd_hbm, y_hbm,
                   pair_ref, h2v, xg, ystage, wgf, wuf, wdf, wgb, wub, wdb, hsem, wsem, ysem):
    n_used = nu_ref[0]
    rows = MOE_BLOCK
    n_tok = h2v.shape[0]

    def weight_copy(e, m):
        src, stage = ((wg_hbm, wgf), (wu_hbm, wuf), (wd_hbm, wdf))[m]
        return pltpu.make_async_copy(src.at[e], stage, wsem.at[m])

    def gather_row(blk, r, half):
        pair = jnp.maximum(pair_ref[blk * rows + r], 0)
        xg[half, r] = h2v[jnp.where(pair >= n_tok, pair - n_tok, pair)]

    def row_write(blk, r, half, dummy):
        pair = pair_ref[blk * rows + r]
        dst = jnp.where((pair < 0) | dummy, TOP_K * n_tok + half * rows + r, pair)
        return pltpu.make_async_copy(ystage.at[half, r], y_hbm.at[dst], ysem.at[half])

    def wait_row_writes(half):
        pltpu.make_async_copy(ystage.at[half], y_hbm.at[pl.ds(0, rows)], ysem.at[half]).wait()

    weight_priority = 1

    fill = pltpu.make_async_copy(neg_hbm, pair_ref, hsem.at[1])
    fill.start()
    h2_copy = pltpu.make_async_copy(h2_hbm, h2v, hsem.at[0])
    h2_copy.start()
    for m in range(3):
        weight_copy(be_ref[0], m).start(priority=weight_priority)
    ystage[...] = jnp.zeros_like(ystage)

    fill.wait()

    def place(p, carry):
        pair_ref[dest_ref[p]] = p
        return carry
    lax.fori_loop(0, dest_ref.shape[0], place, 0, unroll=8)

    h2_copy.wait()
    spare_fill = pltpu.make_async_copy(ystage.at[0], y_hbm.at[pl.ds(TOP_K * n_tok, rows)], hsem.at[0])
    spare_fill.start()
    spare_fill.wait()

    def first_gather(r, carry):
        gather_row(0, r, 0)
        return carry
    lax.fori_loop(0, rows, first_gather, 0)

    def block(b, carry):
        slot = b % 2
        first_of_expert = (b == 0) | (be_ref[b] != be_ref[jnp.maximum(b - 1, 0)])

        @pl.when(first_of_expert)
        def _():
            nxt = nx_ref[b]
            for m, (stage, dst) in enumerate(((wgf, wgb), (wuf, wub), (wdf, wdb))):
                weight_copy(be_ref[b], m).wait()
                dst[...] = stage[...].astype(BF16)

                @pl.when(nxt >= 0)
                def _():
                    weight_copy(nxt, m).start(priority=weight_priority)

        xb = xg[slot].reshape(rows, wgb.shape[0])
        prev_blk = jnp.maximum(b - 1, 0)
        for r in range(rows):
            row_write(prev_blk, r, 1 - slot, b == 0).start(priority=r % 2)
        for r in range(rows):
            gather_row(b + 1, r, 1 - slot)
        hg = _dot(xb, wgb[...])
        hu = _dot(xb, wub[...])
        act = (hg * _sigmoid(hg)) * hu
        y = _dot(act.astype(BF16), wdb[...]).astype(BF16)

        @pl.when(b >= 1)
        def _():
            wait_row_writes(slot)

        ystage[slot] = y.reshape(ystage.shape[1:])
        return carry

    lax.fori_loop(0, n_used, block, 0)

    last_half = (n_used - 1) % 2

    def last_writes(r, carry):
        row_write(n_used - 1, r, last_half, False).start()
        return carry
    lax.fori_loop(0, rows, last_writes, 0)
    wait_row_writes(1 - last_half)
    wait_row_writes(last_half)


def _experts(block_e, next_e, n_used, dest, h2c, wg, wu, wd):
    n_rows = block_e.shape[0] * MOE_BLOCK
    n_tok = h2c.shape[0]
    _, d, de = wg.shape
    slab = (d // LANES, LANES)
    grid_spec = pltpu.PrefetchScalarGridSpec(
        num_scalar_prefetch=4,
        grid=(1,),
        in_specs=[pl.BlockSpec(memory_space=pl.ANY)] * 5,
        out_specs=pl.BlockSpec(memory_space=pl.ANY),
        scratch_shapes=[
            pltpu.SMEM((n_rows,), jnp.int32),
            pltpu.VMEM((n_tok,) + slab, BF16),
            pltpu.VMEM((2, MOE_BLOCK) + slab, BF16),
            pltpu.VMEM((2, MOE_BLOCK) + slab, BF16),
            pltpu.VMEM((d, de), F32),
            pltpu.VMEM((d, de), F32),
            pltpu.VMEM((de, d), F32),
            pltpu.VMEM((d, de), BF16),
            pltpu.VMEM((d, de), BF16),
            pltpu.VMEM((de, d), BF16),
            pltpu.SemaphoreType.DMA((2,)),
            pltpu.SemaphoreType.DMA((3,)),
            pltpu.SemaphoreType.DMA((2,)),
        ],
    )
    padding_marks = jnp.full((n_rows,), -1, jnp.int32)
    return pl.pallas_call(
        _expert_kernel,
        grid_spec=grid_spec,
        out_shape=jax.ShapeDtypeStruct((TOP_K * n_tok + 2 * MOE_BLOCK,) + slab, BF16),
        compiler_params=_cparams(("arbitrary",), vmem=RESIDENT_VMEM_LIMIT),
        name="experts",
    )(block_e, next_e, n_used, dest, padding_marks, h2c, wg, wu, wd)


def _combine_kernel(y0_ref, y1_ref, gate_ref, x1_ref, g_ref, o_ref):
    gates = gate_ref[...]
    y0 = y0_ref[...].reshape(x1_ref.shape).astype(F32)
    y1 = y1_ref[...].reshape(x1_ref.shape).astype(F32)
    x2 = x1_ref[...] + (y0 * gates[:, 0:1] + y1 * gates[:, 1:2])
    ms = jnp.mean(x2 * x2, axis=-1, keepdims=True)
    o_ref[...] = x2 * lax.rsqrt(ms + RMS_EPS) * g_ref[...]


def _combine(y2, gates, x1, g, tm):
    s, d = x1.shape
    slab = (d // LANES, LANES)
    n_steps = s // tm
    return pl.pallas_call(
        _combine_kernel,
        grid=(n_steps,),
        in_specs=[
            pl.BlockSpec((tm,) + slab, lambda i: (i, 0, 0)),
            pl.BlockSpec((tm,) + slab, lambda i: (n_steps + i, 0, 0)),
            pl.BlockSpec((tm, TOP_K), lambda i: (i, 0)),
            pl.BlockSpec((tm, d), lambda i: (i, 0)),
            pl.BlockSpec((1, d), lambda i: (0, 0)),
        ],
        out_specs=pl.BlockSpec((tm, d), lambda i: (i, 0)),
        out_shape=jax.ShapeDtypeStruct((s, d), F32),
        compiler_params=_cparams(("parallel",)),
        name="combine",
    )(y2, y2, gates, x1, g)


def _tile(n, pref):
    return pref if n % pref == 0 else n


def kernel(x, norm_mix_g, w_in, w_gla_a2, b_gla_a, gla_norm_g, attn_sinks, rel_bias_table, w_out,
           norm_ffn_g, w_router_group, b_router_group, w_router_expert, b_router_expert,
           w_expert_gate, w_expert_up, w_expert_down, norm_final_g):
    b, s, d = x.shape
    assert b == 1 and w_in.shape[0] == 1, "single batch, single layer"
    assert (s * TOP_K) % MOE_BLOCK == 0
    x2d = x.reshape(s, d)
    w_in_t = w_in[0].T

    w2 = jnp.pad(w_gla_a2[0], ((0, LANES - GLA_LOWRANK), (0, 0))).astype(BF16)
    h, log_a = _norm_loga(x2d, norm_mix_g[0].reshape(1, d), w_in_t, w2, b_gla_a[0].reshape(1, -1),
                          _tile(s, 1024))
    proj = _in_proj(h, w_in_t, _tile(s, 4096), 512)

    bucket, is_prev = _folded_maps()
    o_attn = _swa(proj, attn_sinks[0], rel_bias_table.T, jnp.asarray(bucket), jnp.asarray(is_prev))
    o_gla, w_out_bf = _gla(proj, log_a, gla_norm_g[0].reshape(1, -1), w_out[0], _tile(s, 256))

    wr = jnp.zeros((ROUTER_ROWS, d), F32)
    wr = wr.at[0:N_GROUPS].set(w_router_group[0].T).at[8:].set(w_router_expert[0].T).astype(BF16)
    bg = jnp.pad(b_router_group[0], (0, 8 - N_GROUPS)).reshape(8, 1)
    x1, h2, idx, gate, counts = _merge_out(o_attn, o_gla, proj, x2d, w_out_bf,
                                           norm_ffn_g[0].reshape(1, d), wr, bg,
                                           b_router_expert[0].reshape(N_EXPERTS, 1), _tile(s, 512))
    counts = counts[:, 0].astype(jnp.int32)
    padded = (counts + MOE_BLOCK - 1) // MOE_BLOCK * MOE_BLOCK
    pend = jnp.cumsum(padded)
    pstart = pend - padded
    expert = idx[0:TOP_K]
    eids = jnp.arange(N_EXPERTS, dtype=jnp.int32)[:, None, None]
    dest = jnp.sum(jnp.where(expert[None] == eids, pstart[:, None, None], 0), axis=0) + idx[TOP_K:2 * TOP_K]
    n_pairs = s * TOP_K
    n_blocks = (n_pairs + N_EXPERTS * (MOE_BLOCK - 1) + MOE_BLOCK - 1) // MOE_BLOCK
    n_used = (pend[-1] // MOE_BLOCK).reshape(1)
    block_start = jnp.arange(n_blocks, dtype=jnp.int32) * MOE_BLOCK
    block_e = jnp.minimum(jnp.sum((pend[None, :] <= block_start[:, None]).astype(jnp.int32), axis=1),
                          N_EXPERTS - 1)

    e_col = jnp.arange(N_EXPERTS, dtype=jnp.int32)[:, None]
    e_row = jnp.arange(N_EXPERTS, dtype=jnp.int32)[None, :]
    later = (e_row > e_col) & (counts > 0)[None, :]
    next_nonempty = jnp.min(jnp.where(later, e_row, N_EXPERTS), axis=1)
    next_nonempty = jnp.where(next_nonempty < N_EXPERTS, next_nonempty, -1)
    next_e = jnp.sum(jnp.where(block_e[:, None] == e_row, next_nonempty[None, :], 0), axis=1)
    y2 = _experts(block_e, next_e, n_used, dest.reshape(-1), h2,
                  w_expert_gate[0], w_expert_up[0], w_expert_down[0])
    out = _combine(y2, gate[0:TOP_K].T, x1, norm_final_g.reshape(1, d), _tile(s, 512))
    return out.reshape(b, s, d)
```
